```python
import math
import jax, jax.numpy as jnp
from jax import lax
import numpy as np

D_MODEL = 1024
BATCH = 8
SEQ = 2048
DEPTH = 2

N_BRANCH = 3
BRANCH_W = 512
GLA_HEADS = 4
GLA_DK = 64
GLA_DV = BRANCH_W // GLA_HEADS
GLA_RANK = 16
GLA_TAU = 16.0
GLA_CHUNK = 64
S5_GROUP = 16
S5_GROUPS = BRANCH_W // S5_GROUP
S5_STATE = 64
ML_HEADS = 4
ML_DK = 64
ML_DV = BRANCH_W // ML_HEADS
ML_CONV = 4
ML_CHUNK = 64
D_FF = 2816
N_EXPERTS = 8
TOP_K = 2
D_FF_EXPERT = 3584
PLE_DIM = 256
N_DENSE = (DEPTH + 1) // 2
N_MOE = DEPTH // 2
DN_ALPHA = (2.0 * DEPTH) ** 0.25
DN_BETA = (8.0 * DEPTH) ** -0.25
LN_EPS = 1e-5

IN_WIDTHS = (
    GLA_HEADS * GLA_DK, GLA_HEADS * GLA_DK, BRANCH_W, GLA_RANK, BRANCH_W,
    BRANCH_W,
    ML_HEADS * ML_DK, ML_HEADS * ML_DK, BRANCH_W, ML_HEADS, ML_HEADS, BRANCH_W,
    N_BRANCH * D_MODEL,
)
D_IN = int(sum(IN_WIDTHS))
IN_OFFSETS = tuple(int(o) for o in np.cumsum(IN_WIDTHS)[:-1])
ML_F_START = IN_OFFSETS[9]

kernel_name = 'hybrid_gla_s5_mlstm_moe_deepnorm'


def layer_norm(x, g, b):
    xf = x.astype(jnp.float32)
    mu = jnp.mean(xf, axis=-1, keepdims=True)
    var = jnp.mean(jnp.square(xf - mu), axis=-1, keepdims=True)
    y = (xf - mu) * lax.rsqrt(var + LN_EPS) * g.astype(jnp.float32) + b.astype(jnp.float32)
    return y.astype(x.dtype)


def head_norm(t, g):
    mu = jnp.mean(t, axis=-1, keepdims=True)
    var = jnp.mean(jnp.square(t - mu), axis=-1, keepdims=True)
    b, s, h, d = t.shape
    return ((t - mu) * lax.rsqrt(var + LN_EPS)).reshape(b, s, h * d) * g.astype(jnp.float32)


def to_chunks(t, heads, chunk):
    b, s, w = t.shape
    return t.reshape(b, s // chunk, chunk, heads, w // heads).transpose(1, 0, 3, 2, 4)


def gate_chunks(t, chunk):
    b, s, h = t.shape
    return t.reshape(b, s // chunk, chunk, h).transpose(1, 0, 3, 2)


def from_chunks(t):
    nc, b, h, l, d = t.shape
    return t.transpose(1, 0, 3, 2, 4).reshape(b, nc * l, h, d)


def causal_conv_silu(t, w, b):
    k_w = w.shape[0]
    s = t.shape[1]
    tp = jnp.pad(t, ((0, 0), (k_w - 1, 0), (0, 0)))
    y = b.astype(jnp.float32)
    for j in range(k_w):
        y = y + tp[:, j:j + s] * w[j].astype(jnp.float32)
    return jax.nn.silu(y)


def gla_mixer(q, k, v, a_lr, g, w_a2, b_a2, norm_g):
    f32 = jnp.float32
    bsz = q.shape[0]
    log_a = jax.nn.log_sigmoid(a_lr.astype(f32) @ w_a2.astype(f32) + b_a2.astype(f32)) / GLA_TAU
    qc = to_chunks(q.astype(f32) * GLA_DK ** -0.5, GLA_HEADS, GLA_CHUNK)
    kc = to_chunks(k.astype(f32), GLA_HEADS, GLA_CHUNK)
    vc = to_chunks(v.astype(f32), GLA_HEADS, GLA_CHUNK)
    ac = to_chunks(log_a, GLA_HEADS, GLA_CHUNK)
    causal = jnp.tril(jnp.ones((GLA_CHUNK, GLA_CHUNK), dtype=bool))

    def step(state, blk):
        qb, kb, vb, ab = blk
        cum = jnp.cumsum(ab, axis=2)
        last = cum[:, :, -1, :]
        q_dec = qb * jnp.exp(cum)
        k_inv = kb * jnp.exp(-cum)
        att = jnp.where(causal, jnp.einsum('bhld,bhmd->bhlm', q_dec, k_inv), 0.0)
        out = jnp.einsum('bhlm,bhme->bhle', att, vb) + jnp.einsum('bhld,bhde->bhle', q_dec, state)
        k_tail = kb * jnp.exp(last[:, :, None, :] - cum)
        state = jnp.exp(last)[..., None] * state + jnp.einsum('bhld,bhle->bhde', k_tail, vb)
        return state, out

    s0 = jnp.zeros((bsz, GLA_HEADS, GLA_DK, GLA_DV), f32)
    _, o = lax.scan(step, s0, (qc, kc, vc, ac))
    o = head_norm(from_chunks(o), norm_g)
    return o * jax.nn.silu(g.astype(f32))


def s5_mixer(u, a_re, a_im, log_dt, b_re, b_im, c_re, c_im, d_skip, w_glu, b_glu):
    f32 = jnp.float32
    bsz, s, _ = u.shape
    ug = u.astype(f32).reshape(bsz, s, S5_GROUPS, S5_GROUP)
    lam = lax.complex(a_re.astype(f32), a_im.astype(f32))
    dt = jnp.exp(log_dt.astype(f32))[:, None]
    lam_bar = jnp.exp(lam * dt)
    b_c = lax.complex(b_re.astype(f32), b_im.astype(f32))
    b_bar = ((lam_bar - 1.0) / lam)[..., None] * b_c
    bu = jnp.einsum('gpn,bsgn->bsgp', b_bar, ug.astype(jnp.complex64))
    a_seq = jnp.broadcast_to(lam_bar, bu.shape)

    def combine(e1, e2):
        a1, x1 = e1
        a2, x2 = e2
        return a1 * a2, a2 * x1 + x2

    _, states = lax.associative_scan(combine, (a_seq, bu), axis=1)
    c_c = lax.complex(c_re.astype(f32), c_im.astype(f32))
    y = jnp.einsum('gnp,bsgp->bsgn', c_c, states).real + d_skip.astype(f32).reshape(S5_GROUPS, S5_GROUP) * ug
    y = jax.nn.gelu(y.reshape(bsz, s, BRANCH_W))
    return y * jax.nn.sigmoid(y @ w_glu.astype(f32) + b_glu.astype(f32))


def mlstm_mixer(q, k, v, i_pre, f_pre, o_pre, conv_w, conv_b, norm_g):
    f32 = jnp.float32
    bsz = q.shape[0]
    qk = causal_conv_silu(jnp.concatenate([q, k], axis=-1).astype(f32), conv_w, conv_b)
    qf, kf = jnp.split(qk, 2, axis=-1)
    qc = to_chunks(qf, ML_HEADS, ML_CHUNK)
    kc = to_chunks(kf * ML_DK ** -0.5, ML_HEADS, ML_CHUNK)
    vc = to_chunks(v.astype(f32), ML_HEADS, ML_CHUNK)
    ic = gate_chunks(i_pre.astype(f32), ML_CHUNK)
    fc = gate_chunks(jax.nn.log_sigmoid(f_pre.astype(f32)), ML_CHUNK)
    causal = jnp.tril(jnp.ones((ML_CHUNK, ML_CHUNK), dtype=bool))

    def step(carry, blk):
        c_st, n_st, m_st = carry
        qb, kb, vb, ib, lfb = blk
        bcum = jnp.cumsum(lfb, axis=-1)
        dmat = jnp.where(causal, bcum[..., :, None] - bcum[..., None, :] + ib[..., None, :], -jnp.inf)
        inter = bcum + m_st[..., None]
        m_row = jnp.maximum(inter, jnp.max(dmat, axis=-1))
        wts = jnp.exp(dmat - m_row[..., None])
        sc = jnp.einsum('bhld,bhmd->bhlm', qb, kb) * wts
        w_inter = jnp.exp(inter - m_row)
        num = jnp.einsum('bhlm,bhme->bhle', sc, vb) + w_inter[..., None] * jnp.einsum('bhld,bhde->bhle', qb, c_st)
        den = jnp.sum(sc, axis=-1) + w_inter * jnp.einsum('bhld,bhd->bhl', qb, n_st)
        h = num / jnp.maximum(jnp.abs(den), jnp.exp(-m_row))[..., None]
        g_tot = bcum[..., -1]
        tail = g_tot[..., None] - bcum + ib
        m_new = jnp.maximum(g_tot + m_st, jnp.max(tail, axis=-1))
        wt = jnp.exp(tail - m_new[..., None])
        decay = jnp.exp(g_tot + m_st - m_new)
        c_st = decay[..., None, None] * c_st + jnp.einsum('bhl,bhld,bhle->bhde', wt, kb, vb)
        n_st = decay[..., None] * n_st + jnp.einsum('bhl,bhld->bhd', wt, kb)
        return (c_st, n_st, m_new), h

    init = (jnp.zeros((bsz, ML_HEADS, ML_DK, ML_DV), f32),
            jnp.zeros((bsz, ML_HEADS, ML_DK), f32),
            jnp.zeros((bsz, ML_HEADS), f32))
    _, h = lax.scan(step, init, (qc, kc, vc, ic, fc))
    h = head_norm(from_chunks(h), norm_g)
    return h * jax.nn.sigmoid(o_pre.astype(f32))


def swiglu(x, w_gate, w_up, w_down):
    return (jax.nn.silu(x @ w_gate) * (x @ w_up)) @ w_down


def moe_swiglu(x, w_router, b_router, w_gate, w_up, w_down):
    logits = (x @ w_router + b_router).astype(jnp.float32)
    top_v, top_i = lax.top_k(logits, TOP_K)
    probs = jax.nn.softmax(top_v, axis=-1)
    comb = jnp.sum(jax.nn.one_hot(top_i, N_EXPERTS, dtype=jnp.float32) * probs[..., None], axis=-2)
    out = jnp.zeros_like(x)
    for e in range(N_EXPERTS):
        out = out + comb[..., e:e + 1].astype(x.dtype) * swiglu(x, w_gate[e], w_up[e], w_down[e])
    return out


def setup_inputs(seed: int = 0) -> dict:
    key = jax.random.key(seed)
    ks = iter(jax.random.split(key, 48))
    f32 = jnp.float32
    L = DEPTH

    def nrm(shape, scale):
        return jax.random.normal(next(ks), shape, f32) * scale

    def gain(shape):
        return 1.0 + nrm(shape, 0.02)

    x = nrm((BATCH, SEQ, D_MODEL), 1.0)
    p = nrm((DEPTH, BATCH, SEQ, PLE_DIM), 1.0)
    w_in = nrm((L, D_MODEL, D_IN), D_MODEL ** -0.5)
    b_in = nrm((L, D_IN), 0.02)
    b_in = b_in.at[:, ML_F_START:ML_F_START + ML_HEADS].add(jnp.linspace(3.0, 6.0, ML_HEADS, dtype=f32))
    gla_w_a2 = nrm((L, GLA_RANK, GLA_HEADS * GLA_DK), GLA_RANK ** -0.5)
    gla_b_a2 = nrm((L, GLA_HEADS * GLA_DK), 0.1)
    gla_norm_g = gain((L, BRANCH_W))
    s5_a_re = -0.5 + nrm((L, S5_GROUPS, S5_STATE), 0.01)
    s5_a_im = math.pi * jnp.arange(S5_STATE, dtype=f32) + nrm((L, S5_GROUPS, S5_STATE), 0.01)
    s5_log_dt = jax.random.uniform(next(ks), (L, S5_GROUPS), f32, math.log(1e-3), math.log(1e-1))
    s5_b_re = nrm((L, S5_GROUPS, S5_STATE, S5_GROUP), (2.0 * S5_GROUP) ** -0.5)
    s5_b_im = nrm((L, S5_GROUPS, S5_STATE, S5_GROUP), (2.0 * S5_GROUP) ** -0.5)
    s5_c_re = nrm((L, S5_GROUPS, S5_GROUP, S5_STATE), S5_STATE ** -0.5)
    s5_c_im = nrm((L, S5_GROUPS, S5_GROUP, S5_STATE), S5_STATE ** -0.5)
    s5_d = nrm((L, BRANCH_W), 1.0)
    s5_w_glu = nrm((L, BRANCH_W, BRANCH_W), BRANCH_W ** -0.5)
    s5_b_glu = nrm((L, BRANCH_W), 0.02)
    ml_conv_w = nrm((L, ML_CONV, 2 * ML_HEADS * ML_DK), ML_CONV ** -0.5)
    ml_conv_b = nrm((L, 2 * ML_HEADS * ML_DK), 0.02)
    ml_norm_g = gain((L, BRANCH_W))
    w_up = nrm((L, N_BRANCH, BRANCH_W, D_MODEL), BRANCH_W ** -0.5)
    w_o = nrm((L, D_MODEL, D_MODEL), DN_BETA * D_MODEL ** -0.5)
    ln1_g = gain((L, D_MODEL))
    ln1_b = nrm((L, D_MODEL), 0.02)
    ffn_wg = nrm((N_DENSE, D_MODEL, D_FF), D_MODEL ** -0.5)
    ffn_wu = nrm((N_DENSE, D_MODEL, D_FF), D_MODEL ** -0.5)
    ffn_wd = nrm((N_DENSE, D_FF, D_MODEL), DN_BETA * D_FF ** -0.5)
    moe_router = nrm((N_MOE, D_MODEL, N_EXPERTS), D_MODEL ** -0.5)
    moe_router_b = nrm((N_MOE, N_EXPERTS), 0.01)
    moe_wg = nrm((N_MOE, N_EXPERTS, D_MODEL, D_FF_EXPERT), D_MODEL ** -0.5)
    moe_wu = nrm((N_MOE, N_EXPERTS, D_MODEL, D_FF_EXPERT), D_MODEL ** -0.5)
    moe_wd = nrm((N_MOE, N_EXPERTS, D_FF_EXPERT, D_MODEL), DN_BETA * D_FF_EXPERT ** -0.5)
    ple_w_gate = nrm((L, D_MODEL, D_MODEL), D_MODEL ** -0.5)
    ple_w_proj = nrm((L, PLE_DIM, D_MODEL), DN_BETA * PLE_DIM ** -0.5)
    ln2_g = gain((L, D_MODEL))
    ln2_b = nrm((L, D_MODEL), 0.02)
    return {
        'x': x, 'p': p, 'w_in': w_in, 'b_in': b_in,
        'gla_w_a2': gla_w_a2, 'gla_b_a2': gla_b_a2, 'gla_norm_g': gla_norm_g,
        's5_a_re': s5_a_re, 's5_a_im': s5_a_im, 's5_log_dt': s5_log_dt,
        's5_b_re': s5_b_re, 's5_b_im': s5_b_im, 's5_c_re': s5_c_re, 's5_c_im': s5_c_im,
        's5_d': s5_d, 's5_w_glu': s5_w_glu, 's5_b_glu': s5_b_glu,
        'ml_conv_w': ml_conv_w, 'ml_conv_b': ml_conv_b, 'ml_norm_g': ml_norm_g,
        'w_up': w_up, 'w_o': w_o, 'ln1_g': ln1_g, 'ln1_b': ln1_b,
        'ffn_wg': ffn_wg, 'ffn_wu': ffn_wu, 'ffn_wd': ffn_wd,
        'moe_router': moe_router, 'moe_router_b': moe_router_b,
        'moe_wg': moe_wg, 'moe_wu': moe_wu, 'moe_wd': moe_wd,
        'ple_w_gate': ple_w_gate, 'ple_w_proj': ple_w_proj, 'ln2_g': ln2_g, 'ln2_b': ln2_b,
    }


def reference(x, p, w_in, b_in, gla_w_a2, gla_b_a2, gla_norm_g,
              s5_a_re, s5_a_im, s5_log_dt, s5_b_re, s5_b_im, s5_c_re, s5_c_im,
              s5_d, s5_w_glu, s5_b_glu, ml_conv_w, ml_conv_b, ml_norm_g,
              w_up, w_o, ln1_g, ln1_b, ffn_wg, ffn_wu, ffn_wd,
              moe_router, moe_router_b, moe_wg, moe_wu, moe_wd,
              ple_w_gate, ple_w_proj, ln2_g, ln2_b):
    bsz, s, _ = x.shape
    for i in range(DEPTH):
        h = x @ w_in[i] + b_in[i]
        (g_q, g_k, g_v, g_a, g_g, s_u, m_q, m_k, m_v, m_i, m_f, m_o, gate_pre) = jnp.split(h, IN_OFFSETS, axis=-1)
        y_gla = gla_mixer(g_q, g_k, g_v, g_a, g_g, gla_w_a2[i], gla_b_a2[i], gla_norm_g[i])
        y_s5 = s5_mixer(s_u, s5_a_re[i], s5_a_im[i], s5_log_dt[i], s5_b_re[i], s5_b_im[i],
                        s5_c_re[i], s5_c_im[i], s5_d[i], s5_w_glu[i], s5_b_glu[i])
        y_ml = mlstm_mixer(m_q, m_k, m_v, m_i, m_f, m_o, ml_conv_w[i], ml_conv_b[i], ml_norm_g[i])
        ys = jnp.stack([y_gla, y_s5, y_ml], axis=2).astype(x.dtype)
        ups = jnp.einsum('bsrc,rcd->bsrd', ys, w_up[i])
        gates = jax.nn.sigmoid(gate_pre).reshape(bsz, s, N_BRANCH, D_MODEL)
        mix = jnp.einsum('bsrd,bsrd->bsd', gates, ups) @ w_o[i]
        x = layer_norm(DN_ALPHA * x + mix, ln1_g[i], ln1_b[i])
        if i % 2 == 0:
            j = i // 2
            f = swiglu(x, ffn_wg[j], ffn_wu[j], ffn_wd[j])
        else:
            j = i // 2
            f = moe_swiglu(x, moe_router[j], moe_router_b[j], moe_wg[j], moe_wu[j], moe_wd[j])
        ple = jax.nn.sigmoid(x @ ple_w_gate[i]) * (p[i] @ ple_w_proj[i])
        x = layer_norm(DN_ALPHA * x + f + ple, ln2_g[i], ln2_b[i])
    return x
```

```python
import functools
import math

import numpy as np
import jax
import jax.numpy as jnp
from jax import lax
from jax.experimental import pallas as pl
from jax.experimental.pallas import tpu as pltpu

F32 = jnp.float32
BF16 = jnp.bfloat16

D_MODEL = 1024
DEPTH = 2
N_BRANCH = 3
BRANCH_W = 512
HEADS = 4
DK = 64
DV = BRANCH_W // HEADS
GLA_RANK = 16
GLA_TAU = 16.0
CHUNK = 64
S5_GROUP = 16
S5_GROUPS = BRANCH_W // S5_GROUP
S5_STATE = 64
S5_CHUNK = 16
ML_CONV = 4
D_FF = 2816
N_EXPERTS = 8
D_FF_EXPERT = 3584
PLE_DIM = 256
DN_ALPHA = (2.0 * DEPTH) ** 0.25
LN_EPS = 1e-5

IN_WIDTHS = (
    HEADS * DK, HEADS * DK, BRANCH_W, GLA_RANK, BRANCH_W,
    BRANCH_W,
    HEADS * DK, HEADS * DK, BRANCH_W, HEADS, HEADS, BRANCH_W,
    N_BRANCH * D_MODEL,
)
IN_OFF = tuple(int(o) for o in np.concatenate([[0], np.cumsum(IN_WIDTHS)]))

LANES = 128
SEQ_BLOCK = 256
PAIR_W = 2 * DK
VMEM_LIMIT = 56 * 1024 * 1024


def _cparams(sem):
    return pltpu.CompilerParams(dimension_semantics=sem, vmem_limit_bytes=VMEM_LIMIT)


def _dot(a, b):
    return jnp.dot(a, b, preferred_element_type=F32)


def _dot_nt(a, b):
    return lax.dot_general(a, b, (((1,), (1,)), ((), ())), preferred_element_type=F32)


def _dot_tn(a, b):
    return lax.dot_general(a, b, (((0,), (0,)), ((), ())), preferred_element_type=F32)


def _split3(a):
    hi = a.astype(BF16)
    r = a - hi.astype(F32)
    mid = r.astype(BF16)
    lo = (r - mid.astype(F32)).astype(BF16)
    return hi, mid, lo


def _split2(a):
    hi = a.astype(BF16)
    lo = (a - hi.astype(F32)).astype(BF16)
    return hi, lo


def _log_sigmoid(x):
    return jnp.minimum(x, 0.0) - jnp.log(1.0 + jnp.exp(-jnp.abs(x)))


def _sigmoid(x):
    return 1.0 / (1.0 + jnp.exp(-x))


def _silu(x):
    return x * _sigmoid(x)


def _gelu_tanh(x):
    return 0.5 * x * (1.0 + jnp.tanh(math.sqrt(2.0 / math.pi) * (x + 0.044715 * (x * x * x))))


def _layer_norm(v, g, b):
    mu = jnp.mean(v, axis=-1, keepdims=True)
    c = v - mu
    var = jnp.mean(c * c, axis=-1, keepdims=True)
    return c * lax.rsqrt(var + LN_EPS) * g + b


def _head_norm(o):
    mu = jnp.mean(o, axis=-1, keepdims=True)
    c = o - mu
    var = jnp.mean(c * c, axis=-1, keepdims=True)
    return c * lax.rsqrt(var + LN_EPS)


def _chunk_tri(n, chunk):
    i = np.arange(n)
    return ((i[:, None] >= i[None, :]) & (i[:, None] // chunk == i[None, :] // chunk)).astype(np.float32)


def _full(shape):
    nd = len(shape)
    return pl.BlockSpec(shape, lambda *_: (0,) * nd)


def _gla_kernel(x_ref, w_ref, b_ref, ng_ref, tri_ref, y_ref, st_ref, o_ref):
    @pl.when(pl.program_id(1) == 0)
    def _():
        st_ref[...] = jnp.zeros_like(st_ref)

    hk = HEADS * DK
    xb = x_ref[0].astype(BF16)
    h = _dot(xb, w_ref[...]) + b_ref[...]
    q = h[:, 0:hk]
    k = h[:, hk:2 * hk]
    z = h[:, 2 * hk:3 * hk]
    v = h[:, 3 * hk:3 * hk + BRANCH_W]
    g = h[:, 3 * hk + BRANCH_W:3 * hk + 2 * BRANCH_W]

    la = _log_sigmoid(z) * (1.0 / GLA_TAU)
    tri = tri_ref[...]
    la_h, la_m, la_l = _split3(la)
    cum = _dot(tri, la_h) + _dot(tri, la_m) + _dot(tri, la_l)
    qd = q * (DK ** -0.5) * jnp.exp(cum)
    ki = k * jnp.exp(-cum)
    vb = v.astype(BF16)

    lane = lax.broadcasted_iota(jnp.int32, (1, PAIR_W), 1)
    row_i = lax.broadcasted_iota(jnp.int32, (CHUNK, CHUNK), 0)
    col_i = lax.broadcasted_iota(jnp.int32, (CHUNK, CHUNK), 1)
    causal = row_i >= col_i
    bd_r = lax.broadcasted_iota(jnp.int32, (2 * DV, PAIR_W), 0)
    bd_c = lax.broadcasted_iota(jnp.int32, (2 * DV, PAIR_W), 1)
    blockdiag = (bd_r >= DV) == (bd_c >= DK)

    n_chunks = x_ref.shape[1] // CHUNK
    for p in range(HEADS // 2):
        st = st_ref[p]
        lsl = slice(p * PAIR_W, (p + 1) * PAIR_W)
        for c in range(n_chunks):
            rsl = slice(c * CHUNK, (c + 1) * CHUNK)
            qd_c = qd[rsl, lsl]
            ki_c = ki[rsl, lsl].astype(BF16)
            cum_c = cum[rsl, lsl]
            last = cum_c[CHUNK - 1:CHUNK, :]
            kt = (k[rsl, lsl] * jnp.exp(last - cum_c)).astype(BF16)
            inter = _dot_nt(qd_c.astype(BF16), st.astype(BF16))
            for hh in range(2):
                head = 2 * p + hh
                in_head = (lane >= hh * DK) & (lane < (hh + 1) * DK)
                qm = jnp.where(in_head, qd_c, 0.0).astype(BF16)
                att = jnp.where(causal, _dot_nt(qm, ki_c), 0.0)
                o_h = _dot(att.astype(BF16), vb[rsl, head * DV:(head + 1) * DV])
                o_ref[rsl, head * DV:(head + 1) * DV] = o_h + inter[:, hh * DV:(hh + 1) * DV]
            upd = _dot_tn(vb[rsl, p * 2 * DV:(p + 1) * 2 * DV], kt)
            st = st * jnp.exp(last) + jnp.where(blockdiag, upd, 0.0)
        st_ref[p] = st

    ng = ng_ref[...]
    for head in range(HEADS):
        hsl = slice(head * DV, (head + 1) * DV)
        y = _head_norm(o_ref[:, hsl]) * ng[:, hsl] * _silu(g[:, hsl])
        y_ref[0, :, hsl] = y.astype(y_ref.dtype)


def _gla_mixer(x, w, b, ng, tri):
    bsz, s, d = x.shape
    wcols = w.shape[1]
    return pl.pallas_call(
        _gla_kernel,
        grid=(bsz, s // SEQ_BLOCK),
        in_specs=[
            pl.BlockSpec((1, SEQ_BLOCK, d), lambda i, j: (i, j, 0)),
            _full((d, wcols)), _full((1, wcols)), _full((1, BRANCH_W)),
            _full((SEQ_BLOCK, SEQ_BLOCK)),
        ],
        out_specs=pl.BlockSpec((1, SEQ_BLOCK, BRANCH_W), lambda i, j: (i, j, 0)),
        out_shape=jax.ShapeDtypeStruct((bsz, s, BRANCH_W), BF16),
        scratch_shapes=[pltpu.VMEM((HEADS // 2, 2 * DV, PAIR_W), F32),
                        pltpu.VMEM((SEQ_BLOCK, BRANCH_W), F32)],
        compiler_params=_cparams(("arbitrary", "arbitrary")),
        name="gla_mixer",
    )(x, w, b, ng, tri)


ML_ST_ROWS = 2 * DV + LANES


def _ml_kernel(x_ref, w_ref, b_ref, wgc_ref, bgc_ref, wgr_ref, bgr_ref, cw_ref, cb_ref, ng_ref,
               tri_ref, trit_ref, y_ref, ct_ref, m_ref, carry_ref, o_ref):
    @pl.when(pl.program_id(1) == 0)
    def _():
        ct_ref[...] = jnp.zeros_like(ct_ref)
        m_ref[...] = jnp.zeros_like(m_ref)
        carry_ref[...] = jnp.zeros_like(carry_ref)

    hk = HEADS * DK
    lb = x_ref.shape[1]
    x32 = x_ref[0]
    xh, xl = _split2(x32)
    h = _dot(xh, w_ref[...]) + b_ref[...]
    qk = h[:, 0:2 * hk]
    v = h[:, 2 * hk:2 * hk + BRANCH_W]
    o_pre = h[:, 2 * hk + BRANCH_W:2 * hk + 2 * BRANCH_W]

    ext = jnp.concatenate([carry_ref[...], qk], axis=0)
    cw = cw_ref[...]
    acc = cb_ref[...] + ext[8 - (ML_CONV - 1):8 - (ML_CONV - 1) + lb] * cw[0:1]
    for j in range(1, ML_CONV):
        off = 8 - (ML_CONV - 1) + j
        acc = acc + ext[off:off + lb] * cw[j:j + 1]
    carry_ref[...] = qk[lb - 8:lb]
    qkc = _silu(acc)
    qf = qkc[:, 0:hk]
    kf = qkc[:, hk:2 * hk] * (DK ** -0.5)
    vb = v.astype(BF16)

    gc = (_dot(xh, wgc_ref[0]) + _dot(xl, wgc_ref[0]) + _dot(xh, wgc_ref[1])) + bgc_ref[...]
    gr = (_dot_nt(wgr_ref[0], xh) + _dot_nt(wgr_ref[0], xl) + _dot_nt(wgr_ref[1], xh)) + bgr_ref[...]
    lf_c = _log_sigmoid(gc)
    lf_r = _log_sigmoid(gr)
    tri = tri_ref[...]
    trit = trit_ref[...]
    c_h, c_m, c_l = _split3(lf_c)
    bc = _dot(tri, c_h) + _dot(tri, c_m) + _dot(tri, c_l)
    r_h, r_m, r_l = _split3(lf_r)
    br = _dot(r_h, trit) + _dot(r_m, trit) + _dot(r_l, trit)

    lane = lax.broadcasted_iota(jnp.int32, (1, PAIR_W), 1)
    row_i = lax.broadcasted_iota(jnp.int32, (CHUNK, CHUNK), 0)
    col_i = lax.broadcasted_iota(jnp.int32, (CHUNK, CHUNK), 1)
    causal = row_i >= col_i
    sr = lax.broadcasted_iota(jnp.int32, (ML_ST_ROWS, PAIR_W), 0)
    sc_ = lax.broadcasted_iota(jnp.int32, (ML_ST_ROWS, PAIR_W), 1)
    first = sc_ < DK
    rows_h0 = (sr < DV) | (sr == 2 * DV)
    rows_h1 = ((sr >= DV) & (sr < 2 * DV)) | (sr == 2 * DV + 1)
    st_mask = (rows_h0 & first) | (rows_h1 & ~first)
    ones_blk = jnp.ones((CHUNK, LANES), BF16)

    n_chunks = lb // CHUNK
    for p in range(HEADS // 2):
        ct = ct_ref[p]
        lsl = slice(p * PAIR_W, (p + 1) * PAIR_W)
        m_pair = [m_ref[2 * p + hh][0:1, 0:1] for hh in range(2)]
        for c in range(n_chunks):
            rsl = slice(c * CHUNK, (c + 1) * CHUNK)
            q_c = qf[rsl, lsl]
            k_c = kf[rsl, lsl]
            k_cb = k_c.astype(BF16)
            inter_mm = _dot_nt(q_c.astype(BF16), ct.astype(BF16))
            wt_cols, decays = [], []
            for hh in range(2):
                head = 2 * p + hh
                m_st = m_pair[hh]
                b_col = bc[rsl, HEADS + head:HEADS + head + 1]
                i_col = gc[rsl, head:head + 1]
                b_row = br[HEADS + head:HEADS + head + 1, rsl]
                i_row = gr[head:head + 1, rsl]
                dmat = jnp.where(causal, b_col - b_row + i_row, -jnp.inf)
                inter = b_col + m_st
                m_row = jnp.maximum(inter, jnp.max(dmat, axis=-1, keepdims=True))
                wts = jnp.exp(dmat - m_row)
                in_head = (lane >= hh * DK) & (lane < (hh + 1) * DK)
                qm = jnp.where(in_head, q_c, 0.0).astype(BF16)
                sc = _dot_nt(qm, k_cb) * wts
                w_inter = jnp.exp(inter - m_row)
                num = _dot(sc.astype(BF16), vb[rsl, head * DV:(head + 1) * DV]) \
                    + w_inter * inter_mm[:, hh * DV:(hh + 1) * DV]
                den = jnp.sum(sc, axis=-1, keepdims=True) \
                    + w_inter * inter_mm[:, 2 * DV + hh:2 * DV + hh + 1]
                o_ref[rsl, head * DV:(head + 1) * DV] = num / jnp.maximum(jnp.abs(den), jnp.exp(-m_row))
                g_tot = b_col[CHUNK - 1:CHUNK, :]
                tail = g_tot - b_col + i_col
                m_new = jnp.maximum(g_tot + m_st, jnp.max(tail, axis=0, keepdims=True))
                wt_cols.append(jnp.exp(tail - m_new))
                decays.append(jnp.exp(g_tot + m_st - m_new))
                m_pair[hh] = m_new
            wk = (k_c * jnp.where(lane < DK, wt_cols[0], wt_cols[1])).astype(BF16)
            vp = jnp.concatenate([vb[rsl, p * 2 * DV:(p + 1) * 2 * DV], ones_blk], axis=1)
            upd = _dot_tn(vp, wk)
            ct = ct * jnp.where(lane < DK, decays[0], decays[1]) + jnp.where(st_mask, upd, 0.0)
        ct_ref[p] = ct
        for hh in range(2):
            m_ref[2 * p + hh] = jnp.broadcast_to(m_pair[hh], m_ref.shape[1:])

    ng = ng_ref[...]
    for head in range(HEADS):
        hsl = slice(head * DV, (head + 1) * DV)
        y = _head_norm(o_ref[:, hsl]) * ng[:, hsl] * _sigmoid(o_pre[:, hsl])
        y_ref[0, :, hsl] = y.astype(y_ref.dtype)


def _ml_mixer(x, w, b, wgc, bgc, wgr, bgr, cw, cb, ng, tri, trit):
    bsz, s, d = x.shape
    wcols = w.shape[1]
    return pl.pallas_call(
        _ml_kernel,
        grid=(bsz, s // SEQ_BLOCK),
        in_specs=[
            pl.BlockSpec((1, SEQ_BLOCK, d), lambda i, j: (i, j, 0)),
            _full((d, wcols)), _full((1, wcols)),
            _full((2, d, LANES)), _full((1, LANES)),
            _full((2, 8, d)), _full((8, 1)),
            _full((ML_CONV, 2 * HEADS * DK)), _full((1, 2 * HEADS * DK)),
            _full((1, BRANCH_W)),
            _full((SEQ_BLOCK, SEQ_BLOCK)), _full((SEQ_BLOCK, SEQ_BLOCK)),
        ],
        out_specs=pl.BlockSpec((1, SEQ_BLOCK, BRANCH_W), lambda i, j: (i, j, 0)),
        out_shape=jax.ShapeDtypeStruct((bsz, s, BRANCH_W), BF16),
        scratch_shapes=[pltpu.VMEM((HEADS // 2, ML_ST_ROWS, PAIR_W), F32),
                        pltpu.VMEM((HEADS, 8, LANES), F32),
                        pltpu.VMEM((8, 2 * HEADS * DK), F32),
                        pltpu.VMEM((SEQ_BLOCK, BRANCH_W), F32)],
        compiler_params=_cparams(("arbitrary", "arbitrary")),
        name="mlstm_mixer",
    )(x, w, b, wgc, bgc, wgr, bgr, cw, cb, ng, tri, trit)


def _proj_kernel(x_ref, w_ref, b_ref, o_ref):
    o_ref[...] = (_dot(x_ref[...].astype(BF16), w_ref[...]) + b_ref[...]).astype(o_ref.dtype)


def _project(x2, w, b, tm=512, out_dtype=F32):
    t, d = x2.shape
    n = w.shape[1]
    return pl.pallas_call(
        _proj_kernel,
        grid=(t // tm,),
        in_specs=[pl.BlockSpec((tm, d), lambda i: (i, 0)), _full((d, n)), _full((1, n))],
        out_specs=pl.BlockSpec((tm, n), lambda i: (i, 0)),
        out_shape=jax.ShapeDtypeStruct((t, n), out_dtype),
        compiler_params=_cparams(("parallel",)),
        name="s5_in_proj",
    )(x2, w, b)


def _s5_kernel(bsz, u_ref, e_ref, tz_ref, f_ref, a_ref, y_ref, xin_ref, xprev_ref):
    rows = u_ref.shape[1]
    gw = S5_CHUNK * S5_GROUP
    u = u_ref[0]
    xin_ref[...] = _dot(u, e_ref[0])
    ar = a_ref[0, 0:1, :]
    ai = a_ref[0, 1:2, :]

    def body(c, carry):
        xr, xi = carry
        r0 = pl.multiple_of(c * bsz, bsz)
        xprev_ref[pl.ds(r0, bsz), 0:LANES] = xr
        xprev_ref[pl.ds(r0, bsz), LANES:2 * LANES] = xi
        nr = ar * xr - ai * xi + xin_ref[pl.ds(r0, bsz), 0:LANES]
        ni = ar * xi + ai * xr + xin_ref[pl.ds(r0, bsz), LANES:2 * LANES]
        return nr, ni

    zero = jnp.zeros((bsz, LANES), F32)
    lax.fori_loop(0, rows // bsz, body, (zero, zero))
    xp_h, xp_l = _split2(xprev_ref[...])
    for j in range(2):
        fj = f_ref[0, j]
        y = _dot(u[:, j * gw:(j + 1) * gw], tz_ref[0, j]) + _dot(xp_h, fj) + _dot(xp_l, fj)
        y_ref[0, :, j * gw:(j + 1) * gw] = y


def _s5_core(u_blk, e, tz, f, a, bsz):
    npair, rows, width = u_blk.shape
    return pl.pallas_call(
        functools.partial(_s5_kernel, bsz),
        grid=(npair,),
        in_specs=[
            pl.BlockSpec((1, rows, width), lambda i: (i, 0, 0)),
            pl.BlockSpec((1,) + e.shape[1:], lambda i: (i, 0, 0)),
            pl.BlockSpec((1,) + tz.shape[1:], lambda i: (i, 0, 0, 0)),
            pl.BlockSpec((1,) + f.shape[1:], lambda i: (i, 0, 0, 0)),
            pl.BlockSpec((1,) + a.shape[1:], lambda i: (i, 0, 0)),
        ],
        out_specs=pl.BlockSpec((1, rows, width), lambda i: (i, 0, 0)),
        out_shape=jax.ShapeDtypeStruct((npair, rows, width), F32),
        scratch_shapes=[pltpu.VMEM((rows, 2 * LANES), F32), pltpu.VMEM((rows, 2 * LANES), F32)],
        compiler_params=_cparams(("parallel",)),
        name="s5_scan",
    )(u_blk, e, tz, f, a)


def _s5_tables(a_re, a_im, log_dt, b_re, b_im, c_re, c_im):
    nl = a_re.shape[0]
    g, p_, n, lc = S5_GROUPS, S5_STATE, S5_GROUP, S5_CHUNK
    lam = lax.complex(a_re.astype(F32), a_im.astype(F32))
    dt = jnp.exp(log_dt.astype(F32))[..., None]
    lam_bar = jnp.exp(lam * dt)
    b_bar = ((lam_bar - 1.0) / lam)[..., None] * lax.complex(b_re.astype(F32), b_im.astype(F32))
    c_c = lax.complex(c_re.astype(F32), c_im.astype(F32))
    steps = jnp.arange(lc + 1, dtype=F32)
    pw = jnp.exp((lam * dt)[..., None] * steps)
    kern = jnp.einsum('zgnp,zgpt,zgpm->zgtnm', c_c, pw[..., :lc], b_bar,
                      precision=lax.Precision.HIGHEST).real
    li = np.arange(lc)
    tau = li[None, :] - li[:, None]
    tz = kern[:, :, np.clip(tau, 0, lc - 1)]
    tz = jnp.where((tau >= 0)[None, None, :, :, None, None], tz, 0.0)
    tz = tz.transpose(0, 1, 2, 5, 3, 4).reshape(nl, g // 2, 2, lc * n, lc * n)
    e_c = pw[..., lc - 1 - li][..., None] * b_bar[:, :, :, None, :]
    e_c = e_c.transpose(0, 1, 3, 4, 2).reshape(nl, g // 2, 2, lc * n, p_)
    zeros = jnp.zeros_like(e_c.real[:, :, 0])
    e_top = jnp.concatenate([e_c.real[:, :, 0], zeros, e_c.imag[:, :, 0], zeros], axis=-1)
    e_bot = jnp.concatenate([zeros, e_c.real[:, :, 1], zeros, e_c.imag[:, :, 1]], axis=-1)
    e_pair = jnp.concatenate([e_top, e_bot], axis=-2)
    f_c = c_c[:, :, None, :, :] * pw[..., 1:].transpose(0, 1, 3, 2)[:, :, :, None, :]
    f_c = f_c.transpose(0, 1, 4, 2, 3).reshape(nl, g // 2, 2, p_, lc * n)
    zf = jnp.zeros_like(f_c.real[:, :, 0])
    f0 = jnp.concatenate([f_c.real[:, :, 0], zf, -f_c.imag[:, :, 0], zf], axis=-2)
    f1 = jnp.concatenate([zf, f_c.real[:, :, 1], zf, -f_c.imag[:, :, 1]], axis=-2)
    f_pair = jnp.stack([f0, f1], axis=2)
    a_l = pw[..., lc].reshape(nl, g // 2, 2 * p_)
    a_pair = jnp.stack([a_l.real, a_l.imag], axis=2)
    return e_pair.astype(BF16), tz.astype(BF16), f_pair.astype(BF16), a_pair.astype(F32)


def _s5_mixer_raw(x2, bsz, s, w_u, b_u, e, tz, f, a):
    u = _project(x2, w_u, b_u)
    nc = s // S5_CHUNK
    npair = S5_GROUPS // 2
    ub = u.astype(BF16).reshape(bsz, nc, S5_CHUNK, npair, 2, S5_GROUP)
    ub = ub.transpose(3, 1, 0, 4, 2, 5).reshape(npair, nc * bsz, 2 * S5_CHUNK * S5_GROUP)
    y = _s5_core(ub, e, tz, f, a, bsz)
    y = y.reshape(npair, nc, bsz, 2, S5_CHUNK, S5_GROUP).transpose(2, 1, 4, 0, 3, 5)
    return u, y.reshape(bsz * s, BRANCH_W)


def _merge_kernel(x_ref, yg_ref, y5_ref, u_ref, ym_ref, wgate_ref, bgate_ref, wglu_ref, bglu_ref, d_ref,
                  wup_ref, wo_ref, g_ref, b_ref, o_ref):
    x = x_ref[...]
    xb = x.astype(BF16)
    y5 = _gelu_tanh(y5_ref[...] + d_ref[...] * u_ref[...])
    y5 = y5 * _sigmoid(_dot(y5.astype(BF16), wglu_ref[...]) + bglu_ref[...])
    ys = (yg_ref[...], y5.astype(BF16), ym_ref[...])
    acc = None
    for r in range(N_BRANCH):
        gate = _sigmoid(_dot(xb, wgate_ref[:, r * D_MODEL:(r + 1) * D_MODEL])
                        + bgate_ref[:, r * D_MODEL:(r + 1) * D_MODEL])
        term = gate * _dot(ys[r], wup_ref[r])
        acc = term if acc is None else acc + term
    mix = _dot(acc.astype(BF16), wo_ref[...])
    o_ref[...] = _layer_norm(DN_ALPHA * x + mix, g_ref[...], b_ref[...])


def _merge(x2, yg, y5, u, ym, wgate, bgate, wglu, bglu, d, wup, wo, g, b, tm=256):
    t, dm = x2.shape
    row = lambda w: pl.BlockSpec((tm, w), lambda i: (i, 0))
    return pl.pallas_call(
        _merge_kernel,
        grid=(t // tm,),
        in_specs=[row(dm), row(BRANCH_W), row(BRANCH_W), row(BRANCH_W), row(BRANCH_W),
                  _full(wgate.shape), _full(bgate.shape), _full(wglu.shape), _full(bglu.shape),
                  _full(d.shape), _full(wup.shape), _full(wo.shape), _full(g.shape), _full(b.shape)],
        out_specs=row(dm),
        out_shape=jax.ShapeDtypeStruct((t, dm), F32),
        compiler_params=_cparams(("parallel",)),
        name="merge_ln1",
    )(x2, yg, y5, u, ym, wgate, bgate, wglu, bglu, d, wup, wo, g, b)


def _ple_ln2(x, xb, f, p_ref, pwg_ref, pwp_ref, g_ref, b_ref):
    e = _sigmoid(_dot(xb, pwg_ref[...])) * _dot(p_ref[...].astype(BF16), pwp_ref[...])
    return _layer_norm(DN_ALPHA * x + f + e, g_ref[...], b_ref[...])


def _ffn_kernel(x_ref, p_ref, wg_ref, wu_ref, wd_ref, pwg_ref, pwp_ref, g_ref, b_ref, o_ref):
    x = x_ref[...]
    xb = x.astype(BF16)
    hid = (_silu(_dot(xb, wg_ref[...])) * _dot(xb, wu_ref[...])).astype(BF16)
    f = _dot(hid, wd_ref[...])
    o_ref[...] = _ple_ln2(x, xb, f, p_ref, pwg_ref, pwp_ref, g_ref, b_ref)


def _ffn_layer(x2, p2, wg, wu, wd, pwg, pwp, g, b, tm=256):
    t, dm = x2.shape
    row = lambda w: pl.BlockSpec((tm, w), lambda i: (i, 0))
    return pl.pallas_call(
        _ffn_kernel,
        grid=(t // tm,),
        in_specs=[row(dm), row(PLE_DIM), _full(wg.shape), _full(wu.shape), _full(wd.shape),
                  _full(pwg.shape), _full(pwp.shape), _full(g.shape), _full(b.shape)],
        out_specs=row(dm),
        out_shape=jax.ShapeDtypeStruct((t, dm), F32),
        compiler_params=_cparams(("parallel",)),
        name="ffn_ple_ln2",
    )(x2, p2, wg, wu, wd, pwg, pwp, g, b)


def _router_kernel(x_ref, w_ref, b_ref, comb_ref):
    xh, xl = _split2(x_ref[...])
    logits = _dot(xh, w_ref[0]) + _dot(xl, w_ref[0]) + _dot(xh, w_ref[1]) + b_ref[...]
    lane = lax.broadcasted_iota(jnp.int32, logits.shape, 1)
    neg = -jnp.inf
    logits = jnp.where(lane < N_EXPERTS, logits, neg)
    m1 = jnp.max(logits, axis=-1, keepdims=True)
    i1 = jnp.min(jnp.where(logits == m1, lane, LANES), axis=-1, keepdims=True)
    rest = jnp.where(lane == i1, neg, logits)
    m2 = jnp.max(rest, axis=-1, keepdims=True)
    i2 = jnp.min(jnp.where(rest == m2, lane, LANES), axis=-1, keepdims=True)
    e2 = jnp.exp(m2 - m1)
    p1 = 1.0 / (1.0 + e2)
    p2 = e2 / (1.0 + e2)
    comb_ref[...] = jnp.where(lane == i1, p1, 0.0) + jnp.where(lane == i2, p2, 0.0)


def _router(x2, w, b, tm=512):
    t, dm = x2.shape
    return pl.pallas_call(
        _router_kernel,
        grid=(t // tm,),
        in_specs=[pl.BlockSpec((tm, dm), lambda i: (i, 0)), _full(w.shape), _full(b.shape)],
        out_specs=pl.BlockSpec((tm, LANES), lambda i: (i, 0)),
        out_shape=jax.ShapeDtypeStruct((t, LANES), F32),
        compiler_params=_cparams(("parallel",)),
        name="moe_router",
    )(x2, w, b)


def _moe_dense_kernel(x_ref, comb_ref, wg_ref, wu_ref, wd_ref, o_ref):
    e = pl.program_id(1)
    j = pl.program_id(2)

    @pl.when((e == 0) & (j == 0))
    def _():
        o_ref[...] = jnp.zeros_like(o_ref)

    xb = x_ref[...].astype(BF16)
    hid = (_silu(_dot(xb, wg_ref[0])) * _dot(xb, wu_ref[0])).astype(BF16)
    lane = lax.broadcasted_iota(jnp.int32, comb_ref.shape, 1)
    ce = jnp.sum(jnp.where(lane == e, comb_ref[...], 0.0), axis=-1, keepdims=True)
    o_ref[...] += ce * _dot(hid, wd_ref[0])


def _moe_dense(x2, comb, wg, wu, wd, tm=1024, tf=512):
    t, dm = x2.shape
    ne, _, dff = wg.shape
    return pl.pallas_call(
        _moe_dense_kernel,
        grid=(t // tm, ne, dff // tf),
        in_specs=[pl.BlockSpec((tm, dm), lambda i, e, j: (i, 0)),
                  pl.BlockSpec((tm, LANES), lambda i, e, j: (i, 0)),
                  pl.BlockSpec((1, dm, tf), lambda i, e, j: (e, 0, j)),
                  pl.BlockSpec((1, dm, tf), lambda i, e, j: (e, 0, j)),
                  pl.BlockSpec((1, tf, dm), lambda i, e, j: (e, j, 0))],
        out_specs=pl.BlockSpec((tm, dm), lambda i, e, j: (i, 0)),
        out_shape=jax.ShapeDtypeStruct((t, dm), F32),
        compiler_params=_cparams(("parallel", "arbitrary", "arbitrary")),
        name="moe_dense",
    )(x2, comb, wg, wu, wd)


def _ple_kernel(x_ref, f_ref, p_ref, pwg_ref, pwp_ref, g_ref, b_ref, o_ref):
    x = x_ref[...]
    o_ref[...] = _ple_ln2(x, x.astype(BF16), f_ref[...], p_ref, pwg_ref, pwp_ref, g_ref, b_ref)


def _ple_layer(x2, f, p2, pwg, pwp, g, b, tm=512):
    t, dm = x2.shape
    row = lambda w: pl.BlockSpec((tm, w), lambda i: (i, 0))
    return pl.pallas_call(
        _ple_kernel,
        grid=(t // tm,),
        in_specs=[row(dm), row(dm), row(PLE_DIM), _full(pwg.shape), _full(pwp.shape),
                  _full(g.shape), _full(b.shape)],
        out_specs=row(dm),
        out_shape=jax.ShapeDtypeStruct((t, dm), F32),
        compiler_params=_cparams(("parallel",)),
        name="ple_ln2",
    )(x2, f, p2, pwg, pwp, g, b)


def _row(v):
    return v.reshape(1, -1).astype(F32)


def _pad_lanes(w):
    return jnp.pad(w, ((0, 0), (0, LANES - w.shape[1])))


def kernel(x, p, w_in, b_in, gla_w_a2, gla_b_a2, gla_norm_g, s5_a_re, s5_a_im, s5_log_dt, s5_b_re, s5_b_im,
           s5_c_re, s5_c_im, s5_d, s5_w_glu, s5_b_glu, ml_conv_w, ml_conv_b, ml_norm_g, w_up, w_o, ln1_g, ln1_b,
           ffn_wg, ffn_wu, ffn_wd, moe_router, moe_router_b, moe_wg, moe_wu, moe_wd, ple_w_gate, ple_w_proj,
           ln2_g, ln2_b):
    bsz, s, dm = x.shape
    t = bsz * s
    hi = lax.Precision.HIGHEST
    o = IN_OFF
    tri_np = _chunk_tri(SEQ_BLOCK, CHUNK)
    tri = jnp.asarray(tri_np, BF16)
    trit = jnp.asarray(tri_np.T, BF16)
    s5_e, s5_tz, s5_f, s5_a = _s5_tables(s5_a_re, s5_a_im, s5_log_dt, s5_b_re, s5_b_im, s5_c_re, s5_c_im)

    for i in range(DEPTH):
        w, b = w_in[i], b_in[i]
        sl = lambda k: (w[:, o[k]:o[k + 1]], b[o[k]:o[k + 1]])
        (wq, bq), (wk, bk), (wv, bv), (wa, ba), (wg_, bg_) = sl(0), sl(1), sl(2), sl(3), sl(4)
        wz = jnp.dot(wa, gla_w_a2[i], precision=hi)
        bz = jnp.dot(ba, gla_w_a2[i], precision=hi) + gla_b_a2[i]
        w_gla = jnp.concatenate([wq, wk, wz, wv, wg_], axis=1).astype(BF16)
        b_gla = _row(jnp.concatenate([bq, bk, bz, bv, bg_]))
        y_gla = _gla_mixer(x, w_gla, b_gla, _row(gla_norm_g[i]), tri)

        (wu_, bu_) = sl(5)
        x2 = x.reshape(t, dm)
        u, y5 = _s5_mixer_raw(x2, bsz, s, wu_.astype(BF16), _row(bu_), s5_e[i], s5_tz[i], s5_f[i], s5_a[i])

        (wmq, bmq), (wmk, bmk), (wmv, bmv), (wmi, bmi), (wmf, bmf), (wmo, bmo) = \
            sl(6), sl(7), sl(8), sl(9), sl(10), sl(11)
        w_ml = jnp.concatenate([wmq, wmk, wmv, wmo], axis=1).astype(BF16)
        b_ml = _row(jnp.concatenate([bmq, bmk, bmv, bmo]))
        w_if = jnp.concatenate([wmi, wmf], axis=1)
        b_if = jnp.concatenate([bmi, bmf])
        if_h = w_if.astype(BF16)
        if_l = (w_if - if_h.astype(F32)).astype(BF16)
        wgc = jnp.stack([_pad_lanes(if_h), _pad_lanes(if_l)])
        wgr = jnp.stack([if_h.T, if_l.T])
        y_ml = _ml_mixer(x, w_ml, b_ml, wgc, _pad_lanes(_row(b_if)), wgr, b_if.reshape(-1, 1).astype(F32),
                         ml_conv_w[i].astype(F32), _row(ml_conv_b[i]), _row(ml_norm_g[i]), tri, trit)

        (wgt, bgt) = sl(12)
        x1 = _merge(x2, y_gla.reshape(t, BRANCH_W), y5, u, y_ml.reshape(t, BRANCH_W),
                    wgt.astype(BF16), _row(bgt), s5_w_glu[i].astype(BF16), _row(s5_b_glu[i]), _row(s5_d[i]),
                    w_up[i].astype(BF16), w_o[i].astype(BF16), _row(ln1_g[i]), _row(ln1_b[i]))

        p2 = p[i].reshape(t, PLE_DIM)
        pwg = ple_w_gate[i].astype(BF16)
        pwp = ple_w_proj[i].astype(BF16)
        j = i // 2
        if i % 2 == 0:
            x2n = _ffn_layer(x1, p2, ffn_wg[j].astype(BF16), ffn_wu[j].astype(BF16), ffn_wd[j].astype(BF16),
                             pwg, pwp, _row(ln2_g[i]), _row(ln2_b[i]))
        else:
            wr = moe_router[j]
            wr_h = wr.astype(BF16)
            wr_l = (wr - wr_h.astype(F32)).astype(BF16)
            comb = _router(x1, jnp.stack([_pad_lanes(wr_h), _pad_lanes(wr_l)]), _pad_lanes(_row(moe_router_b[j])))
            f = _moe_dense(x1, comb, moe_wg[j].astype(BF16), moe_wu[j].astype(BF16), moe_wd[j].astype(BF16))
            x2n = _ple_layer(x1, f, p2, pwg, pwp, _row(ln2_g[i]), _row(ln2_b[i]))
        x = x2n.reshape(bsz, s, dm)
    return x
```

```python
import functools
import math

import numpy as np
import jax
import jax.numpy as jnp
from jax import lax
from jax.experimental import pallas as pl
from jax.experimental.pallas import tpu as pltpu

F32 = jnp.float32
BF16 = jnp.bfloat16

D_MODEL = 1024
DEPTH = 2
N_BRANCH = 3
BRANCH_W = 512
HEADS = 4
DK = 64
DV = BRANCH_W // HEADS
GLA_RANK = 16
GLA_TAU = 16.0
CHUNK = 64
S5_GROUP = 16
S5_GROUPS = BRANCH_W // S5_GROUP
S5_STATE = 64
S5_CHUNK = 16
ML_CONV = 4
D_FF = 2816
N_EXPERTS = 8
D_FF_EXPERT = 3584
PLE_DIM = 256
DN_ALPHA = (2.0 * DEPTH) ** 0.25
LN_EPS = 1e-5

IN_WIDTHS = (
    HEADS * DK, HEADS * DK, BRANCH_W, GLA_RANK, BRANCH_W,
    BRANCH_W,
    HEADS * DK, HEADS * DK, BRANCH_W, HEADS, HEADS, BRANCH_W,
    N_BRANCH * D_MODEL,
)
IN_OFF = tuple(int(o) for o in np.concatenate([[0], np.cumsum(IN_WIDTHS)]))

LANES = 128
SEQ_BLOCK = 256
PAIR_W = 2 * DK
VMEM_LIMIT = 56 * 1024 * 1024


def _cparams(sem):
    return pltpu.CompilerParams(dimension_semantics=sem, vmem_limit_bytes=VMEM_LIMIT)


def _dot(a, b):
    return jnp.dot(a, b, preferred_element_type=F32)


def _dot_nt(a, b):
    return lax.dot_general(a, b, (((1,), (1,)), ((), ())), preferred_element_type=F32)


def _dot_tn(a, b):
    return lax.dot_general(a, b, (((0,), (0,)), ((), ())), preferred_element_type=F32)


def _split3(a):
    hi = a.astype(BF16)
    r = a - hi.astype(F32)
    mid = r.astype(BF16)
    lo = (r - mid.astype(F32)).astype(BF16)
    return hi, mid, lo


def _split2(a):
    hi = a.astype(BF16)
    lo = (a - hi.astype(F32)).astype(BF16)
    return hi, lo


def _log_sigmoid(x):
    return jnp.minimum(x, 0.0) - jnp.log(1.0 + jnp.exp(-jnp.abs(x)))


def _sigmoid(x):
    return 1.0 / (1.0 + jnp.exp(-x))


def _silu(x):
    return x * _sigmoid(x)


def _gelu_tanh(x):
    return 0.5 * x * (1.0 + jnp.tanh(math.sqrt(2.0 / math.pi) * (x + 0.044715 * (x * x * x))))


def _layer_norm(v, g, b):
    mu = jnp.mean(v, axis=-1, keepdims=True)
    c = v - mu
    var = jnp.mean(c * c, axis=-1, keepdims=True)
    return c * lax.rsqrt(var + LN_EPS) * g + b


def _head_norm(o):
    mu = jnp.mean(o, axis=-1, keepdims=True)
    c = o - mu
    var = jnp.mean(c * c, axis=-1, keepdims=True)
    return c * lax.rsqrt(var + LN_EPS)


def _chunk_tri(n, chunk):
    i = np.arange(n)
    return ((i[:, None] >= i[None, :]) & (i[:, None] // chunk == i[None, :] // chunk)).astype(np.float32)


def _full(shape):
    nd = len(shape)
    return pl.BlockSpec(shape, lambda *_: (0,) * nd)


def _gla_kernel(x_ref, w_ref, b_ref, ng_ref, tri_ref, y_ref, st_ref, o_ref):
    @pl.when(pl.program_id(1) == 0)
    def _():
        st_ref[...] = jnp.zeros_like(st_ref)

    hk = HEADS * DK
    xb = x_ref[0].astype(BF16)
    h = _dot(xb, w_ref[...]) + b_ref[...]
    q = h[:, 0:hk]
    k = h[:, hk:2 * hk]
    z = h[:, 2 * hk:3 * hk]
    v = h[:, 3 * hk:3 * hk + BRANCH_W]
    g = h[:, 3 * hk + BRANCH_W:3 * hk + 2 * BRANCH_W]

    la = _log_sigmoid(z) * (1.0 / GLA_TAU)
    tri = tri_ref[...]
    la_h, la_m, la_l = _split3(la)
    cum = _dot(tri, la_h) + _dot(tri, la_m) + _dot(tri, la_l)
    qd = q * (DK ** -0.5) * jnp.exp(cum)
    ki = k * jnp.exp(-cum)
    vb = v.astype(BF16)

    lane = lax.broadcasted_iota(jnp.int32, (1, PAIR_W), 1)
    row_i = lax.broadcasted_iota(jnp.int32, (CHUNK, CHUNK), 0)
    col_i = lax.broadcasted_iota(jnp.int32, (CHUNK, CHUNK), 1)
    causal = row_i >= col_i
    bd_r = lax.broadcasted_iota(jnp.int32, (2 * DV, PAIR_W), 0)
    bd_c = lax.broadcasted_iota(jnp.int32, (2 * DV, PAIR_W), 1)
    blockdiag = (bd_r >= DV) == (bd_c >= DK)

    n_chunks = x_ref.shape[1] // CHUNK
    for p in range(HEADS // 2):
        st = st_ref[p]
        lsl = slice(p * PAIR_W, (p + 1) * PAIR_W)
        for c in range(n_chunks):
            rsl = slice(c * CHUNK, (c + 1) * CHUNK)
            qd_c = qd[rsl, lsl]
            ki_c = ki[rsl, lsl].astype(BF16)
            cum_c = cum[rsl, lsl]
            last = cum_c[CHUNK - 1:CHUNK, :]
            kt = (k[rsl, lsl] * jnp.exp(last - cum_c)).astype(BF16)
            inter = _dot_nt(qd_c.astype(BF16), st.astype(BF16))
            for hh in range(2):
                head = 2 * p + hh
                in_head = (lane >= hh * DK) & (lane < (hh + 1) * DK)
                qm = jnp.where(in_head, qd_c, 0.0).astype(BF16)
                att = jnp.where(causal, _dot_nt(qm, ki_c), 0.0)
                o_h = _dot(att.astype(BF16), vb[rsl, head * DV:(head + 1) * DV])
                o_ref[rsl, head * DV:(head + 1) * DV] = o_h + inter[:, hh * DV:(hh + 1) * DV]
            upd = _dot_tn(vb[rsl, p * 2 * DV:(p + 1) * 2 * DV], kt)
            st = st * jnp.exp(last) + jnp.where(blockdiag, upd, 0.0)
        st_ref[p] = st

    ng = ng_ref[...]
    for head in range(HEADS):
        hsl = slice(head * DV, (head + 1) * DV)
        y = _head_norm(o_ref[:, hsl]) * ng[:, hsl] * _silu(g[:, hsl])
        y_ref[0, :, hsl] = y.astype(y_ref.dtype)


def _gla_mixer(x, w, b, ng, tri):
    bsz, s, d = x.shape
    wcols = w.shape[1]
    return pl.pallas_call(
        _gla_kernel,
        grid=(bsz, s // SEQ_BLOCK),
        in_specs=[
            pl.BlockSpec((1, SEQ_BLOCK, d), lambda i, j: (i, j, 0)),
            _full((d, wcols)), _full((1, wcols)), _full((1, BRANCH_W)),
            _full((SEQ_BLOCK, SEQ_BLOCK)),
        ],
        out_specs=pl.BlockSpec((1, SEQ_BLOCK, BRANCH_W), lambda i, j: (i, j, 0)),
        out_shape=jax.ShapeDtypeStruct((bsz, s, BRANCH_W), BF16),
        scratch_shapes=[pltpu.VMEM((HEADS // 2, 2 * DV, PAIR_W), F32),
                        pltpu.VMEM((SEQ_BLOCK, BRANCH_W), F32)],
        compiler_params=_cparams(("arbitrary", "arbitrary")),
        name="gla_mixer",
    )(x, w, b, ng, tri)


ML_ST_ROWS = 2 * DV + LANES


def _ml_kernel(x_ref, w_ref, b_ref, wgc_ref, bgc_ref, wgr_ref, bgr_ref, cw_ref, cb_ref, ng_ref,
               tri_ref, trit_ref, y_ref, ct_ref, m_ref, carry_ref, o_ref):
    @pl.when(pl.program_id(1) == 0)
    def _():
        ct_ref[...] = jnp.zeros_like(ct_ref)
        m_ref[...] = jnp.zeros_like(m_ref)
        carry_ref[...] = jnp.zeros_like(carry_ref)

    hk = HEADS * DK
    lb = x_ref.shape[1]
    x32 = x_ref[0]
    xh, xl = _split2(x32)
    h = _dot(xh, w_ref[...]) + b_ref[...]
    qk = h[:, 0:2 * hk]
    v = h[:, 2 * hk:2 * hk + BRANCH_W]
    o_pre = h[:, 2 * hk + BRANCH_W:2 * hk + 2 * BRANCH_W]

    ext = jnp.concatenate([carry_ref[...], qk], axis=0)
    cw = cw_ref[...]
    acc = cb_ref[...] + ext[8 - (ML_CONV - 1):8 - (ML_CONV - 1) + lb] * cw[0:1]
    for j in range(1, ML_CONV):
        off = 8 - (ML_CONV - 1) + j
        acc = acc + ext[off:off + lb] * cw[j:j + 1]
    carry_ref[...] = qk[lb - 8:lb]
    qkc = _silu(acc)
    qf = qkc[:, 0:hk]
    kf = qkc[:, hk:2 * hk] * (DK ** -0.5)
    vb = v.astype(BF16)

    gc = (_dot(xh, wgc_ref[0]) + _dot(xl, wgc_ref[0]) + _dot(xh, wgc_ref[1])) + bgc_ref[...]
    gr = (_dot_nt(wgr_ref[0], xh) + _dot_nt(wgr_ref[0], xl) + _dot_nt(wgr_ref[1], xh)) + bgr_ref[...]
    lf_c = _log_sigmoid(gc)
    lf_r = _log_sigmoid(gr)
    tri = tri_ref[...]
    trit = trit_ref[...]
    c_h, c_m, c_l = _split3(lf_c)
    bc = _dot(tri, c_h) + _dot(tri, c_m) + _dot(tri, c_l)
    r_h, r_m, r_l = _split3(lf_r)
    br = _dot(r_h, trit) + _dot(r_m, trit) + _dot(r_l, trit)

    lane = lax.broadcasted_iota(jnp.int32, (1, PAIR_W), 1)
    row_i = lax.broadcasted_iota(jnp.int32, (CHUNK, CHUNK), 0)
    col_i = lax.broadcasted_iota(jnp.int32, (CHUNK, CHUNK), 1)
    causal = row_i >= col_i
    sr = lax.broadcasted_iota(jnp.int32, (ML_ST_ROWS, PAIR_W), 0)
    sc_ = lax.broadcasted_iota(jnp.int32, (ML_ST_ROWS, PAIR_W), 1)
    first = sc_ < DK
    rows_h0 = (sr < DV) | (sr == 2 * DV)
    rows_h1 = ((sr >= DV) & (sr < 2 * DV)) | (sr == 2 * DV + 1)
    st_mask = (rows_h0 & first) | (rows_h1 & ~first)
    ones_blk = jnp.ones((CHUNK, LANES), BF16)

    n_chunks = lb // CHUNK
    for p in range(HEADS // 2):
        ct = ct_ref[p]
        lsl = slice(p * PAIR_W, (p + 1) * PAIR_W)
        m_pair = [m_ref[2 * p + hh][0:1, 0:1] for hh in range(2)]
        for c in range(n_chunks):
            rsl = slice(c * CHUNK, (c + 1) * CHUNK)
            q_c = qf[rsl, lsl]
            k_c = kf[rsl, lsl]
            k_cb = k_c.astype(BF16)
            inter_mm = _dot_nt(q_c.astype(BF16), ct.astype(BF16))
            wt_cols, decays = [], []
            for hh in range(2):
                head = 2 * p + hh
                m_st = m_pair[hh]
                b_col = bc[rsl, HEADS + head:HEADS + head + 1]
                i_col = gc[rsl, head:head + 1]
                b_row = br[HEADS + head:HEADS + head + 1, rsl]
                i_row = gr[head:head + 1, rsl]
                dmat = jnp.where(causal, b_col - b_row + i_row, -jnp.inf)
                inter = b_col + m_st
                m_row = jnp.maximum(inter, jnp.max(dmat, axis=-1, keepdims=True))
                wts = jnp.exp(dmat - m_row)
                in_head = (lane >= hh * DK) & (lane < (hh + 1) * DK)
                qm = jnp.where(in_head, q_c, 0.0).astype(BF16)
                sc = _dot_nt(qm, k_cb) * wts
                w_inter = jnp.exp(inter - m_row)
                num = _dot(sc.astype(BF16), vb[rsl, head * DV:(head + 1) * DV]) \
                    + w_inter * inter_mm[:, hh * DV:(hh + 1) * DV]
                den = jnp.sum(sc, axis=-1, keepdims=True) \
                    + w_inter * inter_mm[:, 2 * DV + hh:2 * DV + hh + 1]
                o_ref[rsl, head * DV:(head + 1) * DV] = num / jnp.maximum(jnp.abs(den), jnp.exp(-m_row))
                g_tot = b_col[CHUNK - 1:CHUNK, :]
                tail = g_tot - b_col + i_col
                m_new = jnp.maximum(g_tot + m_st, jnp.max(tail, axis=0, keepdims=True))
                wt_cols.append(jnp.exp(tail - m_new))
                decays.append(jnp.exp(g_tot + m_st - m_new))
                m_pair[hh] = m_new
            wk = (k_c * jnp.where(lane < DK, wt_cols[0], wt_cols[1])).astype(BF16)
            vp = jnp.concatenate([vb[rsl, p * 2 * DV:(p + 1) * 2 * DV], ones_blk], axis=1)
            upd = _dot_tn(vp, wk)
            ct = ct * jnp.where(lane < DK, decays[0], decays[1]) + jnp.where(st_mask, upd, 0.0)
        ct_ref[p] = ct
        for hh in range(2):
            m_ref[2 * p + hh] = jnp.broadcast_to(m_pair[hh], m_ref.shape[1:])

    ng = ng_ref[...]
    for head in range(HEADS):
        hsl = slice(head * DV, (head + 1) * DV)
        y = _head_norm(o_ref[:, hsl]) * ng[:, hsl] * _sigmoid(o_pre[:, hsl])
        y_ref[0, :, hsl] = y.astype(y_ref.dtype)


def _ml_mixer(x, w, b, wgc, bgc, wgr, bgr, cw, cb, ng, tri, trit):
    bsz, s, d = x.shape
    wcols = w.shape[1]
    return pl.pallas_call(
        _ml_kernel,
        grid=(bsz, s // SEQ_BLOCK),
        in_specs=[
            pl.BlockSpec((1, SEQ_BLOCK, d), lambda i, j: (i, j, 0)),
            _full((d, wcols)), _full((1, wcols)),
            _full((2, d, LANES)), _full((1, LANES)),
            _full((2, 8, d)), _full((8, 1)),
            _full((ML_CONV, 2 * HEADS * DK)), _full((1, 2 * HEADS * DK)),
            _full((1, BRANCH_W)),
            _full((SEQ_BLOCK, SEQ_BLOCK)), _full((SEQ_BLOCK, SEQ_BLOCK)),
        ],
        out_specs=pl.BlockSpec((1, SEQ_BLOCK, BRANCH_W), lambda i, j: (i, j, 0)),
        out_shape=jax.ShapeDtypeStruct((bsz, s, BRANCH_W), BF16),
        scratch_shapes=[pltpu.VMEM((HEADS // 2, ML_ST_ROWS, PAIR_W), F32),
                        pltpu.VMEM((HEADS, 8, LANES), F32),
                        pltpu.VMEM((8, 2 * HEADS * DK), F32),
                        pltpu.VMEM((SEQ_BLOCK, BRANCH_W), F32)],
        compiler_params=_cparams(("arbitrary", "arbitrary")),
        name="mlstm_mixer",
    )(x, w, b, wgc, bgc, wgr, bgr, cw, cb, ng, tri, trit)


S5_HALF = 256
S5_HALF_STATE = (S5_HALF // S5_GROUP) * S5_STATE


def _s5_kernel(bsz, nc, x_ref, wu_ref, bu_ref, ec_ref, ftc_ref, krev_ref, a_ref, y_ref,
               in_ref, xs_ref, xsb_ref, kexp_ref):
    s = pl.program_id(1)
    lc = S5_CHUNK
    rows = bsz * nc
    sw = S5_HALF_STATE

    def expand_state(blk):
        tile = jnp.concatenate([blk[:, 0:LANES]] * (sw // LANES) + [blk[:, LANES:2 * LANES]] * (sw // LANES),
                               axis=1)
        r = lax.broadcasted_iota(jnp.int32, tile.shape, 0) // S5_GROUP
        c = (lax.broadcasted_iota(jnp.int32, tile.shape, 1) % sw) // S5_STATE
        return jnp.where(r == c, tile, 0.0).astype(BF16)

    @pl.when(s == 0)
    def _():
        kr = krev_ref[0]
        r = (lax.broadcasted_iota(jnp.int32, kr.shape, 0) % S5_HALF) // S5_GROUP
        c = lax.broadcasted_iota(jnp.int32, kr.shape, 1) // S5_GROUP
        kexp_ref[...] = jnp.where(r == c, kr, 0.0).astype(BF16)
        xs_ref[...] = jnp.zeros_like(xs_ref)

    @pl.when(s < lc)
    def _():
        xl = x_ref[...].reshape(rows, x_ref.shape[-1]).astype(BF16)
        u = (_dot(xl, wu_ref[...]) + bu_ref[...]).astype(BF16)
        in_ref[s] = u
        et = expand_state(ec_ref[0])
        for jc in range(2 * sw // S5_HALF):
            part = _dot(u, et[:, jc * S5_HALF:(jc + 1) * S5_HALF])
            for jj in range(S5_HALF // LANES):
                xs_ref[jc * (S5_HALF // LANES) + jj] += part[:, jj * LANES:(jj + 1) * LANES]

    @pl.when(s == lc - 1)
    def _():
        ar = a_ref[0, 0:1, :]
        ai = a_ref[0, 1:2, :]
        nt = sw // LANES

        def body(c, carry):
            sr, si = carry
            idx = pl.ds(c, bsz, stride=nc)
            xr = jnp.concatenate([xs_ref[j, idx, :] for j in range(nt)], axis=1)
            xi = jnp.concatenate([xs_ref[nt + j, idx, :] for j in range(nt)], axis=1)
            for j in range(nt):
                xs_ref[j, idx, :] = sr[:, j * LANES:(j + 1) * LANES]
                xs_ref[nt + j, idx, :] = si[:, j * LANES:(j + 1) * LANES]
            return ar * sr - ai * si + xr, ar * si + ai * sr + xi

        zero = jnp.zeros((bsz, sw), F32)
        lax.fori_loop(0, nc, body, (zero, zero))
        for j in range(2 * nt):
            xsb_ref[:, j * LANES:(j + 1) * LANES] = xs_ref[j].astype(BF16)

    for lo in range(lc):
        @pl.when(s == lc + lo)
        def _(lo=lo):
            acc = _dot_nt(xsb_ref[...], expand_state(ftc_ref[0]))
            for l in range(lo + 1):
                j = lc - 1 - lo + l
                acc = acc + _dot(in_ref[l], kexp_ref[j * S5_HALF:(j + 1) * S5_HALF, :])
            y_ref[...] = acc.reshape(y_ref.shape)


def _s5_mixer(x, w_u, b_u, ec, ftc, krev, a):
    bsz, s, d = x.shape
    lc = S5_CHUNK
    nc = s // lc
    x3 = x.reshape(bsz, nc, lc * d)
    nh = BRANCH_W // S5_HALF
    y = pl.pallas_call(
        functools.partial(_s5_kernel, bsz, nc),
        grid=(nh, 2 * lc),
        in_specs=[
            pl.BlockSpec((bsz, nc, d), lambda h, t: (0, 0, jnp.minimum(t, lc - 1))),
            pl.BlockSpec((d, S5_HALF), lambda h, t: (0, h)),
            pl.BlockSpec((1, S5_HALF), lambda h, t: (0, h)),
            pl.BlockSpec((1, S5_HALF, 2 * LANES), lambda h, t: (jnp.minimum(t, lc - 1), h, 0)),
            pl.BlockSpec((1, S5_HALF, 2 * LANES), lambda h, t: (jnp.maximum(t - lc, 0), h, 0)),
            pl.BlockSpec((1, lc * S5_HALF, S5_HALF), lambda h, t: (h, 0, 0)),
            pl.BlockSpec((1, 2, S5_HALF_STATE), lambda h, t: (h, 0, 0)),
        ],
        out_specs=pl.BlockSpec((bsz, nc, S5_HALF), lambda h, t: (0, 0, jnp.maximum(t - lc, 0) * nh + h)),
        out_shape=jax.ShapeDtypeStruct((bsz, nc, lc * BRANCH_W), F32),
        scratch_shapes=[pltpu.VMEM((lc, bsz * nc, S5_HALF), BF16),
                        pltpu.VMEM((2 * S5_HALF_STATE // LANES, bsz * nc, LANES), F32),
                        pltpu.VMEM((bsz * nc, 2 * S5_HALF_STATE), BF16),
                        pltpu.VMEM((lc * S5_HALF, S5_HALF), BF16)],
        compiler_params=_cparams(("arbitrary", "arbitrary")),
        name="s5_mixer",
    )(x3, w_u, b_u, ec, ftc, krev, a)
    return y.reshape(bsz * s, BRANCH_W)


def _s5_tables(a_re, a_im, log_dt, b_re, b_im, c_re, c_im, d_skip):
    nl = a_re.shape[0]
    g, p_, n, lc = S5_GROUPS, S5_STATE, S5_GROUP, S5_CHUNK
    hi = lax.Precision.HIGHEST
    a_re, a_im, b_re, b_im, c_re, c_im = (v.astype(F32) for v in (a_re, a_im, b_re, b_im, c_re, c_im))
    dt = jnp.exp(log_dt.astype(F32))[..., None]

    def powers(steps):
        mag = jnp.exp((a_re * dt)[..., None] * steps)
        ang = (a_im * dt)[..., None] * steps
        return mag * jnp.cos(ang), mag * jnp.sin(ang)

    pwr, pwi = powers(jnp.arange(lc + 1, dtype=F32))
    pr, pi = powers(jnp.asarray(np.arange(lc - 1, -1, -1), F32))
    nr, ni = pwr[..., 1] - 1.0, pwi[..., 1]
    den = a_re * a_re + a_im * a_im
    qr, qi = (nr * a_re + ni * a_im) / den, (ni * a_re - nr * a_im) / den
    bbr = qr[..., None] * b_re - qi[..., None] * b_im
    bbi = qr[..., None] * b_im + qi[..., None] * b_re
    cpr = c_re[..., None] * pr[:, :, None] - c_im[..., None] * pi[:, :, None]
    cpi = c_re[..., None] * pi[:, :, None] + c_im[..., None] * pr[:, :, None]
    kern = (jnp.einsum('zgnpt,zgpm->ztgmn', cpr, bbr, precision=hi)
            - jnp.einsum('zgnpt,zgpm->ztgmn', cpi, bbi, precision=hi))
    is_tau0 = jnp.asarray(np.arange(lc) == lc - 1, F32).reshape(1, lc, 1, 1, 1)
    kern = kern + is_tau0 * d_skip.astype(F32).reshape(nl, 1, g, 1, n) * jnp.eye(n, dtype=F32)
    spread = jnp.asarray(np.tile(np.eye(n, dtype=np.float32), (1, S5_HALF // n)))
    kw = jnp.einsum('ztgmn,nk->ztgmk', kern, spread, precision=hi)
    nh = BRANCH_W // S5_HALF
    krev = kw.reshape(nl, lc, nh, S5_HALF, S5_HALF).transpose(0, 2, 1, 3, 4)
    krev = krev.reshape(nl, nh, lc * S5_HALF, S5_HALF)
    er =pr[..., None] * bbr[:, :, :, None, :] - pi[..., None] * bbi[:, :, :, None, :]
    ei = pr[..., None] * bbi[:, :, :, None, :] + pi[..., None] * bbr[:, :, :, None, :]
    er, ei = er.transpose(0, 3, 1, 4, 2), ei.transpose(0, 3, 1, 4, 2)
    ec = jnp.concatenate([er, er, ei, ei], axis=-1).reshape(nl, lc, g * n, 4 * p_)
    pfr = pwr[..., 1:].transpose(0, 3, 1, 2)[:, :, :, None, :]
    pfi = pwi[..., 1:].transpose(0, 3, 1, 2)[:, :, :, None, :]
    fr = c_re[:, None] * pfr - c_im[:, None] * pfi
    fi = -(c_re[:, None] * pfi + c_im[:, None] * pfr)
    ftc = jnp.concatenate([fr, fr, fi, fi], axis=-1).reshape(nl, lc, g * n, 4 * p_)
    a_tab = jnp.stack([pwr[..., lc].reshape(nl, nh, S5_HALF_STATE),
                       pwi[..., lc].reshape(nl, nh, S5_HALF_STATE)], axis=2)
    return ec, ftc, krev, a_tab


def _merge_kernel(x_ref, yg_ref, y5_ref, ym_ref, wgate_ref, bgate_ref, wglu_ref, bglu_ref,
                  wup_ref, wo_ref, g_ref, b_ref, o_ref):
    x = x_ref[...]
    xb = x.astype(BF16)
    y5 = _gelu_tanh(y5_ref[...])
    y5 = y5 * _sigmoid(_dot(y5.astype(BF16), wglu_ref[...]) + bglu_ref[...])
    ys = (yg_ref[...], y5.astype(BF16), ym_ref[...])
    acc = None
    for r in range(N_BRANCH):
        gate = _sigmoid(_dot(xb, wgate_ref[:, r * D_MODEL:(r + 1) * D_MODEL])
                        + bgate_ref[:, r * D_MODEL:(r + 1) * D_MODEL])
        term = gate * _dot(ys[r], wup_ref[r])
        acc = term if acc is None else acc + term
    mix = _dot(acc.astype(BF16), wo_ref[...])
    o_ref[...] = _layer_norm(DN_ALPHA * x + mix, g_ref[...], b_ref[...])


def _merge(x2, yg, y5, ym, wgate, bgate, wglu, bglu, wup, wo, g, b, tm=256):
    t, dm = x2.shape
    row = lambda w: pl.BlockSpec((tm, w), lambda i: (i, 0))
    return pl.pallas_call(
        _merge_kernel,
        grid=(t // tm,),
        in_specs=[row(dm), row(BRANCH_W), row(BRANCH_W), row(BRANCH_W),
                  _full(wgate.shape), _full(bgate.shape), _full(wglu.shape), _full(bglu.shape),
                  _full(wup.shape), _full(wo.shape), _full(g.shape), _full(b.shape)],
        out_specs=row(dm),
        out_shape=jax.ShapeDtypeStruct((t, dm), F32),
        compiler_params=_cparams(("parallel",)),
        name="merge_ln1",
    )(x2, yg, y5, ym, wgate, bgate, wglu, bglu, wup, wo, g, b)


def _ple_ln2(x, xb, f, p_ref, pwg_ref, pwp_ref, g_ref, b_ref):
    e = _sigmoid(_dot(xb, pwg_ref[...])) * _dot(p_ref[...].astype(BF16), pwp_ref[...])
    return _layer_norm(DN_ALPHA * x + f + e, g_ref[...], b_ref[...])


def _ffn_kernel(x_ref, p_ref, wg_ref, wu_ref, wd_ref, pwg_ref, pwp_ref, g_ref, b_ref, o_ref):
    x = x_ref[...]
    xb = x.astype(BF16)
    hid = (_silu(_dot(xb, wg_ref[...])) * _dot(xb, wu_ref[...])).astype(BF16)
    f = _dot(hid, wd_ref[...])
    o_ref[...] = _ple_ln2(x, xb, f, p_ref, pwg_ref, pwp_ref, g_ref, b_ref)


def _ffn_layer(x2, p2, wg, wu, wd, pwg, pwp, g, b, tm=256):
    t, dm = x2.shape
    row = lambda w: pl.BlockSpec((tm, w), lambda i: (i, 0))
    return pl.pallas_call(
        _ffn_kernel,
        grid=(t // tm,),
        in_specs=[row(dm), row(PLE_DIM), _full(wg.shape), _full(wu.shape), _full(wd.shape),
                  _full(pwg.shape), _full(pwp.shape), _full(g.shape), _full(b.shape)],
        out_specs=row(dm),
        out_shape=jax.ShapeDtypeStruct((t, dm), F32),
        compiler_params=_cparams(("parallel",)),
        name="ffn_ple_ln2",
    )(x2, p2, wg, wu, wd, pwg, pwp, g, b)


def _router_kernel(x_ref, w_ref, b_ref, comb_ref):
    xh, xl = _split2(x_ref[...])
    logits = _dot(xh, w_ref[0]) + _dot(xl, w_ref[0]) + _dot(xh, w_ref[1]) + b_ref[...]
    lane = lax.broadcasted_iota(jnp.int32, logits.shape, 1)
    neg = -jnp.inf
    logits = jnp.where(lane < N_EXPERTS, logits, neg)
    m1 = jnp.max(logits, axis=-1, keepdims=True)
    i1 = jnp.min(jnp.where(logits == m1, lane, LANES), axis=-1, keepdims=True)
    rest = jnp.where(lane == i1, neg, logits)
    m2 = jnp.max(rest, axis=-1, keepdims=True)
    i2 = jnp.min(jnp.where(rest == m2, lane, LANES), axis=-1, keepdims=True)
    e2 = jnp.exp(m2 - m1)
    p1 = 1.0 / (1.0 + e2)
    p2 = e2 / (1.0 + e2)
    comb_ref[...] = jnp.where(lane == i1, p1, 0.0) + jnp.where(lane == i2, p2, 0.0)


def _router(x2, w, b, tm=512):
    t, dm = x2.shape
    return pl.pallas_call(
        _router_kernel,
        grid=(t // tm,),
        in_specs=[pl.BlockSpec((tm, dm), lambda i: (i, 0)), _full(w.shape), _full(b.shape)],
        out_specs=pl.BlockSpec((tm, LANES), lambda i: (i, 0)),
        out_shape=jax.ShapeDtypeStruct((t, LANES), F32),
        compiler_params=_cparams(("parallel",)),
        name="moe_router",
    )(x2, w, b)


def _moe_dense_kernel(x_ref, comb_ref, wg_ref, wu_ref, wd_ref, o_ref):
    e = pl.program_id(1)
    j = pl.program_id(2)

    @pl.when((e == 0) & (j == 0))
    def _():
        o_ref[...] = jnp.zeros_like(o_ref)

    xb = x_ref[...].astype(BF16)
    hid = (_silu(_dot(xb, wg_ref[0])) * _dot(xb, wu_ref[0])).astype(BF16)
    lane = lax.broadcasted_iota(jnp.int32, comb_ref.shape, 1)
    ce = jnp.sum(jnp.where(lane == e, comb_ref[...], 0.0), axis=-1, keepdims=True)
    o_ref[...] += ce * _dot(hid, wd_ref[0])


def _moe_dense(x2, comb, wg, wu, wd, tm=1024, tf=512):
    t, dm = x2.shape
    ne, _, dff = wg.shape
    return pl.pallas_call(
        _moe_dense_kernel,
        grid=(t // tm, ne, dff // tf),
        in_specs=[pl.BlockSpec((tm, dm), lambda i, e, j: (i, 0)),
                  pl.BlockSpec((tm, LANES), lambda i, e, j: (i, 0)),
                  pl.BlockSpec((1, dm, tf), lambda i, e, j: (e, 0, j)),
                  pl.BlockSpec((1, dm, tf), lambda i, e, j: (e, 0, j)),
                  pl.BlockSpec((1, tf, dm), lambda i, e, j: (e, j, 0))],
        out_specs=pl.BlockSpec((tm, dm), lambda i, e, j: (i, 0)),
        out_shape=jax.ShapeDtypeStruct((t, dm), F32),
        compiler_params=_cparams(("parallel", "arbitrary", "arbitrary")),
        name="moe_dense",
    )(x2, comb, wg, wu, wd)


def _ple_kernel(x_ref, f_ref, p_ref, pwg_ref, pwp_ref, g_ref, b_ref, o_ref):
    x = x_ref[...]
    o_ref[...] = _ple_ln2(x, x.astype(BF16), f_ref[...], p_ref, pwg_ref, pwp_ref, g_ref, b_ref)


def _ple_layer(x2, f, p2, pwg, pwp, g, b, tm=512):
    t, dm = x2.shape
    row = lambda w: pl.BlockSpec((tm, w), lambda i: (i, 0))
    return pl.pallas_call(
        _ple_kernel,
        grid=(t // tm,),
        in_specs=[row(dm), row(dm), row(PLE_DIM), _full(pwg.shape), _full(pwp.shape),
                  _full(g.shape), _full(b.shape)],
        out_specs=row(dm),
        out_shape=jax.ShapeDtypeStruct((t, dm), F32),
        compiler_params=_cparams(("parallel",)),
        name="ple_ln2",
    )(x2, f, p2, pwg, pwp, g, b)


def _row(v):
    return v.reshape(1, -1).astype(F32)


def _pad_lanes(w):
    return jnp.pad(w, ((0, 0), (0, LANES - w.shape[1])))


def kernel(x, p, w_in, b_in, gla_w_a2, gla_b_a2, gla_norm_g, s5_a_re, s5_a_im, s5_log_dt, s5_b_re, s5_b_im,
           s5_c_re, s5_c_im, s5_d, s5_w_glu, s5_b_glu, ml_conv_w, ml_conv_b, ml_norm_g, w_up, w_o, ln1_g, ln1_b,
           ffn_wg, ffn_wu, ffn_wd, moe_router, moe_router_b, moe_wg, moe_wu, moe_wd, ple_w_gate, ple_w_proj,
           ln2_g, ln2_b):
    bsz, s, dm = x.shape
    t = bsz * s
    hi = lax.Precision.HIGHEST
    o = IN_OFF
    tri_np = _chunk_tri(SEQ_BLOCK, CHUNK)
    tri = jnp.asarray(tri_np, BF16)
    trit = jnp.asarray(tri_np.T, BF16)
    s5_ec, s5_ftc, s5_krev, s5_atab = _s5_tables(s5_a_re, s5_a_im, s5_log_dt, s5_b_re, s5_b_im, s5_c_re, s5_c_im,
                                                 s5_d)

    for i in range(DEPTH):
        w, b = w_in[i], b_in[i]
        sl = lambda k: (w[:, o[k]:o[k + 1]], b[o[k]:o[k + 1]])
        (wq, bq), (wk, bk), (wv, bv), (wa, ba), (wg_, bg_) = sl(0), sl(1), sl(2), sl(3), sl(4)
        wz = jnp.dot(wa, gla_w_a2[i], precision=hi)
        bz = jnp.dot(ba, gla_w_a2[i], precision=hi) + gla_b_a2[i]
        w_gla = jnp.concatenate([wq, wk, wz, wv, wg_], axis=1).astype(BF16)
        b_gla = _row(jnp.concatenate([bq, bk, bz, bv, bg_]))
        y_gla = _gla_mixer(x, w_gla, b_gla, _row(gla_norm_g[i]), tri)

        (wu_, bu_) = sl(5)
        x2 = x.reshape(t, dm)
        y5 = _s5_mixer(x, wu_.astype(BF16), _row(bu_), s5_ec[i], s5_ftc[i], s5_krev[i], s5_atab[i])

        (wmq, bmq), (wmk, bmk), (wmv, bmv), (wmi, bmi), (wmf, bmf), (wmo, bmo) = \
            sl(6), sl(7), sl(8), sl(9), sl(10), sl(11)
        w_ml = jnp.concatenate([wmq, wmk, wmv, wmo], axis=1).astype(BF16)
        b_ml = _row(jnp.concatenate([bmq, bmk, bmv, bmo]))
        w_if = jnp.concatenate([wmi, wmf], axis=1)
        b_if = jnp.concatenate([bmi, bmf])
        if_h = w_if.astype(BF16)
        if_l = (w_if - if_h.astype(F32)).astype(BF16)
        wgc = jnp.stack([_pad_lanes(if_h), _pad_lanes(if_l)])
        wgr = jnp.stack([if_h.T, if_l.T])
        y_ml = _ml_mixer(x, w_ml, b_ml, wgc, _pad_lanes(_row(b_if)), wgr, b_if.reshape(-1, 1).astype(F32),
                         ml_conv_w[i].astype(F32), _row(ml_conv_b[i]), _row(ml_norm_g[i]), tri, trit)

        (wgt, bgt) = sl(12)
        x1 = _merge(x2, y_gla.reshape(t, BRANCH_W), y5, y_ml.reshape(t, BRANCH_W),
                    wgt.astype(BF16), _row(bgt), s5_w_glu[i].astype(BF16), _row(s5_b_glu[i]),
                    w_up[i].astype(BF16), w_o[i].astype(BF16), _row(ln1_g[i]), _row(ln1_b[i]))

        p2 = p[i].reshape(t, PLE_DIM)
        pwg = ple_w_gate[i].astype(BF16)
        pwp = ple_w_proj[i].astype(BF16)
        j = i // 2
        if i % 2 == 0:
            x2n = _ffn_layer(x1, p2, ffn_wg[j].astype(BF16), ffn_wu[j].astype(BF16), ffn_wd[j].astype(BF16),
                             pwg, pwp, _row(ln2_g[i]), _row(ln2_b[i]))
        else:
            wr = moe_router[j]
            wr_h = wr.astype(BF16)
            wr_l = (wr - wr_h.astype(F32)).astype(BF16)
            comb = _router(x1, jnp.stack([_pad_lanes(wr_h), _pad_lanes(wr_l)]), _pad_lanes(_row(moe_router_b[j])))
            f = _moe_dense(x1, comb, moe_wg[j].astype(BF16), moe_wu[j].astype(BF16), moe_wd[j].astype(BF16))
            x2n = _ple_layer(x1, f, p2, pwg, pwp, _row(ln2_g[i]), _row(ln2_b[i]))
        x = x2n.reshape(bsz, s, dm)
    return x
```

```python
import functools
import math

import numpy as np
import jax
import jax.numpy as jnp
from jax import lax
from jax.experimental import pallas as pl
from jax.experimental.pallas import tpu as pltpu

F32 = jnp.float32
BF16 = jnp.bfloat16

D_MODEL = 1024
DEPTH = 2
N_BRANCH = 3
BRANCH_W = 512
HEADS = 4
DK = 64
DV = BRANCH_W // HEADS
GLA_RANK = 16
GLA_TAU = 16.0
CHUNK = 64
S5_GROUP = 16
S5_GROUPS = BRANCH_W // S5_GROUP
S5_STATE = 64
S5_CHUNK = 16
ML_CONV = 4
D_FF = 2816
N_EXPERTS = 8
D_FF_EXPERT = 3584
PLE_DIM = 256
DN_ALPHA = (2.0 * DEPTH) ** 0.25
LN_EPS = 1e-5

IN_WIDTHS = (
    HEADS * DK, HEADS * DK, BRANCH_W, GLA_RANK, BRANCH_W,
    BRANCH_W,
    HEADS * DK, HEADS * DK, BRANCH_W, HEADS, HEADS, BRANCH_W,
    N_BRANCH * D_MODEL,
)
IN_OFF = tuple(int(o) for o in np.concatenate([[0], np.cumsum(IN_WIDTHS)]))

LANES = 128
SEQ_BLOCK = 256
PAIR_W = 2 * DK
VMEM_LIMIT = 56 * 1024 * 1024


def _cparams(sem):
    return pltpu.CompilerParams(dimension_semantics=sem, vmem_limit_bytes=VMEM_LIMIT)


def _dot(a, b):
    return jnp.dot(a, b, preferred_element_type=F32)


def _dot_nt(a, b):
    return lax.dot_general(a, b, (((1,), (1,)), ((), ())), preferred_element_type=F32)


def _dot_tn(a, b):
    return lax.dot_general(a, b, (((0,), (0,)), ((), ())), preferred_element_type=F32)


def _split3(a):
    hi = a.astype(BF16)
    r = a - hi.astype(F32)
    mid = r.astype(BF16)
    lo = (r - mid.astype(F32)).astype(BF16)
    return hi, mid, lo


def _split2(a):
    hi = a.astype(BF16)
    lo = (a - hi.astype(F32)).astype(BF16)
    return hi, lo


def _log_sigmoid(x):
    return jnp.minimum(x, 0.0) - jnp.log(1.0 + jnp.exp(-jnp.abs(x)))


def _sigmoid(x):
    return 1.0 / (1.0 + jnp.exp(-x))


def _silu(x):
    return x * _sigmoid(x)


def _gelu_tanh(x):
    return 0.5 * x * (1.0 + jnp.tanh(math.sqrt(2.0 / math.pi) * (x + 0.044715 * (x * x * x))))


def _layer_norm(v, g, b):
    mu = jnp.mean(v, axis=-1, keepdims=True)
    c = v - mu
    var = jnp.mean(c * c, axis=-1, keepdims=True)
    return c * lax.rsqrt(var + LN_EPS) * g + b


def _head_norm(o):
    mu = jnp.mean(o, axis=-1, keepdims=True)
    c = o - mu
    var = jnp.mean(c * c, axis=-1, keepdims=True)
    return c * lax.rsqrt(var + LN_EPS)


def _chunk_tri(n, chunk):
    i = np.arange(n)
    return ((i[:, None] >= i[None, :]) & (i[:, None] // chunk == i[None, :] // chunk)).astype(np.float32)


def _full(shape):
    nd = len(shape)
    return pl.BlockSpec(shape, lambda *_: (0,) * nd)


def _gla_kernel(x_ref, w_ref, b_ref, ng_ref, tri_ref, y_ref, st_ref, o_ref):
    @pl.when(pl.program_id(1) == 0)
    def _():
        st_ref[...] = jnp.zeros_like(st_ref)

    hk = HEADS * DK
    xb = x_ref[0].astype(BF16)
    h = _dot(xb, w_ref[...]) + b_ref[...]
    q = h[:, 0:hk]
    k = h[:, hk:2 * hk]
    z = h[:, 2 * hk:3 * hk]
    v = h[:, 3 * hk:3 * hk + BRANCH_W]
    g = h[:, 3 * hk + BRANCH_W:3 * hk + 2 * BRANCH_W]

    la = _log_sigmoid(z) * (1.0 / GLA_TAU)
    tri = tri_ref[...]
    la_h, la_m, la_l = _split3(la)
    cum = _dot(tri, la_h) + _dot(tri, la_m) + _dot(tri, la_l)
    qd = q * (DK ** -0.5) * jnp.exp(cum)
    ki = k * jnp.exp(-cum)
    vb = v.astype(BF16)

    lane = lax.broadcasted_iota(jnp.int32, (1, PAIR_W), 1)
    row_i = lax.broadcasted_iota(jnp.int32, (CHUNK, CHUNK), 0)
    col_i = lax.broadcasted_iota(jnp.int32, (CHUNK, CHUNK), 1)
    causal = row_i >= col_i
    bd_r = lax.broadcasted_iota(jnp.int32, (2 * DV, PAIR_W), 0)
    bd_c = lax.broadcasted_iota(jnp.int32, (2 * DV, PAIR_W), 1)
    blockdiag = (bd_r >= DV) == (bd_c >= DK)

    n_chunks = x_ref.shape[1] // CHUNK
    for p in range(HEADS // 2):
        st = st_ref[p]
        lsl = slice(p * PAIR_W, (p + 1) * PAIR_W)
        for c in range(n_chunks):
            rsl = slice(c * CHUNK, (c + 1) * CHUNK)
            qd_c = qd[rsl, lsl]
            ki_c = ki[rsl, lsl].astype(BF16)
            cum_c = cum[rsl, lsl]
            last = cum_c[CHUNK - 1:CHUNK, :]
            kt = (k[rsl, lsl] * jnp.exp(last - cum_c)).astype(BF16)
            inter = _dot_nt(qd_c.astype(BF16), st.astype(BF16))
            for hh in range(2):
                head = 2 * p + hh
                in_head = (lane >= hh * DK) & (lane < (hh + 1) * DK)
                qm = jnp.where(in_head, qd_c, 0.0).astype(BF16)
                att = jnp.where(causal, _dot_nt(qm, ki_c), 0.0)
                o_h = _dot(att.astype(BF16), vb[rsl, head * DV:(head + 1) * DV])
                o_ref[rsl, head * DV:(head + 1) * DV] = o_h + inter[:, hh * DV:(hh + 1) * DV]
            upd = _dot_tn(vb[rsl, p * 2 * DV:(p + 1) * 2 * DV], kt)
            st = st * jnp.exp(last) + jnp.where(blockdiag, upd, 0.0)
        st_ref[p] = st

    ng = ng_ref[...]
    for head in range(HEADS):
        hsl = slice(head * DV, (head + 1) * DV)
        y = _head_norm(o_ref[:, hsl]) * ng[:, hsl] * _silu(g[:, hsl])
        y_ref[0, :, hsl] = y.astype(y_ref.dtype)


def _gla_mixer(x, w, b, ng, tri):
    bsz, s, d = x.shape
    wcols = w.shape[1]
    return pl.pallas_call(
        _gla_kernel,
        grid=(bsz, s // SEQ_BLOCK),
        in_specs=[
            pl.BlockSpec((1, SEQ_BLOCK, d), lambda i, j: (i, j, 0)),
            _full((d, wcols)), _full((1, wcols)), _full((1, BRANCH_W)),
            _full((SEQ_BLOCK, SEQ_BLOCK)),
        ],
        out_specs=pl.BlockSpec((1, SEQ_BLOCK, BRANCH_W), lambda i, j: (i, j, 0)),
        out_shape=jax.ShapeDtypeStruct((bsz, s, BRANCH_W), BF16),
        scratch_shapes=[pltpu.VMEM((HEADS // 2, 2 * DV, PAIR_W), F32),
                        pltpu.VMEM((SEQ_BLOCK, BRANCH_W), F32)],
        compiler_params=_cparams(("arbitrary", "arbitrary")),
        name="gla_mixer",
    )(x, w, b, ng, tri)


ML_ST_ROWS = 2 * DV + LANES


def _ml_kernel(x_ref, w_ref, b_ref, wgc_ref, bgc_ref, wgr_ref, bgr_ref, cw_ref, cb_ref, ng_ref,
               tri_ref, trit_ref, y_ref, ct_ref, m_ref, carry_ref, o_ref):
    @pl.when(pl.program_id(1) == 0)
    def _():
        ct_ref[...] = jnp.zeros_like(ct_ref)
        m_ref[...] = jnp.zeros_like(m_ref)
        carry_ref[...] = jnp.zeros_like(carry_ref)

    hk = HEADS * DK
    lb = x_ref.shape[1]
    x32 = x_ref[0]
    xh, xl = _split2(x32)
    h = _dot(xh, w_ref[...]) + b_ref[...]
    qk = h[:, 0:2 * hk]
    v = h[:, 2 * hk:2 * hk + BRANCH_W]
    o_pre = h[:, 2 * hk + BRANCH_W:2 * hk + 2 * BRANCH_W]

    ext = jnp.concatenate([carry_ref[...], qk], axis=0)
    cw = cw_ref[...]
    acc = cb_ref[...] + ext[8 - (ML_CONV - 1):8 - (ML_CONV - 1) + lb] * cw[0:1]
    for j in range(1, ML_CONV):
        off = 8 - (ML_CONV - 1) + j
        acc = acc + ext[off:off + lb] * cw[j:j + 1]
    carry_ref[...] = qk[lb - 8:lb]
    qkc = _silu(acc)
    qf = qkc[:, 0:hk]
    kf = qkc[:, hk:2 * hk] * (DK ** -0.5)
    vb = v.astype(BF16)

    gc = (_dot(xh, wgc_ref[0]) + _dot(xl, wgc_ref[0]) + _dot(xh, wgc_ref[1])) + bgc_ref[...]
    gr = (_dot_nt(wgr_ref[0], xh) + _dot_nt(wgr_ref[0], xl) + _dot_nt(wgr_ref[1], xh)) + bgr_ref[...]
    lf_c = _log_sigmoid(gc)
    lf_r = _log_sigmoid(gr)
    tri = tri_ref[...]
    trit = trit_ref[...]
    c_h, c_m, c_l = _split3(lf_c)
    bc = _dot(tri, c_h) + _dot(tri, c_m) + _dot(tri, c_l)
    r_h, r_m, r_l = _split3(lf_r)
    br = _dot(r_h, trit) + _dot(r_m, trit) + _dot(r_l, trit)

    lane = lax.broadcasted_iota(jnp.int32, (1, PAIR_W), 1)
    row_i = lax.broadcasted_iota(jnp.int32, (CHUNK, CHUNK), 0)
    col_i = lax.broadcasted_iota(jnp.int32, (CHUNK, CHUNK), 1)
    causal = row_i >= col_i
    sr = lax.broadcasted_iota(jnp.int32, (ML_ST_ROWS, PAIR_W), 0)
    sc_ = lax.broadcasted_iota(jnp.int32, (ML_ST_ROWS, PAIR_W), 1)
    first = sc_ < DK
    rows_h0 = (sr < DV) | (sr == 2 * DV)
    rows_h1 = ((sr >= DV) & (sr < 2 * DV)) | (sr == 2 * DV + 1)
    st_mask = (rows_h0 & first) | (rows_h1 & ~first)
    ones_blk = jnp.ones((CHUNK, LANES), BF16)

    n_chunks = lb // CHUNK
    for p in range(HEADS // 2):
        ct = ct_ref[p]
        lsl = slice(p * PAIR_W, (p + 1) * PAIR_W)
        m_pair = [m_ref[2 * p + hh][0:1, 0:1] for hh in range(2)]
        for c in range(n_chunks):
            rsl = slice(c * CHUNK, (c + 1) * CHUNK)
            q_c = qf[rsl, lsl]
            k_c = kf[rsl, lsl]
            k_cb = k_c.astype(BF16)
            inter_mm = _dot_nt(q_c.astype(BF16), ct.astype(BF16))
            wt_cols, decays = [], []
            for hh in range(2):
                head = 2 * p + hh
                m_st = m_pair[hh]
                b_col = bc[rsl, HEADS + head:HEADS + head + 1]
                i_col = gc[rsl, head:head + 1]
                b_row = br[HEADS + head:HEADS + head + 1, rsl]
                i_row = gr[head:head + 1, rsl]
                dmat = jnp.where(causal, b_col - b_row + i_row, -jnp.inf)
                inter = b_col + m_st
                m_row = jnp.maximum(inter, jnp.max(dmat, axis=-1, keepdims=True))
                wts = jnp.exp(dmat - m_row)
                in_head = (lane >= hh * DK) & (lane < (hh + 1) * DK)
                qm = jnp.where(in_head, q_c, 0.0).astype(BF16)
                sc = _dot_nt(qm, k_cb) * wts
                w_inter = jnp.exp(inter - m_row)
                num = _dot(sc.astype(BF16), vb[rsl, head * DV:(head + 1) * DV]) \
                    + w_inter * inter_mm[:, hh * DV:(hh + 1) * DV]
                den = jnp.sum(sc, axis=-1, keepdims=True) \
                    + w_inter * inter_mm[:, 2 * DV + hh:2 * DV + hh + 1]
                o_ref[rsl, head * DV:(head + 1) * DV] = num / jnp.maximum(jnp.abs(den), jnp.exp(-m_row))
                g_tot = b_col[CHUNK - 1:CHUNK, :]
                tail = g_tot - b_col + i_col
                m_new = jnp.maximum(g_tot + m_st, jnp.max(tail, axis=0, keepdims=True))
                wt_cols.append(jnp.exp(tail - m_new))
                decays.append(jnp.exp(g_tot + m_st - m_new))
                m_pair[hh] = m_new
            wk = (k_c * jnp.where(lane < DK, wt_cols[0], wt_cols[1])).astype(BF16)
            vp = jnp.concatenate([vb[rsl, p * 2 * DV:(p + 1) * 2 * DV], ones_blk], axis=1)
            upd = _dot_tn(vp, wk)
            ct = ct * jnp.where(lane < DK, decays[0], decays[1]) + jnp.where(st_mask, upd, 0.0)
        ct_ref[p] = ct
        for hh in range(2):
            m_ref[2 * p + hh] = jnp.broadcast_to(m_pair[hh], m_ref.shape[1:])

    ng = ng_ref[...]
    for head in range(HEADS):
        hsl = slice(head * DV, (head + 1) * DV)
        y = _head_norm(o_ref[:, hsl]) * ng[:, hsl] * _sigmoid(o_pre[:, hsl])
        y_ref[0, :, hsl] = y.astype(y_ref.dtype)


def _ml_mixer(x, w, b, wgc, bgc, wgr, bgr, cw, cb, ng, tri, trit):
    bsz, s, d = x.shape
    wcols = w.shape[1]
    return pl.pallas_call(
        _ml_kernel,
        grid=(bsz, s // SEQ_BLOCK),
        in_specs=[
            pl.BlockSpec((1, SEQ_BLOCK, d), lambda i, j: (i, j, 0)),
            _full((d, wcols)), _full((1, wcols)),
            _full((2, d, LANES)), _full((1, LANES)),
            _full((2, 8, d)), _full((8, 1)),
            _full((ML_CONV, 2 * HEADS * DK)), _full((1, 2 * HEADS * DK)),
            _full((1, BRANCH_W)),
            _full((SEQ_BLOCK, SEQ_BLOCK)), _full((SEQ_BLOCK, SEQ_BLOCK)),
        ],
        out_specs=pl.BlockSpec((1, SEQ_BLOCK, BRANCH_W), lambda i, j: (i, j, 0)),
        out_shape=jax.ShapeDtypeStruct((bsz, s, BRANCH_W), BF16),
        scratch_shapes=[pltpu.VMEM((HEADS // 2, ML_ST_ROWS, PAIR_W), F32),
                        pltpu.VMEM((HEADS, 8, LANES), F32),
                        pltpu.VMEM((8, 2 * HEADS * DK), F32),
                        pltpu.VMEM((SEQ_BLOCK, BRANCH_W), F32)],
        compiler_params=_cparams(("arbitrary", "arbitrary")),
        name="mlstm_mixer",
    )(x, w, b, wgc, bgc, wgr, bgr, cw, cb, ng, tri, trit)


S5_HALF = 256
S5_HALF_STATE = (S5_HALF // S5_GROUP) * S5_STATE


def _s5_kernel(bsz, nc, x_ref, wu_ref, bu_ref, ec_ref, ftc_ref, krev_ref, a_ref, y_ref,
               in_ref, xs_ref, xsb_ref, kexp_ref):
    s = pl.program_id(1)
    lc = S5_CHUNK
    rows = bsz * nc
    sw = S5_HALF_STATE

    def expand_state(blk):
        tile = jnp.concatenate([blk[:, 0:LANES]] * (sw // LANES) + [blk[:, LANES:2 * LANES]] * (sw // LANES),
                               axis=1)
        r = lax.broadcasted_iota(jnp.int32, tile.shape, 0) // S5_GROUP
        c = (lax.broadcasted_iota(jnp.int32, tile.shape, 1) % sw) // S5_STATE
        return jnp.where(r == c, tile, 0.0).astype(BF16)

    @pl.when(s == 0)
    def _():
        kr = krev_ref[0]
        r = (lax.broadcasted_iota(jnp.int32, kr.shape, 0) % S5_HALF) // S5_GROUP
        c = lax.broadcasted_iota(jnp.int32, kr.shape, 1) // S5_GROUP
        kexp_ref[...] = jnp.where(r == c, kr, 0.0).astype(BF16)
        xs_ref[...] = jnp.zeros_like(xs_ref)

    @pl.when(s < lc)
    def _():
        xl = x_ref[...].reshape(rows, x_ref.shape[-1]).astype(BF16)
        u = (_dot(xl, wu_ref[...]) + bu_ref[...]).astype(BF16)
        in_ref[s] = u
        et = expand_state(ec_ref[0])
        for jc in range(2 * sw // S5_HALF):
            part = _dot(u, et[:, jc * S5_HALF:(jc + 1) * S5_HALF])
            for jj in range(S5_HALF // LANES):
                xs_ref[jc * (S5_HALF // LANES) + jj] += part[:, jj * LANES:(jj + 1) * LANES]

    @pl.when(s == lc - 1)
    def _():
        ar = a_ref[0, 0:1, :]
        ai = a_ref[0, 1:2, :]
        nt = sw // LANES

        def body(c, carry):
            sr, si = carry
            idx = pl.ds(c, bsz, stride=nc)
            xr = jnp.concatenate([xs_ref[j, idx, :] for j in range(nt)], axis=1)
            xi = jnp.concatenate([xs_ref[nt + j, idx, :] for j in range(nt)], axis=1)
            for j in range(nt):
                xs_ref[j, idx, :] = sr[:, j * LANES:(j + 1) * LANES]
                xs_ref[nt + j, idx, :] = si[:, j * LANES:(j + 1) * LANES]
            return ar * sr - ai * si + xr, ar * si + ai * sr + xi

        zero = jnp.zeros((bsz, sw), F32)
        lax.fori_loop(0, nc, body, (zero, zero))
        for j in range(2 * nt):
            xsb_ref[:, j * LANES:(j + 1) * LANES] = xs_ref[j].astype(BF16)

    for lo in range(lc):
        @pl.when(s == lc + lo)
        def _(lo=lo):
            acc = _dot_nt(xsb_ref[...], expand_state(ftc_ref[0]))
            for l in range(lo + 1):
                j = lc - 1 - lo + l
                acc = acc + _dot(in_ref[l], kexp_ref[j * S5_HALF:(j + 1) * S5_HALF, :])
            y_ref[...] = acc.reshape(y_ref.shape)


def _s5_mixer(x, w_u, b_u, ec, ftc, krev, a):
    bsz, s, d = x.shape
    lc = S5_CHUNK
    nc = s // lc
    x3 = x.reshape(bsz, nc, lc * d)
    nh = BRANCH_W // S5_HALF
    y = pl.pallas_call(
        functools.partial(_s5_kernel, bsz, nc),
        grid=(nh, 2 * lc),
        in_specs=[
            pl.BlockSpec((bsz, nc, d), lambda h, t: (0, 0, jnp.minimum(t, lc - 1))),
            pl.BlockSpec((d, S5_HALF), lambda h, t: (0, h)),
            pl.BlockSpec((1, S5_HALF), lambda h, t: (0, h)),
            pl.BlockSpec((1, S5_HALF, 2 * LANES), lambda h, t: (jnp.minimum(t, lc - 1), h, 0)),
            pl.BlockSpec((1, S5_HALF, 2 * LANES), lambda h, t: (jnp.maximum(t - lc, 0), h, 0)),
            pl.BlockSpec((1, lc * S5_HALF, S5_HALF), lambda h, t: (h, 0, 0)),
            pl.BlockSpec((1, 2, S5_HALF_STATE), lambda h, t: (h, 0, 0)),
        ],
        out_specs=pl.BlockSpec((bsz, nc, S5_HALF), lambda h, t: (0, 0, jnp.maximum(t - lc, 0) * nh + h)),
        out_shape=jax.ShapeDtypeStruct((bsz, nc, lc * BRANCH_W), F32),
        scratch_shapes=[pltpu.VMEM((lc, bsz * nc, S5_HALF), BF16),
                        pltpu.VMEM((2 * S5_HALF_STATE // LANES, bsz * nc, LANES), F32),
                        pltpu.VMEM((bsz * nc, 2 * S5_HALF_STATE), BF16),
                        pltpu.VMEM((lc * S5_HALF, S5_HALF), BF16)],
        compiler_params=_cparams(("arbitrary", "arbitrary")),
        name="s5_mixer",
    )(x3, w_u, b_u, ec, ftc, krev, a)
    return y.reshape(bsz * s, BRANCH_W)


def _s5_tables(a_re, a_im, log_dt, b_re, b_im, c_re, c_im, d_skip):
    nl = a_re.shape[0]
    g, p_, n, lc = S5_GROUPS, S5_STATE, S5_GROUP, S5_CHUNK
    hi = lax.Precision.HIGHEST
    a_re, a_im, b_re, b_im, c_re, c_im = (v.astype(F32) for v in (a_re, a_im, b_re, b_im, c_re, c_im))
    dt = jnp.exp(log_dt.astype(F32))[..., None]

    def powers(steps):
        mag = jnp.exp((a_re * dt)[..., None] * steps)
        ang = (a_im * dt)[..., None] * steps
        return mag * jnp.cos(ang), mag * jnp.sin(ang)

    pwr, pwi = powers(jnp.arange(lc + 1, dtype=F32))
    pr, pi = powers(jnp.asarray(np.arange(lc - 1, -1, -1), F32))
    nr, ni = pwr[..., 1] - 1.0, pwi[..., 1]
    den = a_re * a_re + a_im * a_im
    qr, qi = (nr * a_re + ni * a_im) / den, (ni * a_re - nr * a_im) / den
    bbr = qr[..., None] * b_re - qi[..., None] * b_im
    bbi = qr[..., None] * b_im + qi[..., None] * b_re
    cpr = c_re[..., None] * pr[:, :, None] - c_im[..., None] * pi[:, :, None]
    cpi = c_re[..., None] * pi[:, :, None] + c_im[..., None] * pr[:, :, None]
    kern = (jnp.einsum('zgnpt,zgpm->ztgmn', cpr, bbr, precision=hi)
            - jnp.einsum('zgnpt,zgpm->ztgmn', cpi, bbi, precision=hi))
    is_tau0 = jnp.asarray(np.arange(lc) == lc - 1, F32).reshape(1, lc, 1, 1, 1)
    kern = kern + is_tau0 * d_skip.astype(F32).reshape(nl, 1, g, 1, n) * jnp.eye(n, dtype=F32)
    spread = jnp.asarray(np.tile(np.eye(n, dtype=np.float32), (1, S5_HALF // n)))
    kw = jnp.einsum('ztgmn,nk->ztgmk', kern, spread, precision=hi)
    nh = BRANCH_W // S5_HALF
    krev = kw.reshape(nl, lc, nh, S5_HALF, S5_HALF).transpose(0, 2, 1, 3, 4)
    krev = krev.reshape(nl, nh, lc * S5_HALF, S5_HALF)
    er =pr[..., None] * bbr[:, :, :, None, :] - pi[..., None] * bbi[:, :, :, None, :]
    ei = pr[..., None] * bbi[:, :, :, None, :] + pi[..., None] * bbr[:, :, :, None, :]
    er, ei = er.transpose(0, 3, 1, 4, 2), ei.transpose(0, 3, 1, 4, 2)
    ec = jnp.concatenate([er, er, ei, ei], axis=-1).reshape(nl, lc, g * n, 4 * p_)
    pfr = pwr[..., 1:].transpose(0, 3, 1, 2)[:, :, :, None, :]
    pfi = pwi[..., 1:].transpose(0, 3, 1, 2)[:, :, :, None, :]
    fr = c_re[:, None] * pfr - c_im[:, None] * pfi
    fi = -(c_re[:, None] * pfi + c_im[:, None] * pfr)
    ftc = jnp.concatenate([fr, fr, fi, fi], axis=-1).reshape(nl, lc, g * n, 4 * p_)
    a_tab = jnp.stack([pwr[..., lc].reshape(nl, nh, S5_HALF_STATE),
                       pwi[..., lc].reshape(nl, nh, S5_HALF_STATE)], axis=2)
    return ec, ftc, krev, a_tab


def _merge_kernel(x_ref, yg_ref, y5_ref, ym_ref, wgate_ref, bgate_ref, wglu_ref, bglu_ref,
                  wup_ref, wo_ref, g_ref, b_ref, o_ref):
    x = x_ref[...]
    xb = x.astype(BF16)
    y5 = _gelu_tanh(y5_ref[...])
    y5 = y5 * _sigmoid(_dot(y5.astype(BF16), wglu_ref[...]) + bglu_ref[...])
    ys = (yg_ref[...], y5.astype(BF16), ym_ref[...])
    acc = None
    for r in range(N_BRANCH):
        gate = _sigmoid(_dot(xb, wgate_ref[:, r * D_MODEL:(r + 1) * D_MODEL])
                        + bgate_ref[:, r * D_MODEL:(r + 1) * D_MODEL])
        term = gate * _dot(ys[r], wup_ref[r])
        acc = term if acc is None else acc + term
    mix = _dot(acc.astype(BF16), wo_ref[...])
    o_ref[...] = _layer_norm(DN_ALPHA * x + mix, g_ref[...], b_ref[...])


def _merge(x2, yg, y5, ym, wgate, bgate, wglu, bglu, wup, wo, g, b, tm=256):
    t, dm = x2.shape
    row = lambda w: pl.BlockSpec((tm, w), lambda i: (i, 0))
    return pl.pallas_call(
        _merge_kernel,
        grid=(t // tm,),
        in_specs=[row(dm), row(BRANCH_W), row(BRANCH_W), row(BRANCH_W),
                  _full(wgate.shape), _full(bgate.shape), _full(wglu.shape), _full(bglu.shape),
                  _full(wup.shape), _full(wo.shape), _full(g.shape), _full(b.shape)],
        out_specs=row(dm),
        out_shape=jax.ShapeDtypeStruct((t, dm), F32),
        compiler_params=_cparams(("parallel",)),
        name="merge_ln1",
    )(x2, yg, y5, ym, wgate, bgate, wglu, bglu, wup, wo, g, b)


def _ple_ln2(x, xb, f, p_ref, pwg_ref, pwp_ref, g_ref, b_ref):
    e = _sigmoid(_dot(xb, pwg_ref[...])) * _dot(p_ref[...].astype(BF16), pwp_ref[...])
    return _layer_norm(DN_ALPHA * x + f + e, g_ref[...], b_ref[...])


def _ffn_kernel(x_ref, p_ref, wg_ref, wu_ref, wd_ref, pwg_ref, pwp_ref, g_ref, b_ref, o_ref):
    x = x_ref[...]
    xb = x.astype(BF16)
    hid = (_silu(_dot(xb, wg_ref[...])) * _dot(xb, wu_ref[...])).astype(BF16)
    f = _dot(hid, wd_ref[...])
    o_ref[...] = _ple_ln2(x, xb, f, p_ref, pwg_ref, pwp_ref, g_ref, b_ref)


def _ffn_layer(x2, p2, wg, wu, wd, pwg, pwp, g, b, tm=256):
    t, dm = x2.shape
    row = lambda w: pl.BlockSpec((tm, w), lambda i: (i, 0))
    return pl.pallas_call(
        _ffn_kernel,
        grid=(t // tm,),
        in_specs=[row(dm), row(PLE_DIM), _full(wg.shape), _full(wu.shape), _full(wd.shape),
                  _full(pwg.shape), _full(pwp.shape), _full(g.shape), _full(b.shape)],
        out_specs=row(dm),
        out_shape=jax.ShapeDtypeStruct((t, dm), F32),
        compiler_params=_cparams(("parallel",)),
        name="ffn_ple_ln2",
    )(x2, p2, wg, wu, wd, pwg, pwp, g, b)


def _router_kernel(x_ref, w_ref, b_ref, tril_ref, sel_ref, pr_ref, rk_ref, cnt_ref, base_ref):
    @pl.when(pl.program_id(0) == 0)
    def _():
        base_ref[...] = jnp.zeros_like(base_ref)

    xh, xl = _split2(x_ref[...])
    logits = _dot(xh, w_ref[0]) + _dot(xl, w_ref[0]) + _dot(xh, w_ref[1]) + b_ref[...]
    lane = lax.broadcasted_iota(jnp.int32, logits.shape, 1)
    neg = -jnp.inf
    logits = jnp.where(lane < N_EXPERTS, logits, neg)
    m1 = jnp.max(logits, axis=-1, keepdims=True)
    i1 = jnp.min(jnp.where(logits == m1, lane, LANES), axis=-1, keepdims=True)
    rest = jnp.where(lane == i1, neg, logits)
    m2 = jnp.max(rest, axis=-1, keepdims=True)
    i2 = jnp.min(jnp.where(rest == m2, lane, LANES), axis=-1, keepdims=True)
    e2 = jnp.exp(m2 - m1)
    p1 = 1.0 / (1.0 + e2)
    p2 = e2 / (1.0 + e2)
    hot = jnp.where((lane == i1) | (lane == i2), 1.0, 0.0)
    base = base_ref[0:1, :]
    before = _dot(tril_ref[...], hot.astype(BF16)) + base
    r1 = jnp.sum(jnp.where(lane == i1, before, 0.0), axis=-1, keepdims=True)
    r2 = jnp.sum(jnp.where(lane == i2, before, 0.0), axis=-1, keepdims=True)
    sel_ref[...] = jnp.where(lane == 0, i1, jnp.where(lane == 1, i2, 0))
    pr_ref[...] = jnp.where(lane == 0, p1, jnp.where(lane == 1, p2, 0.0))
    rk_ref[...] = jnp.where(lane == 0, r1, jnp.where(lane == 1, r2, 0.0))
    total = base + jnp.sum(hot, axis=0, keepdims=True)
    base_ref[...] = jnp.broadcast_to(total, base_ref.shape)
    cnt_ref[...] = jnp.broadcast_to(total, cnt_ref.shape)


def _router(x2, w, b, tm=512):
    t, dm = x2.shape
    i = np.arange(tm)
    tril = jnp.asarray((i[:, None] > i[None, :]).astype(np.float32), BF16)
    blk = pl.BlockSpec((tm, LANES), lambda i: (i, 0))
    return pl.pallas_call(
        _router_kernel,
        grid=(t // tm,),
        in_specs=[pl.BlockSpec((tm, dm), lambda i: (i, 0)), _full(w.shape), _full(b.shape), _full((tm, tm))],
        out_specs=[blk, blk, blk, _full((8, LANES))],
        out_shape=[jax.ShapeDtypeStruct((t, LANES), jnp.int32), jax.ShapeDtypeStruct((t, LANES), F32),
                   jax.ShapeDtypeStruct((t, LANES), F32), jax.ShapeDtypeStruct((8, LANES), F32)],
        scratch_shapes=[pltpu.VMEM((8, LANES), F32)],
        compiler_params=_cparams(("arbitrary",)),
        name="moe_router",
    )(x2, w, b, tril)


TOP_K = 2
MOE_TILE = 1024
MOE_SUB = 256
MOE_FF_TILE = 512
DISPATCH_TILE = 512


def _row_copy_wait(src_rows, dst_rows, sem):
    pltpu.make_async_copy(src_rows, dst_rows, sem).wait()


def _dispatch_kernel(pos_ref, x_ref, xs_ref, sem):
    tm = x_ref.shape[0]
    base = pl.program_id(0) * tm * TOP_K

    def body(r, carry):
        for k in range(TOP_K):
            dst = pos_ref[base + r * TOP_K + k]
            pltpu.make_async_copy(x_ref.at[pl.ds(r, 1), :], xs_ref.at[pl.ds(dst, 1), :], sem).start()
        return carry

    lax.fori_loop(0, tm, body, 0)
    for _ in range(TOP_K):
        _row_copy_wait(x_ref, xs_ref.at[pl.ds(0, tm), :], sem)


def _dispatch(pos_flat, x2, tm=DISPATCH_TILE):
    t, dm = x2.shape
    return pl.pallas_call(
        _dispatch_kernel,
        grid_spec=pltpu.PrefetchScalarGridSpec(
            num_scalar_prefetch=1,
            grid=(t // tm,),
            in_specs=[pl.BlockSpec((tm, dm), lambda i, pos: (i, 0))],
            out_specs=pl.BlockSpec(memory_space=pl.ANY),
            scratch_shapes=[pltpu.SemaphoreType.DMA],
        ),
        out_shape=jax.ShapeDtypeStruct((t * TOP_K, dm), F32),
        compiler_params=_cparams(("arbitrary",)),
        name="moe_dispatch",
    )(pos_flat, x2)


def _moe_group_kernel(tile_ref, exp_ref, lo_ref, hi_ref, xs_ref, wg_ref, wu_ref, wd_ref, o_ref,
                      acc_ref, wgb_ref, wub_ref, wdb_ref):
    i = pl.program_id(0)
    f = pl.program_id(1)
    lo = lo_ref[i]
    hi = hi_ref[i]
    tm = xs_ref.shape[0]

    @pl.when(hi > lo)
    def _():
        @pl.when((lo == 0) & (f == 0))
        def _():
            acc_ref[...] = jnp.zeros_like(acc_ref)

        wgb_ref[...] = wg_ref[0].astype(BF16)
        wub_ref[...] = wu_ref[0].astype(BF16)
        wdb_ref[...] = wd_ref[0].astype(BF16)
        for sub in range(tm // MOE_SUB):
            @pl.when((hi > sub * MOE_SUB) & (lo < (sub + 1) * MOE_SUB))
            def _(sub=sub):
                rsl = slice(sub * MOE_SUB, (sub + 1) * MOE_SUB)
                xb = xs_ref[rsl, :].astype(BF16)
                hid = _silu(_dot(xb, wgb_ref[...])) * _dot(xb, wub_ref[...])
                rid = lax.broadcasted_iota(jnp.int32, hid.shape, 0) + sub * MOE_SUB
                hid = jnp.where((rid >= lo) & (rid < hi), hid, 0.0).astype(BF16)
                acc_ref[rsl, :] += _dot(hid, wdb_ref[...])

        @pl.when((hi == tm) & (f == pl.num_programs(1) - 1))
        def _():
            o_ref[...] = acc_ref[...]


def _moe_grouped(items, xs, wg, wu, wd, tm=MOE_TILE, tf=MOE_FF_TILE):
    tile, exp, lo, hi = items
    rows, dm = xs.shape
    dff = wg.shape[2]
    return pl.pallas_call(
        _moe_group_kernel,
        grid_spec=pltpu.PrefetchScalarGridSpec(
            num_scalar_prefetch=4,
            grid=(tile.shape[0], dff // tf),
            in_specs=[pl.BlockSpec((tm, dm), lambda i, f, tl, ex, lo_, hi_: (tl[i], 0)),
                      pl.BlockSpec((1, dm, tf), lambda i, f, tl, ex, lo_, hi_: (ex[i], 0, f)),
                      pl.BlockSpec((1, dm, tf), lambda i, f, tl, ex, lo_, hi_: (ex[i], 0, f)),
                      pl.BlockSpec((1, tf, dm), lambda i, f, tl, ex, lo_, hi_: (ex[i], f, 0))],
            out_specs=pl.BlockSpec((tm, dm), lambda i, f, tl, ex, lo_, hi_: (tl[i], 0)),
            scratch_shapes=[pltpu.VMEM((tm, dm), F32), pltpu.VMEM((dm, tf), BF16),
                            pltpu.VMEM((dm, tf), BF16), pltpu.VMEM((tf, dm), BF16)],
        ),
        out_shape=jax.ShapeDtypeStruct((rows, dm), F32),
        compiler_params=_cparams(("arbitrary", "arbitrary")),
        name="moe_grouped",
    )(tile, exp, lo, hi, xs, wg, wu, wd)


def _moe_items(counts, n_rows, tm=MOE_TILE):
    n_tiles = n_rows // tm
    ends = jnp.cumsum(counts)
    cuts = jnp.sort(jnp.concatenate([jnp.arange(n_tiles + 1, dtype=jnp.int32) * tm, ends[:-1]]))
    start, stop = cuts[:-1], cuts[1:]
    tile = jnp.minimum(start // tm, n_tiles - 1)
    exp = jnp.minimum(jnp.sum(ends[None, :] <= start[:, None], axis=1), N_EXPERTS - 1)
    lo = start - tile * tm
    hi = stop - tile * tm
    return tile.astype(jnp.int32), exp.astype(jnp.int32), lo.astype(jnp.int32), hi.astype(jnp.int32)


def _combine_kernel(pos_ref, x_ref, pr_ref, p_ref, pwg_ref, pwp_ref, g_ref, b_ref, ys_ref, o_ref, gat_ref, sem):
    tm = x_ref.shape[0]
    base = pl.program_id(0) * tm * TOP_K

    def body(r, carry):
        for k in range(TOP_K):
            src = pos_ref[base + r * TOP_K + k]
            pltpu.make_async_copy(ys_ref.at[pl.ds(src, 1), :], gat_ref.at[k, pl.ds(r, 1), :], sem).start()
        return carry

    lax.fori_loop(0, tm, body, 0)
    x = x_ref[...]
    xb = x.astype(BF16)
    e = _sigmoid(_dot(xb, pwg_ref[...])) * _dot(p_ref[...].astype(BF16), pwp_ref[...])
    for k in range(TOP_K):
        _row_copy_wait(ys_ref.at[pl.ds(0, tm), :], gat_ref.at[k], sem)
    pr = pr_ref[...]
    f = pr[:, 0:1] * gat_ref[0]
    for k in range(1, TOP_K):
        f = f + pr[:, k:k + 1] * gat_ref[k]
    o_ref[...] = _layer_norm(DN_ALPHA * x + f + e, g_ref[...], b_ref[...])


def _combine_layer(pos_flat, x2, pr, p2, pwg, pwp, g, b, ys, tm=DISPATCH_TILE):
    t, dm = x2.shape
    row = lambda w: pl.BlockSpec((tm, w), lambda i, pos: (i, 0))
    full = lambda a: pl.BlockSpec(a.shape, lambda i, pos: (0,) * a.ndim)
    return pl.pallas_call(
        _combine_kernel,
        grid_spec=pltpu.PrefetchScalarGridSpec(
            num_scalar_prefetch=1,
            grid=(t // tm,),
            in_specs=[row(dm), row(LANES), row(PLE_DIM), full(pwg), full(pwp), full(g), full(b),
                      pl.BlockSpec(memory_space=pl.ANY)],
            out_specs=row(dm),
            scratch_shapes=[pltpu.VMEM((TOP_K, tm, dm), F32), pltpu.SemaphoreType.DMA],
        ),
        out_shape=jax.ShapeDtypeStruct((t, dm), F32),
        compiler_params=_cparams(("arbitrary",)),
        name="moe_combine_ple_ln2",
    )(pos_flat, x2, pr, p2, pwg, pwp, g, b, ys)


def _row(v):
    return v.reshape(1, -1).astype(F32)


def _pad_lanes(w):
    return jnp.pad(w, ((0, 0), (0, LANES - w.shape[1])))


def kernel(x, p, w_in, b_in, gla_w_a2, gla_b_a2, gla_norm_g, s5_a_re, s5_a_im, s5_log_dt, s5_b_re, s5_b_im,
           s5_c_re, s5_c_im, s5_d, s5_w_glu, s5_b_glu, ml_conv_w, ml_conv_b, ml_norm_g, w_up, w_o, ln1_g, ln1_b,
           ffn_wg, ffn_wu, ffn_wd, moe_router, moe_router_b, moe_wg, moe_wu, moe_wd, ple_w_gate, ple_w_proj,
           ln2_g, ln2_b):
    bsz, s, dm = x.shape
    t = bsz * s
    hi = lax.Precision.HIGHEST
    o = IN_OFF
    tri_np = _chunk_tri(SEQ_BLOCK, CHUNK)
    tri = jnp.asarray(tri_np, BF16)
    trit = jnp.asarray(tri_np.T, BF16)
    s5_ec, s5_ftc, s5_krev, s5_atab = _s5_tables(s5_a_re, s5_a_im, s5_log_dt, s5_b_re, s5_b_im, s5_c_re, s5_c_im,
                                                 s5_d)

    for i in range(DEPTH):
        w, b = w_in[i], b_in[i]
        sl = lambda k: (w[:, o[k]:o[k + 1]], b[o[k]:o[k + 1]])
        (wq, bq), (wk, bk), (wv, bv), (wa, ba), (wg_, bg_) = sl(0), sl(1), sl(2), sl(3), sl(4)
        wz = jnp.dot(wa, gla_w_a2[i], precision=hi)
        bz = jnp.dot(ba, gla_w_a2[i], precision=hi) + gla_b_a2[i]
        w_gla = jnp.concatenate([wq, wk, wz, wv, wg_], axis=1).astype(BF16)
        b_gla = _row(jnp.concatenate([bq, bk, bz, bv, bg_]))
        y_gla = _gla_mixer(x, w_gla, b_gla, _row(gla_norm_g[i]), tri)

        (wu_, bu_) = sl(5)
        x2 = x.reshape(t, dm)
        y5 = _s5_mixer(x, wu_.astype(BF16), _row(bu_), s5_ec[i], s5_ftc[i], s5_krev[i], s5_atab[i])

        (wmq, bmq), (wmk, bmk), (wmv, bmv), (wmi, bmi), (wmf, bmf), (wmo, bmo) = \
            sl(6), sl(7), sl(8), sl(9), sl(10), sl(11)
        w_ml = jnp.concatenate([wmq, wmk, wmv, wmo], axis=1).astype(BF16)
        b_ml = _row(jnp.concatenate([bmq, bmk, bmv, bmo]))
        w_if = jnp.concatenate([wmi, wmf], axis=1)
        b_if = jnp.concatenate([bmi, bmf])
        if_h = w_if.astype(BF16)
        if_l = (w_if - if_h.astype(F32)).astype(BF16)
        wgc = jnp.stack([_pad_lanes(if_h), _pad_lanes(if_l)])
        wgr = jnp.stack([if_h.T, if_l.T])
        y_ml = _ml_mixer(x, w_ml, b_ml, wgc, _pad_lanes(_row(b_if)), wgr, b_if.reshape(-1, 1).astype(F32),
                         ml_conv_w[i].astype(F32), _row(ml_conv_b[i]), _row(ml_norm_g[i]), tri, trit)

        (wgt, bgt) = sl(12)
        x1 = _merge(x2, y_gla.reshape(t, BRANCH_W), y5, y_ml.reshape(t, BRANCH_W),
                    wgt.astype(BF16), _row(bgt), s5_w_glu[i].astype(BF16), _row(s5_b_glu[i]),
                    w_up[i].astype(BF16), w_o[i].astype(BF16), _row(ln1_g[i]), _row(ln1_b[i]))

        p2 = p[i].reshape(t, PLE_DIM)
        pwg = ple_w_gate[i].astype(BF16)
        pwp = ple_w_proj[i].astype(BF16)
        j = i // 2
        if i % 2 == 0:
            x2n = _ffn_layer(x1, p2, ffn_wg[j].astype(BF16), ffn_wu[j].astype(BF16), ffn_wd[j].astype(BF16),
                             pwg, pwp, _row(ln2_g[i]), _row(ln2_b[i]))
        else:
            wr = moe_router[j]
            wr_h = wr.astype(BF16)
            wr_l = (wr - wr_h.astype(F32)).astype(BF16)
            sel, pr, rk, cnt = _router(x1, jnp.stack([_pad_lanes(wr_h), _pad_lanes(wr_l)]),
                                       _pad_lanes(_row(moe_router_b[j])))
            counts = cnt[0, :N_EXPERTS].astype(jnp.int32)
            starts = jnp.cumsum(counts) - counts
            sel2 = sel[:, :TOP_K]
            pos = rk[:, :TOP_K].astype(jnp.int32) + jnp.sum(
                jnp.where(sel2[..., None] == jnp.arange(N_EXPERTS), starts, 0), axis=-1)
            pos_flat = pos.reshape(-1)
            xs = _dispatch(pos_flat, x1)
            ys = _moe_grouped(_moe_items(counts, t * TOP_K), xs, moe_wg[j], moe_wu[j], moe_wd[j])
            x2n = _combine_layer(pos_flat, x1, pr, p2, pwg, pwp, _row(ln2_g[i]), _row(ln2_b[i]), ys)
        x = x2n.reshape(bsz, s, dm)
    return x
```

```python
import functools
import math

import numpy as np
import jax
import jax.numpy as jnp
from jax import lax
from jax.experimental import pallas as pl
from jax.experimental.pallas import tpu as pltpu

F32 = jnp.float32
BF16 = jnp.bfloat16

D_MODEL = 1024
DEPTH = 2
N_BRANCH = 3
BRANCH_W = 512
HEADS = 4
DK = 64
DV = BRANCH_W // HEADS
GLA_RANK = 16
GLA_TAU = 16.0
CHUNK = 64
ML_CHUNK = 256
S5_GROUP = 16
S5_GROUPS = BRANCH_W // S5_GROUP
S5_STATE = 64
S5_CHUNK = 16
ML_CONV = 4
D_FF = 2816
N_EXPERTS = 8
D_FF_EXPERT = 3584
PLE_DIM = 256
DN_ALPHA = (2.0 * DEPTH) ** 0.25
LN_EPS = 1e-5

IN_WIDTHS = (
    HEADS * DK, HEADS * DK, BRANCH_W, GLA_RANK, BRANCH_W,
    BRANCH_W,
    HEADS * DK, HEADS * DK, BRANCH_W, HEADS, HEADS, BRANCH_W,
    N_BRANCH * D_MODEL,
)
IN_OFF = tuple(int(o) for o in np.concatenate([[0], np.cumsum(IN_WIDTHS)]))

LANES = 128
SEQ_BLOCK = 256
PAIR_W = 2 * DK
VMEM_LIMIT = 56 * 1024 * 1024


def _cparams(sem):
    return pltpu.CompilerParams(dimension_semantics=sem, vmem_limit_bytes=VMEM_LIMIT)


def _dot(a, b):
    return jnp.dot(a, b, preferred_element_type=F32)


def _dot_nt(a, b):
    return lax.dot_general(a, b, (((1,), (1,)), ((), ())), preferred_element_type=F32)


def _dot_tn(a, b):
    return lax.dot_general(a, b, (((0,), (0,)), ((), ())), preferred_element_type=F32)


def _split3(a):
    hi = a.astype(BF16)
    r = a - hi.astype(F32)
    mid = r.astype(BF16)
    lo = (r - mid.astype(F32)).astype(BF16)
    return hi, mid, lo


def _split2(a):
    hi = a.astype(BF16)
    lo = (a - hi.astype(F32)).astype(BF16)
    return hi, lo


def _log_sigmoid(x):
    return jnp.minimum(x, 0.0) - jnp.log(1.0 + jnp.exp(-jnp.abs(x)))


def _sigmoid(x):
    return 1.0 / (1.0 + jnp.exp(-x))


def _silu(x):
    return x * _sigmoid(x)


def _gelu_tanh(x):
    return 0.5 * x * (1.0 + jnp.tanh(math.sqrt(2.0 / math.pi) * (x + 0.044715 * (x * x * x))))


def _layer_norm(v, g, b):
    mu = jnp.mean(v, axis=-1, keepdims=True)
    c = v - mu
    var = jnp.mean(c * c, axis=-1, keepdims=True)
    return c * lax.rsqrt(var + LN_EPS) * g + b


def _head_norm(o):
    mu = jnp.mean(o, axis=-1, keepdims=True)
    c = o - mu
    var = jnp.mean(c * c, axis=-1, keepdims=True)
    return c * lax.rsqrt(var + LN_EPS)


def _chunk_tri(n, chunk):
    i = np.arange(n)
    return ((i[:, None] >= i[None, :]) & (i[:, None] // chunk == i[None, :] // chunk)).astype(np.float32)


def _full(shape):
    nd = len(shape)
    return pl.BlockSpec(shape, lambda *_: (0,) * nd, pipeline_mode=pl.Buffered(1))


def _gla_kernel(x_ref, w_ref, b_ref, ng_ref, tri_ref, y_ref, u_ref, st_ref, o_ref):
    @pl.when(pl.program_id(1) == 0)
    def _():
        st_ref[...] = jnp.zeros_like(st_ref)

    hk = HEADS * DK
    xb = x_ref[0].astype(BF16)
    h = _dot(xb, w_ref[...]) + b_ref[...]
    q = h[:, 0:hk]
    k = h[:, hk:2 * hk]
    z = h[:, 2 * hk:3 * hk]
    v = h[:, 3 * hk:3 * hk + BRANCH_W]
    g = h[:, 3 * hk + BRANCH_W:3 * hk + 2 * BRANCH_W]
    u_ref[0] = h[:, 3 * hk + 2 * BRANCH_W:3 * hk + 3 * BRANCH_W].astype(u_ref.dtype)

    la = _log_sigmoid(z) * (1.0 / GLA_TAU)
    tri = tri_ref[...]
    la_h, la_m, la_l = _split3(la)
    cum = _dot(tri, la_h) + _dot(tri, la_m) + _dot(tri, la_l)
    qd = q * (DK ** -0.5) * jnp.exp(cum)
    ki = k * jnp.exp(-cum)
    vb = v.astype(BF16)

    lane = lax.broadcasted_iota(jnp.int32, (1, PAIR_W), 1)
    row_i = lax.broadcasted_iota(jnp.int32, (CHUNK, CHUNK), 0)
    col_i = lax.broadcasted_iota(jnp.int32, (CHUNK, CHUNK), 1)
    causal = row_i >= col_i
    bd_r = lax.broadcasted_iota(jnp.int32, (2 * DV, PAIR_W), 0)
    bd_c = lax.broadcasted_iota(jnp.int32, (2 * DV, PAIR_W), 1)
    blockdiag = (bd_r >= DV) == (bd_c >= DK)

    n_chunks = x_ref.shape[1] // CHUNK
    for p in range(HEADS // 2):
        st = st_ref[p]
        lsl = slice(p * PAIR_W, (p + 1) * PAIR_W)
        for c in range(n_chunks):
            rsl = slice(c * CHUNK, (c + 1) * CHUNK)
            qd_c = qd[rsl, lsl]
            ki_c = ki[rsl, lsl].astype(BF16)
            cum_c = cum[rsl, lsl]
            last = cum_c[CHUNK - 1:CHUNK, :]
            kt = (k[rsl, lsl] * jnp.exp(last - cum_c)).astype(BF16)
            inter = _dot_nt(qd_c.astype(BF16), st.astype(BF16))
            for hh in range(2):
                head = 2 * p + hh
                in_head = (lane >= hh * DK) & (lane < (hh + 1) * DK)
                qm = jnp.where(in_head, qd_c, 0.0).astype(BF16)
                att = jnp.where(causal, _dot_nt(qm, ki_c), 0.0)
                o_h = _dot(att.astype(BF16), vb[rsl, head * DV:(head + 1) * DV])
                o_ref[rsl, head * DV:(head + 1) * DV] = o_h + inter[:, hh * DV:(hh + 1) * DV]
            upd = _dot_tn(vb[rsl, p * 2 * DV:(p + 1) * 2 * DV], kt)
            st = st * jnp.exp(last) + jnp.where(blockdiag, upd, 0.0)
        st_ref[p] = st

    ng = ng_ref[...]
    for head in range(HEADS):
        hsl = slice(head * DV, (head + 1) * DV)
        y = _head_norm(o_ref[:, hsl]) * ng[:, hsl] * _silu(g[:, hsl])
        y_ref[0, :, hsl] = y.astype(y_ref.dtype)


def _gla_mixer(x, w, b, ng, tri):
    bsz, s, d = x.shape
    wcols = w.shape[1]
    return pl.pallas_call(
        _gla_kernel,
        grid=(bsz, s // SEQ_BLOCK),
        in_specs=[
            pl.BlockSpec((1, SEQ_BLOCK, d), lambda i, j: (i, j, 0)),
            _full((d, wcols)), _full((1, wcols)), _full((1, BRANCH_W)),
            _full((SEQ_BLOCK, SEQ_BLOCK)),
        ],
        out_specs=[pl.BlockSpec((1, SEQ_BLOCK, BRANCH_W), lambda i, j: (i, j, 0))] * 2,
        out_shape=[jax.ShapeDtypeStruct((bsz, s, BRANCH_W), BF16)] * 2,
        scratch_shapes=[pltpu.VMEM((HEADS // 2, 2 * DV, PAIR_W), F32),
                        pltpu.VMEM((SEQ_BLOCK, BRANCH_W), F32)],
        compiler_params=_cparams(("arbitrary", "arbitrary")),
        name="gla_mixer",
    )(x, w, b, ng, tri)


ML_ST_ROWS = 2 * DV + LANES


def _ml_kernel(x_ref, w_ref, b_ref, wgc_ref, bgc_ref, wgr_ref, bgr_ref, cw_ref, cb_ref, ng_ref,
               tri_ref, trit_ref, y_ref, ct_ref, m_ref, carry_ref, o_ref):
    @pl.when(pl.program_id(1) == 0)
    def _():
        ct_ref[...] = jnp.zeros_like(ct_ref)
        m_ref[...] = jnp.zeros_like(m_ref)
        carry_ref[...] = jnp.zeros_like(carry_ref)

    hk = HEADS * DK
    lb = x_ref.shape[1]
    x32 = x_ref[0]
    xh, xl = _split2(x32)
    h = _dot(xh, w_ref[...]) + b_ref[...]
    qk = h[:, 0:2 * hk]
    v = h[:, 2 * hk:2 * hk + BRANCH_W]
    o_pre = h[:, 2 * hk + BRANCH_W:2 * hk + 2 * BRANCH_W]

    ext = jnp.concatenate([carry_ref[...], qk], axis=0)
    cw = cw_ref[...]
    acc = cb_ref[...] + ext[8 - (ML_CONV - 1):8 - (ML_CONV - 1) + lb] * cw[0:1]
    for j in range(1, ML_CONV):
        off = 8 - (ML_CONV - 1) + j
        acc = acc + ext[off:off + lb] * cw[j:j + 1]
    carry_ref[...] = qk[lb - 8:lb]
    qkc = _silu(acc)
    qf = qkc[:, 0:hk]
    kf = qkc[:, hk:2 * hk] * (DK ** -0.5)
    vb = v.astype(BF16)

    gc = (_dot(xh, wgc_ref[0]) + _dot(xl, wgc_ref[0]) + _dot(xh, wgc_ref[1])) + bgc_ref[...]
    gr = (_dot_nt(wgr_ref[0], xh) + _dot_nt(wgr_ref[0], xl) + _dot_nt(wgr_ref[1], xh)) + bgr_ref[...]
    lf_c = _log_sigmoid(gc)
    lf_r = _log_sigmoid(gr)
    tri = tri_ref[...]
    trit = trit_ref[...]
    c_h, c_m, c_l = _split3(lf_c)
    bc = _dot(tri, c_h) + _dot(tri, c_m) + _dot(tri, c_l)
    r_h, r_m, r_l = _split3(lf_r)
    br = _dot(r_h, trit) + _dot(r_m, trit) + _dot(r_l, trit)

    lane = lax.broadcasted_iota(jnp.int32, (1, PAIR_W), 1)
    row_i = lax.broadcasted_iota(jnp.int32, (ML_CHUNK, ML_CHUNK), 0)
    col_i = lax.broadcasted_iota(jnp.int32, (ML_CHUNK, ML_CHUNK), 1)
    causal = row_i >= col_i
    sr = lax.broadcasted_iota(jnp.int32, (ML_ST_ROWS, PAIR_W), 0)
    sc_ = lax.broadcasted_iota(jnp.int32, (ML_ST_ROWS, PAIR_W), 1)
    first = sc_ < DK
    rows_h0 = (sr < DV) | (sr == 2 * DV)
    rows_h1 = ((sr >= DV) & (sr < 2 * DV)) | (sr == 2 * DV + 1)
    st_mask = (rows_h0 & first) | (rows_h1 & ~first)
    ones_blk = jnp.ones((ML_CHUNK, LANES), BF16)

    n_chunks = lb // ML_CHUNK
    for p in range(HEADS // 2):
        ct = ct_ref[p]
        lsl = slice(p * PAIR_W, (p + 1) * PAIR_W)
        m_pair = [m_ref[2 * p + hh][0:1, 0:1] for hh in range(2)]
        for c in range(n_chunks):
            rsl = slice(c * ML_CHUNK, (c + 1) * ML_CHUNK)
            q_c = qf[rsl, lsl]
            k_c = kf[rsl, lsl]
            k_cb = k_c.astype(BF16)
            inter_mm = _dot_nt(q_c.astype(BF16), ct.astype(BF16))
            wt_cols, decays = [], []
            for hh in range(2):
                head = 2 * p + hh
                m_st = m_pair[hh]
                b_col = bc[rsl, HEADS + head:HEADS + head + 1]
                i_col = gc[rsl, head:head + 1]
                b_row = br[HEADS + head:HEADS + head + 1, rsl]
                i_row = gr[head:head + 1, rsl]
                dmat = jnp.where(causal, b_col - b_row + i_row, -jnp.inf)
                inter = b_col + m_st
                m_row = jnp.maximum(inter, jnp.max(dmat, axis=-1, keepdims=True))
                wts = jnp.exp(dmat - m_row)
                in_head = (lane >= hh * DK) & (lane < (hh + 1) * DK)
                qm = jnp.where(in_head, q_c, 0.0).astype(BF16)
                sc = _dot_nt(qm, k_cb) * wts
                w_inter = jnp.exp(inter - m_row)
                num = _dot(sc.astype(BF16), vb[rsl, head * DV:(head + 1) * DV]) \
                    + w_inter * inter_mm[:, hh * DV:(hh + 1) * DV]
                den = jnp.sum(sc, axis=-1, keepdims=True) \
                    + w_inter * inter_mm[:, 2 * DV + hh:2 * DV + hh + 1]
                o_ref[rsl, head * DV:(head + 1) * DV] = num / jnp.maximum(jnp.abs(den), jnp.exp(-m_row))
                g_tot = b_col[ML_CHUNK - 1:ML_CHUNK, :]
                tail = g_tot - b_col + i_col
                m_new = jnp.maximum(g_tot + m_st, jnp.max(tail, axis=0, keepdims=True))
                wt_cols.append(jnp.exp(tail - m_new))
                decays.append(jnp.exp(g_tot + m_st - m_new))
                m_pair[hh] = m_new
            wk = (k_c * jnp.where(lane < DK, wt_cols[0], wt_cols[1])).astype(BF16)
            vp = jnp.concatenate([vb[rsl, p * 2 * DV:(p + 1) * 2 * DV], ones_blk], axis=1)
            upd = _dot_tn(vp, wk)
            ct = ct * jnp.where(lane < DK, decays[0], decays[1]) + jnp.where(st_mask, upd, 0.0)
        ct_ref[p] = ct
        for hh in range(2):
            m_ref[2 * p + hh] = jnp.broadcast_to(m_pair[hh], m_ref.shape[1:])

    ng = ng_ref[...]
    for head in range(HEADS):
        hsl = slice(head * DV, (head + 1) * DV)
        y = _head_norm(o_ref[:, hsl]) * ng[:, hsl] * _sigmoid(o_pre[:, hsl])
        y_ref[0, :, hsl] = y.astype(y_ref.dtype)


def _ml_mixer(x, w, b, wgc, bgc, wgr, bgr, cw, cb, ng, tri, trit):
    bsz, s, d = x.shape
    wcols = w.shape[1]
    return pl.pallas_call(
        _ml_kernel,
        grid=(bsz, s // SEQ_BLOCK),
        in_specs=[
            pl.BlockSpec((1, SEQ_BLOCK, d), lambda i, j: (i, j, 0)),
            _full((d, wcols)), _full((1, wcols)),
            _full((2, d, LANES)), _full((1, LANES)),
            _full((2, 8, d)), _full((8, 1)),
            _full((ML_CONV, 2 * HEADS * DK)), _full((1, 2 * HEADS * DK)),
            _full((1, BRANCH_W)),
            _full((SEQ_BLOCK, SEQ_BLOCK)), _full((SEQ_BLOCK, SEQ_BLOCK)),
        ],
        out_specs=pl.BlockSpec((1, SEQ_BLOCK, BRANCH_W), lambda i, j: (i, j, 0)),
        out_shape=jax.ShapeDtypeStruct((bsz, s, BRANCH_W), BF16),
        scratch_shapes=[pltpu.VMEM((HEADS // 2, ML_ST_ROWS, PAIR_W), F32),
                        pltpu.VMEM((HEADS, 8, LANES), F32),
                        pltpu.VMEM((8, 2 * HEADS * DK), F32),
                        pltpu.VMEM((SEQ_BLOCK, BRANCH_W), F32)],
        compiler_params=_cparams(("arbitrary", "arbitrary")),
        name="mlstm_mixer",
    )(x, w, b, wgc, bgc, wgr, bgr, cw, cb, ng, tri, trit)


S5_HALF = 256
S5_HALF_STATE = (S5_HALF // S5_GROUP) * S5_STATE


def _s5_kernel(bsz, nc, u_ref, ec_ref, ftc_ref, krev_ref, a_ref, y_ref,
               in_ref, xs_ref, xsb_ref, kexp_ref):
    s = pl.program_id(1)
    lc = S5_CHUNK
    rows = bsz * nc
    sw = S5_HALF_STATE

    def expand_state(blk):
        tile = jnp.concatenate([blk[:, 0:LANES]] * (sw // LANES) + [blk[:, LANES:2 * LANES]] * (sw // LANES),
                               axis=1)
        r = lax.broadcasted_iota(jnp.int32, tile.shape, 0) // S5_GROUP
        c = (lax.broadcasted_iota(jnp.int32, tile.shape, 1) % sw) // S5_STATE
        return jnp.where(r == c, tile, 0.0).astype(BF16)

    @pl.when(s == 0)
    def _():
        kr = krev_ref[0]
        r = (lax.broadcasted_iota(jnp.int32, kr.shape, 0) % S5_HALF) // S5_GROUP
        c = lax.broadcasted_iota(jnp.int32, kr.shape, 1) // S5_GROUP
        kexp_ref[...] = jnp.where(r == c, kr, 0.0).astype(BF16)
        xs_ref[...] = jnp.zeros_like(xs_ref)

    @pl.when(s < lc)
    def _():
        u = u_ref[...].reshape(rows, u_ref.shape[-1])
        in_ref[s] = u
        et = expand_state(ec_ref[0])
        for jc in range(2 * sw // S5_HALF):
            part = _dot(u, et[:, jc * S5_HALF:(jc + 1) * S5_HALF])
            for jj in range(S5_HALF // LANES):
                xs_ref[jc * (S5_HALF // LANES) + jj] += part[:, jj * LANES:(jj + 1) * LANES]

    @pl.when(s == lc - 1)
    def _():
        ar = a_ref[0, 0:1, :]
        ai = a_ref[0, 1:2, :]
        nt = sw // LANES

        def body(c, carry):
            sr, si = carry
            idx = pl.ds(c, bsz, stride=nc)
            xr = jnp.concatenate([xs_ref[j, idx, :] for j in range(nt)], axis=1)
            xi = jnp.concatenate([xs_ref[nt + j, idx, :] for j in range(nt)], axis=1)
            for j in range(nt):
                xs_ref[j, idx, :] = sr[:, j * LANES:(j + 1) * LANES]
                xs_ref[nt + j, idx, :] = si[:, j * LANES:(j + 1) * LANES]
            return ar * sr - ai * si + xr, ar * si + ai * sr + xi

        zero = jnp.zeros((bsz, sw), F32)
        lax.fori_loop(0, nc, body, (zero, zero))
        for j in range(2 * nt):
            xsb_ref[:, j * LANES:(j + 1) * LANES] = xs_ref[j].astype(BF16)

    for lo in range(lc):
        @pl.when(s == lc + lo)
        def _(lo=lo):
            acc = _dot_nt(xsb_ref[...], expand_state(ftc_ref[0]))
            for l in range(lo + 1):
                j = lc - 1 - lo + l
                acc = acc + _dot(in_ref[l], kexp_ref[j * S5_HALF:(j + 1) * S5_HALF, :])
            y_ref[...] = acc.reshape(y_ref.shape).astype(y_ref.dtype)


def _s5_mixer(u, ec, ftc, krev, a):
    bsz, s, _ = u.shape
    lc = S5_CHUNK
    nc = s // lc
    u3 = u.reshape(bsz, nc, lc * BRANCH_W)
    nh = BRANCH_W // S5_HALF
    y = pl.pallas_call(
        functools.partial(_s5_kernel, bsz, nc),
        grid=(nh, 2 * lc),
        in_specs=[
            pl.BlockSpec((bsz, nc, S5_HALF), lambda h, t: (0, 0, jnp.minimum(t, lc - 1) * nh + h)),
            pl.BlockSpec((1, S5_HALF, 2 * LANES), lambda h, t: (jnp.minimum(t, lc - 1), h, 0)),
            pl.BlockSpec((1, S5_HALF, 2 * LANES), lambda h, t: (jnp.maximum(t - lc, 0), h, 0)),
            pl.BlockSpec((1, lc * S5_HALF, S5_HALF), lambda h, t: (h, 0, 0)),
            pl.BlockSpec((1, 2, S5_HALF_STATE), lambda h, t: (h, 0, 0)),
        ],
        out_specs=pl.BlockSpec((bsz, nc, S5_HALF), lambda h, t: (0, 0, jnp.maximum(t - lc, 0) * nh + h)),
        out_shape=jax.ShapeDtypeStruct((bsz, nc, lc * BRANCH_W), BF16),
        scratch_shapes=[pltpu.VMEM((lc, bsz * nc, S5_HALF), BF16),
                        pltpu.VMEM((2 * S5_HALF_STATE // LANES, bsz * nc, LANES), F32),
                        pltpu.VMEM((bsz * nc, 2 * S5_HALF_STATE), BF16),
                        pltpu.VMEM((lc * S5_HALF, S5_HALF), BF16)],
        compiler_params=_cparams(("arbitrary", "arbitrary")),
        name="s5_mixer",
    )(u3, ec, ftc, krev, a)
    return y.reshape(bsz * s, BRANCH_W)


def _s5_tables(a_re, a_im, log_dt, b_re, b_im, c_re, c_im, d_skip):
    nl = a_re.shape[0]
    g, p_, n, lc = S5_GROUPS, S5_STATE, S5_GROUP, S5_CHUNK
    hi = lax.Precision.HIGHEST
    a_re, a_im, b_re, b_im, c_re, c_im = (v.astype(F32) for v in (a_re, a_im, b_re, b_im, c_re, c_im))
    dt = jnp.exp(log_dt.astype(F32))[..., None]
    adt_r = (a_re * dt)[:, None, :, None, :]
    adt_i = (a_im * dt)[:, None, :, None, :]

    def powers(steps):
        st = jnp.asarray(np.asarray(steps, np.float32).reshape(1, -1, 1, 1, 1))
        mag = jnp.exp(adt_r * st)
        return mag * jnp.cos(adt_i * st), mag * jnp.sin(adt_i * st)

    lbr, lbi = powers([1])
    are, aim = a_re[:, None, :, None, :], a_im[:, None, :, None, :]
    den = are * are + aim * aim
    qr = ((lbr - 1.0) * are + lbi * aim) / den
    qi = (lbi * are - (lbr - 1.0) * aim) / den
    bt_re, bt_im = b_re.transpose(0, 1, 3, 2)[:, None], b_im.transpose(0, 1, 3, 2)[:, None]
    bbr = qr * bt_re - qi * bt_im
    bbi = qr * bt_im + qi * bt_re
    cr, ci = c_re[:, None], c_im[:, None]
    pr, pi = powers(np.arange(lc - 1, -1, -1))
    er, ei = pr * bbr - pi * bbi, pr * bbi + pi * bbr
    ec = jnp.concatenate([er, er, ei, ei], axis=-1).reshape(nl, lc, g * n, 4 * p_)
    pfr, pfi = powers(np.arange(1, lc + 1))
    fr, fi = cr * pfr - ci * pfi, -(cr * pfi + ci * pfr)
    ftc = jnp.concatenate([fr, fr, fi, fi], axis=-1).reshape(nl, lc, g * n, 4 * p_)
    kcr, kci = cr * pr - ci * pi, cr * pi + ci * pr
    kern = (jnp.einsum('zjgnp,zgmp->zjgmn', kcr, bbr[:, 0], precision=hi)
            - jnp.einsum('zjgnp,zgmp->zjgmn', kci, bbi[:, 0], precision=hi))
    is_tau0 = jnp.asarray(np.arange(lc) == lc - 1, F32).reshape(1, lc, 1, 1, 1)
    kern = kern + is_tau0 * d_skip.astype(F32).reshape(nl, 1, g, 1, n) * jnp.eye(n, dtype=F32)
    spread = jnp.asarray(np.tile(np.eye(n, dtype=np.float32), (1, S5_HALF // n)))
    kw = jnp.einsum('zjgmn,nk->zjgmk', kern, spread, precision=hi)
    nh = BRANCH_W // S5_HALF
    krev = kw.reshape(nl, lc, nh, S5_HALF, S5_HALF).transpose(0, 2, 1, 3, 4)
    krev = krev.reshape(nl, nh, lc * S5_HALF, S5_HALF)
    ar, ai = powers([lc])
    a_tab = jnp.stack([ar.reshape(nl, nh, S5_HALF_STATE), ai.reshape(nl, nh, S5_HALF_STATE)], axis=2)
    return ec, ftc, krev, a_tab


def _merge_kernel(x_ref, yg_ref, y5_ref, ym_ref, wgate_ref, bgate_ref, wglu_ref, bglu_ref,
                  wup_ref, wo_ref, g_ref, b_ref, o_ref):
    x = x_ref[...]
    xb = x.astype(BF16)
    y5 = _gelu_tanh(y5_ref[...].astype(F32))
    y5 = y5 * _sigmoid(_dot(y5.astype(BF16), wglu_ref[...]) + bglu_ref[...])
    ys = (yg_ref[...], y5.astype(BF16), ym_ref[...])
    acc = None
    for r in range(N_BRANCH):
        gate = _sigmoid(_dot(xb, wgate_ref[:, r * D_MODEL:(r + 1) * D_MODEL])
                        + bgate_ref[:, r * D_MODEL:(r + 1) * D_MODEL])
        term = gate * _dot(ys[r], wup_ref[r])
        acc = term if acc is None else acc + term
    mix = _dot(acc.astype(BF16), wo_ref[...])
    o_ref[...] = _layer_norm(DN_ALPHA * x + mix, g_ref[...], b_ref[...])


def _merge(x2, yg, y5, ym, wgate, bgate, wglu, bglu, wup, wo, g, b, tm=512):
    t, dm = x2.shape
    row = lambda w: pl.BlockSpec((tm, w), lambda i: (i, 0))
    return pl.pallas_call(
        _merge_kernel,
        grid=(t // tm,),
        in_specs=[row(dm), row(BRANCH_W), row(BRANCH_W), row(BRANCH_W),
                  _full(wgate.shape), _full(bgate.shape), _full(wglu.shape), _full(bglu.shape),
                  _full(wup.shape), _full(wo.shape), _full(g.shape), _full(b.shape)],
        out_specs=row(dm),
        out_shape=jax.ShapeDtypeStruct((t, dm), F32),
        compiler_params=_cparams(("parallel",)),
        name="merge_ln1",
    )(x2, yg, y5, ym, wgate, bgate, wglu, bglu, wup, wo, g, b)


def _ple_ln2(x, xb, f, p_ref, pwg_ref, pwp_ref, g_ref, b_ref):
    e = _sigmoid(_dot(xb, pwg_ref[...])) * _dot(p_ref[...].astype(BF16), pwp_ref[...])
    return _layer_norm(DN_ALPHA * x + f + e, g_ref[...], b_ref[...])


def _ffn_kernel(x_ref, p_ref, wg_ref, wu_ref, wd_ref, pwg_ref, pwp_ref, g_ref, b_ref, o_ref):
    x = x_ref[...]
    xb = x.astype(BF16)
    hid = (_silu(_dot(xb, wg_ref[...])) * _dot(xb, wu_ref[...])).astype(BF16)
    f = _dot(hid, wd_ref[...])
    o_ref[...] = _ple_ln2(x, xb, f, p_ref, pwg_ref, pwp_ref, g_ref, b_ref)


def _ffn_layer(x2, p2, wg, wu, wd, pwg, pwp, g, b, tm=256):
    t, dm = x2.shape
    row = lambda w: pl.BlockSpec((tm, w), lambda i: (i, 0))
    return pl.pallas_call(
        _ffn_kernel,
        grid=(t // tm,),
        in_specs=[row(dm), row(PLE_DIM), _full(wg.shape), _full(wu.shape), _full(wd.shape),
                  _full(pwg.shape), _full(pwp.shape), _full(g.shape), _full(b.shape)],
        out_specs=row(dm),
        out_shape=jax.ShapeDtypeStruct((t, dm), F32),
        compiler_params=_cparams(("parallel",)),
        name="ffn_ple_ln2",
    )(x2, p2, wg, wu, wd, pwg, pwp, g, b)


def _router_kernel(x_ref, w_ref, b_ref, tril_ref, sel_ref, pr_ref, rk_ref, cnt_ref, base_ref):
    @pl.when(pl.program_id(0) == 0)
    def _():
        base_ref[...] = jnp.zeros_like(base_ref)

    xh, xl = _split2(x_ref[...])
    logits = _dot(xh, w_ref[0]) + _dot(xl, w_ref[0]) + _dot(xh, w_ref[1]) + b_ref[...]
    lane = lax.broadcasted_iota(jnp.int32, logits.shape, 1)
    neg = -jnp.inf
    logits = jnp.where(lane < N_EXPERTS, logits, neg)
    m1 = jnp.max(logits, axis=-1, keepdims=True)
    i1 = jnp.min(jnp.where(logits == m1, lane, LANES), axis=-1, keepdims=True)
    rest = jnp.where(lane == i1, neg, logits)
    m2 = jnp.max(rest, axis=-1, keepdims=True)
    i2 = jnp.min(jnp.where(rest == m2, lane, LANES), axis=-1, keepdims=True)
    e2 = jnp.exp(m2 - m1)
    p1 = 1.0 / (1.0 + e2)
    p2 = e2 / (1.0 + e2)
    hot = jnp.where((lane == i1) | (lane == i2), 1.0, 0.0)
    base = base_ref[0:1, :]
    before = _dot(tril_ref[...], hot.astype(BF16)) + base
    r1 = jnp.sum(jnp.where(lane == i1, before, 0.0), axis=-1, keepdims=True)
    r2 = jnp.sum(jnp.where(lane == i2, before, 0.0), axis=-1, keepdims=True)
    sel_ref[...] = jnp.where(lane == 0, i1, jnp.where(lane == 1, i2, 0))
    pr_ref[...] = jnp.where(lane == 0, p1, jnp.where(lane == 1, p2, 0.0))
    rk_ref[...] = jnp.where(lane == 0, r1, jnp.where(lane == 1, r2, 0.0))
    total = base + jnp.sum(hot, axis=0, keepdims=True)
    base_ref[...] = jnp.broadcast_to(total, base_ref.shape)
    cnt_ref[...] = jnp.broadcast_to(total, cnt_ref.shape)


def _router(x2, w, b, tm=512):
    t, dm = x2.shape
    i = np.arange(tm)
    tril = jnp.asarray((i[:, None] > i[None, :]).astype(np.float32), BF16)
    blk = pl.BlockSpec((tm, LANES), lambda i: (i, 0))
    return pl.pallas_call(
        _router_kernel,
        grid=(t // tm,),
        in_specs=[pl.BlockSpec((tm, dm), lambda i: (i, 0)), _full(w.shape), _full(b.shape), _full((tm, tm))],
        out_specs=[blk, blk, blk, pl.BlockSpec((8, LANES), lambda i: (0, 0))],
        out_shape=[jax.ShapeDtypeStruct((t, LANES), jnp.int32), jax.ShapeDtypeStruct((t, LANES), F32),
                   jax.ShapeDtypeStruct((t, LANES), F32), jax.ShapeDtypeStruct((8, LANES), F32)],
        scratch_shapes=[pltpu.VMEM((8, LANES), F32)],
        compiler_params=_cparams(("arbitrary",)),
        name="moe_router",
    )(x2, w, b, tril)


TOP_K = 2
MOE_TILE = 1024
MOE_SUB = 256
MOE_FF_TILE = 512
DISPATCH_TILE = 512


def _row_copy_wait(src_rows, dst_rows, sem):
    pltpu.make_async_copy(src_rows, dst_rows, sem).wait()


def _dispatch_kernel(pos_ref, x_ref, xs_ref, sem):
    tm = x_ref.shape[0]
    base = pl.program_id(0) * tm * TOP_K

    def body(r, carry):
        for k in range(TOP_K):
            dst = pos_ref[base + r * TOP_K + k]
            pltpu.make_async_copy(x_ref.at[pl.ds(r, 1), :], xs_ref.at[pl.ds(dst, 1), :], sem).start()
        return carry

    lax.fori_loop(0, tm, body, 0)
    for _ in range(TOP_K):
        _row_copy_wait(x_ref, xs_ref.at[pl.ds(0, tm), :], sem)


def _dispatch(pos_flat, x2, tm=DISPATCH_TILE):
    t, dm = x2.shape
    return pl.pallas_call(
        _dispatch_kernel,
        grid_spec=pltpu.PrefetchScalarGridSpec(
            num_scalar_prefetch=1,
            grid=(t // tm,),
            in_specs=[pl.BlockSpec((tm, dm), lambda i, pos: (i, 0))],
            out_specs=pl.BlockSpec(memory_space=pl.ANY),
            scratch_shapes=[pltpu.SemaphoreType.DMA],
        ),
        out_shape=jax.ShapeDtypeStruct((t * TOP_K, dm), F32),
        compiler_params=_cparams(("arbitrary",)),
        name="moe_dispatch",
    )(pos_flat, x2)


def _moe_group_kernel(tile_ref, exp_ref, lo_ref, hi_ref, xs_ref, wg_ref, wu_ref, wd_ref, o_ref,
                      acc_ref, wgb_ref, wub_ref, wdb_ref):
    i = pl.program_id(0)
    f = pl.program_id(1)
    lo = lo_ref[i]
    hi = hi_ref[i]
    tm = xs_ref.shape[0]

    @pl.when(hi > lo)
    def _():
        @pl.when((lo == 0) & (f == 0))
        def _():
            acc_ref[...] = jnp.zeros_like(acc_ref)

        wgb_ref[...] = wg_ref[0].astype(BF16)
        wub_ref[...] = wu_ref[0].astype(BF16)
        wdb_ref[...] = wd_ref[0].astype(BF16)
        for sub in range(tm // MOE_SUB):
            @pl.when((hi > sub * MOE_SUB) & (lo < (sub + 1) * MOE_SUB))
            def _(sub=sub):
                rsl = slice(sub * MOE_SUB, (sub + 1) * MOE_SUB)
                xb = xs_ref[rsl, :].astype(BF16)
                hid = _silu(_dot(xb, wgb_ref[...])) * _dot(xb, wub_ref[...])
                rid = lax.broadcasted_iota(jnp.int32, hid.shape, 0) + sub * MOE_SUB
                hid = jnp.where((rid >= lo) & (rid < hi), hid, 0.0).astype(BF16)
                acc_ref[rsl, :] += _dot(hid, wdb_ref[...])

        @pl.when((hi == tm) & (f == pl.num_programs(1) - 1))
        def _():
            o_ref[...] = acc_ref[...]


def _moe_grouped(items, xs, wg, wu, wd, tm=MOE_TILE, tf=MOE_FF_TILE):
    tile, exp, lo, hi = items
    rows, dm = xs.shape
    dff = wg.shape[2]
    return pl.pallas_call(
        _moe_group_kernel,
        grid_spec=pltpu.PrefetchScalarGridSpec(
            num_scalar_prefetch=4,
            grid=(tile.shape[0], dff // tf),
            in_specs=[pl.BlockSpec((tm, dm), lambda i, f, tl, ex, lo_, hi_: (tl[i], 0)),
                      pl.BlockSpec((1, dm, tf), lambda i, f, tl, ex, lo_, hi_: (ex[i], 0, f)),
                      pl.BlockSpec((1, dm, tf), lambda i, f, tl, ex, lo_, hi_: (ex[i], 0, f)),
                      pl.BlockSpec((1, tf, dm), lambda i, f, tl, ex, lo_, hi_: (ex[i], f, 0))],
            out_specs=pl.BlockSpec((tm, dm), lambda i, f, tl, ex, lo_, hi_: (tl[i], 0)),
            scratch_shapes=[pltpu.VMEM((tm, dm), F32), pltpu.VMEM((dm, tf), BF16),
                            pltpu.VMEM((dm, tf), BF16), pltpu.VMEM((tf, dm), BF16)],
        ),
        out_shape=jax.ShapeDtypeStruct((rows, dm), F32),
        compiler_params=_cparams(("arbitrary", "arbitrary")),
        name="moe_grouped",
    )(tile, exp, lo, hi, xs, wg, wu, wd)


def _moe_items(counts, n_rows, tm=MOE_TILE):
    n_tiles = n_rows // tm
    ends = jnp.cumsum(counts)
    cuts = jnp.sort(jnp.concatenate([jnp.arange(n_tiles + 1, dtype=jnp.int32) * tm, ends[:-1]]))
    start, stop = cuts[:-1], cuts[1:]
    tile = jnp.minimum(start // tm, n_tiles - 1)
    exp = jnp.minimum(jnp.sum(ends[None, :] <= start[:, None], axis=1), N_EXPERTS - 1)
    lo = start - tile * tm
    hi = stop - tile * tm
    return tile.astype(jnp.int32), exp.astype(jnp.int32), lo.astype(jnp.int32), hi.astype(jnp.int32)


def _combine_kernel(pos_ref, x_ref, pr_ref, p_ref, pwg_ref, pwp_ref, g_ref, b_ref, ys_ref, o_ref, gat_ref, sem):
    tm = x_ref.shape[0]
    base = pl.program_id(0) * tm * TOP_K

    def body(r, carry):
        for k in range(TOP_K):
            src = pos_ref[base + r * TOP_K + k]
            pltpu.make_async_copy(ys_ref.at[pl.ds(src, 1), :], gat_ref.at[k, pl.ds(r, 1), :], sem).start()
        return carry

    lax.fori_loop(0, tm, body, 0)
    x = x_ref[...]
    xb = x.astype(BF16)
    e = _sigmoid(_dot(xb, pwg_ref[...])) * _dot(p_ref[...].astype(BF16), pwp_ref[...])
    for k in range(TOP_K):
        _row_copy_wait(ys_ref.at[pl.ds(0, tm), :], gat_ref.at[k], sem)
    pr = pr_ref[...]
    f = pr[:, 0:1] * gat_ref[0]
    for k in range(1, TOP_K):
        f = f + pr[:, k:k + 1] * gat_ref[k]
    o_ref[...] = _layer_norm(DN_ALPHA * x + f + e, g_ref[...], b_ref[...])


def _combine_layer(pos_flat, x2, pr, p2, pwg, pwp, g, b, ys, tm=DISPATCH_TILE):
    t, dm = x2.shape
    row = lambda w: pl.BlockSpec((tm, w), lambda i, pos: (i, 0))
    full = lambda a: pl.BlockSpec(a.shape, lambda i, pos: (0,) * a.ndim)
    return pl.pallas_call(
        _combine_kernel,
        grid_spec=pltpu.PrefetchScalarGridSpec(
            num_scalar_prefetch=1,
            grid=(t // tm,),
            in_specs=[row(dm), row(LANES), row(PLE_DIM), full(pwg), full(pwp), full(g), full(b),
                      pl.BlockSpec(memory_space=pl.ANY)],
            out_specs=row(dm),
            scratch_shapes=[pltpu.VMEM((TOP_K, tm, dm), F32), pltpu.SemaphoreType.DMA],
        ),
        out_shape=jax.ShapeDtypeStruct((t, dm), F32),
        compiler_params=_cparams(("arbitrary",)),
        name="moe_combine_ple_ln2",
    )(pos_flat, x2, pr, p2, pwg, pwp, g, b, ys)


def _row(v):
    return v.reshape(1, -1).astype(F32)


def _pad_lanes(w):
    return jnp.pad(w, ((0, 0), (0, LANES - w.shape[1])))


def kernel(x, p, w_in, b_in, gla_w_a2, gla_b_a2, gla_norm_g, s5_a_re, s5_a_im, s5_log_dt, s5_b_re, s5_b_im,
           s5_c_re, s5_c_im, s5_d, s5_w_glu, s5_b_glu, ml_conv_w, ml_conv_b, ml_norm_g, w_up, w_o, ln1_g, ln1_b,
           ffn_wg, ffn_wu, ffn_wd, moe_router, moe_router_b, moe_wg, moe_wu, moe_wd, ple_w_gate, ple_w_proj,
           ln2_g, ln2_b):
    bsz, s, dm = x.shape
    t = bsz * s
    hi = lax.Precision.HIGHEST
    o = IN_OFF
    tri = jnp.asarray(_chunk_tri(SEQ_BLOCK, CHUNK), BF16)
    ml_tri_np = _chunk_tri(SEQ_BLOCK, ML_CHUNK)
    ml_tri = jnp.asarray(ml_tri_np, BF16)
    ml_trit = jnp.asarray(ml_tri_np.T, BF16)
    s5_ec, s5_ftc, s5_krev, s5_atab = _s5_tables(s5_a_re, s5_a_im, s5_log_dt, s5_b_re, s5_b_im, s5_c_re, s5_c_im,
                                                 s5_d)

    for i in range(DEPTH):
        w, b = w_in[i], b_in[i]
        sl = lambda k: (w[:, o[k]:o[k + 1]], b[o[k]:o[k + 1]])
        (wq, bq), (wk, bk), (wv, bv), (wa, ba), (wg_, bg_) = sl(0), sl(1), sl(2), sl(3), sl(4)
        wz = jnp.dot(wa, gla_w_a2[i], precision=hi)
        bz = jnp.dot(ba, gla_w_a2[i], precision=hi) + gla_b_a2[i]
        (wu_, bu_) = sl(5)
        w_gla = jnp.concatenate([wq, wk, wz, wv, wg_, wu_], axis=1).astype(BF16)
        b_gla = _row(jnp.concatenate([bq, bk, bz, bv, bg_, bu_]))
        y_gla, u = _gla_mixer(x, w_gla, b_gla, _row(gla_norm_g[i]), tri)

        x2 = x.reshape(t, dm)
        y5 = _s5_mixer(u, s5_ec[i], s5_ftc[i], s5_krev[i], s5_atab[i])

        (wmq, bmq), (wmk, bmk), (wmv, bmv), (wmi, bmi), (wmf, bmf), (wmo, bmo) = \
            sl(6), sl(7), sl(8), sl(9), sl(10), sl(11)
        w_ml = jnp.concatenate([wmq, wmk, wmv, wmo], axis=1).astype(BF16)
        b_ml = _row(jnp.concatenate([bmq, bmk, bmv, bmo]))
        w_if = jnp.concatenate([wmi, wmf], axis=1)
        b_if = jnp.concatenate([bmi, bmf])
        if_h = w_if.astype(BF16)
        if_l = (w_if - if_h.astype(F32)).astype(BF16)
        wgc = jnp.stack([_pad_lanes(if_h), _pad_lanes(if_l)])
        wgr = jnp.stack([if_h.T, if_l.T])
        y_ml = _ml_mixer(x, w_ml, b_ml, wgc, _pad_lanes(_row(b_if)), wgr, b_if.reshape(-1, 1).astype(F32),
                         ml_conv_w[i].astype(F32), _row(ml_conv_b[i]), _row(ml_norm_g[i]), ml_tri, ml_trit)

        (wgt, bgt) = sl(12)
        x1 = _merge(x2, y_gla.reshape(t, BRANCH_W), y5, y_ml.reshape(t, BRANCH_W),
                    wgt.astype(BF16), _row(bgt), s5_w_glu[i].astype(BF16), _row(s5_b_glu[i]),
                    w_up[i].astype(BF16), w_o[i].astype(BF16), _row(ln1_g[i]), _row(ln1_b[i]))

        p2 = p[i].reshape(t, PLE_DIM)
        pwg = ple_w_gate[i].astype(BF16)
        pwp = ple_w_proj[i].astype(BF16)
        j = i // 2
        if i % 2 == 0:
            x2n = _ffn_layer(x1, p2, ffn_wg[j].astype(BF16), ffn_wu[j].astype(BF16), ffn_wd[j].astype(BF16),
                             pwg, pwp, _row(ln2_g[i]), _row(ln2_b[i]))
        else:
            wr = moe_router[j]
            wr_h = wr.astype(BF16)
            wr_l = (wr - wr_h.astype(F32)).astype(BF16)
            sel, pr, rk, cnt = _router(x1, jnp.stack([_pad_lanes(wr_h), _pad_lanes(wr_l)]),
                                       _pad_lanes(_row(moe_router_b[j])))
            counts = cnt[0, :N_EXPERTS].astype(jnp.int32)
            starts = jnp.cumsum(counts) - counts
            sel2 = sel[:, :TOP_K]
            pos = rk[:, :TOP_K].astype(jnp.int32) + jnp.sum(
                jnp.where(sel2[..., None] == jnp.arange(N_EXPERTS), starts, 0), axis=-1)
            pos_flat = pos.reshape(-1)
            xs = _dispatch(pos_flat, x1)
            ys = _moe_grouped(_moe_items(counts, t * TOP_K), xs, moe_wg[j], moe_wu[j], moe_wd[j])
            x2n = _combine_layer(pos_flat, x1, pr, p2, pwg, pwp, _row(ln2_g[i]), _row(ln2_b[i]), ys)
        x = x2n.reshape(bsz, s, dm)
    return x
```

```python
import functools
import math

import numpy as np
import jax
import jax.numpy as jnp
from jax import lax
from jax.experimental import pallas as pl
from jax.experimental.pallas import tpu as pltpu

F32 = jnp.float32
BF16 = jnp.bfloat16

D_MODEL = 1024
DEPTH = 2
N_BRANCH = 3
BRANCH_W = 512
HEADS = 4
DK = 64
DV = BRANCH_W // HEADS
GLA_RANK = 16
GLA_TAU = 16.0
CHUNK = 64
ML_CHUNK = 256
S5_GROUP = 16
S5_GROUPS = BRANCH_W // S5_GROUP
S5_STATE = 64
S5_CHUNK = 16
ML_CONV = 4
D_FF = 2816
N_EXPERTS = 8
D_FF_EXPERT = 3584
PLE_DIM = 256
DN_ALPHA = (2.0 * DEPTH) ** 0.25
LN_EPS = 1e-5

IN_WIDTHS = (
    HEADS * DK, HEADS * DK, BRANCH_W, GLA_RANK, BRANCH_W,
    BRANCH_W,
    HEADS * DK, HEADS * DK, BRANCH_W, HEADS, HEADS, BRANCH_W,
    N_BRANCH * D_MODEL,
)
IN_OFF = tuple(int(o) for o in np.concatenate([[0], np.cumsum(IN_WIDTHS)]))

LANES = 128
SEQ_BLOCK = 256
PAIR_W = 2 * DK
VMEM_LIMIT = 56 * 1024 * 1024


def _cparams(sem):
    return pltpu.CompilerParams(dimension_semantics=sem, vmem_limit_bytes=VMEM_LIMIT)


def _dot(a, b):
    return jnp.dot(a, b, preferred_element_type=F32)


def _dot_nt(a, b):
    return lax.dot_general(a, b, (((1,), (1,)), ((), ())), preferred_element_type=F32)


def _dot_tn(a, b):
    return lax.dot_general(a, b, (((0,), (0,)), ((), ())), preferred_element_type=F32)


def _split3(a):
    hi = a.astype(BF16)
    r = a - hi.astype(F32)
    mid = r.astype(BF16)
    lo = (r - mid.astype(F32)).astype(BF16)
    return hi, mid, lo


def _split2(a):
    hi = a.astype(BF16)
    lo = (a - hi.astype(F32)).astype(BF16)
    return hi, lo


def _log_sigmoid(x):
    return jnp.minimum(x, 0.0) - jnp.log(1.0 + jnp.exp(-jnp.abs(x)))


def _sigmoid(x):
    return 1.0 / (1.0 + jnp.exp(-x))


def _silu(x):
    return x * _sigmoid(x)


def _gelu_tanh(x):
    return 0.5 * x * (1.0 + jnp.tanh(math.sqrt(2.0 / math.pi) * (x + 0.044715 * (x * x * x))))


def _layer_norm(v, g, b):
    mu = jnp.mean(v, axis=-1, keepdims=True)
    c = v - mu
    var = jnp.mean(c * c, axis=-1, keepdims=True)
    return c * lax.rsqrt(var + LN_EPS) * g + b


def _head_norm(o):
    mu = jnp.mean(o, axis=-1, keepdims=True)
    c = o - mu
    var = jnp.mean(c * c, axis=-1, keepdims=True)
    return c * lax.rsqrt(var + LN_EPS)


def _chunk_tri(n, chunk):
    i = np.arange(n)
    return ((i[:, None] >= i[None, :]) & (i[:, None] // chunk == i[None, :] // chunk)).astype(np.float32)


def _full(shape):
    nd = len(shape)
    return pl.BlockSpec(shape, lambda *_: (0,) * nd, pipeline_mode=pl.Buffered(1))


def _gla_kernel(x_ref, w_ref, b_ref, ng_ref, tri_ref, y_ref, u_ref, st_ref, o_ref):
    @pl.when(pl.program_id(1) == 0)
    def _():
        st_ref[...] = jnp.zeros_like(st_ref)

    hk = HEADS * DK
    xb = x_ref[0].astype(BF16)
    h = _dot(xb, w_ref[...]) + b_ref[...]
    q = h[:, 0:hk]
    k = h[:, hk:2 * hk]
    z = h[:, 2 * hk:3 * hk]
    v = h[:, 3 * hk:3 * hk + BRANCH_W]
    g = h[:, 3 * hk + BRANCH_W:3 * hk + 2 * BRANCH_W]
    u_ref[0] = h[:, 3 * hk + 2 * BRANCH_W:3 * hk + 3 * BRANCH_W].astype(u_ref.dtype)

    la = _log_sigmoid(z) * (1.0 / GLA_TAU)
    tri = tri_ref[...]
    la_h, la_m, la_l = _split3(la)
    cum = _dot(tri, la_h) + _dot(tri, la_m) + _dot(tri, la_l)
    qd = q * (DK ** -0.5) * jnp.exp(cum)
    ki = k * jnp.exp(-cum)
    vb = v.astype(BF16)

    lane = lax.broadcasted_iota(jnp.int32, (1, PAIR_W), 1)
    row_i = lax.broadcasted_iota(jnp.int32, (CHUNK, CHUNK), 0)
    col_i = lax.broadcasted_iota(jnp.int32, (CHUNK, CHUNK), 1)
    causal = row_i >= col_i
    bd_r = lax.broadcasted_iota(jnp.int32, (2 * DV, PAIR_W), 0)
    bd_c = lax.broadcasted_iota(jnp.int32, (2 * DV, PAIR_W), 1)
    blockdiag = (bd_r >= DV) == (bd_c >= DK)

    n_chunks = x_ref.shape[1] // CHUNK
    for p in range(HEADS // 2):
        st = st_ref[p]
        lsl = slice(p * PAIR_W, (p + 1) * PAIR_W)
        for c in range(n_chunks):
            rsl = slice(c * CHUNK, (c + 1) * CHUNK)
            qd_c = qd[rsl, lsl]
            ki_c = ki[rsl, lsl].astype(BF16)
            cum_c = cum[rsl, lsl]
            last = cum_c[CHUNK - 1:CHUNK, :]
            kt = (k[rsl, lsl] * jnp.exp(last - cum_c)).astype(BF16)
            inter = _dot_nt(qd_c.astype(BF16), st.astype(BF16))
            for hh in range(2):
                head = 2 * p + hh
                in_head = (lane >= hh * DK) & (lane < (hh + 1) * DK)
                qm = jnp.where(in_head, qd_c, 0.0).astype(BF16)
                att = jnp.where(causal, _dot_nt(qm, ki_c), 0.0)
                o_h = _dot(att.astype(BF16), vb[rsl, head * DV:(head + 1) * DV])
                o_ref[rsl, head * DV:(head + 1) * DV] = o_h + inter[:, hh * DV:(hh + 1) * DV]
            upd = _dot_tn(vb[rsl, p * 2 * DV:(p + 1) * 2 * DV], kt)
            st = st * jnp.exp(last) + jnp.where(blockdiag, upd, 0.0)
        st_ref[p] = st

    ng = ng_ref[...]
    for head in range(HEADS):
        hsl = slice(head * DV, (head + 1) * DV)
        y = _head_norm(o_ref[:, hsl]) * ng[:, hsl] * _silu(g[:, hsl])
        y_ref[0, :, hsl] = y.astype(y_ref.dtype)


def _gla_mixer(x, w, b, ng, tri):
    bsz, s, d = x.shape
    wcols = w.shape[1]
    return pl.pallas_call(
        _gla_kernel,
        grid=(bsz, s // SEQ_BLOCK),
        in_specs=[
            pl.BlockSpec((1, SEQ_BLOCK, d), lambda i, j: (i, j, 0)),
            _full((d, wcols)), _full((1, wcols)), _full((1, BRANCH_W)),
            _full((SEQ_BLOCK, SEQ_BLOCK)),
        ],
        out_specs=[pl.BlockSpec((1, SEQ_BLOCK, BRANCH_W), lambda i, j: (i, j, 0))] * 2,
        out_shape=[jax.ShapeDtypeStruct((bsz, s, BRANCH_W), BF16)] * 2,
        scratch_shapes=[pltpu.VMEM((HEADS // 2, 2 * DV, PAIR_W), F32),
                        pltpu.VMEM((SEQ_BLOCK, BRANCH_W), F32)],
        compiler_params=_cparams(("arbitrary", "arbitrary")),
        name="gla_mixer",
    )(x, w, b, ng, tri)


ML_ST_ROWS = 2 * DV + LANES


def _ml_kernel(x_ref, w_ref, b_ref, wgc_ref, bgc_ref, wgr_ref, bgr_ref, cw_ref, cb_ref, ng_ref,
               tri_ref, trit_ref, y_ref, ct_ref, m_ref, carry_ref, o_ref):
    @pl.when(pl.program_id(1) == 0)
    def _():
        ct_ref[...] = jnp.zeros_like(ct_ref)
        m_ref[...] = jnp.zeros_like(m_ref)
        carry_ref[...] = jnp.zeros_like(carry_ref)

    hk = HEADS * DK
    lb = x_ref.shape[1]
    x32 = x_ref[0]
    xh, xl = _split2(x32)
    h = _dot(xh, w_ref[...]) + b_ref[...]
    qk = h[:, 0:2 * hk]
    v = h[:, 2 * hk:2 * hk + BRANCH_W]
    o_pre = h[:, 2 * hk + BRANCH_W:2 * hk + 2 * BRANCH_W]

    ext = jnp.concatenate([carry_ref[...], qk], axis=0)
    cw = cw_ref[...]
    acc = cb_ref[...] + ext[8 - (ML_CONV - 1):8 - (ML_CONV - 1) + lb] * cw[0:1]
    for j in range(1, ML_CONV):
        off = 8 - (ML_CONV - 1) + j
        acc = acc + ext[off:off + lb] * cw[j:j + 1]
    carry_ref[...] = qk[lb - 8:lb]
    qkc = _silu(acc)
    qf = qkc[:, 0:hk]
    kf = qkc[:, hk:2 * hk] * (DK ** -0.5)
    vb = v.astype(BF16)

    gc = (_dot(xh, wgc_ref[0]) + _dot(xl, wgc_ref[0]) + _dot(xh, wgc_ref[1])) + bgc_ref[...]
    gr = (_dot_nt(wgr_ref[0], xh) + _dot_nt(wgr_ref[0], xl) + _dot_nt(wgr_ref[1], xh)) + bgr_ref[...]
    lf_c = _log_sigmoid(gc)
    lf_r = _log_sigmoid(gr)
    tri = tri_ref[...]
    trit = trit_ref[...]
    c_h, c_m, c_l = _split3(lf_c)
    bc = _dot(tri, c_h) + _dot(tri, c_m) + _dot(tri, c_l)
    r_h, r_m, r_l = _split3(lf_r)
    br = _dot(r_h, trit) + _dot(r_m, trit) + _dot(r_l, trit)

    lane = lax.broadcasted_iota(jnp.int32, (1, PAIR_W), 1)
    row_i = lax.broadcasted_iota(jnp.int32, (ML_CHUNK, ML_CHUNK), 0)
    col_i = lax.broadcasted_iota(jnp.int32, (ML_CHUNK, ML_CHUNK), 1)
    causal = row_i >= col_i
    sr = lax.broadcasted_iota(jnp.int32, (ML_ST_ROWS, PAIR_W), 0)
    sc_ = lax.broadcasted_iota(jnp.int32, (ML_ST_ROWS, PAIR_W), 1)
    first = sc_ < DK
    rows_h0 = (sr < DV) | (sr == 2 * DV)
    rows_h1 = ((sr >= DV) & (sr < 2 * DV)) | (sr == 2 * DV + 1)
    st_mask = (rows_h0 & first) | (rows_h1 & ~first)
    ones_blk = jnp.ones((ML_CHUNK, LANES), BF16)

    n_chunks = lb // ML_CHUNK
    for p in range(HEADS // 2):
        ct = ct_ref[p]
        lsl = slice(p * PAIR_W, (p + 1) * PAIR_W)
        m_pair = [m_ref[2 * p + hh][0:1, 0:1] for hh in range(2)]
        for c in range(n_chunks):
            rsl = slice(c * ML_CHUNK, (c + 1) * ML_CHUNK)
            q_c = qf[rsl, lsl]
            k_c = kf[rsl, lsl]
            k_cb = k_c.astype(BF16)
            inter_mm = _dot_nt(q_c.astype(BF16), ct.astype(BF16))
            wt_cols, decays = [], []
            for hh in range(2):
                head = 2 * p + hh
                m_st = m_pair[hh]
                b_col = bc[rsl, HEADS + head:HEADS + head + 1]
                i_col = gc[rsl, head:head + 1]
                b_row = br[HEADS + head:HEADS + head + 1, rsl]
                i_row = gr[head:head + 1, rsl]
                dmat = jnp.where(causal, b_col - b_row + i_row, -jnp.inf)
                inter = b_col + m_st
                m_row = jnp.maximum(inter, jnp.max(dmat, axis=-1, keepdims=True))
                wts = jnp.exp(dmat - m_row)
                in_head = (lane >= hh * DK) & (lane < (hh + 1) * DK)
                qm = jnp.where(in_head, q_c, 0.0).astype(BF16)
                sc = _dot_nt(qm, k_cb) * wts
                w_inter = jnp.exp(inter - m_row)
                num = _dot(sc.astype(BF16), vb[rsl, head * DV:(head + 1) * DV]) \
                    + w_inter * inter_mm[:, hh * DV:(hh + 1) * DV]
                den = jnp.sum(sc, axis=-1, keepdims=True) \
                    + w_inter * inter_mm[:, 2 * DV + hh:2 * DV + hh + 1]
                o_ref[rsl, head * DV:(head + 1) * DV] = num / jnp.maximum(jnp.abs(den), jnp.exp(-m_row))
                g_tot = b_col[ML_CHUNK - 1:ML_CHUNK, :]
                tail = g_tot - b_col + i_col
                m_new = jnp.maximum(g_tot + m_st, jnp.max(tail, axis=0, keepdims=True))
                wt_cols.append(jnp.exp(tail - m_new))
                decays.append(jnp.exp(g_tot + m_st - m_new))
                m_pair[hh] = m_new
            wk = (k_c * jnp.where(lane < DK, wt_cols[0], wt_cols[1])).astype(BF16)
            vp = jnp.concatenate([vb[rsl, p * 2 * DV:(p + 1) * 2 * DV], ones_blk], axis=1)
            upd = _dot_tn(vp, wk)
            ct = ct * jnp.where(lane < DK, decays[0], decays[1]) + jnp.where(st_mask, upd, 0.0)
        ct_ref[p] = ct
        for hh in range(2):
            m_ref[2 * p + hh] = jnp.broadcast_to(m_pair[hh], m_ref.shape[1:])

    ng = ng_ref[...]
    for head in range(HEADS):
        hsl = slice(head * DV, (head + 1) * DV)
        y = _head_norm(o_ref[:, hsl]) * ng[:, hsl] * _sigmoid(o_pre[:, hsl])
        y_ref[0, :, hsl] = y.astype(y_ref.dtype)


def _ml_mixer(x, w, b, wgc, bgc, wgr, bgr, cw, cb, ng, tri, trit):
    bsz, s, d = x.shape
    wcols = w.shape[1]
    return pl.pallas_call(
        _ml_kernel,
        grid=(bsz, s // SEQ_BLOCK),
        in_specs=[
            pl.BlockSpec((1, SEQ_BLOCK, d), lambda i, j: (i, j, 0)),
            _full((d, wcols)), _full((1, wcols)),
            _full((2, d, LANES)), _full((1, LANES)),
            _full((2, 8, d)), _full((8, 1)),
            _full((ML_CONV, 2 * HEADS * DK)), _full((1, 2 * HEADS * DK)),
            _full((1, BRANCH_W)),
            _full((SEQ_BLOCK, SEQ_BLOCK)), _full((SEQ_BLOCK, SEQ_BLOCK)),
        ],
        out_specs=pl.BlockSpec((1, SEQ_BLOCK, BRANCH_W), lambda i, j: (i, j, 0)),
        out_shape=jax.ShapeDtypeStruct((bsz, s, BRANCH_W), BF16),
        scratch_shapes=[pltpu.VMEM((HEADS // 2, ML_ST_ROWS, PAIR_W), F32),
                        pltpu.VMEM((HEADS, 8, LANES), F32),
                        pltpu.VMEM((8, 2 * HEADS * DK), F32),
                        pltpu.VMEM((SEQ_BLOCK, BRANCH_W), F32)],
        compiler_params=_cparams(("arbitrary", "arbitrary")),
        name="mlstm_mixer",
    )(x, w, b, wgc, bgc, wgr, bgr, cw, cb, ng, tri, trit)


S5_HALF = 256
S5_HALF_STATE = (S5_HALF // S5_GROUP) * S5_STATE


def _s5_kernel(bsz, nc, u_ref, ec_ref, ftc_ref, c0_ref, d_ref, a_ref, y_ref,
               in_ref, xs_ref, xsb_ref, kexp_ref, c0x_ref):
    s = pl.program_id(1)
    lc = S5_CHUNK
    rows = bsz * nc
    sw = S5_HALF_STATE

    def expand_state(blk):
        tile = jnp.concatenate([blk[:, 0:LANES]] * (sw // LANES) + [blk[:, LANES:2 * LANES]] * (sw // LANES),
                               axis=1)
        r = lax.broadcasted_iota(jnp.int32, tile.shape, 0) // S5_GROUP
        c = (lax.broadcasted_iota(jnp.int32, tile.shape, 1) % sw) // S5_STATE
        return jnp.where(r == c, tile, 0.0).astype(BF16)

    @pl.when(s == 0)
    def _():
        c0x_ref[...] = expand_state(c0_ref[...])
        xs_ref[...] = jnp.zeros_like(xs_ref)

    @pl.when(s < lc)
    def _():
        u = u_ref[...].reshape(rows, u_ref.shape[-1])
        in_ref[s] = u
        et = expand_state(ec_ref[0])
        kblk = _dot_nt(et, c0x_ref[...])
        r = lax.broadcasted_iota(jnp.int32, kblk.shape, 0)
        c = lax.broadcasted_iota(jnp.int32, kblk.shape, 1)
        kblk = kblk + jnp.where((r == c) & (s == lc - 1), d_ref[...], 0.0)
        kexp_ref[pl.ds(pl.multiple_of(s * S5_HALF, S5_HALF), S5_HALF), :] = kblk.astype(BF16)
        for jc in range(2 * sw // S5_HALF):
            part = _dot(u, et[:, jc * S5_HALF:(jc + 1) * S5_HALF])
            for jj in range(S5_HALF // LANES):
                xs_ref[jc * (S5_HALF // LANES) + jj] += part[:, jj * LANES:(jj + 1) * LANES]

    @pl.when(s == lc - 1)
    def _():
        ar = a_ref[0, 0:1, :]
        ai = a_ref[0, 1:2, :]
        nt = sw // LANES

        def body(c, carry):
            sr, si = carry
            idx = pl.ds(c, bsz, stride=nc)
            xr = jnp.concatenate([xs_ref[j, idx, :] for j in range(nt)], axis=1)
            xi = jnp.concatenate([xs_ref[nt + j, idx, :] for j in range(nt)], axis=1)
            for j in range(nt):
                xs_ref[j, idx, :] = sr[:, j * LANES:(j + 1) * LANES]
                xs_ref[nt + j, idx, :] = si[:, j * LANES:(j + 1) * LANES]
            return ar * sr - ai * si + xr, ar * si + ai * sr + xi

        zero = jnp.zeros((bsz, sw), F32)
        lax.fori_loop(0, nc, body, (zero, zero))
        for j in range(2 * nt):
            xsb_ref[:, j * LANES:(j + 1) * LANES] = xs_ref[j].astype(BF16)

    for lo in range(lc):
        @pl.when(s == lc + lo)
        def _(lo=lo):
            acc = _dot_nt(xsb_ref[...], expand_state(ftc_ref[0]))
            for l in range(lo + 1):
                j = lc - 1 - lo + l
                acc = acc + _dot(in_ref[l], kexp_ref[j * S5_HALF:(j + 1) * S5_HALF, :])
            y_ref[...] = acc.reshape(y_ref.shape).astype(y_ref.dtype)


def _s5_mixer(u, ec, ftc, c0, d, a):
    bsz, s, _ = u.shape
    lc = S5_CHUNK
    nc = s // lc
    u3 = u.reshape(bsz, nc, lc * BRANCH_W)
    nh = BRANCH_W // S5_HALF
    y = pl.pallas_call(
        functools.partial(_s5_kernel, bsz, nc),
        grid=(nh, 2 * lc),
        in_specs=[
            pl.BlockSpec((bsz, nc, S5_HALF), lambda h, t: (0, 0, jnp.minimum(t, lc - 1) * nh + h)),
            pl.BlockSpec((1, S5_HALF, 2 * LANES), lambda h, t: (jnp.minimum(t, lc - 1), h, 0)),
            pl.BlockSpec((1, S5_HALF, 2 * LANES), lambda h, t: (jnp.maximum(t - lc, 0), h, 0)),
            pl.BlockSpec((S5_HALF, 2 * LANES), lambda h, t: (h, 0)),
            pl.BlockSpec((1, S5_HALF), lambda h, t: (0, h)),
            pl.BlockSpec((1, 2, S5_HALF_STATE), lambda h, t: (h, 0, 0)),
        ],
        out_specs=pl.BlockSpec((bsz, nc, S5_HALF), lambda h, t: (0, 0, jnp.maximum(t - lc, 0) * nh + h)),
        out_shape=jax.ShapeDtypeStruct((bsz, nc, lc * BRANCH_W), BF16),
        scratch_shapes=[pltpu.VMEM((lc, bsz * nc, S5_HALF), BF16),
                        pltpu.VMEM((2 * S5_HALF_STATE // LANES, bsz * nc, LANES), F32),
                        pltpu.VMEM((bsz * nc, 2 * S5_HALF_STATE), BF16),
                        pltpu.VMEM((lc * S5_HALF, S5_HALF), BF16),
                        pltpu.VMEM((S5_HALF, 2 * S5_HALF_STATE), BF16)],
        compiler_params=_cparams(("arbitrary", "arbitrary")),
        name="s5_mixer",
    )(u3, ec, ftc, c0, d, a)
    return y.reshape(bsz * s, BRANCH_W)


def _s5_tables(a_re, a_im, log_dt, b_re, b_im, c_re, c_im):
    nl = a_re.shape[0]
    g, p_, n, lc = S5_GROUPS, S5_STATE, S5_GROUP, S5_CHUNK
    a_re, a_im, b_re, b_im, c_re, c_im = (v.astype(F32) for v in (a_re, a_im, b_re, b_im, c_re, c_im))
    dt = jnp.exp(log_dt.astype(F32))[..., None]
    adt_r = (a_re * dt)[:, None, :, None, :]
    adt_i = (a_im * dt)[:, None, :, None, :]

    def powers(steps):
        st = jnp.asarray(np.asarray(steps, np.float32).reshape(1, -1, 1, 1, 1))
        mag = jnp.exp(adt_r * st)
        return mag * jnp.cos(adt_i * st), mag * jnp.sin(adt_i * st)

    lbr, lbi = powers([1])
    are, aim = a_re[:, None, :, None, :], a_im[:, None, :, None, :]
    den = are * are + aim * aim
    qr = ((lbr - 1.0) * are + lbi * aim) / den
    qi = (lbi * are - (lbr - 1.0) * aim) / den
    bt_re, bt_im = b_re.transpose(0, 1, 3, 2)[:, None], b_im.transpose(0, 1, 3, 2)[:, None]
    bbr = qr * bt_re - qi * bt_im
    bbi = qr * bt_im + qi * bt_re
    cr, ci = c_re[:, None], c_im[:, None]
    pr, pi = powers(np.arange(lc - 1, -1, -1))
    er, ei = pr * bbr - pi * bbi, pr * bbi + pi * bbr
    ec = jnp.concatenate([er, er, ei, ei], axis=-1).reshape(nl, lc, g * n, 4 * p_)
    pfr, pfi = powers(np.arange(1, lc + 1))
    fr, fi = cr * pfr - ci * pfi, -(cr * pfi + ci * pfr)
    ftc = jnp.concatenate([fr, fr, fi, fi], axis=-1).reshape(nl, lc, g * n, 4 * p_)
    c0 = jnp.concatenate([c_re, c_re, -c_im, -c_im], axis=-1).reshape(nl, g * n, 4 * p_)
    nh = BRANCH_W // S5_HALF
    ar, ai = powers([lc])
    a_tab = jnp.stack([ar.reshape(nl, nh, S5_HALF_STATE), ai.reshape(nl, nh, S5_HALF_STATE)], axis=2)
    return ec, ftc, c0, a_tab


def _merge_kernel(x_ref, yg_ref, y5_ref, ym_ref, wgate_ref, bgate_ref, wglu_ref, bglu_ref,
                  wup_ref, wo_ref, g_ref, b_ref, o_ref):
    x = x_ref[...]
    xb = x.astype(BF16)
    y5 = _gelu_tanh(y5_ref[...].astype(F32))
    y5 = y5 * _sigmoid(_dot(y5.astype(BF16), wglu_ref[...]) + bglu_ref[...])
    ys = (yg_ref[...], y5.astype(BF16), ym_ref[...])
    acc = None
    for r in range(N_BRANCH):
        gate = _sigmoid(_dot(xb, wgate_ref[:, r * D_MODEL:(r + 1) * D_MODEL])
                        + bgate_ref[:, r * D_MODEL:(r + 1) * D_MODEL])
        term = gate * _dot(ys[r], wup_ref[r])
        acc = term if acc is None else acc + term
    mix = _dot(acc.astype(BF16), wo_ref[...])
    o_ref[...] = _layer_norm(DN_ALPHA * x + mix, g_ref[...], b_ref[...])


def _merge(x2, yg, y5, ym, wgate, bgate, wglu, bglu, wup, wo, g, b, tm=512):
    t, dm = x2.shape
    row = lambda w: pl.BlockSpec((tm, w), lambda i: (i, 0))
    return pl.pallas_call(
        _merge_kernel,
        grid=(t // tm,),
        in_specs=[row(dm), row(BRANCH_W), row(BRANCH_W), row(BRANCH_W),
                  _full(wgate.shape), _full(bgate.shape), _full(wglu.shape), _full(bglu.shape),
                  _full(wup.shape), _full(wo.shape), _full(g.shape), _full(b.shape)],
        out_specs=row(dm),
        out_shape=jax.ShapeDtypeStruct((t, dm), F32),
        compiler_params=_cparams(("parallel",)),
        name="merge_ln1",
    )(x2, yg, y5, ym, wgate, bgate, wglu, bglu, wup, wo, g, b)


def _ple_ln2(x, xb, f, p_ref, pwg_ref, pwp_ref, g_ref, b_ref):
    e = _sigmoid(_dot(xb, pwg_ref[...])) * _dot(p_ref[...].astype(BF16), pwp_ref[...])
    return _layer_norm(DN_ALPHA * x + f + e, g_ref[...], b_ref[...])


def _ffn_kernel(x_ref, p_ref, wg_ref, wu_ref, wd_ref, pwg_ref, pwp_ref, g_ref, b_ref, o_ref):
    x = x_ref[...]
    xb = x.astype(BF16)
    hid = (_silu(_dot(xb, wg_ref[...])) * _dot(xb, wu_ref[...])).astype(BF16)
    f = _dot(hid, wd_ref[...])
    o_ref[...] = _ple_ln2(x, xb, f, p_ref, pwg_ref, pwp_ref, g_ref, b_ref)


def _ffn_layer(x2, p3, layer, wg, wu, wd, pwg, pwp, g, b, tm=256):
    t, dm = x2.shape
    row = lambda w: pl.BlockSpec((tm, w), lambda i: (i, 0))
    return pl.pallas_call(
        _ffn_kernel,
        grid=(t // tm,),
        in_specs=[row(dm), pl.BlockSpec((None, tm, PLE_DIM), lambda i: (layer, i, 0)),
                  _full(wg.shape), _full(wu.shape), _full(wd.shape),
                  _full(pwg.shape), _full(pwp.shape), _full(g.shape), _full(b.shape)],
        out_specs=row(dm),
        out_shape=jax.ShapeDtypeStruct((t, dm), F32),
        compiler_params=_cparams(("parallel",)),
        name="ffn_ple_ln2",
    )(x2, p3, wg, wu, wd, pwg, pwp, g, b)


def _router_kernel(x_ref, w_ref, b_ref, tril_ref, sel_ref, pr_ref, rk_ref, cnt_ref, base_ref):
    @pl.when(pl.program_id(0) == 0)
    def _():
        base_ref[...] = jnp.zeros_like(base_ref)

    xh, xl = _split2(x_ref[...])
    logits = _dot(xh, w_ref[0]) + _dot(xl, w_ref[0]) + _dot(xh, w_ref[1]) + b_ref[...]
    lane = lax.broadcasted_iota(jnp.int32, logits.shape, 1)
    neg = -jnp.inf
    logits = jnp.where(lane < N_EXPERTS, logits, neg)
    m1 = jnp.max(logits, axis=-1, keepdims=True)
    i1 = jnp.min(jnp.where(logits == m1, lane, LANES), axis=-1, keepdims=True)
    rest = jnp.where(lane == i1, neg, logits)
    m2 = jnp.max(rest, axis=-1, keepdims=True)
    i2 = jnp.min(jnp.where(rest == m2, lane, LANES), axis=-1, keepdims=True)
    e2 = jnp.exp(m2 - m1)
    p1 = 1.0 / (1.0 + e2)
    p2 = e2 / (1.0 + e2)
    hot = jnp.where((lane == i1) | (lane == i2), 1.0, 0.0)
    base = base_ref[0:1, :]
    before = _dot(tril_ref[...], hot.astype(BF16)) + base
    r1 = jnp.sum(jnp.where(lane == i1, before, 0.0), axis=-1, keepdims=True)
    r2 = jnp.sum(jnp.where(lane == i2, before, 0.0), axis=-1, keepdims=True)
    sel_ref[...] = jnp.where(lane == 0, i1, jnp.where(lane == 1, i2, 0))
    pr_ref[...] = jnp.where(lane == 0, p1, jnp.where(lane == 1, p2, 0.0))
    rk_ref[...] = jnp.where(lane == 0, r1, jnp.where(lane == 1, r2, 0.0))
    total = base + jnp.sum(hot, axis=0, keepdims=True)
    base_ref[...] = jnp.broadcast_to(total, base_ref.shape)
    cnt_ref[...] = jnp.broadcast_to(total, cnt_ref.shape)


def _router(x2, w, b, tm=512):
    t, dm = x2.shape
    i = np.arange(tm)
    tril = jnp.asarray((i[:, None] > i[None, :]).astype(np.float32), BF16)
    blk = pl.BlockSpec((tm, LANES), lambda i: (i, 0))
    return pl.pallas_call(
        _router_kernel,
        grid=(t // tm,),
        in_specs=[pl.BlockSpec((tm, dm), lambda i: (i, 0)), _full(w.shape), _full(b.shape), _full((tm, tm))],
        out_specs=[blk, blk, blk, pl.BlockSpec((8, LANES), lambda i: (0, 0))],
        out_shape=[jax.ShapeDtypeStruct((t, LANES), jnp.int32), jax.ShapeDtypeStruct((t, LANES), F32),
                   jax.ShapeDtypeStruct((t, LANES), F32), jax.ShapeDtypeStruct((8, LANES), F32)],
        scratch_shapes=[pltpu.VMEM((8, LANES), F32)],
        compiler_params=_cparams(("arbitrary",)),
        name="moe_router",
    )(x2, w, b, tril)


TOP_K = 2
MOE_TILE = 1024
MOE_SUB = 256
MOE_FF_TILE = 512
DISPATCH_TILE = 512


def _row_copy_wait(src_rows, dst_rows, sem):
    pltpu.make_async_copy(src_rows, dst_rows, sem).wait()


def _dispatch_kernel(pos_ref, x_ref, xs_ref, sem):
    tm = x_ref.shape[0]
    base = pl.program_id(0) * tm * TOP_K

    def body(r, carry):
        for k in range(TOP_K):
            dst = pos_ref[base + r * TOP_K + k]
            pltpu.make_async_copy(x_ref.at[pl.ds(r, 1), :], xs_ref.at[pl.ds(dst, 1), :], sem).start()
        return carry

    lax.fori_loop(0, tm, body, 0)
    for _ in range(TOP_K):
        _row_copy_wait(x_ref, xs_ref.at[pl.ds(0, tm), :], sem)


def _dispatch(pos_flat, x2, tm=DISPATCH_TILE):
    t, dm = x2.shape
    return pl.pallas_call(
        _dispatch_kernel,
        grid_spec=pltpu.PrefetchScalarGridSpec(
            num_scalar_prefetch=1,
            grid=(t // tm,),
            in_specs=[pl.BlockSpec((tm, dm), lambda i, pos: (i, 0))],
            out_specs=pl.BlockSpec(memory_space=pl.ANY),
            scratch_shapes=[pltpu.SemaphoreType.DMA],
        ),
        out_shape=jax.ShapeDtypeStruct((t * TOP_K, dm), F32),
        compiler_params=_cparams(("arbitrary",)),
        name="moe_dispatch",
    )(pos_flat, x2)


def _moe_group_kernel(tile_ref, exp_ref, lo_ref, hi_ref, xs_ref, wg_ref, wu_ref, wd_ref, o_ref,
                      acc_ref, xb_ref):
    i = pl.program_id(0)
    f = pl.program_id(1)
    lo = lo_ref[i]
    hi = hi_ref[i]
    tm = xs_ref.shape[0]

    @pl.when(hi > lo)
    def _():
        @pl.when((lo == 0) & (f == 0))
        def _():
            acc_ref[...] = jnp.zeros_like(acc_ref)

        @pl.when(f == 0)
        def _():
            xb_ref[...] = xs_ref[...].astype(BF16)

        wg = wg_ref[0].astype(BF16)
        wu = wu_ref[0].astype(BF16)
        wd = wd_ref[0].astype(BF16)
        for sub in range(tm // MOE_SUB):
            rsl = slice(sub * MOE_SUB, (sub + 1) * MOE_SUB)
            xb = xb_ref[rsl, :]
            hid = _silu(_dot(xb, wg)) * _dot(xb, wu)
            rid = lax.broadcasted_iota(jnp.int32, (MOE_SUB, 1), 0) + sub * MOE_SUB
            hid = jnp.where((rid >= lo) & (rid < hi), hid, 0.0).astype(BF16)
            acc_ref[rsl, :] += _dot(hid, wd)

        @pl.when((hi == tm) & (f == pl.num_programs(1) - 1))
        def _():
            o_ref[...] = acc_ref[...]


def _moe_grouped(items, xs, wg, wu, wd, tm=MOE_TILE, tf=MOE_FF_TILE):
    tile, exp, lo, hi = items
    rows, dm = xs.shape
    dff = wg.shape[2]
    return pl.pallas_call(
        _moe_group_kernel,
        grid_spec=pltpu.PrefetchScalarGridSpec(
            num_scalar_prefetch=4,
            grid=(tile.shape[0], dff // tf),
            in_specs=[pl.BlockSpec((tm, dm), lambda i, f, tl, ex, lo_, hi_: (tl[i], 0)),
                      pl.BlockSpec((1, dm, tf), lambda i, f, tl, ex, lo_, hi_: (ex[i], 0, f)),
                      pl.BlockSpec((1, dm, tf), lambda i, f, tl, ex, lo_, hi_: (ex[i], 0, f)),
                      pl.BlockSpec((1, tf, dm), lambda i, f, tl, ex, lo_, hi_: (ex[i], f, 0))],
            out_specs=pl.BlockSpec((tm, dm), lambda i, f, tl, ex, lo_, hi_: (tl[i], 0)),
            scratch_shapes=[pltpu.VMEM((tm, dm), F32), pltpu.VMEM((tm, dm), BF16)],
        ),
        out_shape=jax.ShapeDtypeStruct((rows, dm), F32),
        compiler_params=_cparams(("arbitrary", "arbitrary")),
        name="moe_grouped",
    )(tile, exp, lo, hi, xs, wg, wu, wd)


def _moe_items(counts, n_rows, tm=MOE_TILE):
    n_tiles = n_rows // tm
    ends = jnp.cumsum(counts)
    cuts = jnp.sort(jnp.concatenate([jnp.arange(n_tiles + 1, dtype=jnp.int32) * tm, ends[:-1]]))
    start, stop = cuts[:-1], cuts[1:]
    tile = jnp.minimum(start // tm, n_tiles - 1)
    exp = jnp.minimum(jnp.sum(ends[None, :] <= start[:, None], axis=1), N_EXPERTS - 1)
    lo = start - tile * tm
    hi = stop - tile * tm
    return tile.astype(jnp.int32), exp.astype(jnp.int32), lo.astype(jnp.int32), hi.astype(jnp.int32)


def _combine_kernel(pos_ref, x_ref, pr_ref, p_ref, pwg_ref, pwp_ref, g_ref, b_ref, ys_ref, o_ref, gat_ref, sem):
    tm = x_ref.shape[0]
    base = pl.program_id(0) * tm * TOP_K

    def body(r, carry):
        for k in range(TOP_K):
            src = pos_ref[base + r * TOP_K + k]
            pltpu.make_async_copy(ys_ref.at[pl.ds(src, 1), :], gat_ref.at[k, pl.ds(r, 1), :], sem).start()
        return carry

    lax.fori_loop(0, tm, body, 0)
    x = x_ref[...]
    xb = x.astype(BF16)
    e = _sigmoid(_dot(xb, pwg_ref[...])) * _dot(p_ref[...].astype(BF16), pwp_ref[...])
    for k in range(TOP_K):
        _row_copy_wait(ys_ref.at[pl.ds(0, tm), :], gat_ref.at[k], sem)
    pr = pr_ref[...]
    f = pr[:, 0:1] * gat_ref[0]
    for k in range(1, TOP_K):
        f = f + pr[:, k:k + 1] * gat_ref[k]
    o_ref[...] = _layer_norm(DN_ALPHA * x + f + e, g_ref[...], b_ref[...])


def _combine_layer(pos_flat, x2, pr, p3, layer, pwg, pwp, g, b, ys, tm=DISPATCH_TILE):
    t, dm = x2.shape
    row = lambda w: pl.BlockSpec((tm, w), lambda i, pos: (i, 0))
    full = lambda a: pl.BlockSpec(a.shape, lambda i, pos: (0,) * a.ndim, pipeline_mode=pl.Buffered(1))
    return pl.pallas_call(
        _combine_kernel,
        grid_spec=pltpu.PrefetchScalarGridSpec(
            num_scalar_prefetch=1,
            grid=(t // tm,),
            in_specs=[row(dm), row(LANES), pl.BlockSpec((None, tm, PLE_DIM), lambda i, pos: (layer, i, 0)),
                      full(pwg), full(pwp), full(g), full(b),
                      pl.BlockSpec(memory_space=pl.ANY)],
            out_specs=row(dm),
            scratch_shapes=[pltpu.VMEM((TOP_K, tm, dm), F32), pltpu.SemaphoreType.DMA],
        ),
        out_shape=jax.ShapeDtypeStruct((t, dm), F32),
        compiler_params=_cparams(("arbitrary",)),
        name="moe_combine_ple_ln2",
    )(pos_flat, x2, pr, p3, pwg, pwp, g, b, ys)


def _row(v):
    return v.reshape(1, -1).astype(F32)


def _pad_lanes(w):
    return jnp.pad(w, ((0, 0), (0, LANES - w.shape[1])))


def kernel(x, p, w_in, b_in, gla_w_a2, gla_b_a2, gla_norm_g, s5_a_re, s5_a_im, s5_log_dt, s5_b_re, s5_b_im,
           s5_c_re, s5_c_im, s5_d, s5_w_glu, s5_b_glu, ml_conv_w, ml_conv_b, ml_norm_g, w_up, w_o, ln1_g, ln1_b,
           ffn_wg, ffn_wu, ffn_wd, moe_router, moe_router_b, moe_wg, moe_wu, moe_wd, ple_w_gate, ple_w_proj,
           ln2_g, ln2_b):
    bsz, s, dm = x.shape
    t = bsz * s
    hi = lax.Precision.HIGHEST
    o = IN_OFF
    tri = jnp.asarray(_chunk_tri(SEQ_BLOCK, CHUNK), BF16)
    ml_tri_np = _chunk_tri(SEQ_BLOCK, ML_CHUNK)
    ml_tri = jnp.asarray(ml_tri_np, BF16)
    ml_trit = jnp.asarray(ml_tri_np.T, BF16)
    s5_ec, s5_ftc, s5_c0, s5_atab = _s5_tables(s5_a_re, s5_a_im, s5_log_dt, s5_b_re, s5_b_im, s5_c_re, s5_c_im)
    p3 = p.reshape(DEPTH, t, PLE_DIM)

    w_in_b = w_in.astype(BF16)
    for i in range(DEPTH):
        w, b = w_in_b[i], b_in[i]
        sl = lambda k: (w[:, o[k]:o[k + 1]], b[o[k]:o[k + 1]])
        (wq, bq), (wk, bk), (wv, bv), (_, ba), (wg_, bg_) = sl(0), sl(1), sl(2), sl(3), sl(4)
        wz = jnp.dot(w_in[i, :, o[3]:o[4]], gla_w_a2[i], precision=hi).astype(BF16)
        bz = jnp.dot(ba, gla_w_a2[i], precision=hi) + gla_b_a2[i]
        (wu_, bu_) = sl(5)
        w_gla = jnp.concatenate([wq, wk, wz, wv, wg_, wu_], axis=1)
        b_gla = _row(jnp.concatenate([bq, bk, bz, bv, bg_, bu_]))
        y_gla, u = _gla_mixer(x, w_gla, b_gla, _row(gla_norm_g[i]), tri)

        x2 = x.reshape(t, dm)
        y5 = _s5_mixer(u, s5_ec[i], s5_ftc[i], s5_c0[i], _row(s5_d[i]), s5_atab[i])

        (wmq, bmq), (wmk, bmk), (wmv, bmv), (wmo, bmo) = sl(6), sl(7), sl(8), sl(11)
        w_ml = jnp.concatenate([wmq, wmk, wmv, wmo], axis=1)
        b_ml = _row(jnp.concatenate([bmq, bmk, bmv, bmo]))
        w_if = w_in[i, :, o[9]:o[11]]
        b_if = b[o[9]:o[11]]
        if_h = w_if.astype(BF16)
        if_l = (w_if - if_h.astype(F32)).astype(BF16)
        wgc = jnp.stack([_pad_lanes(if_h), _pad_lanes(if_l)])
        wgr = jnp.stack([if_h.T, if_l.T])
        y_ml = _ml_mixer(x, w_ml, b_ml, wgc, _pad_lanes(_row(b_if)), wgr, b_if.reshape(-1, 1).astype(F32),
                         ml_conv_w[i].astype(F32), _row(ml_conv_b[i]), _row(ml_norm_g[i]), ml_tri, ml_trit)

        (wgt, bgt) = sl(12)
        x1 = _merge(x2, y_gla.reshape(t, BRANCH_W), y5, y_ml.reshape(t, BRANCH_W),
                    wgt, _row(bgt), s5_w_glu[i].astype(BF16), _row(s5_b_glu[i]),
                    w_up[i].astype(BF16), w_o[i].astype(BF16), _row(ln1_g[i]), _row(ln1_b[i]))

        pwg = ple_w_gate[i].astype(BF16)
        pwp = ple_w_proj[i].astype(BF16)
        j = i // 2
        if i % 2 == 0:
            x2n = _ffn_layer(x1, p3, i, ffn_wg[j].astype(BF16), ffn_wu[j].astype(BF16), ffn_wd[j].astype(BF16),
                             pwg, pwp, _row(ln2_g[i]), _row(ln2_b[i]))
        else:
            wr = moe_router[j]
            wr_h = wr.astype(BF16)
            wr_l = (wr - wr_h.astype(F32)).astype(BF16)
            sel, pr, rk, cnt = _router(x1, jnp.stack([_pad_lanes(wr_h), _pad_lanes(wr_l)]),
                                       _pad_lanes(_row(moe_router_b[j])))
            counts = cnt[0, :N_EXPERTS].astype(jnp.int32)
            starts = jnp.cumsum(counts) - counts
            sel2 = sel[:, :TOP_K]
            pos = rk[:, :TOP_K].astype(jnp.int32) + jnp.sum(
                jnp.where(sel2[..., None] == jnp.arange(N_EXPERTS), starts, 0), axis=-1)
            pos_flat = pos.reshape(-1)
            xs = _dispatch(pos_flat, x1)
            ys = _moe_grouped(_moe_items(counts, t * TOP_K), xs, moe_wg[j], moe_wu[j], moe_wd[j])
            x2n = _combine_layer(pos_flat, x1, pr, p3, i, pwg, pwp, _row(ln2_g[i]), _row(ln2_b[i]), ys)
        x = x2n.reshape(bsz, s, dm)
    return x
```

```python
import functools
import math

import numpy as np
import jax
import jax.numpy as jnp
from jax import lax
from jax.experimental import pallas as pl
from jax.experimental.pallas import tpu as pltpu

F32 = jnp.float32
BF16 = jnp.bfloat16

D_MODEL = 1024
DEPTH = 2
N_BRANCH = 3
BRANCH_W = 512
HEADS = 4
DK = 64
DV = BRANCH_W // HEADS
GLA_RANK = 16
GLA_TAU = 16.0
CHUNK = 64
ML_CHUNK = 256
S5_GROUP = 16
S5_GROUPS = BRANCH_W // S5_GROUP
S5_STATE = 64
S5_CHUNK = 16
ML_CONV = 4
D_FF = 2816
N_EXPERTS = 8
D_FF_EXPERT = 3584
PLE_DIM = 256
DN_ALPHA = (2.0 * DEPTH) ** 0.25
LN_EPS = 1e-5

IN_WIDTHS = (
    HEADS * DK, HEADS * DK, BRANCH_W, GLA_RANK, BRANCH_W,
    BRANCH_W,
    HEADS * DK, HEADS * DK, BRANCH_W, HEADS, HEADS, BRANCH_W,
    N_BRANCH * D_MODEL,
)
IN_OFF = tuple(int(o) for o in np.concatenate([[0], np.cumsum(IN_WIDTHS)]))

LANES = 128
SEQ_BLOCK = 256
SEQ_PER_STEP = 2
PAIR_W = 2 * DK
VMEM_LIMIT = 56 * 1024 * 1024


def _cparams(sem):
    return pltpu.CompilerParams(dimension_semantics=sem, vmem_limit_bytes=VMEM_LIMIT)


def _dot(a, b):
    return jnp.dot(a, b, preferred_element_type=F32)


def _dot_nt(a, b):
    return lax.dot_general(a, b, (((1,), (1,)), ((), ())), preferred_element_type=F32)


def _dot_tn(a, b):
    return lax.dot_general(a, b, (((0,), (0,)), ((), ())), preferred_element_type=F32)


def _split3(a):
    hi = a.astype(BF16)
    r = a - hi.astype(F32)
    mid = r.astype(BF16)
    lo = (r - mid.astype(F32)).astype(BF16)
    return hi, mid, lo


def _split2(a):
    hi = a.astype(BF16)
    lo = (a - hi.astype(F32)).astype(BF16)
    return hi, lo


def _log_sigmoid(x):
    return jnp.minimum(x, 0.0) - jnp.log(1.0 + jnp.exp(-jnp.abs(x)))


def _sigmoid(x):
    return 0.5 * jnp.tanh(0.5 * x) + 0.5


def _silu(x):
    return x * _sigmoid(x)


def _gelu_tanh(x):
    return 0.5 * x * (1.0 + jnp.tanh(math.sqrt(2.0 / math.pi) * (x + 0.044715 * (x * x * x))))


def _layer_norm(v, g, b):
    mu = jnp.mean(v, axis=-1, keepdims=True)
    c = v - mu
    var = jnp.mean(c * c, axis=-1, keepdims=True)
    return c * lax.rsqrt(var + LN_EPS) * g + b


def _head_norm(o):
    mu = jnp.mean(o, axis=-1, keepdims=True)
    c = o - mu
    var = jnp.mean(c * c, axis=-1, keepdims=True)
    return c * lax.rsqrt(var + LN_EPS)


def _chunk_tri(n, chunk):
    i = np.arange(n)
    return ((i[:, None] >= i[None, :]) & (i[:, None] // chunk == i[None, :] // chunk)).astype(np.float32)


def _full(shape):
    nd = len(shape)
    return pl.BlockSpec(shape, lambda *_: (0,) * nd, pipeline_mode=pl.Buffered(1))


def _gla_kernel(x_ref, w_ref, b_ref, ng_ref, tri_ref, y_ref, u_ref, st_ref, o_ref):
    @pl.when(pl.program_id(1) == 0)
    def _():
        st_ref[...] = jnp.zeros_like(st_ref)

    hk = HEADS * DK
    nb, lb, d = x_ref.shape
    xb = x_ref[...].reshape(nb * lb, d).astype(BF16)
    h = _dot(xb, w_ref[...]) + b_ref[...]
    q = h[:, 0:hk]
    k = h[:, hk:2 * hk]
    z = h[:, 2 * hk:3 * hk]
    v = h[:, 3 * hk:3 * hk + BRANCH_W]
    g = h[:, 3 * hk + BRANCH_W:3 * hk + 2 * BRANCH_W]
    u_ref[...] = h[:, 3 * hk + 2 * BRANCH_W:3 * hk + 3 * BRANCH_W].reshape(nb, lb, BRANCH_W).astype(u_ref.dtype)

    la = _log_sigmoid(z) * (1.0 / GLA_TAU)
    tri = tri_ref[...]
    la_h, la_m, la_l = _split3(la)
    cum = jnp.concatenate(
        [_dot(tri, la_h[r:r + lb]) + _dot(tri, la_m[r:r + lb]) + _dot(tri, la_l[r:r + lb])
         for r in range(0, nb * lb, lb)], axis=0)
    qd = q * (DK ** -0.5) * jnp.exp(cum)
    ki = k * jnp.exp(-cum)
    vb = v.astype(BF16)

    lane = lax.broadcasted_iota(jnp.int32, (1, PAIR_W), 1)
    row_i = lax.broadcasted_iota(jnp.int32, (CHUNK, CHUNK), 0)
    col_i = lax.broadcasted_iota(jnp.int32, (CHUNK, CHUNK), 1)
    causal = row_i >= col_i
    bd_r = lax.broadcasted_iota(jnp.int32, (2 * DV, PAIR_W), 0)
    bd_c = lax.broadcasted_iota(jnp.int32, (2 * DV, PAIR_W), 1)
    blockdiag = (bd_r >= DV) == (bd_c >= DK)

    for bb, p in [(bb, p) for bb in range(nb) for p in range(HEADS // 2)]:
        st = st_ref[bb, p]
        lsl = slice(p * PAIR_W, (p + 1) * PAIR_W)
        for c in range(lb // CHUNK):
            rsl = slice(bb * lb + c * CHUNK, bb * lb + (c + 1) * CHUNK)
            qd_c = qd[rsl, lsl]
            ki_c = ki[rsl, lsl].astype(BF16)
            cum_c = cum[rsl, lsl]
            last = cum_c[CHUNK - 1:CHUNK, :]
            kt = (k[rsl, lsl] * jnp.exp(last - cum_c)).astype(BF16)
            inter = _dot_nt(qd_c.astype(BF16), st.astype(BF16))
            for hh in range(2):
                head = 2 * p + hh
                in_head = (lane >= hh * DK) & (lane < (hh + 1) * DK)
                qm = jnp.where(in_head, qd_c, 0.0).astype(BF16)
                att = jnp.where(causal, _dot_nt(qm, ki_c), 0.0)
                o_h = _dot(att.astype(BF16), vb[rsl, head * DV:(head + 1) * DV])
                o_ref[rsl, head * DV:(head + 1) * DV] = o_h + inter[:, hh * DV:(hh + 1) * DV]
            upd = _dot_tn(vb[rsl, p * 2 * DV:(p + 1) * 2 * DV], kt)
            st = st * jnp.exp(last) + jnp.where(blockdiag, upd, 0.0)
        st_ref[bb, p] = st

    ng = ng_ref[...]
    for head in range(HEADS):
        hsl = slice(head * DV, (head + 1) * DV)
        y = _head_norm(o_ref[:, hsl]) * ng[:, hsl] * _silu(g[:, hsl])
        y_ref[:, :, hsl] = y.reshape(nb, lb, DV).astype(y_ref.dtype)


def _gla_mixer(x, w, b, ng, tri):
    bsz, s, d = x.shape
    wcols = w.shape[1]
    nb = SEQ_PER_STEP
    return pl.pallas_call(
        _gla_kernel,
        grid=(bsz // nb, s // SEQ_BLOCK),
        in_specs=[
            pl.BlockSpec((nb, SEQ_BLOCK, d), lambda i, j: (i, j, 0)),
            _full((d, wcols)), _full((1, wcols)), _full((1, BRANCH_W)),
            _full((SEQ_BLOCK, SEQ_BLOCK)),
        ],
        out_specs=[pl.BlockSpec((nb, SEQ_BLOCK, BRANCH_W), lambda i, j: (i, j, 0))] * 2,
        out_shape=[jax.ShapeDtypeStruct((bsz, s, BRANCH_W), BF16)] * 2,
        scratch_shapes=[pltpu.VMEM((nb, HEADS // 2, 2 * DV, PAIR_W), F32),
                        pltpu.VMEM((nb * SEQ_BLOCK, BRANCH_W), F32)],
        compiler_params=_cparams(("arbitrary", "arbitrary")),
        name="gla_mixer",
    )(x, w, b, ng, tri)


ML_ST_ROWS = 2 * DV + LANES


def _ml_kernel(x_ref, w_ref, b_ref, wgc_ref, bgc_ref, wgr_ref, bgr_ref, cw_ref, cb_ref, ng_ref,
               tri_ref, trit_ref, y_ref, ct_ref, m_ref, carry_ref, o_ref):
    @pl.when(pl.program_id(1) == 0)
    def _():
        ct_ref[...] = jnp.zeros_like(ct_ref)
        m_ref[...] = jnp.zeros_like(m_ref)
        carry_ref[...] = jnp.zeros_like(carry_ref)

    hk = HEADS * DK
    nb, lb, d = x_ref.shape
    x32 = x_ref[...].reshape(nb * lb, d)
    xh, xl = _split2(x32)
    h = _dot(xh, w_ref[...]) + b_ref[...]
    qk = h[:, 0:2 * hk]
    v = h[:, 2 * hk:2 * hk + BRANCH_W]
    o_pre = h[:, 2 * hk + BRANCH_W:2 * hk + 2 * BRANCH_W]

    cw = cw_ref[...]
    conv = []
    for bb in range(nb):
        qk_b = qk[bb * lb:(bb + 1) * lb]
        ext = jnp.concatenate([carry_ref[bb], qk_b], axis=0)
        acc = cb_ref[...] + ext[8 - (ML_CONV - 1):8 - (ML_CONV - 1) + lb] * cw[0:1]
        for j in range(1, ML_CONV):
            off = 8 - (ML_CONV - 1) + j
            acc = acc + ext[off:off + lb] * cw[j:j + 1]
        carry_ref[bb] = qk_b[lb - 8:lb]
        conv.append(acc)
    qkc = _silu(jnp.concatenate(conv, axis=0))
    qf = qkc[:, 0:hk]
    kf = qkc[:, hk:2 * hk] * (DK ** -0.5)
    vb = v.astype(BF16)

    gc = (_dot(xh, wgc_ref[0]) + _dot(xl, wgc_ref[0]) + _dot(xh, wgc_ref[1])) + bgc_ref[...]
    gr = (_dot_nt(wgr_ref[0], xh) + _dot_nt(wgr_ref[0], xl) + _dot_nt(wgr_ref[1], xh)) + bgr_ref[...]
    lf_c = _log_sigmoid(gc)
    lf_r = _log_sigmoid(gr)
    tri = tri_ref[...]
    trit = trit_ref[...]
    c_h, c_m, c_l = _split3(lf_c)
    r_h, r_m, r_l = _split3(lf_r)
    blocks = [slice(r, r + lb) for r in range(0, nb * lb, lb)]
    bc = jnp.concatenate([_dot(tri, c_h[r]) + _dot(tri, c_m[r]) + _dot(tri, c_l[r]) for r in blocks],
                         axis=0)
    br = jnp.concatenate([_dot(r_h[:, r], trit) + _dot(r_m[:, r], trit) + _dot(r_l[:, r], trit)
                          for r in blocks], axis=1)

    lane = lax.broadcasted_iota(jnp.int32, (1, PAIR_W), 1)
    row_i = lax.broadcasted_iota(jnp.int32, (ML_CHUNK, ML_CHUNK), 0)
    col_i = lax.broadcasted_iota(jnp.int32, (ML_CHUNK, ML_CHUNK), 1)
    causal = row_i >= col_i
    sr = lax.broadcasted_iota(jnp.int32, (ML_ST_ROWS, PAIR_W), 0)
    sc_ = lax.broadcasted_iota(jnp.int32, (ML_ST_ROWS, PAIR_W), 1)
    first = sc_ < DK
    rows_h0 = (sr < DV) | (sr == 2 * DV)
    rows_h1 = ((sr >= DV) & (sr < 2 * DV)) | (sr == 2 * DV + 1)
    st_mask = (rows_h0 & first) | (rows_h1 & ~first)
    ones_blk = jnp.ones((ML_CHUNK, LANES), BF16)

    for bb, p in [(bb, p) for bb in range(nb) for p in range(HEADS // 2)]:
        ct = ct_ref[bb, p]
        lsl = slice(p * PAIR_W, (p + 1) * PAIR_W)
        m_pair = [m_ref[bb, 2 * p + hh][0:1, 0:1] for hh in range(2)]
        for c in range(lb // ML_CHUNK):
            rsl = slice(bb * lb + c * ML_CHUNK, bb * lb + (c + 1) * ML_CHUNK)
            q_c = qf[rsl, lsl]
            k_c = kf[rsl, lsl]
            k_cb = k_c.astype(BF16)
            inter_mm = _dot_nt(q_c.astype(BF16), ct.astype(BF16))
            wt_cols, decays = [], []
            for hh in range(2):
                head = 2 * p + hh
                m_st = m_pair[hh]
                b_col = bc[rsl, HEADS + head:HEADS + head + 1]
                i_col = gc[rsl, head:head + 1]
                b_row = br[HEADS + head:HEADS + head + 1, rsl]
                i_row = gr[head:head + 1, rsl]
                dmat = jnp.where(causal, b_col - b_row + i_row, -jnp.inf)
                inter = b_col + m_st
                m_row = jnp.maximum(inter, jnp.max(dmat, axis=-1, keepdims=True))
                wts = jnp.exp(dmat - m_row)
                in_head = (lane >= hh * DK) & (lane < (hh + 1) * DK)
                qm = jnp.where(in_head, q_c, 0.0).astype(BF16)
                sc = _dot_nt(qm, k_cb) * wts
                w_inter = jnp.exp(inter - m_row)
                num = _dot(sc.astype(BF16), vb[rsl, head * DV:(head + 1) * DV]) \
                    + w_inter * inter_mm[:, hh * DV:(hh + 1) * DV]
                den = jnp.sum(sc, axis=-1, keepdims=True) \
                    + w_inter * inter_mm[:, 2 * DV + hh:2 * DV + hh + 1]
                o_ref[rsl, head * DV:(head + 1) * DV] = num / jnp.maximum(jnp.abs(den), jnp.exp(-m_row))
                g_tot = b_col[ML_CHUNK - 1:ML_CHUNK, :]
                tail = g_tot - b_col + i_col
                m_new = jnp.maximum(g_tot + m_st, jnp.max(tail, axis=0, keepdims=True))
                wt_cols.append(jnp.exp(tail - m_new))
                decays.append(jnp.exp(g_tot + m_st - m_new))
                m_pair[hh] = m_new
            wk = (k_c * jnp.where(lane < DK, wt_cols[0], wt_cols[1])).astype(BF16)
            vp = jnp.concatenate([vb[rsl, p * 2 * DV:(p + 1) * 2 * DV], ones_blk], axis=1)
            upd = _dot_tn(vp, wk)
            ct = ct * jnp.where(lane < DK, decays[0], decays[1]) + jnp.where(st_mask, upd, 0.0)
        ct_ref[bb, p] = ct
        for hh in range(2):
            m_ref[bb, 2 * p + hh] = jnp.broadcast_to(m_pair[hh], m_ref.shape[2:])

    ng = ng_ref[...]
    for head in range(HEADS):
        hsl = slice(head * DV, (head + 1) * DV)
        y = _head_norm(o_ref[:, hsl]) * ng[:, hsl] * _sigmoid(o_pre[:, hsl])
        y_ref[:, :, hsl] = y.reshape(nb, lb, DV).astype(y_ref.dtype)


def _ml_mixer(x, w, b, wgc, bgc, wgr, bgr, cw, cb, ng, tri, trit):
    bsz, s, d = x.shape
    wcols = w.shape[1]
    nb = SEQ_PER_STEP
    return pl.pallas_call(
        _ml_kernel,
        grid=(bsz // nb, s // SEQ_BLOCK),
        in_specs=[
            pl.BlockSpec((nb, SEQ_BLOCK, d), lambda i, j: (i, j, 0)),
            _full((d, wcols)), _full((1, wcols)),
            _full((2, d, LANES)), _full((1, LANES)),
            _full((2, 8, d)), _full((8, 1)),
            _full((ML_CONV, 2 * HEADS * DK)), _full((1, 2 * HEADS * DK)),
            _full((1, BRANCH_W)),
            _full((SEQ_BLOCK, SEQ_BLOCK)), _full((SEQ_BLOCK, SEQ_BLOCK)),
        ],
        out_specs=pl.BlockSpec((nb, SEQ_BLOCK, BRANCH_W), lambda i, j: (i, j, 0)),
        out_shape=jax.ShapeDtypeStruct((bsz, s, BRANCH_W), BF16),
        scratch_shapes=[pltpu.VMEM((nb, HEADS // 2, ML_ST_ROWS, PAIR_W), F32),
                        pltpu.VMEM((nb, HEADS, 8, LANES), F32),
                        pltpu.VMEM((nb, 8, 2 * HEADS * DK), F32),
                        pltpu.VMEM((nb * SEQ_BLOCK, BRANCH_W), F32)],
        compiler_params=_cparams(("arbitrary", "arbitrary")),
        name="mlstm_mixer",
    )(x, w, b, wgc, bgc, wgr, bgr, cw, cb, ng, tri, trit)


S5_HALF = 256
S5_HALF_STATE = (S5_HALF // S5_GROUP) * S5_STATE


def _s5_kernel(bsz, nc, u_ref, ec_ref, ftc_ref, c0_ref, d_ref, a_ref, y_ref,
               in_ref, xs_ref, xsb_ref, kexp_ref, c0x_ref):
    s = pl.program_id(1)
    lc = S5_CHUNK
    rows = bsz * nc
    sw = S5_HALF_STATE

    def expand_state(blk):
        tile = jnp.concatenate([blk[:, 0:LANES]] * (sw // LANES) + [blk[:, LANES:2 * LANES]] * (sw // LANES),
                               axis=1)
        r = lax.broadcasted_iota(jnp.int32, tile.shape, 0) // S5_GROUP
        c = (lax.broadcasted_iota(jnp.int32, tile.shape, 1) % sw) // S5_STATE
        return jnp.where(r == c, tile, 0.0).astype(BF16)

    @pl.when(s == 0)
    def _():
        c0x_ref[...] = expand_state(c0_ref[...])
        xs_ref[...] = jnp.zeros_like(xs_ref)

    @pl.when(s < lc)
    def _():
        u = u_ref[...].reshape(rows, u_ref.shape[-1])
        in_ref[s] = u
        et = expand_state(ec_ref[0])
        kblk = _dot_nt(et, c0x_ref[...])
        r = lax.broadcasted_iota(jnp.int32, kblk.shape, 0)
        c = lax.broadcasted_iota(jnp.int32, kblk.shape, 1)
        kblk = kblk + jnp.where((r == c) & (s == lc - 1), d_ref[...], 0.0)
        kexp_ref[pl.ds(pl.multiple_of(s * S5_HALF, S5_HALF), S5_HALF), :] = kblk.astype(BF16)
        for jc in range(2 * sw // S5_HALF):
            part = _dot(u, et[:, jc * S5_HALF:(jc + 1) * S5_HALF])
            for jj in range(S5_HALF // LANES):
                xs_ref[jc * (S5_HALF // LANES) + jj] += part[:, jj * LANES:(jj + 1) * LANES]

    @pl.when(s == lc - 1)
    def _():
        ar = a_ref[0, 0:1, :]
        ai = a_ref[0, 1:2, :]
        nt = sw // LANES

        def body(c, carry):
            sr, si = carry
            idx = pl.ds(c, bsz, stride=nc)
            xr = jnp.concatenate([xs_ref[j, idx, :] for j in range(nt)], axis=1)
            xi = jnp.concatenate([xs_ref[nt + j, idx, :] for j in range(nt)], axis=1)
            for j in range(nt):
                xs_ref[j, idx, :] = sr[:, j * LANES:(j + 1) * LANES]
                xs_ref[nt + j, idx, :] = si[:, j * LANES:(j + 1) * LANES]
            return ar * sr - ai * si + xr, ar * si + ai * sr + xi

        zero = jnp.zeros((bsz, sw), F32)
        lax.fori_loop(0, nc, body, (zero, zero))
        for j in range(2 * nt):
            xsb_ref[:, j * LANES:(j + 1) * LANES] = xs_ref[j].astype(BF16)

    for lo in range(lc):
        @pl.when(s == lc + lo)
        def _(lo=lo):
            acc = _dot_nt(xsb_ref[...], expand_state(ftc_ref[0]))
            for l in range(lo + 1):
                j = lc - 1 - lo + l
                acc = acc + _dot(in_ref[l], kexp_ref[j * S5_HALF:(j + 1) * S5_HALF, :])
            y_ref[...] = acc.reshape(y_ref.shape).astype(y_ref.dtype)


def _s5_mixer(u, ec, ftc, c0, d, a):
    bsz, s, _ = u.shape
    lc = S5_CHUNK
    nc = s // lc
    u3 = u.reshape(bsz, nc, lc * BRANCH_W)
    nh = BRANCH_W // S5_HALF
    y = pl.pallas_call(
        functools.partial(_s5_kernel, bsz, nc),
        grid=(nh, 2 * lc),
        in_specs=[
            pl.BlockSpec((bsz, nc, S5_HALF), lambda h, t: (0, 0, jnp.minimum(t, lc - 1) * nh + h)),
            pl.BlockSpec((1, S5_HALF, 2 * LANES), lambda h, t: (jnp.minimum(t, lc - 1), h, 0)),
            pl.BlockSpec((1, S5_HALF, 2 * LANES), lambda h, t: (jnp.maximum(t - lc, 0), h, 0)),
            pl.BlockSpec((S5_HALF, 2 * LANES), lambda h, t: (h, 0)),
            pl.BlockSpec((1, S5_HALF), lambda h, t: (0, h)),
            pl.BlockSpec((1, 2, S5_HALF_STATE), lambda h, t: (h, 0, 0)),
        ],
        out_specs=pl.BlockSpec((bsz, nc, S5_HALF), lambda h, t: (0, 0, jnp.maximum(t - lc, 0) * nh + h)),
        out_shape=jax.ShapeDtypeStruct((bsz, nc, lc * BRANCH_W), BF16),
        scratch_shapes=[pltpu.VMEM((lc, bsz * nc, S5_HALF), BF16),
                        pltpu.VMEM((2 * S5_HALF_STATE // LANES, bsz * nc, LANES), F32),
                        pltpu.VMEM((bsz * nc, 2 * S5_HALF_STATE), BF16),
                        pltpu.VMEM((lc * S5_HALF, S5_HALF), BF16),
                        pltpu.VMEM((S5_HALF, 2 * S5_HALF_STATE), BF16)],
        compiler_params=_cparams(("arbitrary", "arbitrary")),
        name="s5_mixer",
    )(u3, ec, ftc, c0, d, a)
    return y.reshape(bsz * s, BRANCH_W)


def _s5_tables(a_re, a_im, log_dt, b_re, b_im, c_re, c_im):
    nl = a_re.shape[0]
    g, p_, n, lc = S5_GROUPS, S5_STATE, S5_GROUP, S5_CHUNK
    a_re, a_im, b_re, b_im, c_re, c_im = (v.astype(F32) for v in (a_re, a_im, b_re, b_im, c_re, c_im))
    dt = jnp.exp(log_dt.astype(F32))[..., None]
    adt_r = (a_re * dt)[:, None, :, None, :]
    adt_i = (a_im * dt)[:, None, :, None, :]

    def powers(steps):
        st = jnp.asarray(np.asarray(steps, np.float32).reshape(1, -1, 1, 1, 1))
        mag = jnp.exp(adt_r * st)
        return mag * jnp.cos(adt_i * st), mag * jnp.sin(adt_i * st)

    lbr, lbi = powers([1])
    are, aim = a_re[:, None, :, None, :], a_im[:, None, :, None, :]
    den = are * are + aim * aim
    qr = ((lbr - 1.0) * are + lbi * aim) / den
    qi = (lbi * are - (lbr - 1.0) * aim) / den
    bt_re, bt_im = b_re.transpose(0, 1, 3, 2)[:, None], b_im.transpose(0, 1, 3, 2)[:, None]
    bbr = qr * bt_re - qi * bt_im
    bbi = qr * bt_im + qi * bt_re
    cr, ci = c_re[:, None], c_im[:, None]
    pr, pi = powers(np.arange(lc - 1, -1, -1))
    er, ei = pr * bbr - pi * bbi, pr * bbi + pi * bbr
    ec = jnp.concatenate([er, er, ei, ei], axis=-1).reshape(nl, lc, g * n, 4 * p_)
    pfr, pfi = powers(np.arange(1, lc + 1))
    fr, fi = cr * pfr - ci * pfi, -(cr * pfi + ci * pfr)
    ftc = jnp.concatenate([fr, fr, fi, fi], axis=-1).reshape(nl, lc, g * n, 4 * p_)
    c0 = jnp.concatenate([c_re, c_re, -c_im, -c_im], axis=-1).reshape(nl, g * n, 4 * p_)
    nh = BRANCH_W // S5_HALF
    ar, ai = powers([lc])
    a_tab = jnp.stack([ar.reshape(nl, nh, S5_HALF_STATE), ai.reshape(nl, nh, S5_HALF_STATE)], axis=2)
    return ec, ftc, c0, a_tab


def _merge_kernel(x_ref, yg_ref, y5_ref, ym_ref, wgate_ref, bgate_ref, wglu_ref, bglu_ref,
                  wup_ref, wo_ref, g_ref, b_ref, o_ref):
    x = x_ref[...]
    xb = x.astype(BF16)
    y5 = _gelu_tanh(y5_ref[...].astype(F32))
    y5 = y5 * _sigmoid(_dot(y5.astype(BF16), wglu_ref[...]) + bglu_ref[...])
    ys = (yg_ref[...], y5.astype(BF16), ym_ref[...])
    acc = None
    for r in range(N_BRANCH):
        gate = _sigmoid(_dot(xb, wgate_ref[:, r * D_MODEL:(r + 1) * D_MODEL])
                        + bgate_ref[:, r * D_MODEL:(r + 1) * D_MODEL])
        term = gate * _dot(ys[r], wup_ref[r])
        acc = term if acc is None else acc + term
    mix = _dot(acc.astype(BF16), wo_ref[...])
    o_ref[...] = _layer_norm(DN_ALPHA * x + mix, g_ref[...], b_ref[...])


def _merge(x2, yg, y5, ym, wgate, bgate, wglu, bglu, wup, wo, g, b, tm=512):
    t, dm = x2.shape
    row = lambda w: pl.BlockSpec((tm, w), lambda i: (i, 0))
    return pl.pallas_call(
        _merge_kernel,
        grid=(t // tm,),
        in_specs=[row(dm), row(BRANCH_W), row(BRANCH_W), row(BRANCH_W),
                  _full(wgate.shape), _full(bgate.shape), _full(wglu.shape), _full(bglu.shape),
                  _full(wup.shape), _full(wo.shape), _full(g.shape), _full(b.shape)],
        out_specs=row(dm),
        out_shape=jax.ShapeDtypeStruct((t, dm), F32),
        compiler_params=_cparams(("parallel",)),
        name="merge_ln1",
    )(x2, yg, y5, ym, wgate, bgate, wglu, bglu, wup, wo, g, b)


def _ple_ln2(x, xb, f, p_ref, pwg_ref, pwp_ref, g_ref, b_ref):
    e = _sigmoid(_dot(xb, pwg_ref[...])) * _dot(p_ref[...].astype(BF16), pwp_ref[...])
    return _layer_norm(DN_ALPHA * x + f + e, g_ref[...], b_ref[...])


def _ffn_kernel(x_ref, p_ref, wg_ref, wu_ref, wd_ref, pwg_ref, pwp_ref, g_ref, b_ref, o_ref):
    x = x_ref[...]
    xb = x.astype(BF16)
    dff = wg_ref.shape[1]
    f = None
    for half in range(2):
        csl = slice(half * (dff // 2), (half + 1) * (dff // 2))
        hid = (_silu(_dot(xb, wg_ref[:, csl])) * _dot(xb, wu_ref[:, csl])).astype(BF16)
        part = _dot(hid, wd_ref[csl, :])
        f = part if f is None else f + part
    o_ref[...] = _ple_ln2(x, xb, f, p_ref, pwg_ref, pwp_ref, g_ref, b_ref)


def _ffn_layer(x2, p3, layer, wg, wu, wd, pwg, pwp, g, b, tm=512):
    t, dm = x2.shape
    row = lambda w: pl.BlockSpec((tm, w), lambda i: (i, 0))
    return pl.pallas_call(
        _ffn_kernel,
        grid=(t // tm,),
        in_specs=[row(dm), pl.BlockSpec((None, tm, PLE_DIM), lambda i: (layer, i, 0)),
                  _full(wg.shape), _full(wu.shape), _full(wd.shape),
                  _full(pwg.shape), _full(pwp.shape), _full(g.shape), _full(b.shape)],
        out_specs=row(dm),
        out_shape=jax.ShapeDtypeStruct((t, dm), F32),
        compiler_params=_cparams(("parallel",)),
        name="ffn_ple_ln2",
    )(x2, p3, wg, wu, wd, pwg, pwp, g, b)


def _router_kernel(x_ref, w_ref, b_ref, tril_ref, sel_ref, pr_ref, rk_ref, cnt_ref, base_ref):
    @pl.when(pl.program_id(0) == 0)
    def _():
        base_ref[...] = jnp.zeros_like(base_ref)

    xh, xl = _split2(x_ref[...])
    logits = _dot(xh, w_ref[0]) + _dot(xl, w_ref[0]) + _dot(xh, w_ref[1]) + b_ref[...]
    lane = lax.broadcasted_iota(jnp.int32, logits.shape, 1)
    neg = -jnp.inf
    logits = jnp.where(lane < N_EXPERTS, logits, neg)
    m1 = jnp.max(logits, axis=-1, keepdims=True)
    i1 = jnp.min(jnp.where(logits == m1, lane, LANES), axis=-1, keepdims=True)
    rest = jnp.where(lane == i1, neg, logits)
    m2 = jnp.max(rest, axis=-1, keepdims=True)
    i2 = jnp.min(jnp.where(rest == m2, lane, LANES), axis=-1, keepdims=True)
    e2 = jnp.exp(m2 - m1)
    p1 = 1.0 / (1.0 + e2)
    p2 = e2 / (1.0 + e2)
    hot = jnp.where((lane == i1) | (lane == i2), 1.0, 0.0)
    base = base_ref[0:1, :]
    before = _dot(tril_ref[...], hot.astype(BF16)) + base
    r1 = jnp.sum(jnp.where(lane == i1, before, 0.0), axis=-1, keepdims=True)
    r2 = jnp.sum(jnp.where(lane == i2, before, 0.0), axis=-1, keepdims=True)
    sel_ref[...] = jnp.where(lane == 0, i1, jnp.where(lane == 1, i2, 0))
    pr_ref[...] = jnp.where(lane == 0, p1, jnp.where(lane == 1, p2, 0.0))
    rk_ref[...] = jnp.where(lane == 0, r1, jnp.where(lane == 1, r2, 0.0))
    total = base + jnp.sum(hot, axis=0, keepdims=True)
    base_ref[...] = jnp.broadcast_to(total, base_ref.shape)
    cnt_ref[...] = jnp.broadcast_to(total, cnt_ref.shape)


def _router(x2, w, b, tm=512):
    t, dm = x2.shape
    i = np.arange(tm)
    tril = jnp.asarray((i[:, None] > i[None, :]).astype(np.float32), BF16)
    blk = pl.BlockSpec((tm, LANES), lambda i: (i, 0))
    return pl.pallas_call(
        _router_kernel,
        grid=(t // tm,),
        in_specs=[pl.BlockSpec((tm, dm), lambda i: (i, 0)), _full(w.shape), _full(b.shape), _full((tm, tm))],
        out_specs=[blk, blk, blk, pl.BlockSpec((8, LANES), lambda i: (0, 0))],
        out_shape=[jax.ShapeDtypeStruct((t, LANES), jnp.int32), jax.ShapeDtypeStruct((t, LANES), F32),
                   jax.ShapeDtypeStruct((t, LANES), F32), jax.ShapeDtypeStruct((8, LANES), F32)],
        scratch_shapes=[pltpu.VMEM((8, LANES), F32)],
        compiler_params=_cparams(("arbitrary",)),
        name="moe_router",
    )(x2, w, b, tril)


TOP_K = 2
MOE_TILE = 1024
MOE_FF_TILE = 512
DISPATCH_TILE = 512


def _row_copy_wait(src_rows, dst_rows, sem):
    pltpu.make_async_copy(src_rows, dst_rows, sem).wait()


def _dispatch_kernel(pos_ref, x_ref, xs_ref, sem):
    tm = x_ref.shape[0]
    base = pl.program_id(0) * tm * TOP_K

    def body(r, carry):
        for k in range(TOP_K):
            dst = pos_ref[base + r * TOP_K + k]
            pltpu.make_async_copy(x_ref.at[pl.ds(r, 1), :], xs_ref.at[pl.ds(dst, 1), :], sem).start()
        return carry

    lax.fori_loop(0, tm, body, 0)
    for _ in range(TOP_K):
        _row_copy_wait(x_ref, xs_ref.at[pl.ds(0, tm), :], sem)


def _dispatch(pos_flat, x2, tm=DISPATCH_TILE):
    t, dm = x2.shape
    return pl.pallas_call(
        _dispatch_kernel,
        grid_spec=pltpu.PrefetchScalarGridSpec(
            num_scalar_prefetch=1,
            grid=(t // tm,),
            in_specs=[pl.BlockSpec((tm, dm), lambda i, pos: (i, 0))],
            out_specs=pl.BlockSpec(memory_space=pl.ANY),
            scratch_shapes=[pltpu.SemaphoreType.DMA],
        ),
        out_shape=jax.ShapeDtypeStruct((t * TOP_K, dm), F32),
        compiler_params=_cparams(("arbitrary",)),
        name="moe_dispatch",
    )(pos_flat, x2)


def _moe_group_kernel(tile_ref, exp_ref, lo_ref, hi_ref, xs_ref, wg_ref, wu_ref, wd_ref, o_ref,
                      acc_ref, xb_ref):
    i = pl.program_id(0)
    f = pl.program_id(1)
    lo = lo_ref[i]
    hi = hi_ref[i]
    tm = xs_ref.shape[0]

    @pl.when(hi > lo)
    def _():
        @pl.when((lo == 0) & (f == 0))
        def _():
            acc_ref[...] = jnp.zeros_like(acc_ref)

        @pl.when(f == 0)
        def _():
            xb_ref[...] = xs_ref[...].astype(BF16)

        xb = xb_ref[...]
        hid = _silu(_dot(xb, wg_ref[0].astype(BF16))) * _dot(xb, wu_ref[0].astype(BF16))
        rid = lax.broadcasted_iota(jnp.int32, (tm, 1), 0)
        hid = jnp.where((rid >= lo) & (rid < hi), hid, 0.0).astype(BF16)
        acc_ref[...] += _dot(hid, wd_ref[0].astype(BF16))

        @pl.when((hi == tm) & (f == pl.num_programs(1) - 1))
        def _():
            o_ref[...] = acc_ref[...]


def _moe_grouped(items, xs, wg, wu, wd, tm=MOE_TILE, tf=MOE_FF_TILE):
    tile, exp, lo, hi = items
    rows, dm = xs.shape
    dff = wg.shape[2]
    return pl.pallas_call(
        _moe_group_kernel,
        grid_spec=pltpu.PrefetchScalarGridSpec(
            num_scalar_prefetch=4,
            grid=(tile.shape[0], dff // tf),
            in_specs=[pl.BlockSpec((tm, dm), lambda i, f, tl, ex, lo_, hi_: (tl[i], 0)),
                      pl.BlockSpec((1, dm, tf), lambda i, f, tl, ex, lo_, hi_: (ex[i], 0, f)),
                      pl.BlockSpec((1, dm, tf), lambda i, f, tl, ex, lo_, hi_: (ex[i], 0, f)),
                      pl.BlockSpec((1, tf, dm), lambda i, f, tl, ex, lo_, hi_: (ex[i], f, 0))],
            out_specs=pl.BlockSpec((tm, dm), lambda i, f, tl, ex, lo_, hi_: (tl[i], 0)),
            scratch_shapes=[pltpu.VMEM((tm, dm), F32), pltpu.VMEM((tm, dm), BF16)],
        ),
        out_shape=jax.ShapeDtypeStruct((rows, dm), F32),
        compiler_params=_cparams(("arbitrary", "arbitrary")),
        name="moe_grouped",
    )(tile, exp, lo, hi, xs, wg, wu, wd)


def _moe_items(counts, n_rows, tm=MOE_TILE):
    n_tiles = n_rows // tm
    ends = jnp.cumsum(counts)
    cuts = jnp.sort(jnp.concatenate([jnp.arange(n_tiles + 1, dtype=jnp.int32) * tm, ends[:-1]]))
    start, stop = cuts[:-1], cuts[1:]
    tile = jnp.minimum(start // tm, n_tiles - 1)
    exp = jnp.minimum(jnp.sum(ends[None, :] <= start[:, None], axis=1), N_EXPERTS - 1)
    lo = start - tile * tm
    hi = stop - tile * tm
    return tile.astype(jnp.int32), exp.astype(jnp.int32), lo.astype(jnp.int32), hi.astype(jnp.int32)


def _combine_kernel(pos_ref, x_ref, pr_ref, p_ref, pwg_ref, pwp_ref, g_ref, b_ref, ys_ref, o_ref, gat_ref, sem):
    tm = x_ref.shape[0]
    base = pl.program_id(0) * tm * TOP_K

    def body(r, carry):
        for k in range(TOP_K):
            src = pos_ref[base + r * TOP_K + k]
            pltpu.make_async_copy(ys_ref.at[pl.ds(src, 1), :], gat_ref.at[k, pl.ds(r, 1), :], sem).start()
        return carry

    lax.fori_loop(0, tm, body, 0)
    x = x_ref[...]
    xb = x.astype(BF16)
    e = _sigmoid(_dot(xb, pwg_ref[...])) * _dot(p_ref[...].astype(BF16), pwp_ref[...])
    for k in range(TOP_K):
        _row_copy_wait(ys_ref.at[pl.ds(0, tm), :], gat_ref.at[k], sem)
    pr = pr_ref[...]
    f = pr[:, 0:1] * gat_ref[0]
    for k in range(1, TOP_K):
        f = f + pr[:, k:k + 1] * gat_ref[k]
    o_ref[...] = _layer_norm(DN_ALPHA * x + f + e, g_ref[...], b_ref[...])


def _combine_layer(pos_flat, x2, pr, p3, layer, pwg, pwp, g, b, ys, tm=DISPATCH_TILE):
    t, dm = x2.shape
    row = lambda w: pl.BlockSpec((tm, w), lambda i, pos: (i, 0))
    full = lambda a: pl.BlockSpec(a.shape, lambda i, pos: (0,) * a.ndim, pipeline_mode=pl.Buffered(1))
    return pl.pallas_call(
        _combine_kernel,
        grid_spec=pltpu.PrefetchScalarGridSpec(
            num_scalar_prefetch=1,
            grid=(t // tm,),
            in_specs=[row(dm), row(LANES), pl.BlockSpec((None, tm, PLE_DIM), lambda i, pos: (layer, i, 0)),
                      full(pwg), full(pwp), full(g), full(b),
                      pl.BlockSpec(memory_space=pl.ANY)],
            out_specs=row(dm),
            scratch_shapes=[pltpu.VMEM((TOP_K, tm, dm), F32), pltpu.SemaphoreType.DMA],
        ),
        out_shape=jax.ShapeDtypeStruct((t, dm), F32),
        compiler_params=_cparams(("arbitrary",)),
        name="moe_combine_ple_ln2",
    )(pos_flat, x2, pr, p3, pwg, pwp, g, b, ys)


def _row(v):
    return v.reshape(1, -1).astype(F32)


def _pad_lanes(w):
    return jnp.pad(w, ((0, 0), (0, LANES - w.shape[1])))


def kernel(x, p, w_in, b_in, gla_w_a2, gla_b_a2, gla_norm_g, s5_a_re, s5_a_im, s5_log_dt, s5_b_re, s5_b_im,
           s5_c_re, s5_c_im, s5_d, s5_w_glu, s5_b_glu, ml_conv_w, ml_conv_b, ml_norm_g, w_up, w_o, ln1_g, ln1_b,
           ffn_wg, ffn_wu, ffn_wd, moe_router, moe_router_b, moe_wg, moe_wu, moe_wd, ple_w_gate, ple_w_proj,
           ln2_g, ln2_b):
    bsz, s, dm = x.shape
    t = bsz * s
    hi = lax.Precision.HIGHEST
    o = IN_OFF
    tri = jnp.asarray(_chunk_tri(SEQ_BLOCK, CHUNK), BF16)
    ml_tri_np = _chunk_tri(SEQ_BLOCK, ML_CHUNK)
    ml_tri = jnp.asarray(ml_tri_np, BF16)
    ml_trit = jnp.asarray(ml_tri_np.T, BF16)
    s5_ec, s5_ftc, s5_c0, s5_atab = _s5_tables(s5_a_re, s5_a_im, s5_log_dt, s5_b_re, s5_b_im, s5_c_re, s5_c_im)
    p3 = p.reshape(DEPTH, t, PLE_DIM)

    w_in_b = w_in.astype(BF16)
    for i in range(DEPTH):
        w, b = w_in_b[i], b_in[i]
        sl = lambda k: (w[:, o[k]:o[k + 1]], b[o[k]:o[k + 1]])
        (wq, bq), (wk, bk), (wv, bv), (_, ba), (wg_, bg_) = sl(0), sl(1), sl(2), sl(3), sl(4)
        wz = jnp.dot(w_in[i, :, o[3]:o[4]], gla_w_a2[i], precision=hi).astype(BF16)
        bz = jnp.dot(ba, gla_w_a2[i], precision=hi) + gla_b_a2[i]
        (wu_, bu_) = sl(5)
        w_gla = jnp.concatenate([wq, wk, wz, wv, wg_, wu_], axis=1)
        b_gla = _row(jnp.concatenate([bq, bk, bz, bv, bg_, bu_]))
        y_gla, u = _gla_mixer(x, w_gla, b_gla, _row(gla_norm_g[i]), tri)

        x2 = x.reshape(t, dm)
        y5 = _s5_mixer(u, s5_ec[i], s5_ftc[i], s5_c0[i], _row(s5_d[i]), s5_atab[i])

        (wmq, bmq), (wmk, bmk), (wmv, bmv), (wmo, bmo) = sl(6), sl(7), sl(8), sl(11)
        w_ml = jnp.concatenate([wmq, wmk, wmv, wmo], axis=1)
        b_ml = _row(jnp.concatenate([bmq, bmk, bmv, bmo]))
        w_if = w_in[i, :, o[9]:o[11]]
        b_if = b[o[9]:o[11]]
        if_h = w_if.astype(BF16)
        if_l = (w_if - if_h.astype(F32)).astype(BF16)
        wgc = jnp.stack([_pad_lanes(if_h), _pad_lanes(if_l)])
        wgr = jnp.stack([if_h.T, if_l.T])
        y_ml = _ml_mixer(x, w_ml, b_ml, wgc, _pad_lanes(_row(b_if)), wgr, b_if.reshape(-1, 1).astype(F32),
                         ml_conv_w[i].astype(F32), _row(ml_conv_b[i]), _row(ml_norm_g[i]), ml_tri, ml_trit)

        (wgt, bgt) = sl(12)
        x1 = _merge(x2, y_gla.reshape(t, BRANCH_W), y5, y_ml.reshape(t, BRANCH_W),
                    wgt, _row(bgt), s5_w_glu[i].astype(BF16), _row(s5_b_glu[i]),
                    w_up[i].astype(BF16), w_o[i].astype(BF16), _row(ln1_g[i]), _row(ln1_b[i]))

        pwg = ple_w_gate[i].astype(BF16)
        pwp = ple_w_proj[i].astype(BF16)
        j = i // 2
        if i % 2 == 0:
            x2n = _ffn_layer(x1, p3, i, ffn_wg[j].astype(BF16), ffn_wu[j].astype(BF16), ffn_wd[j].astype(BF16),
                             pwg, pwp, _row(ln2_g[i]), _row(ln2_b[i]))
        else:
            wr = moe_router[j]
            wr_h = wr.astype(BF16)
            wr_l = (wr - wr_h.astype(F32)).astype(BF16)
            sel, pr, rk, cnt = _router(x1, jnp.stack([_pad_lanes(wr_h), _pad_lanes(wr_l)]),
                                       _pad_lanes(_row(moe_router_b[j])))
            counts = cnt[0, :N_EXPERTS].astype(jnp.int32)
            starts = jnp.cumsum(counts) - counts
            sel2 = sel[:, :TOP_K]
            pos = rk[:, :TOP_K].astype(jnp.int32) + jnp.sum(
                jnp.where(sel2[..., None] == jnp.arange(N_EXPERTS), starts, 0), axis=-1)
            pos_flat = pos.reshape(-1)
            xs = _dispatch(pos_flat, x1)
            ys = _moe_grouped(_moe_items(counts, t * TOP_K), xs, moe_wg[j], moe_wu[j], moe_wd[j])
            x2n = _combine_layer(pos_flat, x1, pr, p3, i, pwg, pwp, _row(ln2_g[i]), _row(ln2_b[i]), ys)
        x = x2n.reshape(bsz, s, dm)
    return x
```

```python
import functools
import math

import numpy as np
import jax
import jax.numpy as jnp
from jax import lax
from jax.experimental import pallas as pl
from jax.experimental.pallas import tpu as pltpu

F32 = jnp.float32
BF16 = jnp.bfloat16

D_MODEL = 1024
DEPTH = 2
N_BRANCH = 3
BRANCH_W = 512
HEADS = 4
DK = 64
DV = BRANCH_W // HEADS
GLA_RANK = 16
GLA_TAU = 16.0
CHUNK = 64
ML_CHUNK = 256
S5_GROUP = 16
S5_GROUPS = BRANCH_W // S5_GROUP
S5_STATE = 64
S5_CHUNK = 16
ML_CONV = 4
D_FF = 2816
N_EXPERTS = 8
D_FF_EXPERT = 3584
PLE_DIM = 256
DN_ALPHA = (2.0 * DEPTH) ** 0.25
LN_EPS = 1e-5

IN_WIDTHS = (
    HEADS * DK, HEADS * DK, BRANCH_W, GLA_RANK, BRANCH_W,
    BRANCH_W,
    HEADS * DK, HEADS * DK, BRANCH_W, HEADS, HEADS, BRANCH_W,
    N_BRANCH * D_MODEL,
)
IN_OFF = tuple(int(o) for o in np.concatenate([[0], np.cumsum(IN_WIDTHS)]))

LANES = 128
SEQ_BLOCK = 256
GLA_SEQ_PER_STEP = 4
ML_SEQ_PER_STEP = 1
PAIR_W = 2 * DK
VMEM_LIMIT = 56 * 1024 * 1024


def _cparams(sem):
    return pltpu.CompilerParams(dimension_semantics=sem, vmem_limit_bytes=VMEM_LIMIT)


def _dot(a, b):
    return jnp.dot(a, b, preferred_element_type=F32)


def _dot_nt(a, b):
    return lax.dot_general(a, b, (((1,), (1,)), ((), ())), preferred_element_type=F32)


def _dot_tn(a, b):
    return lax.dot_general(a, b, (((0,), (0,)), ((), ())), preferred_element_type=F32)


def _split3(a):
    hi = a.astype(BF16)
    r = a - hi.astype(F32)
    mid = r.astype(BF16)
    lo = (r - mid.astype(F32)).astype(BF16)
    return hi, mid, lo


def _split2(a):
    hi = a.astype(BF16)
    lo = (a - hi.astype(F32)).astype(BF16)
    return hi, lo


def _log_sigmoid(x):
    return jnp.minimum(x, 0.0) - jnp.log(1.0 + jnp.exp(-jnp.abs(x)))


def _sigmoid(x):
    return 0.5 * jnp.tanh(0.5 * x) + 0.5


def _silu(x):
    return x * _sigmoid(x)


def _gelu_tanh(x):
    return 0.5 * x * (1.0 + jnp.tanh(math.sqrt(2.0 / math.pi) * (x + 0.044715 * (x * x * x))))


def _layer_norm(v, g, b):
    mu = jnp.mean(v, axis=-1, keepdims=True)
    c = v - mu
    var = jnp.mean(c * c, axis=-1, keepdims=True)
    return c * lax.rsqrt(var + LN_EPS) * g + b


def _head_norm(o):
    mu = jnp.mean(o, axis=-1, keepdims=True)
    c = o - mu
    var = jnp.mean(c * c, axis=-1, keepdims=True)
    return c * lax.rsqrt(var + LN_EPS)


def _chunk_tri(n, chunk):
    i = np.arange(n)
    return ((i[:, None] >= i[None, :]) & (i[:, None] // chunk == i[None, :] // chunk)).astype(np.float32)


def _full(shape):
    nd = len(shape)
    return pl.BlockSpec(shape, lambda *_: (0,) * nd, pipeline_mode=pl.Buffered(1))


def _gla_kernel(x_ref, w_ref, b_ref, ng_ref, tri_ref, y_ref, u_ref, st_ref, o_ref):
    @pl.when(pl.program_id(1) == 0)
    def _():
        st_ref[...] = jnp.zeros_like(st_ref)

    hk = HEADS * DK
    nb, lb, d = x_ref.shape
    xb = x_ref[...].reshape(nb * lb, d).astype(BF16)
    h = _dot(xb, w_ref[...]) + b_ref[...]
    q = h[:, 0:hk]
    k = h[:, hk:2 * hk]
    z = h[:, 2 * hk:3 * hk]
    v = h[:, 3 * hk:3 * hk + BRANCH_W]
    g = h[:, 3 * hk + BRANCH_W:3 * hk + 2 * BRANCH_W]
    u_ref[...] = h[:, 3 * hk + 2 * BRANCH_W:3 * hk + 3 * BRANCH_W].reshape(nb, lb, BRANCH_W).astype(u_ref.dtype)

    la = _log_sigmoid(z) * (1.0 / GLA_TAU)
    tri = tri_ref[...]
    la_h, la_m, la_l = _split3(la)
    cum = jnp.concatenate(
        [_dot(tri, la_h[r:r + lb]) + _dot(tri, la_m[r:r + lb]) + _dot(tri, la_l[r:r + lb])
         for r in range(0, nb * lb, lb)], axis=0)
    qd = q * (DK ** -0.5) * jnp.exp(cum)
    ki = k * jnp.exp(-cum)
    vb = v.astype(BF16)

    lane = lax.broadcasted_iota(jnp.int32, (1, PAIR_W), 1)
    row_i = lax.broadcasted_iota(jnp.int32, (CHUNK, CHUNK), 0)
    col_i = lax.broadcasted_iota(jnp.int32, (CHUNK, CHUNK), 1)
    causal = row_i >= col_i
    bd_r = lax.broadcasted_iota(jnp.int32, (2 * DV, PAIR_W), 0)
    bd_c = lax.broadcasted_iota(jnp.int32, (2 * DV, PAIR_W), 1)
    blockdiag = (bd_r >= DV) == (bd_c >= DK)

    for bb, p in [(bb, p) for bb in range(nb) for p in range(HEADS // 2)]:
        st = st_ref[bb, p]
        lsl = slice(p * PAIR_W, (p + 1) * PAIR_W)
        for c in range(lb // CHUNK):
            rsl = slice(bb * lb + c * CHUNK, bb * lb + (c + 1) * CHUNK)
            qd_c = qd[rsl, lsl]
            ki_c = ki[rsl, lsl].astype(BF16)
            cum_c = cum[rsl, lsl]
            last = cum_c[CHUNK - 1:CHUNK, :]
            kt = (k[rsl, lsl] * jnp.exp(last - cum_c)).astype(BF16)
            inter = _dot_nt(qd_c.astype(BF16), st.astype(BF16))
            for hh in range(2):
                head = 2 * p + hh
                in_head = (lane >= hh * DK) & (lane < (hh + 1) * DK)
                qm = jnp.where(in_head, qd_c, 0.0).astype(BF16)
                att = jnp.where(causal, _dot_nt(qm, ki_c), 0.0)
                o_h = _dot(att.astype(BF16), vb[rsl, head * DV:(head + 1) * DV])
                o_ref[rsl, head * DV:(head + 1) * DV] = o_h + inter[:, hh * DV:(hh + 1) * DV]
            upd = _dot_tn(vb[rsl, p * 2 * DV:(p + 1) * 2 * DV], kt)
            st = st * jnp.exp(last) + jnp.where(blockdiag, upd, 0.0)
        st_ref[bb, p] = st

    ng = ng_ref[...]
    for head in range(HEADS):
        hsl = slice(head * DV, (head + 1) * DV)
        y = _head_norm(o_ref[:, hsl]) * ng[:, hsl] * _silu(g[:, hsl])
        y_ref[:, :, hsl] = y.reshape(nb, lb, DV).astype(y_ref.dtype)


def _gla_mixer(x, w, b, ng, tri):
    bsz, s, d = x.shape
    wcols = w.shape[1]
    nb = GLA_SEQ_PER_STEP
    return pl.pallas_call(
        _gla_kernel,
        grid=(bsz // nb, s // SEQ_BLOCK),
        in_specs=[
            pl.BlockSpec((nb, SEQ_BLOCK, d), lambda i, j: (i, j, 0)),
            _full((d, wcols)), _full((1, wcols)), _full((1, BRANCH_W)),
            _full((SEQ_BLOCK, SEQ_BLOCK)),
        ],
        out_specs=[pl.BlockSpec((nb, SEQ_BLOCK, BRANCH_W), lambda i, j: (i, j, 0))] * 2,
        out_shape=[jax.ShapeDtypeStruct((bsz, s, BRANCH_W), BF16)] * 2,
        scratch_shapes=[pltpu.VMEM((nb, HEADS // 2, 2 * DV, PAIR_W), F32),
                        pltpu.VMEM((nb * SEQ_BLOCK, BRANCH_W), F32)],
        compiler_params=_cparams(("arbitrary", "arbitrary")),
        name="gla_mixer",
    )(x, w, b, ng, tri)


ML_ST_ROWS = 2 * DV + LANES


def _ml_kernel(x_ref, w_ref, b_ref, wgc_ref, bgc_ref, wgr_ref, bgr_ref, cw_ref, cb_ref, ng_ref,
               tri_ref, trit_ref, y_ref, ct_ref, m_ref, carry_ref, o_ref):
    @pl.when(pl.program_id(1) == 0)
    def _():
        ct_ref[...] = jnp.zeros_like(ct_ref)
        m_ref[...] = jnp.zeros_like(m_ref)
        carry_ref[...] = jnp.zeros_like(carry_ref)

    hk = HEADS * DK
    nb, lb, d = x_ref.shape
    x32 = x_ref[...].reshape(nb * lb, d)
    xh, xl = _split2(x32)
    h = _dot(xh, w_ref[...]) + b_ref[...]
    qk = h[:, 0:2 * hk]
    v = h[:, 2 * hk:2 * hk + BRANCH_W]
    o_pre = h[:, 2 * hk + BRANCH_W:2 * hk + 2 * BRANCH_W]

    cw = cw_ref[...]
    conv = []
    for bb in range(nb):
        qk_b = qk[bb * lb:(bb + 1) * lb]
        ext = jnp.concatenate([carry_ref[bb], qk_b], axis=0)
        acc = cb_ref[...] + ext[8 - (ML_CONV - 1):8 - (ML_CONV - 1) + lb] * cw[0:1]
        for j in range(1, ML_CONV):
            off = 8 - (ML_CONV - 1) + j
            acc = acc + ext[off:off + lb] * cw[j:j + 1]
        carry_ref[bb] = qk_b[lb - 8:lb]
        conv.append(acc)
    qkc = _silu(jnp.concatenate(conv, axis=0))
    qf = qkc[:, 0:hk]
    kf = qkc[:, hk:2 * hk] * (DK ** -0.5)
    vb = v.astype(BF16)

    gc = (_dot(xh, wgc_ref[0]) + _dot(xl, wgc_ref[0]) + _dot(xh, wgc_ref[1])) + bgc_ref[...]
    gr = (_dot_nt(wgr_ref[0], xh) + _dot_nt(wgr_ref[0], xl) + _dot_nt(wgr_ref[1], xh)) + bgr_ref[...]
    lf_c = _log_sigmoid(gc)
    lf_r = _log_sigmoid(gr)
    tri = tri_ref[...]
    trit = trit_ref[...]
    c_h, c_m, c_l = _split3(lf_c)
    r_h, r_m, r_l = _split3(lf_r)
    blocks = [slice(r, r + lb) for r in range(0, nb * lb, lb)]
    bc = jnp.concatenate([_dot(tri, c_h[r]) + _dot(tri, c_m[r]) + _dot(tri, c_l[r]) for r in blocks],
                         axis=0)
    br = jnp.concatenate([_dot(r_h[:, r], trit) + _dot(r_m[:, r], trit) + _dot(r_l[:, r], trit)
                          for r in blocks], axis=1)

    lane = lax.broadcasted_iota(jnp.int32, (1, PAIR_W), 1)
    row_i = lax.broadcasted_iota(jnp.int32, (ML_CHUNK, ML_CHUNK), 0)
    col_i = lax.broadcasted_iota(jnp.int32, (ML_CHUNK, ML_CHUNK), 1)
    causal = row_i >= col_i
    sr = lax.broadcasted_iota(jnp.int32, (ML_ST_ROWS, PAIR_W), 0)
    sc_ = lax.broadcasted_iota(jnp.int32, (ML_ST_ROWS, PAIR_W), 1)
    first = sc_ < DK
    rows_h0 = (sr < DV) | (sr == 2 * DV)
    rows_h1 = ((sr >= DV) & (sr < 2 * DV)) | (sr == 2 * DV + 1)
    st_mask = (rows_h0 & first) | (rows_h1 & ~first)
    ones_blk = jnp.ones((ML_CHUNK, LANES), BF16)

    for bb, p in [(bb, p) for bb in range(nb) for p in range(HEADS // 2)]:
        ct = ct_ref[bb, p]
        lsl = slice(p * PAIR_W, (p + 1) * PAIR_W)
        m_pair = [m_ref[bb, 2 * p + hh][0:1, 0:1] for hh in range(2)]
        for c in range(lb // ML_CHUNK):
            rsl = slice(bb * lb + c * ML_CHUNK, bb * lb + (c + 1) * ML_CHUNK)
            q_c = qf[rsl, lsl]
            k_c = kf[rsl, lsl]
            k_cb = k_c.astype(BF16)
            inter_mm = _dot_nt(q_c.astype(BF16), ct.astype(BF16))
            wt_cols, decays = [], []
            for hh in range(2):
                head = 2 * p + hh
                m_st = m_pair[hh]
                b_col = bc[rsl, HEADS + head:HEADS + head + 1]
                i_col = gc[rsl, head:head + 1]
                b_row = br[HEADS + head:HEADS + head + 1, rsl]
                i_row = gr[head:head + 1, rsl]
                dmat = jnp.where(causal, b_col - b_row + i_row, -jnp.inf)
                inter = b_col + m_st
                m_row = jnp.maximum(inter, jnp.max(dmat, axis=-1, keepdims=True))
                wts = jnp.exp(dmat - m_row)
                in_head = (lane >= hh * DK) & (lane < (hh + 1) * DK)
                qm = jnp.where(in_head, q_c, 0.0).astype(BF16)
                sc = _dot_nt(qm, k_cb) * wts
                w_inter = jnp.exp(inter - m_row)
                num = _dot(sc.astype(BF16), vb[rsl, head * DV:(head + 1) * DV]) \
                    + w_inter * inter_mm[:, hh * DV:(hh + 1) * DV]
                den = jnp.sum(sc, axis=-1, keepdims=True) \
                    + w_inter * inter_mm[:, 2 * DV + hh:2 * DV + hh + 1]
                o_ref[rsl, head * DV:(head + 1) * DV] = num / jnp.maximum(jnp.abs(den), jnp.exp(-m_row))
                g_tot = b_col[ML_CHUNK - 1:ML_CHUNK, :]
                tail = g_tot - b_col + i_col
                m_new = jnp.maximum(g_tot + m_st, jnp.max(tail, axis=0, keepdims=True))
                wt_cols.append(jnp.exp(tail - m_new))
                decays.append(jnp.exp(g_tot + m_st - m_new))
                m_pair[hh] = m_new
            wk = (k_c * jnp.where(lane < DK, wt_cols[0], wt_cols[1])).astype(BF16)
            vp = jnp.concatenate([vb[rsl, p * 2 * DV:(p + 1) * 2 * DV], ones_blk], axis=1)
            upd = _dot_tn(vp, wk)
            ct = ct * jnp.where(lane < DK, decays[0], decays[1]) + jnp.where(st_mask, upd, 0.0)
        ct_ref[bb, p] = ct
        for hh in range(2):
            m_ref[bb, 2 * p + hh] = jnp.broadcast_to(m_pair[hh], m_ref.shape[2:])

    ng = ng_ref[...]
    for head in range(HEADS):
        hsl = slice(head * DV, (head + 1) * DV)
        y = _head_norm(o_ref[:, hsl]) * ng[:, hsl] * _sigmoid(o_pre[:, hsl])
        y_ref[:, :, hsl] = y.reshape(nb, lb, DV).astype(y_ref.dtype)


def _ml_mixer(x, w, b, wgc, bgc, wgr, bgr, cw, cb, ng, tri, trit):
    bsz, s, d = x.shape
    wcols = w.shape[1]
    nb = ML_SEQ_PER_STEP
    return pl.pallas_call(
        _ml_kernel,
        grid=(bsz // nb, s // SEQ_BLOCK),
        in_specs=[
            pl.BlockSpec((nb, SEQ_BLOCK, d), lambda i, j: (i, j, 0)),
            _full((d, wcols)), _full((1, wcols)),
            _full((2, d, LANES)), _full((1, LANES)),
            _full((2, 8, d)), _full((8, 1)),
            _full((ML_CONV, 2 * HEADS * DK)), _full((1, 2 * HEADS * DK)),
            _full((1, BRANCH_W)),
            _full((SEQ_BLOCK, SEQ_BLOCK)), _full((SEQ_BLOCK, SEQ_BLOCK)),
        ],
        out_specs=pl.BlockSpec((nb, SEQ_BLOCK, BRANCH_W), lambda i, j: (i, j, 0)),
        out_shape=jax.ShapeDtypeStruct((bsz, s, BRANCH_W), BF16),
        scratch_shapes=[pltpu.VMEM((nb, HEADS // 2, ML_ST_ROWS, PAIR_W), F32),
                        pltpu.VMEM((nb, HEADS, 8, LANES), F32),
                        pltpu.VMEM((nb, 8, 2 * HEADS * DK), F32),
                        pltpu.VMEM((nb * SEQ_BLOCK, BRANCH_W), F32)],
        compiler_params=_cparams(("arbitrary", "arbitrary")),
        name="mlstm_mixer",
    )(x, w, b, wgc, bgc, wgr, bgr, cw, cb, ng, tri, trit)


S5_HALF = 256
S5_HALF_STATE = (S5_HALF // S5_GROUP) * S5_STATE


def _s5_kernel(bsz, nc, u_ref, ec_ref, ftc_ref, c0_ref, d_ref, a_ref, y_ref,
               in_ref, xs_ref, xsb_ref, kexp_ref, c0x_ref):
    s = pl.program_id(1)
    lc = S5_CHUNK
    rows = bsz * nc
    sw = S5_HALF_STATE

    def expand_state(blk):
        tile = jnp.concatenate([blk[:, 0:LANES]] * (sw // LANES) + [blk[:, LANES:2 * LANES]] * (sw // LANES),
                               axis=1)
        r = lax.broadcasted_iota(jnp.int32, tile.shape, 0) // S5_GROUP
        c = (lax.broadcasted_iota(jnp.int32, tile.shape, 1) % sw) // S5_STATE
        return jnp.where(r == c, tile, 0.0).astype(BF16)

    @pl.when(s == 0)
    def _():
        c0x_ref[...] = expand_state(c0_ref[...])
        xs_ref[...] = jnp.zeros_like(xs_ref)

    @pl.when(s < lc)
    def _():
        u = u_ref[...].reshape(rows, u_ref.shape[-1])
        in_ref[s] = u
        et = expand_state(ec_ref[0])
        kblk = _dot_nt(et, c0x_ref[...])
        r = lax.broadcasted_iota(jnp.int32, kblk.shape, 0)
        c = lax.broadcasted_iota(jnp.int32, kblk.shape, 1)
        kblk = kblk + jnp.where((r == c) & (s == lc - 1), d_ref[...], 0.0)
        kexp_ref[pl.ds(pl.multiple_of(s * S5_HALF, S5_HALF), S5_HALF), :] = kblk.astype(BF16)
        for jc in range(2 * sw // S5_HALF):
            part = _dot(u, et[:, jc * S5_HALF:(jc + 1) * S5_HALF])
            for jj in range(S5_HALF // LANES):
                xs_ref[jc * (S5_HALF // LANES) + jj] += part[:, jj * LANES:(jj + 1) * LANES]

    @pl.when(s == lc - 1)
    def _():
        ar = a_ref[0, 0:1, :]
        ai = a_ref[0, 1:2, :]
        nt = sw // LANES

        def body(c, carry):
            sr, si = carry
            idx = pl.ds(c, bsz, stride=nc)
            xr = jnp.concatenate([xs_ref[j, idx, :] for j in range(nt)], axis=1)
            xi = jnp.concatenate([xs_ref[nt + j, idx, :] for j in range(nt)], axis=1)
            for j in range(nt):
                xs_ref[j, idx, :] = sr[:, j * LANES:(j + 1) * LANES]
                xs_ref[nt + j, idx, :] = si[:, j * LANES:(j + 1) * LANES]
            return ar * sr - ai * si + xr, ar * si + ai * sr + xi

        zero = jnp.zeros((bsz, sw), F32)
        lax.fori_loop(0, nc, body, (zero, zero))
        for j in range(2 * nt):
            xsb_ref[:, j * LANES:(j + 1) * LANES] = xs_ref[j].astype(BF16)

    for lo in range(lc):
        @pl.when(s == lc + lo)
        def _(lo=lo):
            acc = _dot_nt(xsb_ref[...], expand_state(ftc_ref[0]))
            for l in range(lo + 1):
                j = lc - 1 - lo + l
                acc = acc + _dot(in_ref[l], kexp_ref[j * S5_HALF:(j + 1) * S5_HALF, :])
            y_ref[...] = acc.reshape(y_ref.shape).astype(y_ref.dtype)


def _s5_mixer(u, ec, ftc, c0, d, a):
    bsz, s, _ = u.shape
    lc = S5_CHUNK
    nc = s // lc
    u3 = u.reshape(bsz, nc, lc * BRANCH_W)
    nh = BRANCH_W // S5_HALF
    y = pl.pallas_call(
        functools.partial(_s5_kernel, bsz, nc),
        grid=(nh, 2 * lc),
        in_specs=[
            pl.BlockSpec((bsz, nc, S5_HALF), lambda h, t: (0, 0, jnp.minimum(t, lc - 1) * nh + h)),
            pl.BlockSpec((1, S5_HALF, 2 * LANES), lambda h, t: (jnp.minimum(t, lc - 1), h, 0)),
            pl.BlockSpec((1, S5_HALF, 2 * LANES), lambda h, t: (jnp.maximum(t - lc, 0), h, 0)),
            pl.BlockSpec((S5_HALF, 2 * LANES), lambda h, t: (h, 0)),
            pl.BlockSpec((1, S5_HALF), lambda h, t: (0, h)),
            pl.BlockSpec((1, 2, S5_HALF_STATE), lambda h, t: (h, 0, 0)),
        ],
        out_specs=pl.BlockSpec((bsz, nc, S5_HALF), lambda h, t: (0, 0, jnp.maximum(t - lc, 0) * nh + h)),
        out_shape=jax.ShapeDtypeStruct((bsz, nc, lc * BRANCH_W), BF16),
        scratch_shapes=[pltpu.VMEM((lc, bsz * nc, S5_HALF), BF16),
                        pltpu.VMEM((2 * S5_HALF_STATE // LANES, bsz * nc, LANES), F32),
                        pltpu.VMEM((bsz * nc, 2 * S5_HALF_STATE), BF16),
                        pltpu.VMEM((lc * S5_HALF, S5_HALF), BF16),
                        pltpu.VMEM((S5_HALF, 2 * S5_HALF_STATE), BF16)],
        compiler_params=_cparams(("arbitrary", "arbitrary")),
        name="s5_mixer",
    )(u3, ec, ftc, c0, d, a)
    return y.reshape(bsz * s, BRANCH_W)


def _s5_tables(a_re, a_im, log_dt, b_re, b_im, c_re, c_im):
    nl = a_re.shape[0]
    g, p_, n, lc = S5_GROUPS, S5_STATE, S5_GROUP, S5_CHUNK
    a_re, a_im, b_re, b_im, c_re, c_im = (v.astype(F32) for v in (a_re, a_im, b_re, b_im, c_re, c_im))
    dt = jnp.exp(log_dt.astype(F32))[..., None]
    adt_r = (a_re * dt)[:, None, :, None, :]
    adt_i = (a_im * dt)[:, None, :, None, :]

    def powers(steps):
        st = jnp.asarray(np.asarray(steps, np.float32).reshape(1, -1, 1, 1, 1))
        mag = jnp.exp(adt_r * st)
        return mag * jnp.cos(adt_i * st), mag * jnp.sin(adt_i * st)

    lbr, lbi = powers([1])
    are, aim = a_re[:, None, :, None, :], a_im[:, None, :, None, :]
    den = are * are + aim * aim
    qr = ((lbr - 1.0) * are + lbi * aim) / den
    qi = (lbi * are - (lbr - 1.0) * aim) / den
    bt_re, bt_im = b_re.transpose(0, 1, 3, 2)[:, None], b_im.transpose(0, 1, 3, 2)[:, None]
    bbr = qr * bt_re - qi * bt_im
    bbi = qr * bt_im + qi * bt_re
    cr, ci = c_re[:, None], c_im[:, None]
    pr, pi = powers(np.arange(lc - 1, -1, -1))
    er, ei = pr * bbr - pi * bbi, pr * bbi + pi * bbr
    ec = jnp.concatenate([er, er, ei, ei], axis=-1).reshape(nl, lc, g * n, 4 * p_)
    pfr, pfi = powers(np.arange(1, lc + 1))
    fr, fi = cr * pfr - ci * pfi, -(cr * pfi + ci * pfr)
    ftc = jnp.concatenate([fr, fr, fi, fi], axis=-1).reshape(nl, lc, g * n, 4 * p_)
    c0 = jnp.concatenate([c_re, c_re, -c_im, -c_im], axis=-1).reshape(nl, g * n, 4 * p_)
    nh = BRANCH_W // S5_HALF
    ar, ai = powers([lc])
    a_tab = jnp.stack([ar.reshape(nl, nh, S5_HALF_STATE), ai.reshape(nl, nh, S5_HALF_STATE)], axis=2)
    return ec, ftc, c0, a_tab


def _merge_kernel(x_ref, yg_ref, y5_ref, ym_ref, wgate_ref, bgate_ref, wglu_ref, bglu_ref,
                  wup_ref, wo_ref, g_ref, b_ref, o_ref):
    x = x_ref[...]
    xb = x.astype(BF16)
    y5 = _gelu_tanh(y5_ref[...].astype(F32))
    y5 = y5 * _sigmoid(_dot(y5.astype(BF16), wglu_ref[...]) + bglu_ref[...])
    ys = (yg_ref[...], y5.astype(BF16), ym_ref[...])
    acc = None
    for r in range(N_BRANCH):
        gate = _sigmoid(_dot(xb, wgate_ref[:, r * D_MODEL:(r + 1) * D_MODEL])
                        + bgate_ref[:, r * D_MODEL:(r + 1) * D_MODEL])
        term = gate * _dot(ys[r], wup_ref[r])
        acc = term if acc is None else acc + term
    mix = _dot(acc.astype(BF16), wo_ref[...])
    o_ref[...] = _layer_norm(DN_ALPHA * x + mix, g_ref[...], b_ref[...])


def _merge(x2, yg, y5, ym, wgate, bgate, wglu, bglu, wup, wo, g, b, tm=512):
    t, dm = x2.shape
    row = lambda w: pl.BlockSpec((tm, w), lambda i: (i, 0))
    return pl.pallas_call(
        _merge_kernel,
        grid=(t // tm,),
        in_specs=[row(dm), row(BRANCH_W), row(BRANCH_W), row(BRANCH_W),
                  _full(wgate.shape), _full(bgate.shape), _full(wglu.shape), _full(bglu.shape),
                  _full(wup.shape), _full(wo.shape), _full(g.shape), _full(b.shape)],
        out_specs=row(dm),
        out_shape=jax.ShapeDtypeStruct((t, dm), F32),
        compiler_params=_cparams(("parallel",)),
        name="merge_ln1",
    )(x2, yg, y5, ym, wgate, bgate, wglu, bglu, wup, wo, g, b)


def _ple_ln2(x, xb, f, p_ref, pwg_ref, pwp_ref, g_ref, b_ref):
    e = _sigmoid(_dot(xb, pwg_ref[...])) * _dot(p_ref[...].astype(BF16), pwp_ref[...])
    return _layer_norm(DN_ALPHA * x + f + e, g_ref[...], b_ref[...])


def _ffn_kernel(x_ref, p_ref, wg_ref, wu_ref, wd_ref, pwg_ref, pwp_ref, g_ref, b_ref, o_ref):
    x = x_ref[...]
    xb = x.astype(BF16)
    hid = (_silu(_dot(xb, wg_ref[...])) * _dot(xb, wu_ref[...])).astype(BF16)
    f = _dot(hid, wd_ref[...])
    o_ref[...] = _ple_ln2(x, xb, f, p_ref, pwg_ref, pwp_ref, g_ref, b_ref)


def _ffn_layer(x2, p3, layer, wg, wu, wd, pwg, pwp, g, b, tm=256):
    t, dm = x2.shape
    row = lambda w: pl.BlockSpec((tm, w), lambda i: (i, 0))
    return pl.pallas_call(
        _ffn_kernel,
        grid=(t // tm,),
        in_specs=[row(dm), pl.BlockSpec((None, tm, PLE_DIM), lambda i: (layer, i, 0)),
                  _full(wg.shape), _full(wu.shape), _full(wd.shape),
                  _full(pwg.shape), _full(pwp.shape), _full(g.shape), _full(b.shape)],
        out_specs=row(dm),
        out_shape=jax.ShapeDtypeStruct((t, dm), F32),
        compiler_params=_cparams(("parallel",)),
        name="ffn_ple_ln2",
    )(x2, p3, wg, wu, wd, pwg, pwp, g, b)


def _router_kernel(x_ref, w_ref, b_ref, tril_ref, sel_ref, pr_ref, rk_ref, cnt_ref, base_ref):
    @pl.when(pl.program_id(0) == 0)
    def _():
        base_ref[...] = jnp.zeros_like(base_ref)

    xh, xl = _split2(x_ref[...])
    logits = _dot(xh, w_ref[0]) + _dot(xl, w_ref[0]) + _dot(xh, w_ref[1]) + b_ref[...]
    lane = lax.broadcasted_iota(jnp.int32, logits.shape, 1)
    neg = -jnp.inf
    logits = jnp.where(lane < N_EXPERTS, logits, neg)
    m1 = jnp.max(logits, axis=-1, keepdims=True)
    i1 = jnp.min(jnp.where(logits == m1, lane, LANES), axis=-1, keepdims=True)
    rest = jnp.where(lane == i1, neg, logits)
    m2 = jnp.max(rest, axis=-1, keepdims=True)
    i2 = jnp.min(jnp.where(rest == m2, lane, LANES), axis=-1, keepdims=True)
    e2 = jnp.exp(m2 - m1)
    p1 = 1.0 / (1.0 + e2)
    p2 = e2 / (1.0 + e2)
    hot = jnp.where((lane == i1) | (lane == i2), 1.0, 0.0)
    base = base_ref[0:1, :]
    before = _dot(tril_ref[...], hot.astype(BF16)) + base
    r1 = jnp.sum(jnp.where(lane == i1, before, 0.0), axis=-1, keepdims=True)
    r2 = jnp.sum(jnp.where(lane == i2, before, 0.0), axis=-1, keepdims=True)
    sel_ref[...] = jnp.where(lane == 0, i1, jnp.where(lane == 1, i2, 0))
    pr_ref[...] = jnp.where(lane == 0, p1, jnp.where(lane == 1, p2, 0.0))
    rk_ref[...] = jnp.where(lane == 0, r1, jnp.where(lane == 1, r2, 0.0))
    total = base + jnp.sum(hot, axis=0, keepdims=True)
    base_ref[...] = jnp.broadcast_to(total, base_ref.shape)
    cnt_ref[...] = jnp.broadcast_to(total, cnt_ref.shape)


def _router(x2, w, b, tm=512):
    t, dm = x2.shape
    i = np.arange(tm)
    tril = jnp.asarray((i[:, None] > i[None, :]).astype(np.float32), BF16)
    blk = pl.BlockSpec((tm, LANES), lambda i: (i, 0))
    return pl.pallas_call(
        _router_kernel,
        grid=(t // tm,),
        in_specs=[pl.BlockSpec((tm, dm), lambda i: (i, 0)), _full(w.shape), _full(b.shape), _full((tm, tm))],
        out_specs=[blk, blk, blk, pl.BlockSpec((8, LANES), lambda i: (0, 0))],
        out_shape=[jax.ShapeDtypeStruct((t, LANES), jnp.int32), jax.ShapeDtypeStruct((t, LANES), F32),
                   jax.ShapeDtypeStruct((t, LANES), F32), jax.ShapeDtypeStruct((8, LANES), F32)],
        scratch_shapes=[pltpu.VMEM((8, LANES), F32)],
        compiler_params=_cparams(("arbitrary",)),
        name="moe_router",
    )(x2, w, b, tril)


TOP_K = 2
MOE_TILE = 1024
MOE_FF_TILE = 512
DISPATCH_TILE = 512
DMA_ISSUE_UNROLL = 8


def _row_copy_wait(src_rows, dst_rows, sem):
    pltpu.make_async_copy(src_rows, dst_rows, sem).wait()


def _dispatch_kernel(pos_ref, x_ref, xs_ref, sem):
    tm = x_ref.shape[0]
    base = pl.program_id(0) * tm * TOP_K

    def body(r, carry):
        for k in range(TOP_K):
            dst = pos_ref[base + r * TOP_K + k]
            pltpu.make_async_copy(x_ref.at[pl.ds(r, 1), :], xs_ref.at[pl.ds(dst, 1), :], sem).start(priority=k)
        return carry

    lax.fori_loop(0, tm, body, 0, unroll=DMA_ISSUE_UNROLL)
    for _ in range(TOP_K):
        _row_copy_wait(x_ref, xs_ref.at[pl.ds(0, tm), :], sem)


def _dispatch(pos_flat, x2, tm=DISPATCH_TILE):
    t, dm = x2.shape
    return pl.pallas_call(
        _dispatch_kernel,
        grid_spec=pltpu.PrefetchScalarGridSpec(
            num_scalar_prefetch=1,
            grid=(t // tm,),
            in_specs=[pl.BlockSpec((tm, dm), lambda i, pos: (i, 0))],
            out_specs=pl.BlockSpec(memory_space=pl.ANY),
            scratch_shapes=[pltpu.SemaphoreType.DMA],
        ),
        out_shape=jax.ShapeDtypeStruct((t * TOP_K, dm), F32),
        compiler_params=_cparams(("arbitrary",)),
        name="moe_dispatch",
    )(pos_flat, x2)


def _moe_group_kernel(tile_ref, exp_ref, lo_ref, hi_ref, xs_ref, wg_ref, wu_ref, wd_ref, o_ref,
                      acc_ref, xb_ref):
    i = pl.program_id(0)
    f = pl.program_id(1)
    lo = lo_ref[i]
    hi = hi_ref[i]
    tm = xs_ref.shape[0]

    @pl.when(hi > lo)
    def _():
        @pl.when((lo == 0) & (f == 0))
        def _():
            acc_ref[...] = jnp.zeros_like(acc_ref)

        @pl.when(f == 0)
        def _():
            xb_ref[...] = xs_ref[...].astype(BF16)

        xb = xb_ref[...]
        hid = _silu(_dot(xb, wg_ref[0].astype(BF16))) * _dot(xb, wu_ref[0].astype(BF16))
        rid = lax.broadcasted_iota(jnp.int32, (tm, 1), 0)
        hid = jnp.where((rid >= lo) & (rid < hi), hid, 0.0).astype(BF16)
        acc_ref[...] += _dot(hid, wd_ref[0].astype(BF16))

        @pl.when((hi == tm) & (f == pl.num_programs(1) - 1))
        def _():
            o_ref[...] = acc_ref[...]


def _moe_grouped(items, xs, wg, wu, wd, tm=MOE_TILE, tf=MOE_FF_TILE):
    tile, exp, lo, hi = items
    rows, dm = xs.shape
    dff = wg.shape[2]
    return pl.pallas_call(
        _moe_group_kernel,
        grid_spec=pltpu.PrefetchScalarGridSpec(
            num_scalar_prefetch=4,
            grid=(tile.shape[0], dff // tf),
            in_specs=[pl.BlockSpec((tm, dm), lambda i, f, tl, ex, lo_, hi_: (tl[i], 0)),
                      pl.BlockSpec((1, dm, tf), lambda i, f, tl, ex, lo_, hi_: (ex[i], 0, f)),
                      pl.BlockSpec((1, dm, tf), lambda i, f, tl, ex, lo_, hi_: (ex[i], 0, f)),
                      pl.BlockSpec((1, tf, dm), lambda i, f, tl, ex, lo_, hi_: (ex[i], f, 0))],
            out_specs=pl.BlockSpec((tm, dm), lambda i, f, tl, ex, lo_, hi_: (tl[i], 0)),
            scratch_shapes=[pltpu.VMEM((tm, dm), F32), pltpu.VMEM((tm, dm), BF16)],
        ),
        out_shape=jax.ShapeDtypeStruct((rows, dm), F32),
        compiler_params=_cparams(("arbitrary", "arbitrary")),
        name="moe_grouped",
    )(tile, exp, lo, hi, xs, wg, wu, wd)


def _moe_items(counts, n_rows, tm=MOE_TILE):
    n_tiles = n_rows // tm
    ends = jnp.cumsum(counts)
    cuts = jnp.sort(jnp.concatenate([jnp.arange(n_tiles + 1, dtype=jnp.int32) * tm, ends[:-1]]))
    start, stop = cuts[:-1], cuts[1:]
    tile = jnp.minimum(start // tm, n_tiles - 1)
    exp = jnp.minimum(jnp.sum(ends[None, :] <= start[:, None], axis=1), N_EXPERTS - 1)
    lo = start - tile * tm
    hi = stop - tile * tm
    return tile.astype(jnp.int32), exp.astype(jnp.int32), lo.astype(jnp.int32), hi.astype(jnp.int32)


def _combine_kernel(pos_ref, x_ref, pr_ref, p_ref, pwg_ref, pwp_ref, g_ref, b_ref, ys_ref, o_ref, gat_ref, sem):
    tm = x_ref.shape[0]
    base = pl.program_id(0) * tm * TOP_K

    def body(r, carry):
        for k in range(TOP_K):
            src = pos_ref[base + r * TOP_K + k]
            pltpu.make_async_copy(ys_ref.at[pl.ds(src, 1), :], gat_ref.at[k, pl.ds(r, 1), :],
                                  sem).start(priority=k)
        return carry

    lax.fori_loop(0, tm, body, 0, unroll=DMA_ISSUE_UNROLL)
    x = x_ref[...]
    xb = x.astype(BF16)
    e = _sigmoid(_dot(xb, pwg_ref[...])) * _dot(p_ref[...].astype(BF16), pwp_ref[...])
    for k in range(TOP_K):
        _row_copy_wait(ys_ref.at[pl.ds(0, tm), :], gat_ref.at[k], sem)
    pr = pr_ref[...]
    f = pr[:, 0:1] * gat_ref[0]
    for k in range(1, TOP_K):
        f = f + pr[:, k:k + 1] * gat_ref[k]
    o_ref[...] = _layer_norm(DN_ALPHA * x + f + e, g_ref[...], b_ref[...])


def _combine_layer(pos_flat, x2, pr, p3, layer, pwg, pwp, g, b, ys, tm=DISPATCH_TILE):
    t, dm = x2.shape
    row = lambda w: pl.BlockSpec((tm, w), lambda i, pos: (i, 0))
    full = lambda a: pl.BlockSpec(a.shape, lambda i, pos: (0,) * a.ndim, pipeline_mode=pl.Buffered(1))
    return pl.pallas_call(
        _combine_kernel,
        grid_spec=pltpu.PrefetchScalarGridSpec(
            num_scalar_prefetch=1,
            grid=(t // tm,),
            in_specs=[row(dm), row(LANES), pl.BlockSpec((None, tm, PLE_DIM), lambda i, pos: (layer, i, 0)),
                      full(pwg), full(pwp), full(g), full(b),
                      pl.BlockSpec(memory_space=pl.ANY)],
            out_specs=row(dm),
            scratch_shapes=[pltpu.VMEM((TOP_K, tm, dm), F32), pltpu.SemaphoreType.DMA],
        ),
        out_shape=jax.ShapeDtypeStruct((t, dm), F32),
        compiler_params=_cparams(("arbitrary",)),
        name="moe_combine_ple_ln2",
    )(pos_flat, x2, pr, p3, pwg, pwp, g, b, ys)


def _row(v):
    return v.reshape(1, -1).astype(F32)


def _pad_lanes(w):
    return jnp.pad(w, ((0, 0), (0, LANES - w.shape[1])))


def kernel(x, p, w_in, b_in, gla_w_a2, gla_b_a2, gla_norm_g, s5_a_re, s5_a_im, s5_log_dt, s5_b_re, s5_b_im,
           s5_c_re, s5_c_im, s5_d, s5_w_glu, s5_b_glu, ml_conv_w, ml_conv_b, ml_norm_g, w_up, w_o, ln1_g, ln1_b,
           ffn_wg, ffn_wu, ffn_wd, moe_router, moe_router_b, moe_wg, moe_wu, moe_wd, ple_w_gate, ple_w_proj,
           ln2_g, ln2_b):
    bsz, s, dm = x.shape
    t = bsz * s
    hi = lax.Precision.HIGHEST
    o = IN_OFF
    tri = jnp.asarray(_chunk_tri(SEQ_BLOCK, CHUNK), BF16)
    ml_tri_np = _chunk_tri(SEQ_BLOCK, ML_CHUNK)
    ml_tri = jnp.asarray(ml_tri_np, BF16)
    ml_trit = jnp.asarray(ml_tri_np.T, BF16)
    s5_ec, s5_ftc, s5_c0, s5_atab = _s5_tables(s5_a_re, s5_a_im, s5_log_dt, s5_b_re, s5_b_im, s5_c_re, s5_c_im)
    p3 = p.reshape(DEPTH, t, PLE_DIM)

    w_in_b = w_in.astype(BF16)
    for i in range(DEPTH):
        w, b = w_in_b[i], b_in[i]
        sl = lambda k: (w[:, o[k]:o[k + 1]], b[o[k]:o[k + 1]])
        (wq, bq), (wk, bk), (wv, bv), (_, ba), (wg_, bg_) = sl(0), sl(1), sl(2), sl(3), sl(4)
        wz = jnp.dot(w_in[i, :, o[3]:o[4]], gla_w_a2[i], precision=hi).astype(BF16)
        bz = jnp.dot(ba, gla_w_a2[i], precision=hi) + gla_b_a2[i]
        (wu_, bu_) = sl(5)
        w_gla = jnp.concatenate([wq, wk, wz, wv, wg_, wu_], axis=1)
        b_gla = _row(jnp.concatenate([bq, bk, bz, bv, bg_, bu_]))
        y_gla, u = _gla_mixer(x, w_gla, b_gla, _row(gla_norm_g[i]), tri)

        x2 = x.reshape(t, dm)
        y5 = _s5_mixer(u, s5_ec[i], s5_ftc[i], s5_c0[i], _row(s5_d[i]), s5_atab[i])

        (wmq, bmq), (wmk, bmk), (wmv, bmv), (wmo, bmo) = sl(6), sl(7), sl(8), sl(11)
        w_ml = jnp.concatenate([wmq, wmk, wmv, wmo], axis=1)
        b_ml = _row(jnp.concatenate([bmq, bmk, bmv, bmo]))
        w_if = w_in[i, :, o[9]:o[11]]
        b_if = b[o[9]:o[11]]
        if_h = w_if.astype(BF16)
        if_l = (w_if - if_h.astype(F32)).astype(BF16)
        wgc = jnp.stack([_pad_lanes(if_h), _pad_lanes(if_l)])
        wgr = jnp.stack([if_h.T, if_l.T])
        y_ml = _ml_mixer(x, w_ml, b_ml, wgc, _pad_lanes(_row(b_if)), wgr, b_if.reshape(-1, 1).astype(F32),
                         ml_conv_w[i].astype(F32), _row(ml_conv_b[i]), _row(ml_norm_g[i]), ml_tri, ml_trit)

        (wgt, bgt) = sl(12)
        x1 = _merge(x2, y_gla.reshape(t, BRANCH_W), y5, y_ml.reshape(t, BRANCH_W),
                    wgt, _row(bgt), s5_w_glu[i].astype(BF16), _row(s5_b_glu[i]),
                    w_up[i].astype(BF16), w_o[i].astype(BF16), _row(ln1_g[i]), _row(ln1_b[i]))

        pwg = ple_w_gate[i].astype(BF16)
        pwp = ple_w_proj[i].astype(BF16)
        j = i // 2
        if i % 2 == 0:
            x2n = _ffn_layer(x1, p3, i, ffn_wg[j].astype(BF16), ffn_wu[j].astype(BF16), ffn_wd[j].astype(BF16),
                             pwg, pwp, _row(ln2_g[i]), _row(ln2_b[i]))
        else:
            wr = moe_router[j]
            wr_h = wr.astype(BF16)
            wr_l = (wr - wr_h.astype(F32)).astype(BF16)
            sel, pr, rk, cnt = _router(x1, jnp.stack([_pad_lanes(wr_h), _pad_lanes(wr_l)]),
                                       _pad_lanes(_row(moe_router_b[j])))
            counts = cnt[0, :N_EXPERTS].astype(jnp.int32)
            starts = jnp.cumsum(counts) - counts
            sel2 = sel[:, :TOP_K]
            pos = rk[:, :TOP_K].astype(jnp.int32) + jnp.sum(
                jnp.where(sel2[..., None] == jnp.arange(N_EXPERTS), starts, 0), axis=-1)
            pos_flat = pos.reshape(-1)
            xs = _dispatch(pos_flat, x1)
            ys = _moe_grouped(_moe_items(counts, t * TOP_K), xs, moe_wg[j], moe_wu[j], moe_wd[j])
            x2n = _combine_layer(pos_flat, x1, pr, p3, i, pwg, pwp, _row(ln2_g[i]), _row(ln2_b[i]), ys)
        x = x2n.reshape(bsz, s, dm)
    return x
```

```python
import functools
import math

import numpy as np
import jax
import jax.numpy as jnp
from jax import lax
from jax.experimental import pallas as pl
from jax.experimental.pallas import tpu as pltpu

F32 = jnp.float32
BF16 = jnp.bfloat16

D_MODEL = 1024
DEPTH = 2
N_BRANCH = 3
BRANCH_W = 512
HEADS = 4
DK = 64
DV = BRANCH_W // HEADS
GLA_RANK = 16
GLA_TAU = 16.0
CHUNK = 64
ML_CHUNK = 256
S5_GROUP = 16
S5_GROUPS = BRANCH_W // S5_GROUP
S5_STATE = 64
S5_CHUNK = 16
ML_CONV = 4
D_FF = 2816
N_EXPERTS = 8
D_FF_EXPERT = 3584
PLE_DIM = 256
DN_ALPHA = (2.0 * DEPTH) ** 0.25
LN_EPS = 1e-5

IN_WIDTHS = (
    HEADS * DK, HEADS * DK, BRANCH_W, GLA_RANK, BRANCH_W,
    BRANCH_W,
    HEADS * DK, HEADS * DK, BRANCH_W, HEADS, HEADS, BRANCH_W,
    N_BRANCH * D_MODEL,
)
IN_OFF = tuple(int(o) for o in np.concatenate([[0], np.cumsum(IN_WIDTHS)]))

LANES = 128
SEQ_BLOCK = 256
GLA_SEQ_PER_STEP = 4
ML_SEQ_PER_STEP = 1
PAIR_W = 2 * DK
VMEM_LIMIT = 56 * 1024 * 1024


def _cparams(sem):
    return pltpu.CompilerParams(dimension_semantics=sem, vmem_limit_bytes=VMEM_LIMIT)


def _dot(a, b):
    return jnp.dot(a, b, preferred_element_type=F32)


def _dot_nt(a, b):
    return lax.dot_general(a, b, (((1,), (1,)), ((), ())), preferred_element_type=F32)


def _dot_tn(a, b):
    return lax.dot_general(a, b, (((0,), (0,)), ((), ())), preferred_element_type=F32)


def _split3(a):
    hi = a.astype(BF16)
    r = a - hi.astype(F32)
    mid = r.astype(BF16)
    lo = (r - mid.astype(F32)).astype(BF16)
    return hi, mid, lo


def _split2(a):
    hi = a.astype(BF16)
    lo = (a - hi.astype(F32)).astype(BF16)
    return hi, lo


def _log_sigmoid(x):
    return jnp.minimum(x, 0.0) - jnp.log(1.0 + jnp.exp(-jnp.abs(x)))


def _sigmoid(x):
    return 0.5 * jnp.tanh(0.5 * x) + 0.5


def _silu(x):
    return x * _sigmoid(x)


def _gelu_tanh(x):
    return 0.5 * x * (1.0 + jnp.tanh(math.sqrt(2.0 / math.pi) * (x + 0.044715 * (x * x * x))))


def _layer_norm(v, g, b):
    mu = jnp.mean(v, axis=-1, keepdims=True)
    c = v - mu
    var = jnp.mean(c * c, axis=-1, keepdims=True)
    return c * lax.rsqrt(var + LN_EPS) * g + b


def _head_norm(o):
    mu = jnp.mean(o, axis=-1, keepdims=True)
    c = o - mu
    var = jnp.mean(c * c, axis=-1, keepdims=True)
    return c * lax.rsqrt(var + LN_EPS)


def _chunk_tri(n, chunk):
    i = np.arange(n)
    return ((i[:, None] >= i[None, :]) & (i[:, None] // chunk == i[None, :] // chunk)).astype(np.float32)


def _full(shape):
    nd = len(shape)
    return pl.BlockSpec(shape, lambda *_: (0,) * nd, pipeline_mode=pl.Buffered(1))


def _gla_kernel(x_ref, w_ref, b_ref, ng_ref, tri_ref, y_ref, u_ref, st_ref, o_ref):
    @pl.when(pl.program_id(1) == 0)
    def _():
        st_ref[...] = jnp.zeros_like(st_ref)

    hk = HEADS * DK
    nb, lb, d = x_ref.shape
    xb = x_ref[...].reshape(nb * lb, d).astype(BF16)
    h = _dot(xb, w_ref[...]) + b_ref[...]
    q = h[:, 0:hk]
    k = h[:, hk:2 * hk]
    z = h[:, 2 * hk:3 * hk]
    v = h[:, 3 * hk:3 * hk + BRANCH_W]
    g = h[:, 3 * hk + BRANCH_W:3 * hk + 2 * BRANCH_W]
    u_ref[...] = h[:, 3 * hk + 2 * BRANCH_W:3 * hk + 3 * BRANCH_W].reshape(nb, lb, BRANCH_W).astype(u_ref.dtype)

    la = _log_sigmoid(z) * (1.0 / GLA_TAU)
    tri = tri_ref[...]
    la_h, la_m, la_l = _split3(la)
    cum = jnp.concatenate(
        [_dot(tri, la_h[r:r + lb]) + _dot(tri, la_m[r:r + lb]) + _dot(tri, la_l[r:r + lb])
         for r in range(0, nb * lb, lb)], axis=0)
    qd = q * (DK ** -0.5) * jnp.exp(cum)
    ki = k * jnp.exp(-cum)
    vb = v.astype(BF16)

    lane = lax.broadcasted_iota(jnp.int32, (1, PAIR_W), 1)
    row_i = lax.broadcasted_iota(jnp.int32, (CHUNK, CHUNK), 0)
    col_i = lax.broadcasted_iota(jnp.int32, (CHUNK, CHUNK), 1)
    causal = row_i >= col_i
    bd_r = lax.broadcasted_iota(jnp.int32, (2 * DV, PAIR_W), 0)
    bd_c = lax.broadcasted_iota(jnp.int32, (2 * DV, PAIR_W), 1)
    blockdiag = (bd_r >= DV) == (bd_c >= DK)

    for bb, p in [(bb, p) for bb in range(nb) for p in range(HEADS // 2)]:
        st = st_ref[bb, p]
        lsl = slice(p * PAIR_W, (p + 1) * PAIR_W)
        for c in range(lb // CHUNK):
            rsl = slice(bb * lb + c * CHUNK, bb * lb + (c + 1) * CHUNK)
            qd_c = qd[rsl, lsl]
            ki_c = ki[rsl, lsl].astype(BF16)
            cum_c = cum[rsl, lsl]
            last = cum_c[CHUNK - 1:CHUNK, :]
            kt = (k[rsl, lsl] * jnp.exp(last - cum_c)).astype(BF16)
            inter = _dot_nt(qd_c.astype(BF16), st.astype(BF16))
            for hh in range(2):
                head = 2 * p + hh
                in_head = (lane >= hh * DK) & (lane < (hh + 1) * DK)
                qm = jnp.where(in_head, qd_c, 0.0).astype(BF16)
                att = jnp.where(causal, _dot_nt(qm, ki_c), 0.0)
                o_h = _dot(att.astype(BF16), vb[rsl, head * DV:(head + 1) * DV])
                o_ref[rsl, head * DV:(head + 1) * DV] = o_h + inter[:, hh * DV:(hh + 1) * DV]
            upd = _dot_tn(vb[rsl, p * 2 * DV:(p + 1) * 2 * DV], kt)
            st = st * jnp.exp(last) + jnp.where(blockdiag, upd, 0.0)
        st_ref[bb, p] = st

    ng = ng_ref[...]
    for head in range(HEADS):
        hsl = slice(head * DV, (head + 1) * DV)
        y = _head_norm(o_ref[:, hsl]) * ng[:, hsl] * _silu(g[:, hsl])
        y_ref[:, :, hsl] = y.reshape(nb, lb, DV).astype(y_ref.dtype)


def _gla_mixer(x, w, b, ng, tri):
    bsz, s, d = x.shape
    wcols = w.shape[1]
    nb = GLA_SEQ_PER_STEP
    return pl.pallas_call(
        _gla_kernel,
        grid=(bsz // nb, s // SEQ_BLOCK),
        in_specs=[
            pl.BlockSpec((nb, SEQ_BLOCK, d), lambda i, j: (i, j, 0)),
            _full((d, wcols)), _full((1, wcols)), _full((1, BRANCH_W)),
            _full((SEQ_BLOCK, SEQ_BLOCK)),
        ],
        out_specs=[pl.BlockSpec((nb, SEQ_BLOCK, BRANCH_W), lambda i, j: (i, j, 0))] * 2,
        out_shape=[jax.ShapeDtypeStruct((bsz, s, BRANCH_W), BF16)] * 2,
        scratch_shapes=[pltpu.VMEM((nb, HEADS // 2, 2 * DV, PAIR_W), F32),
                        pltpu.VMEM((nb * SEQ_BLOCK, BRANCH_W), F32)],
        compiler_params=_cparams(("arbitrary", "arbitrary")),
        name="gla_mixer",
    )(x, w, b, ng, tri)


ML_ST_ROWS = 2 * DV + LANES


def _ml_kernel(x_ref, w_ref, b_ref, wgc_ref, bgc_ref, wgr_ref, bgr_ref, cw_ref, cb_ref, ng_ref,
               tri_ref, trit_ref, y_ref, ct_ref, m_ref, carry_ref, o_ref):
    @pl.when(pl.program_id(1) == 0)
    def _():
        ct_ref[...] = jnp.zeros_like(ct_ref)
        m_ref[...] = jnp.zeros_like(m_ref)
        carry_ref[...] = jnp.zeros_like(carry_ref)

    hk = HEADS * DK
    nb, lb, d = x_ref.shape
    x32 = x_ref[...].reshape(nb * lb, d)
    xh, xl = _split2(x32)
    h = _dot(xh, w_ref[...]) + b_ref[...]
    qk = h[:, 0:2 * hk]
    v = h[:, 2 * hk:2 * hk + BRANCH_W]
    o_pre = h[:, 2 * hk + BRANCH_W:2 * hk + 2 * BRANCH_W]

    cw = cw_ref[...]
    conv = []
    for bb in range(nb):
        qk_b = qk[bb * lb:(bb + 1) * lb]
        ext = jnp.concatenate([carry_ref[bb], qk_b], axis=0)
        acc = cb_ref[...] + ext[8 - (ML_CONV - 1):8 - (ML_CONV - 1) + lb] * cw[0:1]
        for j in range(1, ML_CONV):
            off = 8 - (ML_CONV - 1) + j
            acc = acc + ext[off:off + lb] * cw[j:j + 1]
        carry_ref[bb] = qk_b[lb - 8:lb]
        conv.append(acc)
    qkc = _silu(jnp.concatenate(conv, axis=0))
    qf = qkc[:, 0:hk]
    kf = qkc[:, hk:2 * hk] * (DK ** -0.5)
    vb = v.astype(BF16)

    gc = (_dot(xh, wgc_ref[0]) + _dot(xl, wgc_ref[0]) + _dot(xh, wgc_ref[1])) + bgc_ref[...]
    gr = (_dot_nt(wgr_ref[0], xh) + _dot_nt(wgr_ref[0], xl) + _dot_nt(wgr_ref[1], xh)) + bgr_ref[...]
    lf_c = _log_sigmoid(gc)
    lf_r = _log_sigmoid(gr)
    tri = tri_ref[...]
    trit = trit_ref[...]
    c_h, c_m, c_l = _split3(lf_c)
    r_h, r_m, r_l = _split3(lf_r)
    blocks = [slice(r, r + lb) for r in range(0, nb * lb, lb)]
    bc = jnp.concatenate([_dot(tri, c_h[r]) + _dot(tri, c_m[r]) + _dot(tri, c_l[r]) for r in blocks],
                         axis=0)
    br = jnp.concatenate([_dot(r_h[:, r], trit) + _dot(r_m[:, r], trit) + _dot(r_l[:, r], trit)
                          for r in blocks], axis=1)

    lane = lax.broadcasted_iota(jnp.int32, (1, PAIR_W), 1)
    row_i = lax.broadcasted_iota(jnp.int32, (ML_CHUNK, ML_CHUNK), 0)
    col_i = lax.broadcasted_iota(jnp.int32, (ML_CHUNK, ML_CHUNK), 1)
    causal = row_i >= col_i
    sr = lax.broadcasted_iota(jnp.int32, (ML_ST_ROWS, PAIR_W), 0)
    sc_ = lax.broadcasted_iota(jnp.int32, (ML_ST_ROWS, PAIR_W), 1)
    first = sc_ < DK
    rows_h0 = (sr < DV) | (sr == 2 * DV)
    rows_h1 = ((sr >= DV) & (sr < 2 * DV)) | (sr == 2 * DV + 1)
    st_mask = (rows_h0 & first) | (rows_h1 & ~first)
    ones_blk = jnp.ones((ML_CHUNK, LANES), BF16)

    for bb, p in [(bb, p) for bb in range(nb) for p in range(HEADS // 2)]:
        ct = ct_ref[bb, p]
        lsl = slice(p * PAIR_W, (p + 1) * PAIR_W)
        m_pair = [m_ref[bb, 2 * p + hh][0:1, 0:1] for hh in range(2)]
        for c in range(lb // ML_CHUNK):
            rsl = slice(bb * lb + c * ML_CHUNK, bb * lb + (c + 1) * ML_CHUNK)
            q_c = qf[rsl, lsl]
            k_c = kf[rsl, lsl]
            k_cb = k_c.astype(BF16)
            inter_mm = _dot_nt(q_c.astype(BF16), ct.astype(BF16))
            wt_cols, decays = [], []
            for hh in range(2):
                head = 2 * p + hh
                m_st = m_pair[hh]
                b_col = bc[rsl, HEADS + head:HEADS + head + 1]
                i_col = gc[rsl, head:head + 1]
                b_row = br[HEADS + head:HEADS + head + 1, rsl]
                i_row = gr[head:head + 1, rsl]
                dmat = jnp.where(causal, b_col - b_row + i_row, -jnp.inf)
                inter = b_col + m_st
                m_row = jnp.maximum(inter, jnp.max(dmat, axis=-1, keepdims=True))
                wts = jnp.exp(dmat - m_row)
                in_head = (lane >= hh * DK) & (lane < (hh + 1) * DK)
                qm = jnp.where(in_head, q_c, 0.0).astype(BF16)
                sc = _dot_nt(qm, k_cb) * wts
                w_inter = jnp.exp(inter - m_row)
                num = _dot(sc.astype(BF16), vb[rsl, head * DV:(head + 1) * DV]) \
                    + w_inter * inter_mm[:, hh * DV:(hh + 1) * DV]
                den = jnp.sum(sc, axis=-1, keepdims=True) \
                    + w_inter * inter_mm[:, 2 * DV + hh:2 * DV + hh + 1]
                o_ref[rsl, head * DV:(head + 1) * DV] = num / jnp.maximum(jnp.abs(den), jnp.exp(-m_row))
                g_tot = b_col[ML_CHUNK - 1:ML_CHUNK, :]
                tail = g_tot - b_col + i_col
                m_new = jnp.maximum(g_tot + m_st, jnp.max(tail, axis=0, keepdims=True))
                wt_cols.append(jnp.exp(tail - m_new))
                decays.append(jnp.exp(g_tot + m_st - m_new))
                m_pair[hh] = m_new
            wk = (k_c * jnp.where(lane < DK, wt_cols[0], wt_cols[1])).astype(BF16)
            vp = jnp.concatenate([vb[rsl, p * 2 * DV:(p + 1) * 2 * DV], ones_blk], axis=1)
            upd = _dot_tn(vp, wk)
            ct = ct * jnp.where(lane < DK, decays[0], decays[1]) + jnp.where(st_mask, upd, 0.0)
        ct_ref[bb, p] = ct
        for hh in range(2):
            m_ref[bb, 2 * p + hh] = jnp.broadcast_to(m_pair[hh], m_ref.shape[2:])

    ng = ng_ref[...]
    for head in range(HEADS):
        hsl = slice(head * DV, (head + 1) * DV)
        y = _head_norm(o_ref[:, hsl]) * ng[:, hsl] * _sigmoid(o_pre[:, hsl])
        y_ref[:, :, hsl] = y.reshape(nb, lb, DV).astype(y_ref.dtype)


def _ml_mixer(x, w, b, wgc, bgc, wgr, bgr, cw, cb, ng, tri, trit):
    bsz, s, d = x.shape
    wcols = w.shape[1]
    nb = ML_SEQ_PER_STEP
    return pl.pallas_call(
        _ml_kernel,
        grid=(bsz // nb, s // SEQ_BLOCK),
        in_specs=[
            pl.BlockSpec((nb, SEQ_BLOCK, d), lambda i, j: (i, j, 0)),
            _full((d, wcols)), _full((1, wcols)),
            _full((2, d, LANES)), _full((1, LANES)),
            _full((2, 8, d)), _full((8, 1)),
            _full((ML_CONV, 2 * HEADS * DK)), _full((1, 2 * HEADS * DK)),
            _full((1, BRANCH_W)),
            _full((SEQ_BLOCK, SEQ_BLOCK)), _full((SEQ_BLOCK, SEQ_BLOCK)),
        ],
        out_specs=pl.BlockSpec((nb, SEQ_BLOCK, BRANCH_W), lambda i, j: (i, j, 0)),
        out_shape=jax.ShapeDtypeStruct((bsz, s, BRANCH_W), BF16),
        scratch_shapes=[pltpu.VMEM((nb, HEADS // 2, ML_ST_ROWS, PAIR_W), F32),
                        pltpu.VMEM((nb, HEADS, 8, LANES), F32),
                        pltpu.VMEM((nb, 8, 2 * HEADS * DK), F32),
                        pltpu.VMEM((nb * SEQ_BLOCK, BRANCH_W), F32)],
        compiler_params=_cparams(("arbitrary", "arbitrary")),
        name="mlstm_mixer",
    )(x, w, b, wgc, bgc, wgr, bgr, cw, cb, ng, tri, trit)


S5_IO_STEPS = 4
S5_PERM_BLOCK = 1024
S5_HALF = 256
S5_HALF_STATE = (S5_HALF // S5_GROUP) * S5_STATE


def _s5_kernel(bsz, nc, u_ref, perm_ref, permt_ref, ec_ref, ftc_ref, c0_ref, d_ref, a_ref, y_ref,
               in_ref, yall_ref, xs_ref, xsb_ref, kexp_ref, c0x_ref):
    s = pl.program_id(1)
    lc = S5_CHUNK
    rows = bsz * nc
    sw = S5_HALF_STATE
    pb = perm_ref.shape[0]
    cb = pb // lc
    n_io = S5_IO_STEPS
    subs = rows * lc // (n_io * pb)

    def expand_state(blk):
        tile = jnp.concatenate([blk[:, 0:LANES]] * (sw // LANES) + [blk[:, LANES:2 * LANES]] * (sw // LANES),
                               axis=1)
        r = lax.broadcasted_iota(jnp.int32, tile.shape, 0) // S5_GROUP
        c = (lax.broadcasted_iota(jnp.int32, tile.shape, 1) % sw) // S5_STATE
        return jnp.where(r == c, tile, 0.0).astype(BF16)

    @pl.when(s == 0)
    def _():
        c0x_ref[...] = expand_state(c0_ref[...])
        xs_ref[...] = jnp.zeros_like(xs_ref)

    @pl.when(s < n_io)
    def _():
        for sb in range(subs):
            pu = _dot(perm_ref[...], u_ref[sb * pb:(sb + 1) * pb, :]).astype(BF16)
            r0 = pl.multiple_of((s * subs + sb) * cb, cb)
            for l in range(lc):
                in_ref[l, pl.ds(r0, cb), :] = pu[l * cb:(l + 1) * cb]

    @pl.when((s >= n_io) & (s < n_io + lc))
    def _():
        l = s - n_io
        u = in_ref[l]
        et = expand_state(ec_ref[0])
        kblk = _dot_nt(et, c0x_ref[...])
        r = lax.broadcasted_iota(jnp.int32, kblk.shape, 0)
        c = lax.broadcasted_iota(jnp.int32, kblk.shape, 1)
        kblk = kblk + jnp.where((r == c) & (l == lc - 1), d_ref[...], 0.0)
        kexp_ref[pl.ds(pl.multiple_of(l * S5_HALF, S5_HALF), S5_HALF), :] = kblk.astype(BF16)
        for jc in range(2 * sw // S5_HALF):
            part = _dot(u, et[:, jc * S5_HALF:(jc + 1) * S5_HALF])
            for jj in range(S5_HALF // LANES):
                xs_ref[jc * (S5_HALF // LANES) + jj] += part[:, jj * LANES:(jj + 1) * LANES]

    @pl.when(s == n_io + lc - 1)
    def _():
        ar = a_ref[0, 0:1, :]
        ai = a_ref[0, 1:2, :]
        nt = sw // LANES

        def body(c, carry):
            sr, si = carry
            idx = pl.ds(c, bsz, stride=nc)
            xr = jnp.concatenate([xs_ref[j, idx, :] for j in range(nt)], axis=1)
            xi = jnp.concatenate([xs_ref[nt + j, idx, :] for j in range(nt)], axis=1)
            for j in range(nt):
                xs_ref[j, idx, :] = sr[:, j * LANES:(j + 1) * LANES]
                xs_ref[nt + j, idx, :] = si[:, j * LANES:(j + 1) * LANES]
            return ar * sr - ai * si + xr, ar * si + ai * sr + xi

        zero = jnp.zeros((bsz, sw), F32)
        lax.fori_loop(0, nc, body, (zero, zero))
        for j in range(2 * nt):
            xsb_ref[:, j * LANES:(j + 1) * LANES] = xs_ref[j].astype(BF16)

    for lo in range(lc):
        @pl.when(s == n_io + lc + lo)
        def _(lo=lo):
            acc = _dot_nt(xsb_ref[...], expand_state(ftc_ref[0]))
            for l in range(lo + 1):
                j = lc - 1 - lo + l
                acc = acc + _dot(in_ref[l], kexp_ref[j * S5_HALF:(j + 1) * S5_HALF, :])
            yall_ref[lo] = acc.astype(BF16)

    @pl.when(s >= n_io + 2 * lc)
    def _():
        t = s - (n_io + 2 * lc)
        for sb in range(subs):
            r0 = pl.multiple_of((t * subs + sb) * cb, cb)
            ycat = jnp.concatenate([yall_ref[l, pl.ds(r0, cb), :] for l in range(lc)], axis=0)
            y_ref[sb * pb:(sb + 1) * pb, :] = _dot(permt_ref[...], ycat).astype(y_ref.dtype)


def _s5_mixer(u, ec, ftc, c0, d, a):
    bsz, s, _ = u.shape
    lc = S5_CHUNK
    nc = s // lc
    t = bsz * s
    nh = BRANCH_W // S5_HALF
    n_io = S5_IO_STEPS
    tb = t // n_io
    pb = S5_PERM_BLOCK
    src = (np.arange(pb) % (pb // lc)) * lc + np.arange(pb) // (pb // lc)
    perm_np = (src[:, None] == np.arange(pb)[None, :]).astype(np.float32)
    perm, permt = jnp.asarray(perm_np, BF16), jnp.asarray(perm_np.T, BF16)
    n_steps = 2 * n_io + 2 * lc
    y = pl.pallas_call(
        functools.partial(_s5_kernel, bsz, nc),
        grid=(nh, n_steps),
        in_specs=[
            pl.BlockSpec((tb, S5_HALF), lambda h, t: (jnp.minimum(t, n_io - 1), h)),
            _full((pb, pb)), _full((pb, pb)),
            pl.BlockSpec((1, S5_HALF, 2 * LANES), lambda h, t: (jnp.clip(t - n_io, 0, lc - 1), h, 0)),
            pl.BlockSpec((1, S5_HALF, 2 * LANES), lambda h, t: (jnp.clip(t - n_io - lc, 0, lc - 1), h, 0)),
            pl.BlockSpec((S5_HALF, 2 * LANES), lambda h, t: (h, 0)),
            pl.BlockSpec((1, S5_HALF), lambda h, t: (0, h)),
            pl.BlockSpec((1, 2, S5_HALF_STATE), lambda h, t: (h, 0, 0)),
        ],
        out_specs=pl.BlockSpec((tb, S5_HALF), lambda h, t: (jnp.maximum(t - n_io - 2 * lc, 0), h)),
        out_shape=jax.ShapeDtypeStruct((t, BRANCH_W), BF16),
        scratch_shapes=[pltpu.VMEM((lc, bsz * nc, S5_HALF), BF16),
                        pltpu.VMEM((lc, bsz * nc, S5_HALF), BF16),
                        pltpu.VMEM((2 * S5_HALF_STATE // LANES, bsz * nc, LANES), F32),
                        pltpu.VMEM((bsz * nc, 2 * S5_HALF_STATE), BF16),
                        pltpu.VMEM((lc * S5_HALF, S5_HALF), BF16),
                        pltpu.VMEM((S5_HALF, 2 * S5_HALF_STATE), BF16)],
        compiler_params=_cparams(("arbitrary", "arbitrary")),
        name="s5_mixer",
    )(u.reshape(t, BRANCH_W), perm, permt, ec, ftc, c0, d, a)
    return y


def _s5_tables(a_re, a_im, log_dt, b_re, b_im, c_re, c_im):
    nl = a_re.shape[0]
    g, p_, n, lc = S5_GROUPS, S5_STATE, S5_GROUP, S5_CHUNK
    a_re, a_im, b_re, b_im, c_re, c_im = (v.astype(F32) for v in (a_re, a_im, b_re, b_im, c_re, c_im))
    dt = jnp.exp(log_dt.astype(F32))[..., None]
    adt_r = (a_re * dt)[:, None, :, None, :]
    adt_i = (a_im * dt)[:, None, :, None, :]

    def powers(steps):
        st = jnp.asarray(np.asarray(steps, np.float32).reshape(1, -1, 1, 1, 1))
        mag = jnp.exp(adt_r * st)
        return mag * jnp.cos(adt_i * st), mag * jnp.sin(adt_i * st)

    lbr, lbi = powers([1])
    are, aim = a_re[:, None, :, None, :], a_im[:, None, :, None, :]
    den = are * are + aim * aim
    qr = ((lbr - 1.0) * are + lbi * aim) / den
    qi = (lbi * are - (lbr - 1.0) * aim) / den
    bt_re, bt_im = b_re.transpose(0, 1, 3, 2)[:, None], b_im.transpose(0, 1, 3, 2)[:, None]
    bbr = qr * bt_re - qi * bt_im
    bbi = qr * bt_im + qi * bt_re
    cr, ci = c_re[:, None], c_im[:, None]
    pr, pi = powers(np.arange(lc - 1, -1, -1))
    er, ei = pr * bbr - pi * bbi, pr * bbi + pi * bbr
    ec = jnp.concatenate([er, er, ei, ei], axis=-1).reshape(nl, lc, g * n, 4 * p_)
    pfr, pfi = powers(np.arange(1, lc + 1))
    fr, fi = cr * pfr - ci * pfi, -(cr * pfi + ci * pfr)
    ftc = jnp.concatenate([fr, fr, fi, fi], axis=-1).reshape(nl, lc, g * n, 4 * p_)
    c0 = jnp.concatenate([c_re, c_re, -c_im, -c_im], axis=-1).reshape(nl, g * n, 4 * p_)
    nh = BRANCH_W // S5_HALF
    ar, ai = powers([lc])
    a_tab = jnp.stack([ar.reshape(nl, nh, S5_HALF_STATE), ai.reshape(nl, nh, S5_HALF_STATE)], axis=2)
    return ec, ftc, c0, a_tab


def _merge_kernel(x_ref, yg_ref, y5_ref, ym_ref, wgate_ref, bgate_ref, wglu_ref, bglu_ref,
                  wup_ref, wo_ref, g_ref, b_ref, o_ref):
    x = x_ref[...]
    xb = x.astype(BF16)
    y5 = _gelu_tanh(y5_ref[...].astype(F32))
    y5 = y5 * _sigmoid(_dot(y5.astype(BF16), wglu_ref[...]) + bglu_ref[...])
    ys = (yg_ref[...], y5.astype(BF16), ym_ref[...])
    acc = None
    for r in range(N_BRANCH):
        gate = _sigmoid(_dot(xb, wgate_ref[:, r * D_MODEL:(r + 1) * D_MODEL])
                        + bgate_ref[:, r * D_MODEL:(r + 1) * D_MODEL])
        term = gate * _dot(ys[r], wup_ref[r])
        acc = term if acc is None else acc + term
    mix = _dot(acc.astype(BF16), wo_ref[...])
    o_ref[...] = _layer_norm(DN_ALPHA * x + mix, g_ref[...], b_ref[...])


def _merge(x2, yg, y5, ym, wgate, bgate, wglu, bglu, wup, wo, g, b, tm=512):
    t, dm = x2.shape
    row = lambda w: pl.BlockSpec((tm, w), lambda i: (i, 0))
    return pl.pallas_call(
        _merge_kernel,
        grid=(t // tm,),
        in_specs=[row(dm), row(BRANCH_W), row(BRANCH_W), row(BRANCH_W),
                  _full(wgate.shape), _full(bgate.shape), _full(wglu.shape), _full(bglu.shape),
                  _full(wup.shape), _full(wo.shape), _full(g.shape), _full(b.shape)],
        out_specs=row(dm),
        out_shape=jax.ShapeDtypeStruct((t, dm), F32),
        compiler_params=_cparams(("parallel",)),
        name="merge_ln1",
    )(x2, yg, y5, ym, wgate, bgate, wglu, bglu, wup, wo, g, b)


def _ple_ln2(x, xb, f, p_ref, pwg_ref, pwp_ref, g_ref, b_ref):
    e = _sigmoid(_dot(xb, pwg_ref[...])) * _dot(p_ref[...].astype(BF16), pwp_ref[...])
    return _layer_norm(DN_ALPHA * x + f + e, g_ref[...], b_ref[...])


def _ffn_kernel(x_ref, p_ref, wg_ref, wu_ref, wd_ref, pwg_ref, pwp_ref, g_ref, b_ref, o_ref):
    x = x_ref[...]
    xb = x.astype(BF16)
    hid = (_silu(_dot(xb, wg_ref[...])) * _dot(xb, wu_ref[...])).astype(BF16)
    f = _dot(hid, wd_ref[...])
    o_ref[...] = _ple_ln2(x, xb, f, p_ref, pwg_ref, pwp_ref, g_ref, b_ref)


def _ffn_layer(x2, p3, layer, wg, wu, wd, pwg, pwp, g, b, tm=256):
    t, dm = x2.shape
    row = lambda w: pl.BlockSpec((tm, w), lambda i: (i, 0))
    return pl.pallas_call(
        _ffn_kernel,
        grid=(t // tm,),
        in_specs=[row(dm), pl.BlockSpec((None, tm, PLE_DIM), lambda i: (layer, i, 0)),
                  _full(wg.shape), _full(wu.shape), _full(wd.shape),
                  _full(pwg.shape), _full(pwp.shape), _full(g.shape), _full(b.shape)],
        out_specs=row(dm),
        out_shape=jax.ShapeDtypeStruct((t, dm), F32),
        compiler_params=_cparams(("parallel",)),
        name="ffn_ple_ln2",
    )(x2, p3, wg, wu, wd, pwg, pwp, g, b)


def _router_kernel(x_ref, w_ref, b_ref, tril_ref, sel_ref, pr_ref, rk_ref, cnt_ref, base_ref):
    @pl.when(pl.program_id(0) == 0)
    def _():
        base_ref[...] = jnp.zeros_like(base_ref)

    xh, xl = _split2(x_ref[...])
    logits = _dot(xh, w_ref[0]) + _dot(xl, w_ref[0]) + _dot(xh, w_ref[1]) + b_ref[...]
    lane = lax.broadcasted_iota(jnp.int32, logits.shape, 1)
    neg = -jnp.inf
    logits = jnp.where(lane < N_EXPERTS, logits, neg)
    m1 = jnp.max(logits, axis=-1, keepdims=True)
    i1 = jnp.min(jnp.where(logits == m1, lane, LANES), axis=-1, keepdims=True)
    rest = jnp.where(lane == i1, neg, logits)
    m2 = jnp.max(rest, axis=-1, keepdims=True)
    i2 = jnp.min(jnp.where(rest == m2, lane, LANES), axis=-1, keepdims=True)
    e2 = jnp.exp(m2 - m1)
    p1 = 1.0 / (1.0 + e2)
    p2 = e2 / (1.0 + e2)
    hot = jnp.where((lane == i1) | (lane == i2), 1.0, 0.0)
    base = base_ref[0:1, :]
    before = _dot(tril_ref[...], hot.astype(BF16)) + base
    r1 = jnp.sum(jnp.where(lane == i1, before, 0.0), axis=-1, keepdims=True)
    r2 = jnp.sum(jnp.where(lane == i2, before, 0.0), axis=-1, keepdims=True)
    sel_ref[...] = jnp.where(lane == 0, i1, jnp.where(lane == 1, i2, 0))
    pr_ref[...] = jnp.where(lane == 0, p1, jnp.where(lane == 1, p2, 0.0))
    rk_ref[...] = jnp.where(lane == 0, r1, jnp.where(lane == 1, r2, 0.0))
    total = base + jnp.sum(hot, axis=0, keepdims=True)
    base_ref[...] = jnp.broadcast_to(total, base_ref.shape)
    cnt_ref[...] = jnp.broadcast_to(total, cnt_ref.shape)


def _router(x2, w, b, tm=512):
    t, dm = x2.shape
    i = np.arange(tm)
    tril = jnp.asarray((i[:, None] > i[None, :]).astype(np.float32), BF16)
    blk = pl.BlockSpec((tm, LANES), lambda i: (i, 0))
    return pl.pallas_call(
        _router_kernel,
        grid=(t // tm,),
        in_specs=[pl.BlockSpec((tm, dm), lambda i: (i, 0)), _full(w.shape), _full(b.shape), _full((tm, tm))],
        out_specs=[blk, blk, blk, pl.BlockSpec((8, LANES), lambda i: (0, 0))],
        out_shape=[jax.ShapeDtypeStruct((t, LANES), jnp.int32), jax.ShapeDtypeStruct((t, LANES), F32),
                   jax.ShapeDtypeStruct((t, LANES), F32), jax.ShapeDtypeStruct((8, LANES), F32)],
        scratch_shapes=[pltpu.VMEM((8, LANES), F32)],
        compiler_params=_cparams(("arbitrary",)),
        name="moe_router",
    )(x2, w, b, tril)


TOP_K = 2
MOE_TILE = 1024
MOE_SUB = 256
MOE_FF_TILE = 512
DISPATCH_TILE = 512
DMA_ISSUE_UNROLL = 8


def _row_copy_wait(src_rows, dst_rows, sem):
    pltpu.make_async_copy(src_rows, dst_rows, sem).wait()


def _dispatch_kernel(pos_ref, x_ref, xs_ref, sem):
    tm = x_ref.shape[0]
    base = pl.program_id(0) * tm * TOP_K

    def body(r, carry):
        for k in range(TOP_K):
            dst = pos_ref[base + r * TOP_K + k]
            pltpu.make_async_copy(x_ref.at[pl.ds(r, 1), :], xs_ref.at[pl.ds(dst, 1), :], sem).start(priority=k)
        return carry

    lax.fori_loop(0, tm, body, 0, unroll=DMA_ISSUE_UNROLL)
    for _ in range(TOP_K):
        _row_copy_wait(x_ref, xs_ref.at[pl.ds(0, tm), :], sem)


def _dispatch(pos_flat, x2, tm=DISPATCH_TILE):
    t, dm = x2.shape
    return pl.pallas_call(
        _dispatch_kernel,
        grid_spec=pltpu.PrefetchScalarGridSpec(
            num_scalar_prefetch=1,
            grid=(t // tm,),
            in_specs=[pl.BlockSpec((tm, dm), lambda i, pos: (i, 0))],
            out_specs=pl.BlockSpec(memory_space=pl.ANY),
            scratch_shapes=[pltpu.SemaphoreType.DMA],
        ),
        out_shape=jax.ShapeDtypeStruct((t * TOP_K, dm), F32),
        compiler_params=_cparams(("arbitrary",)),
        name="moe_dispatch",
    )(pos_flat, x2)


def _moe_group_kernel(tile_ref, exp_ref, lo_ref, hi_ref, xs_ref, wg_ref, wu_ref, wd_ref, o_ref,
                      acc_ref, xb_ref, wgb_ref, wub_ref, wdb_ref):
    i = pl.program_id(0)
    f = pl.program_id(1)
    lo = lo_ref[i]
    hi = hi_ref[i]
    tm = xs_ref.shape[0]

    @pl.when(hi > lo)
    def _():
        @pl.when((lo == 0) & (f == 0))
        def _():
            acc_ref[...] = jnp.zeros_like(acc_ref)

        @pl.when(f == 0)
        def _():
            xb_ref[...] = xs_ref[...].astype(BF16)

        whole = (lo == 0) & (hi == tm)

        @pl.when(whole)
        def _():
            xb = xb_ref[...]
            hid = (_silu(_dot(xb, wg_ref[0].astype(BF16))) * _dot(xb, wu_ref[0].astype(BF16))).astype(BF16)
            acc_ref[...] += _dot(hid, wd_ref[0].astype(BF16))

        @pl.when(jnp.logical_not(whole))
        def _():
            wgb_ref[...] = wg_ref[0].astype(BF16)
            wub_ref[...] = wu_ref[0].astype(BF16)
            wdb_ref[...] = wd_ref[0].astype(BF16)
            for sub in range(tm // MOE_SUB):
                @pl.when((hi > sub * MOE_SUB) & (lo < (sub + 1) * MOE_SUB))
                def _(sub=sub):
                    rsl = slice(sub * MOE_SUB, (sub + 1) * MOE_SUB)
                    xb = xb_ref[rsl, :]
                    hid = _silu(_dot(xb, wgb_ref[...])) * _dot(xb, wub_ref[...])
                    rid = lax.broadcasted_iota(jnp.int32, (MOE_SUB, 1), 0) + sub * MOE_SUB
                    hid = jnp.where((rid >= lo) & (rid < hi), hid, 0.0).astype(BF16)
                    acc_ref[rsl, :] += _dot(hid, wdb_ref[...])

        @pl.when((hi == tm) & (f == pl.num_programs(1) - 1))
        def _():
            o_ref[...] = acc_ref[...]


def _moe_grouped(items, xs, wg, wu, wd, tm=MOE_TILE, tf=MOE_FF_TILE):
    tile, exp, lo, hi = items
    rows, dm = xs.shape
    dff = wg.shape[2]
    return pl.pallas_call(
        _moe_group_kernel,
        grid_spec=pltpu.PrefetchScalarGridSpec(
            num_scalar_prefetch=4,
            grid=(tile.shape[0], dff // tf),
            in_specs=[pl.BlockSpec((tm, dm), lambda i, f, tl, ex, lo_, hi_: (tl[i], 0)),
                      pl.BlockSpec((1, dm, tf), lambda i, f, tl, ex, lo_, hi_: (ex[i], 0, f)),
                      pl.BlockSpec((1, dm, tf), lambda i, f, tl, ex, lo_, hi_: (ex[i], 0, f)),
                      pl.BlockSpec((1, tf, dm), lambda i, f, tl, ex, lo_, hi_: (ex[i], f, 0))],
            out_specs=pl.BlockSpec((tm, dm), lambda i, f, tl, ex, lo_, hi_: (tl[i], 0)),
            scratch_shapes=[pltpu.VMEM((tm, dm), F32), pltpu.VMEM((tm, dm), BF16), pltpu.VMEM((dm, tf), BF16),
                            pltpu.VMEM((dm, tf), BF16), pltpu.VMEM((tf, dm), BF16)],
        ),
        out_shape=jax.ShapeDtypeStruct((rows, dm), F32),
        compiler_params=_cparams(("arbitrary", "arbitrary")),
        name="moe_grouped",
    )(tile, exp, lo, hi, xs, wg, wu, wd)


def _moe_items(counts, n_rows, tm=MOE_TILE):
    n_tiles = n_rows // tm
    ends = jnp.cumsum(counts)
    cuts = jnp.sort(jnp.concatenate([jnp.arange(n_tiles + 1, dtype=jnp.int32) * tm, ends[:-1]]))
    start, stop = cuts[:-1], cuts[1:]
    tile = jnp.minimum(start // tm, n_tiles - 1)
    exp = jnp.minimum(jnp.sum(ends[None, :] <= start[:, None], axis=1), N_EXPERTS - 1)
    lo = start - tile * tm
    hi = stop - tile * tm
    return tile.astype(jnp.int32), exp.astype(jnp.int32), lo.astype(jnp.int32), hi.astype(jnp.int32)


def _combine_kernel(pos_ref, x_ref, pr_ref, p_ref, pwg_ref, pwp_ref, g_ref, b_ref, ys_ref, o_ref, gat_ref, sem):
    tm = x_ref.shape[0]
    base = pl.program_id(0) * tm * TOP_K

    def body(r, carry):
        for k in range(TOP_K):
            src = pos_ref[base + r * TOP_K + k]
            pltpu.make_async_copy(ys_ref.at[pl.ds(src, 1), :], gat_ref.at[k, pl.ds(r, 1), :],
                                  sem).start(priority=k)
        return carry

    lax.fori_loop(0, tm, body, 0, unroll=DMA_ISSUE_UNROLL)
    x = x_ref[...]
    xb = x.astype(BF16)
    e = _sigmoid(_dot(xb, pwg_ref[...])) * _dot(p_ref[...].astype(BF16), pwp_ref[...])
    for k in range(TOP_K):
        _row_copy_wait(ys_ref.at[pl.ds(0, tm), :], gat_ref.at[k], sem)
    pr = pr_ref[...]
    f = pr[:, 0:1] * gat_ref[0]
    for k in range(1, TOP_K):
        f = f + pr[:, k:k + 1] * gat_ref[k]
    o_ref[...] = _layer_norm(DN_ALPHA * x + f + e, g_ref[...], b_ref[...])


def _combine_layer(pos_flat, x2, pr, p3, layer, pwg, pwp, g, b, ys, tm=DISPATCH_TILE):
    t, dm = x2.shape
    row = lambda w: pl.BlockSpec((tm, w), lambda i, pos: (i, 0))
    full = lambda a: pl.BlockSpec(a.shape, lambda i, pos: (0,) * a.ndim, pipeline_mode=pl.Buffered(1))
    return pl.pallas_call(
        _combine_kernel,
        grid_spec=pltpu.PrefetchScalarGridSpec(
            num_scalar_prefetch=1,
            grid=(t // tm,),
            in_specs=[row(dm), row(LANES), pl.BlockSpec((None, tm, PLE_DIM), lambda i, pos: (layer, i, 0)),
                      full(pwg), full(pwp), full(g), full(b),
                      pl.BlockSpec(memory_space=pl.ANY)],
            out_specs=row(dm),
            scratch_shapes=[pltpu.VMEM((TOP_K, tm, dm), F32), pltpu.SemaphoreType.DMA],
        ),
        out_shape=jax.ShapeDtypeStruct((t, dm), F32),
        compiler_params=_cparams(("arbitrary",)),
        name="moe_combine_ple_ln2",
    )(pos_flat, x2, pr, p3, pwg, pwp, g, b, ys)


def _row(v):
    return v.reshape(1, -1).astype(F32)


def _pad_lanes(w):
    return jnp.pad(w, ((0, 0), (0, LANES - w.shape[1])))


def kernel(x, p, w_in, b_in, gla_w_a2, gla_b_a2, gla_norm_g, s5_a_re, s5_a_im, s5_log_dt, s5_b_re, s5_b_im,
           s5_c_re, s5_c_im, s5_d, s5_w_glu, s5_b_glu, ml_conv_w, ml_conv_b, ml_norm_g, w_up, w_o, ln1_g, ln1_b,
           ffn_wg, ffn_wu, ffn_wd, moe_router, moe_router_b, moe_wg, moe_wu, moe_wd, ple_w_gate, ple_w_proj,
           ln2_g, ln2_b):
    bsz, s, dm = x.shape
    t = bsz * s
    hi = lax.Precision.HIGHEST
    o = IN_OFF
    tri = jnp.asarray(_chunk_tri(SEQ_BLOCK, CHUNK), BF16)
    ml_tri_np = _chunk_tri(SEQ_BLOCK, ML_CHUNK)
    ml_tri = jnp.asarray(ml_tri_np, BF16)
    ml_trit = jnp.asarray(ml_tri_np.T, BF16)
    s5_ec, s5_ftc, s5_c0, s5_atab = _s5_tables(s5_a_re, s5_a_im, s5_log_dt, s5_b_re, s5_b_im, s5_c_re, s5_c_im)
    p3 = p.reshape(DEPTH, t, PLE_DIM)

    w_in_b = w_in.astype(BF16)
    for i in range(DEPTH):
        w, b = w_in_b[i], b_in[i]
        sl = lambda k: (w[:, o[k]:o[k + 1]], b[o[k]:o[k + 1]])
        (wq, bq), (wk, bk), (wv, bv), (_, ba), (wg_, bg_) = sl(0), sl(1), sl(2), sl(3), sl(4)
        wz = jnp.dot(w_in[i, :, o[3]:o[4]], gla_w_a2[i], precision=hi).astype(BF16)
        bz = jnp.dot(ba, gla_w_a2[i], precision=hi) + gla_b_a2[i]
        (wu_, bu_) = sl(5)
        w_gla = jnp.concatenate([wq, wk, wz, wv, wg_, wu_], axis=1)
        b_gla = _row(jnp.concatenate([bq, bk, bz, bv, bg_, bu_]))
        y_gla, u = _gla_mixer(x, w_gla, b_gla, _row(gla_norm_g[i]), tri)

        x2 = x.reshape(t, dm)
        y5 = _s5_mixer(u, s5_ec[i], s5_ftc[i], s5_c0[i], _row(s5_d[i]), s5_atab[i])

        (wmq, bmq), (wmk, bmk), (wmv, bmv), (wmo, bmo) = sl(6), sl(7), sl(8), sl(11)
        w_ml = jnp.concatenate([wmq, wmk, wmv, wmo], axis=1)
        b_ml = _row(jnp.concatenate([bmq, bmk, bmv, bmo]))
        w_if = w_in[i, :, o[9]:o[11]]
        b_if = b[o[9]:o[11]]
        if_h = w_if.astype(BF16)
        if_l = (w_if - if_h.astype(F32)).astype(BF16)
        wgc = jnp.stack([_pad_lanes(if_h), _pad_lanes(if_l)])
        wgr = jnp.stack([if_h.T, if_l.T])
        y_ml = _ml_mixer(x, w_ml, b_ml, wgc, _pad_lanes(_row(b_if)), wgr, b_if.reshape(-1, 1).astype(F32),
                         ml_conv_w[i].astype(F32), _row(ml_conv_b[i]), _row(ml_norm_g[i]), ml_tri, ml_trit)

        (wgt, bgt) = sl(12)
        x1 = _merge(x2, y_gla.reshape(t, BRANCH_W), y5, y_ml.reshape(t, BRANCH_W),
                    wgt, _row(bgt), s5_w_glu[i].astype(BF16), _row(s5_b_glu[i]),
                    w_up[i].astype(BF16), w_o[i].astype(BF16), _row(ln1_g[i]), _row(ln1_b[i]))

        pwg = ple_w_gate[i].astype(BF16)
        pwp = ple_w_proj[i].astype(BF16)
        j = i // 2
        if i % 2 == 0:
            x2n = _ffn_layer(x1, p3, i, ffn_wg[j].astype(BF16), ffn_wu[j].astype(BF16), ffn_wd[j].astype(BF16),
                             pwg, pwp, _row(ln2_g[i]), _row(ln2_b[i]))
        else:
            wr = moe_router[j]
            wr_h = wr.astype(BF16)
            wr_l = (wr - wr_h.astype(F32)).astype(BF16)
            sel, pr, rk, cnt = _router(x1, jnp.stack([_pad_lanes(wr_h), _pad_lanes(wr_l)]),
                                       _pad_lanes(_row(moe_router_b[j])))
            counts = cnt[0, :N_EXPERTS].astype(jnp.int32)
            starts = jnp.cumsum(counts) - counts
            sel2 = sel[:, :TOP_K]
            pos = rk[:, :TOP_K].astype(jnp.int32) + jnp.sum(
                jnp.where(sel2[..., None] == jnp.arange(N_EXPERTS), starts, 0), axis=-1)
            pos_flat = pos.reshape(-1)
            xs = _dispatch(pos_flat, x1)
            ys = _moe_grouped(_moe_items(counts, t * TOP_K), xs, moe_wg[j], moe_wu[j], moe_wd[j])
            x2n = _combine_layer(pos_flat, x1, pr, p3, i, pwg, pwp, _row(ln2_g[i]), _row(ln2_b[i]), ys)
        x = x2n.reshape(bsz, s, dm)
    return x
```

```python
import functools
import math

import numpy as np
import jax
import jax.numpy as jnp
from jax import lax
from jax.experimental import pallas as pl
from jax.experimental.pallas import tpu as pltpu

F32 = jnp.float32
BF16 = jnp.bfloat16

D_MODEL = 1024
DEPTH = 2
N_BRANCH = 3
BRANCH_W = 512
HEADS = 4
DK = 64
DV = BRANCH_W // HEADS
GLA_RANK = 16
GLA_TAU = 16.0
CHUNK = 64
ML_CHUNK = 256
S5_GROUP = 16
S5_GROUPS = BRANCH_W // S5_GROUP
S5_STATE = 64
S5_CHUNK = 16
ML_CONV = 4
D_FF = 2816
N_EXPERTS = 8
D_FF_EXPERT = 3584
PLE_DIM = 256
DN_ALPHA = (2.0 * DEPTH) ** 0.25
LN_EPS = 1e-5

IN_WIDTHS = (
    HEADS * DK, HEADS * DK, BRANCH_W, GLA_RANK, BRANCH_W,
    BRANCH_W,
    HEADS * DK, HEADS * DK, BRANCH_W, HEADS, HEADS, BRANCH_W,
    N_BRANCH * D_MODEL,
)
IN_OFF = tuple(int(o) for o in np.concatenate([[0], np.cumsum(IN_WIDTHS)]))

LANES = 128
SEQ_BLOCK = 256
GLA_SEQ_PER_STEP = 4
ML_SEQ_PER_STEP = 1
PAIR_W = 2 * DK
VMEM_LIMIT = 56 * 1024 * 1024


def _cparams(sem):
    return pltpu.CompilerParams(dimension_semantics=sem, vmem_limit_bytes=VMEM_LIMIT)


def _dot(a, b):
    return jnp.dot(a, b, preferred_element_type=F32)


def _dot_nt(a, b):
    return lax.dot_general(a, b, (((1,), (1,)), ((), ())), preferred_element_type=F32)


def _dot_tn(a, b):
    return lax.dot_general(a, b, (((0,), (0,)), ((), ())), preferred_element_type=F32)


def _split3(a):
    hi = a.astype(BF16)
    r = a - hi.astype(F32)
    mid = r.astype(BF16)
    lo = (r - mid.astype(F32)).astype(BF16)
    return hi, mid, lo


def _split2(a):
    hi = a.astype(BF16)
    lo = (a - hi.astype(F32)).astype(BF16)
    return hi, lo


def _log_sigmoid(x):
    return jnp.minimum(x, 0.0) - jnp.log(1.0 + jnp.exp(-jnp.abs(x)))


def _sigmoid(x):
    return 0.5 * jnp.tanh(0.5 * x) + 0.5


def _silu(x):
    return x * _sigmoid(x)


def _gelu_tanh(x):
    return 0.5 * x * (1.0 + jnp.tanh(math.sqrt(2.0 / math.pi) * (x + 0.044715 * (x * x * x))))


def _layer_norm(v, g, b):
    mu = jnp.mean(v, axis=-1, keepdims=True)
    c = v - mu
    var = jnp.mean(c * c, axis=-1, keepdims=True)
    return c * lax.rsqrt(var + LN_EPS) * g + b


def _head_norm(o):
    mu = jnp.mean(o, axis=-1, keepdims=True)
    c = o - mu
    var = jnp.mean(c * c, axis=-1, keepdims=True)
    return c * lax.rsqrt(var + LN_EPS)


def _chunk_tri(n, chunk):
    i = np.arange(n)
    return ((i[:, None] >= i[None, :]) & (i[:, None] // chunk == i[None, :] // chunk)).astype(np.float32)


def _full(shape):
    nd = len(shape)
    return pl.BlockSpec(shape, lambda *_: (0,) * nd, pipeline_mode=pl.Buffered(1))


def _gla_kernel(x_ref, w_ref, b_ref, ng_ref, tri_ref, y_ref, u_ref, st_ref, o_ref):
    @pl.when(pl.program_id(1) == 0)
    def _():
        st_ref[...] = jnp.zeros_like(st_ref)

    hk = HEADS * DK
    nb, lb, d = x_ref.shape
    xb = x_ref[...].reshape(nb * lb, d).astype(BF16)
    h = _dot(xb, w_ref[...]) + b_ref[...]
    q = h[:, 0:hk]
    k = h[:, hk:2 * hk]
    z = h[:, 2 * hk:3 * hk]
    v = h[:, 3 * hk:3 * hk + BRANCH_W]
    g = h[:, 3 * hk + BRANCH_W:3 * hk + 2 * BRANCH_W]
    u_ref[...] = h[:, 3 * hk + 2 * BRANCH_W:3 * hk + 3 * BRANCH_W].reshape(nb, lb, BRANCH_W).astype(u_ref.dtype)

    la = _log_sigmoid(z) * (1.0 / GLA_TAU)
    tri = tri_ref[...]
    la_h, la_m, la_l = _split3(la)
    cum = jnp.concatenate(
        [_dot(tri, la_h[r:r + lb]) + _dot(tri, la_m[r:r + lb]) + _dot(tri, la_l[r:r + lb])
         for r in range(0, nb * lb, lb)], axis=0)
    qd = q * (DK ** -0.5) * jnp.exp(cum)
    ki = k * jnp.exp(-cum)
    vb = v.astype(BF16)

    lane = lax.broadcasted_iota(jnp.int32, (1, PAIR_W), 1)
    row_i = lax.broadcasted_iota(jnp.int32, (CHUNK, CHUNK), 0)
    col_i = lax.broadcasted_iota(jnp.int32, (CHUNK, CHUNK), 1)
    causal = row_i >= col_i
    bd_r = lax.broadcasted_iota(jnp.int32, (2 * DV, PAIR_W), 0)
    bd_c = lax.broadcasted_iota(jnp.int32, (2 * DV, PAIR_W), 1)
    blockdiag = (bd_r >= DV) == (bd_c >= DK)

    for bb, p in [(bb, p) for bb in range(nb) for p in range(HEADS // 2)]:
        st = st_ref[bb, p]
        lsl = slice(p * PAIR_W, (p + 1) * PAIR_W)
        for c in range(lb // CHUNK):
            rsl = slice(bb * lb + c * CHUNK, bb * lb + (c + 1) * CHUNK)
            qd_c = qd[rsl, lsl]
            ki_c = ki[rsl, lsl].astype(BF16)
            cum_c = cum[rsl, lsl]
            last = cum_c[CHUNK - 1:CHUNK, :]
            kt = (k[rsl, lsl] * jnp.exp(last - cum_c)).astype(BF16)
            inter = _dot_nt(qd_c.astype(BF16), st.astype(BF16))
            for hh in range(2):
                head = 2 * p + hh
                in_head = (lane >= hh * DK) & (lane < (hh + 1) * DK)
                qm = jnp.where(in_head, qd_c, 0.0).astype(BF16)
                att = jnp.where(causal, _dot_nt(qm, ki_c), 0.0)
                o_h = _dot(att.astype(BF16), vb[rsl, head * DV:(head + 1) * DV])
                o_ref[rsl, head * DV:(head + 1) * DV] = o_h + inter[:, hh * DV:(hh + 1) * DV]
            upd = _dot_tn(vb[rsl, p * 2 * DV:(p + 1) * 2 * DV], kt)
            st = st * jnp.exp(last) + jnp.where(blockdiag, upd, 0.0)
        st_ref[bb, p] = st

    ng = ng_ref[...]
    for head in range(HEADS):
        hsl = slice(head * DV, (head + 1) * DV)
        y = _head_norm(o_ref[:, hsl]) * ng[:, hsl] * _silu(g[:, hsl])
        y_ref[:, :, hsl] = y.reshape(nb, lb, DV).astype(y_ref.dtype)


def _gla_mixer(x, w, b, ng, tri):
    bsz, s, d = x.shape
    wcols = w.shape[1]
    nb = GLA_SEQ_PER_STEP
    return pl.pallas_call(
        _gla_kernel,
        grid=(bsz // nb, s // SEQ_BLOCK),
        in_specs=[
            pl.BlockSpec((nb, SEQ_BLOCK, d), lambda i, j: (i, j, 0)),
            _full((d, wcols)), _full((1, wcols)), _full((1, BRANCH_W)),
            _full((SEQ_BLOCK, SEQ_BLOCK)),
        ],
        out_specs=[pl.BlockSpec((nb, SEQ_BLOCK, BRANCH_W), lambda i, j: (i, j, 0))] * 2,
        out_shape=[jax.ShapeDtypeStruct((bsz, s, BRANCH_W), BF16)] * 2,
        scratch_shapes=[pltpu.VMEM((nb, HEADS // 2, 2 * DV, PAIR_W), F32),
                        pltpu.VMEM((nb * SEQ_BLOCK, BRANCH_W), F32)],
        compiler_params=_cparams(("arbitrary", "arbitrary")),
        name="gla_mixer",
    )(x, w, b, ng, tri)


ML_ST_ROWS = 2 * DV + LANES


def _ml_kernel(x_ref, w_ref, b_ref, wgc_ref, bgc_ref, wgr_ref, bgr_ref, cw_ref, cb_ref, ng_ref,
               tri_ref, trit_ref, y_ref, ct_ref, m_ref, carry_ref, o_ref):
    @pl.when(pl.program_id(1) == 0)
    def _():
        ct_ref[...] = jnp.zeros_like(ct_ref)
        m_ref[...] = jnp.zeros_like(m_ref)
        carry_ref[...] = jnp.zeros_like(carry_ref)

    hk = HEADS * DK
    nb, lb, d = x_ref.shape
    x32 = x_ref[...].reshape(nb * lb, d)
    xh, xl = _split2(x32)
    h = _dot(xh, w_ref[...]) + b_ref[...]
    qk = h[:, 0:2 * hk]
    v = h[:, 2 * hk:2 * hk + BRANCH_W]
    o_pre = h[:, 2 * hk + BRANCH_W:2 * hk + 2 * BRANCH_W]

    cw = cw_ref[...]
    conv = []
    for bb in range(nb):
        qk_b = qk[bb * lb:(bb + 1) * lb]
        ext = jnp.concatenate([carry_ref[bb], qk_b], axis=0)
        acc = cb_ref[...] + ext[8 - (ML_CONV - 1):8 - (ML_CONV - 1) + lb] * cw[0:1]
        for j in range(1, ML_CONV):
            off = 8 - (ML_CONV - 1) + j
            acc = acc + ext[off:off + lb] * cw[j:j + 1]
        carry_ref[bb] = qk_b[lb - 8:lb]
        conv.append(acc)
    qkc = _silu(jnp.concatenate(conv, axis=0))
    qf = qkc[:, 0:hk]
    kf = qkc[:, hk:2 * hk] * (DK ** -0.5)
    vb = v.astype(BF16)

    gc = (_dot(xh, wgc_ref[0]) + _dot(xl, wgc_ref[0]) + _dot(xh, wgc_ref[1])) + bgc_ref[...]
    gr = (_dot_nt(wgr_ref[0], xh) + _dot_nt(wgr_ref[0], xl) + _dot_nt(wgr_ref[1], xh)) + bgr_ref[...]
    lf_c = _log_sigmoid(gc)
    lf_r = _log_sigmoid(gr)
    tri = tri_ref[...]
    trit = trit_ref[...]
    c_h, c_m, c_l = _split3(lf_c)
    r_h, r_m, r_l = _split3(lf_r)
    blocks = [slice(r, r + lb) for r in range(0, nb * lb, lb)]
    bc = jnp.concatenate([_dot(tri, c_h[r]) + _dot(tri, c_m[r]) + _dot(tri, c_l[r]) for r in blocks],
                         axis=0)
    br = jnp.concatenate([_dot(r_h[:, r], trit) + _dot(r_m[:, r], trit) + _dot(r_l[:, r], trit)
                          for r in blocks], axis=1)

    lane = lax.broadcasted_iota(jnp.int32, (1, PAIR_W), 1)
    row_i = lax.broadcasted_iota(jnp.int32, (ML_CHUNK, ML_CHUNK), 0)
    col_i = lax.broadcasted_iota(jnp.int32, (ML_CHUNK, ML_CHUNK), 1)
    causal = row_i >= col_i
    sr = lax.broadcasted_iota(jnp.int32, (ML_ST_ROWS, PAIR_W), 0)
    sc_ = lax.broadcasted_iota(jnp.int32, (ML_ST_ROWS, PAIR_W), 1)
    first = sc_ < DK
    rows_h0 = (sr < DV) | (sr == 2 * DV)
    rows_h1 = ((sr >= DV) & (sr < 2 * DV)) | (sr == 2 * DV + 1)
    st_mask = (rows_h0 & first) | (rows_h1 & ~first)
    ones_blk = jnp.ones((ML_CHUNK, LANES), BF16)

    for bb, p in [(bb, p) for bb in range(nb) for p in range(HEADS // 2)]:
        ct = ct_ref[bb, p]
        lsl = slice(p * PAIR_W, (p + 1) * PAIR_W)
        m_pair = [m_ref[bb, 2 * p + hh][0:1, 0:1] for hh in range(2)]
        for c in range(lb // ML_CHUNK):
            rsl = slice(bb * lb + c * ML_CHUNK, bb * lb + (c + 1) * ML_CHUNK)
            q_c = qf[rsl, lsl]
            k_c = kf[rsl, lsl]
            k_cb = k_c.astype(BF16)
            inter_mm = _dot_nt(q_c.astype(BF16), ct.astype(BF16))
            wt_cols, decays = [], []
            for hh in range(2):
                head = 2 * p + hh
                m_st = m_pair[hh]
                b_col = bc[rsl, HEADS + head:HEADS + head + 1]
                i_col = gc[rsl, head:head + 1]
                b_row = br[HEADS + head:HEADS + head + 1, rsl]
                i_row = gr[head:head + 1, rsl]
                dmat = jnp.where(causal, b_col - b_row + i_row, -jnp.inf)
                inter = b_col + m_st
                m_row = jnp.maximum(inter, jnp.max(dmat, axis=-1, keepdims=True))
                wts = jnp.exp(dmat - m_row)
                in_head = (lane >= hh * DK) & (lane < (hh + 1) * DK)
                qm = jnp.where(in_head, q_c, 0.0).astype(BF16)
                sc = _dot_nt(qm, k_cb) * wts
                w_inter = jnp.exp(inter - m_row)
                num = _dot(sc.astype(BF16), vb[rsl, head * DV:(head + 1) * DV]) \
                    + w_inter * inter_mm[:, hh * DV:(hh + 1) * DV]
                den = jnp.sum(sc, axis=-1, keepdims=True) \
                    + w_inter * inter_mm[:, 2 * DV + hh:2 * DV + hh + 1]
                o_ref[rsl, head * DV:(head + 1) * DV] = num / jnp.maximum(jnp.abs(den), jnp.exp(-m_row))
                g_tot = b_col[ML_CHUNK - 1:ML_CHUNK, :]
                tail = g_tot - b_col + i_col
                m_new = jnp.maximum(g_tot + m_st, jnp.max(tail, axis=0, keepdims=True))
                wt_cols.append(jnp.exp(tail - m_new))
                decays.append(jnp.exp(g_tot + m_st - m_new))
                m_pair[hh] = m_new
            wk = (k_c * jnp.where(lane < DK, wt_cols[0], wt_cols[1])).astype(BF16)
            vp = jnp.concatenate([vb[rsl, p * 2 * DV:(p + 1) * 2 * DV], ones_blk], axis=1)
            upd = _dot_tn(vp, wk)
            ct = ct * jnp.where(lane < DK, decays[0], decays[1]) + jnp.where(st_mask, upd, 0.0)
        ct_ref[bb, p] = ct
        for hh in range(2):
            m_ref[bb, 2 * p + hh] = jnp.broadcast_to(m_pair[hh], m_ref.shape[2:])

    ng = ng_ref[...]
    for head in range(HEADS):
        hsl = slice(head * DV, (head + 1) * DV)
        y = _head_norm(o_ref[:, hsl]) * ng[:, hsl] * _sigmoid(o_pre[:, hsl])
        y_ref[:, :, hsl] = y.reshape(nb, lb, DV).astype(y_ref.dtype)


def _ml_mixer(x, w, b, wgc, bgc, wgr, bgr, cw, cb, ng, tri, trit):
    bsz, s, d = x.shape
    wcols = w.shape[1]
    nb = ML_SEQ_PER_STEP
    return pl.pallas_call(
        _ml_kernel,
        grid=(bsz // nb, s // SEQ_BLOCK),
        in_specs=[
            pl.BlockSpec((nb, SEQ_BLOCK, d), lambda i, j: (i, j, 0)),
            _full((d, wcols)), _full((1, wcols)),
            _full((2, d, LANES)), _full((1, LANES)),
            _full((2, 8, d)), _full((8, 1)),
            _full((ML_CONV, 2 * HEADS * DK)), _full((1, 2 * HEADS * DK)),
            _full((1, BRANCH_W)),
            _full((SEQ_BLOCK, SEQ_BLOCK)), _full((SEQ_BLOCK, SEQ_BLOCK)),
        ],
        out_specs=pl.BlockSpec((nb, SEQ_BLOCK, BRANCH_W), lambda i, j: (i, j, 0)),
        out_shape=jax.ShapeDtypeStruct((bsz, s, BRANCH_W), BF16),
        scratch_shapes=[pltpu.VMEM((nb, HEADS // 2, ML_ST_ROWS, PAIR_W), F32),
                        pltpu.VMEM((nb, HEADS, 8, LANES), F32),
                        pltpu.VMEM((nb, 8, 2 * HEADS * DK), F32),
                        pltpu.VMEM((nb * SEQ_BLOCK, BRANCH_W), F32)],
        compiler_params=_cparams(("arbitrary", "arbitrary")),
        name="mlstm_mixer",
    )(x, w, b, wgc, bgc, wgr, bgr, cw, cb, ng, tri, trit)


S5_IO_STEPS = 4
S5_PERM_BLOCK = 256
S5_HALF = 256
S5_HALF_STATE = (S5_HALF // S5_GROUP) * S5_STATE


def _s5_kernel(bsz, nc, u_ref, perm_ref, permt_ref, ec_ref, ftc_ref, c0_ref, d_ref, a_ref, y_ref,
               in_ref, yall_ref, xs_ref, xsb_ref, kexp_ref, c0x_ref):
    s = pl.program_id(1)
    lc = S5_CHUNK
    rows = bsz * nc
    sw = S5_HALF_STATE
    pb = perm_ref.shape[0]
    cb = pb // lc
    n_io = S5_IO_STEPS
    subs = rows * lc // (n_io * pb)

    def expand_state(blk):
        tile = jnp.concatenate([blk[:, 0:LANES]] * (sw // LANES) + [blk[:, LANES:2 * LANES]] * (sw // LANES),
                               axis=1)
        r = lax.broadcasted_iota(jnp.int32, tile.shape, 0) // S5_GROUP
        c = (lax.broadcasted_iota(jnp.int32, tile.shape, 1) % sw) // S5_STATE
        return jnp.where(r == c, tile, 0.0).astype(BF16)

    @pl.when(s == 0)
    def _():
        c0x_ref[...] = expand_state(c0_ref[...])
        xs_ref[...] = jnp.zeros_like(xs_ref)

    @pl.when(s < n_io)
    def _():
        for sb in range(subs):
            pu = _dot(perm_ref[...], u_ref[sb * pb:(sb + 1) * pb, :]).astype(BF16)
            r0 = pl.multiple_of((s * subs + sb) * cb, cb)
            for l in range(lc):
                in_ref[l, pl.ds(r0, cb), :] = pu[l * cb:(l + 1) * cb]

    @pl.when((s >= n_io) & (s < n_io + lc))
    def _():
        l = s - n_io
        u = in_ref[l]
        et = expand_state(ec_ref[0])
        kblk = _dot_nt(et, c0x_ref[...])
        r = lax.broadcasted_iota(jnp.int32, kblk.shape, 0)
        c = lax.broadcasted_iota(jnp.int32, kblk.shape, 1)
        kblk = kblk + jnp.where((r == c) & (l == lc - 1), d_ref[...], 0.0)
        kexp_ref[pl.ds(pl.multiple_of(l * S5_HALF, S5_HALF), S5_HALF), :] = kblk.astype(BF16)
        for jc in range(2 * sw // S5_HALF):
            part = _dot(u, et[:, jc * S5_HALF:(jc + 1) * S5_HALF])
            for jj in range(S5_HALF // LANES):
                xs_ref[jc * (S5_HALF // LANES) + jj] += part[:, jj * LANES:(jj + 1) * LANES]

    @pl.when(s == n_io + lc - 1)
    def _():
        ar = a_ref[0, 0:1, :]
        ai = a_ref[0, 1:2, :]
        nt = sw // LANES

        def body(c, carry):
            sr, si = carry
            idx = pl.ds(c, bsz, stride=nc)
            xr = jnp.concatenate([xs_ref[j, idx, :] for j in range(nt)], axis=1)
            xi = jnp.concatenate([xs_ref[nt + j, idx, :] for j in range(nt)], axis=1)
            for j in range(nt):
                xs_ref[j, idx, :] = sr[:, j * LANES:(j + 1) * LANES]
                xs_ref[nt + j, idx, :] = si[:, j * LANES:(j + 1) * LANES]
            return ar * sr - ai * si + xr, ar * si + ai * sr + xi

        zero = jnp.zeros((bsz, sw), F32)
        lax.fori_loop(0, nc, body, (zero, zero))
        for j in range(2 * nt):
            xsb_ref[:, j * LANES:(j + 1) * LANES] = xs_ref[j].astype(BF16)

    for lo in range(lc):
        @pl.when(s == n_io + lc + lo)
        def _(lo=lo):
            acc = _dot_nt(xsb_ref[...], expand_state(ftc_ref[0]))
            for l in range(lo + 1):
                j = lc - 1 - lo + l
                acc = acc + _dot(in_ref[l], kexp_ref[j * S5_HALF:(j + 1) * S5_HALF, :])
            yall_ref[lo] = acc.astype(BF16)

    @pl.when(s >= n_io + 2 * lc)
    def _():
        t = s - (n_io + 2 * lc)
        for sb in range(subs):
            r0 = pl.multiple_of((t * subs + sb) * cb, cb)
            ycat = jnp.concatenate([yall_ref[l, pl.ds(r0, cb), :] for l in range(lc)], axis=0)
            y_ref[sb * pb:(sb + 1) * pb, :] = _dot(permt_ref[...], ycat).astype(y_ref.dtype)


def _s5_mixer(u, ec, ftc, c0, d, a):
    bsz, s, _ = u.shape
    lc = S5_CHUNK
    nc = s // lc
    t = bsz * s
    nh = BRANCH_W // S5_HALF
    n_io = S5_IO_STEPS
    tb = t // n_io
    pb = S5_PERM_BLOCK
    src = (np.arange(pb) % (pb // lc)) * lc + np.arange(pb) // (pb // lc)
    perm_np = (src[:, None] == np.arange(pb)[None, :]).astype(np.float32)
    perm, permt = jnp.asarray(perm_np, BF16), jnp.asarray(perm_np.T, BF16)
    n_steps = 2 * n_io + 2 * lc
    y = pl.pallas_call(
        functools.partial(_s5_kernel, bsz, nc),
        grid=(nh, n_steps),
        in_specs=[
            pl.BlockSpec((tb, S5_HALF), lambda h, t: (jnp.minimum(t, n_io - 1), h)),
            _full((pb, pb)), _full((pb, pb)),
            pl.BlockSpec((1, S5_HALF, 2 * LANES), lambda h, t: (jnp.clip(t - n_io, 0, lc - 1), h, 0)),
            pl.BlockSpec((1, S5_HALF, 2 * LANES), lambda h, t: (jnp.clip(t - n_io - lc, 0, lc - 1), h, 0)),
            pl.BlockSpec((S5_HALF, 2 * LANES), lambda h, t: (h, 0)),
            pl.BlockSpec((1, S5_HALF), lambda h, t: (0, h)),
            pl.BlockSpec((1, 2, S5_HALF_STATE), lambda h, t: (h, 0, 0)),
        ],
        out_specs=pl.BlockSpec((tb, S5_HALF), lambda h, t: (jnp.maximum(t - n_io - 2 * lc, 0), h)),
        out_shape=jax.ShapeDtypeStruct((t, BRANCH_W), BF16),
        scratch_shapes=[pltpu.VMEM((lc, bsz * nc, S5_HALF), BF16),
                        pltpu.VMEM((lc, bsz * nc, S5_HALF), BF16),
                        pltpu.VMEM((2 * S5_HALF_STATE // LANES, bsz * nc, LANES), F32),
                        pltpu.VMEM((bsz * nc, 2 * S5_HALF_STATE), BF16),
                        pltpu.VMEM((lc * S5_HALF, S5_HALF), BF16),
                        pltpu.VMEM((S5_HALF, 2 * S5_HALF_STATE), BF16)],
        compiler_params=_cparams(("arbitrary", "arbitrary")),
        name="s5_mixer",
    )(u.reshape(t, BRANCH_W), perm, permt, ec, ftc, c0, d, a)
    return y


def _s5_tables(a_re, a_im, log_dt, b_re, b_im, c_re, c_im):
    nl = a_re.shape[0]
    g, p_, n, lc = S5_GROUPS, S5_STATE, S5_GROUP, S5_CHUNK
    a_re, a_im, b_re, b_im, c_re, c_im = (v.astype(F32) for v in (a_re, a_im, b_re, b_im, c_re, c_im))
    dt = jnp.exp(log_dt.astype(F32))[..., None]
    adt_r = (a_re * dt)[:, None, :, None, :]
    adt_i = (a_im * dt)[:, None, :, None, :]

    def powers(steps):
        st = jnp.asarray(np.asarray(steps, np.float32).reshape(1, -1, 1, 1, 1))
        mag = jnp.exp(adt_r * st)
        return mag * jnp.cos(adt_i * st), mag * jnp.sin(adt_i * st)

    lbr, lbi = powers([1])
    are, aim = a_re[:, None, :, None, :], a_im[:, None, :, None, :]
    den = are * are + aim * aim
    qr = ((lbr - 1.0) * are + lbi * aim) / den
    qi = (lbi * are - (lbr - 1.0) * aim) / den
    bt_re, bt_im = b_re.transpose(0, 1, 3, 2)[:, None], b_im.transpose(0, 1, 3, 2)[:, None]
    bbr = qr * bt_re - qi * bt_im
    bbi = qr * bt_im + qi * bt_re
    cr, ci = c_re[:, None], c_im[:, None]
    pr, pi = powers(np.arange(lc - 1, -1, -1))
    er, ei = pr * bbr - pi * bbi, pr * bbi + pi * bbr
    ec = jnp.concatenate([er, er, ei, ei], axis=-1).reshape(nl, lc, g * n, 4 * p_)
    pfr, pfi = powers(np.arange(1, lc + 1))
    fr, fi = cr * pfr - ci * pfi, -(cr * pfi + ci * pfr)
    ftc = jnp.concatenate([fr, fr, fi, fi], axis=-1).reshape(nl, lc, g * n, 4 * p_)
    c0 = jnp.concatenate([c_re, c_re, -c_im, -c_im], axis=-1).reshape(nl, g * n, 4 * p_)
    nh = BRANCH_W // S5_HALF
    ar, ai = powers([lc])
    a_tab = jnp.stack([ar.reshape(nl, nh, S5_HALF_STATE), ai.reshape(nl, nh, S5_HALF_STATE)], axis=2)
    return ec, ftc, c0, a_tab


def _merge_kernel(x_ref, yg_ref, y5_ref, ym_ref, wgate_ref, bgate_ref, wglu_ref, bglu_ref,
                  wup_ref, wo_ref, g_ref, b_ref, o_ref):
    x = x_ref[...]
    xb = x.astype(BF16)
    y5 = _gelu_tanh(y5_ref[...].astype(F32))
    y5 = y5 * _sigmoid(_dot(y5.astype(BF16), wglu_ref[...]) + bglu_ref[...])
    ys = (yg_ref[...], y5.astype(BF16), ym_ref[...])
    acc = None
    for r in range(N_BRANCH):
        gate = _sigmoid(_dot(xb, wgate_ref[:, r * D_MODEL:(r + 1) * D_MODEL])
                        + bgate_ref[:, r * D_MODEL:(r + 1) * D_MODEL])
        term = gate * _dot(ys[r], wup_ref[r])
        acc = term if acc is None else acc + term
    mix = _dot(acc.astype(BF16), wo_ref[...])
    o_ref[...] = _layer_norm(DN_ALPHA * x + mix, g_ref[...], b_ref[...])


def _merge(x2, yg, y5, ym, wgate, bgate, wglu, bglu, wup, wo, g, b, tm=512):
    t, dm = x2.shape
    row = lambda w: pl.BlockSpec((tm, w), lambda i: (i, 0))
    return pl.pallas_call(
        _merge_kernel,
        grid=(t // tm,),
        in_specs=[row(dm), row(BRANCH_W), row(BRANCH_W), row(BRANCH_W),
                  _full(wgate.shape), _full(bgate.shape), _full(wglu.shape), _full(bglu.shape),
                  _full(wup.shape), _full(wo.shape), _full(g.shape), _full(b.shape)],
        out_specs=row(dm),
        out_shape=jax.ShapeDtypeStruct((t, dm), F32),
        compiler_params=_cparams(("parallel",)),
        name="merge_ln1",
    )(x2, yg, y5, ym, wgate, bgate, wglu, bglu, wup, wo, g, b)


def _ple_ln2(x, xb, f, p_ref, pwg_ref, pwp_ref, g_ref, b_ref):
    e = _sigmoid(_dot(xb, pwg_ref[...])) * _dot(p_ref[...].astype(BF16), pwp_ref[...])
    return _layer_norm(DN_ALPHA * x + f + e, g_ref[...], b_ref[...])


def _ffn_kernel(x_ref, p_ref, wg_ref, wu_ref, wd_ref, pwg_ref, pwp_ref, g_ref, b_ref, o_ref):
    x = x_ref[...]
    xb = x.astype(BF16)
    hid = (_silu(_dot(xb, wg_ref[...])) * _dot(xb, wu_ref[...])).astype(BF16)
    f = _dot(hid, wd_ref[...])
    o_ref[...] = _ple_ln2(x, xb, f, p_ref, pwg_ref, pwp_ref, g_ref, b_ref)


def _ffn_layer(x2, p3, layer, wg, wu, wd, pwg, pwp, g, b, tm=256):
    t, dm = x2.shape
    row = lambda w: pl.BlockSpec((tm, w), lambda i: (i, 0))
    return pl.pallas_call(
        _ffn_kernel,
        grid=(t // tm,),
        in_specs=[row(dm), pl.BlockSpec((None, tm, PLE_DIM), lambda i: (layer, i, 0)),
                  _full(wg.shape), _full(wu.shape), _full(wd.shape),
                  _full(pwg.shape), _full(pwp.shape), _full(g.shape), _full(b.shape)],
        out_specs=row(dm),
        out_shape=jax.ShapeDtypeStruct((t, dm), F32),
        compiler_params=_cparams(("parallel",)),
        name="ffn_ple_ln2",
    )(x2, p3, wg, wu, wd, pwg, pwp, g, b)


def _router_kernel(x_ref, w_ref, b_ref, tril_ref, sel_ref, pr_ref, rk_ref, cnt_ref, base_ref):
    @pl.when(pl.program_id(0) == 0)
    def _():
        base_ref[...] = jnp.zeros_like(base_ref)

    xh, xl = _split2(x_ref[...])
    logits = _dot(xh, w_ref[0]) + _dot(xl, w_ref[0]) + _dot(xh, w_ref[1]) + b_ref[...]
    lane = lax.broadcasted_iota(jnp.int32, logits.shape, 1)
    neg = -jnp.inf
    logits = jnp.where(lane < N_EXPERTS, logits, neg)
    m1 = jnp.max(logits, axis=-1, keepdims=True)
    i1 = jnp.min(jnp.where(logits == m1, lane, LANES), axis=-1, keepdims=True)
    rest = jnp.where(lane == i1, neg, logits)
    m2 = jnp.max(rest, axis=-1, keepdims=True)
    i2 = jnp.min(jnp.where(rest == m2, lane, LANES), axis=-1, keepdims=True)
    e2 = jnp.exp(m2 - m1)
    p1 = 1.0 / (1.0 + e2)
    p2 = e2 / (1.0 + e2)
    hot = jnp.where((lane == i1) | (lane == i2), 1.0, 0.0)
    base = base_ref[0:1, :]
    before = _dot(tril_ref[...], hot.astype(BF16)) + base
    r1 = jnp.sum(jnp.where(lane == i1, before, 0.0), axis=-1, keepdims=True)
    r2 = jnp.sum(jnp.where(lane == i2, before, 0.0), axis=-1, keepdims=True)
    sel_ref[...] = jnp.where(lane == 0, i1, jnp.where(lane == 1, i2, 0))
    pr_ref[...] = jnp.where(lane == 0, p1, jnp.where(lane == 1, p2, 0.0))
    rk_ref[...] = jnp.where(lane == 0, r1, jnp.where(lane == 1, r2, 0.0))
    total = base + jnp.sum(hot, axis=0, keepdims=True)
    base_ref[...] = jnp.broadcast_to(total, base_ref.shape)
    cnt_ref[...] = jnp.broadcast_to(total, cnt_ref.shape)


def _router(x2, w, b, tm=512):
    t, dm = x2.shape
    i = np.arange(tm)
    tril = jnp.asarray((i[:, None] > i[None, :]).astype(np.float32), BF16)
    blk = pl.BlockSpec((tm, LANES), lambda i: (i, 0))
    return pl.pallas_call(
        _router_kernel,
        grid=(t // tm,),
        in_specs=[pl.BlockSpec((tm, dm), lambda i: (i, 0)), _full(w.shape), _full(b.shape), _full((tm, tm))],
        out_specs=[blk, blk, blk, pl.BlockSpec((8, LANES), lambda i: (0, 0))],
        out_shape=[jax.ShapeDtypeStruct((t, LANES), jnp.int32), jax.ShapeDtypeStruct((t, LANES), F32),
                   jax.ShapeDtypeStruct((t, LANES), F32), jax.ShapeDtypeStruct((8, LANES), F32)],
        scratch_shapes=[pltpu.VMEM((8, LANES), F32)],
        compiler_params=_cparams(("arbitrary",)),
        name="moe_router",
    )(x2, w, b, tril)


TOP_K = 2
MOE_TILE = 1024
MOE_SUB = 256
MOE_FF_TILE = 512
DISPATCH_TILE = 512
DMA_ISSUE_UNROLL = 8


ROW_SUB = D_MODEL // LANES


def _row_copy_wait(src_rows, dst_rows, sem):
    pltpu.make_async_copy(src_rows, dst_rows, sem).wait()


def _rows_to_tiles(dst_ref, val):
    n = val.shape[0]
    for j in range(ROW_SUB):
        dst_ref[pl.ds(j, n, stride=ROW_SUB), :] = val[:, j * LANES:(j + 1) * LANES]


def _tiles_to_rows(src_ref, n):
    return jnp.concatenate([src_ref[pl.ds(j, n, stride=ROW_SUB), :] for j in range(ROW_SUB)], axis=1)


def _row_tile(ref, r):
    return ref.at[pl.ds(pl.multiple_of(r * ROW_SUB, ROW_SUB), ROW_SUB), :]


def _dispatch_kernel(pos_ref, x_ref, xs_ref, stage_ref, sem):
    tm = x_ref.shape[0]
    base = pl.program_id(0) * tm * TOP_K
    _rows_to_tiles(stage_ref, x_ref[...])

    def body(r, carry):
        for k in range(TOP_K):
            dst = pos_ref[base + r * TOP_K + k]
            pltpu.make_async_copy(_row_tile(stage_ref, r), _row_tile(xs_ref, dst), sem).start(priority=k)
        return carry

    lax.fori_loop(0, tm, body, 0, unroll=DMA_ISSUE_UNROLL)
    for _ in range(TOP_K):
        _row_copy_wait(stage_ref, xs_ref.at[pl.ds(0, tm * ROW_SUB), :], sem)


def _dispatch(pos_flat, x2, tm=DISPATCH_TILE):
    t, dm = x2.shape
    return pl.pallas_call(
        _dispatch_kernel,
        grid_spec=pltpu.PrefetchScalarGridSpec(
            num_scalar_prefetch=1,
            grid=(t // tm,),
            in_specs=[pl.BlockSpec((tm, dm), lambda i, pos: (i, 0))],
            out_specs=pl.BlockSpec(memory_space=pl.ANY),
            scratch_shapes=[pltpu.VMEM((tm * ROW_SUB, LANES), F32), pltpu.SemaphoreType.DMA],
        ),
        out_shape=jax.ShapeDtypeStruct((t * TOP_K * ROW_SUB, LANES), F32),
        compiler_params=_cparams(("arbitrary",)),
        name="moe_dispatch",
    )(pos_flat, x2)


def _moe_group_kernel(tile_ref, exp_ref, lo_ref, hi_ref, xs_ref, wg_ref, wu_ref, wd_ref, o_ref,
                      acc_ref, xb_ref, wgb_ref, wub_ref, wdb_ref):
    i = pl.program_id(0)
    f = pl.program_id(1)
    lo = lo_ref[i]
    hi = hi_ref[i]
    tm = acc_ref.shape[0]

    @pl.when(hi > lo)
    def _():
        @pl.when((lo == 0) & (f == 0))
        def _():
            acc_ref[...] = jnp.zeros_like(acc_ref)

        @pl.when(f == 0)
        def _():
            xb_ref[...] = _tiles_to_rows(xs_ref, tm).astype(BF16)

        whole = (lo == 0) & (hi == tm)

        @pl.when(whole)
        def _():
            xb = xb_ref[...]
            hid = (_silu(_dot(xb, wg_ref[0].astype(BF16))) * _dot(xb, wu_ref[0].astype(BF16))).astype(BF16)
            acc_ref[...] += _dot(hid, wd_ref[0].astype(BF16))

        @pl.when(jnp.logical_not(whole))
        def _():
            wgb_ref[...] = wg_ref[0].astype(BF16)
            wub_ref[...] = wu_ref[0].astype(BF16)
            wdb_ref[...] = wd_ref[0].astype(BF16)
            for sub in range(tm // MOE_SUB):
                @pl.when((hi > sub * MOE_SUB) & (lo < (sub + 1) * MOE_SUB))
                def _(sub=sub):
                    rsl = slice(sub * MOE_SUB, (sub + 1) * MOE_SUB)
                    xb = xb_ref[rsl, :]
                    hid = _silu(_dot(xb, wgb_ref[...])) * _dot(xb, wub_ref[...])
                    rid = lax.broadcasted_iota(jnp.int32, (MOE_SUB, 1), 0) + sub * MOE_SUB
                    hid = jnp.where((rid >= lo) & (rid < hi), hid, 0.0).astype(BF16)
                    acc_ref[rsl, :] += _dot(hid, wdb_ref[...])

        @pl.when((hi == tm) & (f == pl.num_programs(1) - 1))
        def _():
            _rows_to_tiles(o_ref, acc_ref[...])


def _moe_grouped(items, xs, wg, wu, wd, tm=MOE_TILE, tf=MOE_FF_TILE):
    tile, exp, lo, hi = items
    dm = wg.shape[1]
    dff = wg.shape[2]
    return pl.pallas_call(
        _moe_group_kernel,
        grid_spec=pltpu.PrefetchScalarGridSpec(
            num_scalar_prefetch=4,
            grid=(tile.shape[0], dff // tf),
            in_specs=[pl.BlockSpec((tm * ROW_SUB, LANES), lambda i, f, tl, ex, lo_, hi_: (tl[i], 0)),
                      pl.BlockSpec((1, dm, tf), lambda i, f, tl, ex, lo_, hi_: (ex[i], 0, f)),
                      pl.BlockSpec((1, dm, tf), lambda i, f, tl, ex, lo_, hi_: (ex[i], 0, f)),
                      pl.BlockSpec((1, tf, dm), lambda i, f, tl, ex, lo_, hi_: (ex[i], f, 0))],
            out_specs=pl.BlockSpec((tm * ROW_SUB, LANES), lambda i, f, tl, ex, lo_, hi_: (tl[i], 0)),
            scratch_shapes=[pltpu.VMEM((tm, dm), F32), pltpu.VMEM((tm, dm), BF16), pltpu.VMEM((dm, tf), BF16),
                            pltpu.VMEM((dm, tf), BF16), pltpu.VMEM((tf, dm), BF16)],
        ),
        out_shape=jax.ShapeDtypeStruct(xs.shape, F32),
        compiler_params=_cparams(("arbitrary", "arbitrary")),
        name="moe_grouped",
    )(tile, exp, lo, hi, xs, wg, wu, wd)


def _moe_items(counts, n_rows, tm=MOE_TILE):
    n_tiles = n_rows // tm
    ends = jnp.cumsum(counts)
    cuts = jnp.sort(jnp.concatenate([jnp.arange(n_tiles + 1, dtype=jnp.int32) * tm, ends[:-1]]))
    start, stop = cuts[:-1], cuts[1:]
    tile = jnp.minimum(start // tm, n_tiles - 1)
    exp = jnp.minimum(jnp.sum(ends[None, :] <= start[:, None], axis=1), N_EXPERTS - 1)
    lo = start - tile * tm
    hi = stop - tile * tm
    return tile.astype(jnp.int32), exp.astype(jnp.int32), lo.astype(jnp.int32), hi.astype(jnp.int32)


def _combine_kernel(pos_ref, x_ref, pr_ref, p_ref, pwg_ref, pwp_ref, g_ref, b_ref, ys_ref, o_ref, gat_ref, sem):
    tm = x_ref.shape[0]
    base = pl.program_id(0) * tm * TOP_K

    def body(r, carry):
        for k in range(TOP_K):
            src = pos_ref[base + r * TOP_K + k]
            pltpu.make_async_copy(_row_tile(ys_ref, src), _row_tile(gat_ref.at[k], r), sem).start(priority=k)
        return carry

    lax.fori_loop(0, tm, body, 0, unroll=DMA_ISSUE_UNROLL)
    x = x_ref[...]
    xb = x.astype(BF16)
    e = _sigmoid(_dot(xb, pwg_ref[...])) * _dot(p_ref[...].astype(BF16), pwp_ref[...])
    for k in range(TOP_K):
        _row_copy_wait(ys_ref.at[pl.ds(0, tm * ROW_SUB), :], gat_ref.at[k], sem)
    pr = pr_ref[...]
    f = pr[:, 0:1] * _tiles_to_rows(gat_ref.at[0], tm)
    for k in range(1, TOP_K):
        f = f + pr[:, k:k + 1] * _tiles_to_rows(gat_ref.at[k], tm)
    o_ref[...] = _layer_norm(DN_ALPHA * x + f + e, g_ref[...], b_ref[...])


def _combine_layer(pos_flat, x2, pr, p3, layer, pwg, pwp, g, b, ys, tm=DISPATCH_TILE):
    t, dm = x2.shape
    row = lambda w: pl.BlockSpec((tm, w), lambda i, pos: (i, 0))
    full = lambda a: pl.BlockSpec(a.shape, lambda i, pos: (0,) * a.ndim, pipeline_mode=pl.Buffered(1))
    return pl.pallas_call(
        _combine_kernel,
        grid_spec=pltpu.PrefetchScalarGridSpec(
            num_scalar_prefetch=1,
            grid=(t // tm,),
            in_specs=[row(dm), row(LANES), pl.BlockSpec((None, tm, PLE_DIM), lambda i, pos: (layer, i, 0)),
                      full(pwg), full(pwp), full(g), full(b),
                      pl.BlockSpec(memory_space=pl.ANY)],
            out_specs=row(dm),
            scratch_shapes=[pltpu.VMEM((TOP_K, tm * ROW_SUB, LANES), F32), pltpu.SemaphoreType.DMA],
        ),
        out_shape=jax.ShapeDtypeStruct((t, dm), F32),
        compiler_params=_cparams(("arbitrary",)),
        name="moe_combine_ple_ln2",
    )(pos_flat, x2, pr, p3, pwg, pwp, g, b, ys)


def _row(v):
    return v.reshape(1, -1).astype(F32)


def _pad_lanes(w):
    return jnp.pad(w, ((0, 0), (0, LANES - w.shape[1])))


def kernel(x, p, w_in, b_in, gla_w_a2, gla_b_a2, gla_norm_g, s5_a_re, s5_a_im, s5_log_dt, s5_b_re, s5_b_im,
           s5_c_re, s5_c_im, s5_d, s5_w_glu, s5_b_glu, ml_conv_w, ml_conv_b, ml_norm_g, w_up, w_o, ln1_g, ln1_b,
           ffn_wg, ffn_wu, ffn_wd, moe_router, moe_router_b, moe_wg, moe_wu, moe_wd, ple_w_gate, ple_w_proj,
           ln2_g, ln2_b):
    bsz, s, dm = x.shape
    t = bsz * s
    hi = lax.Precision.HIGHEST
    o = IN_OFF
    tri = jnp.asarray(_chunk_tri(SEQ_BLOCK, CHUNK), BF16)
    ml_tri_np = _chunk_tri(SEQ_BLOCK, ML_CHUNK)
    ml_tri = jnp.asarray(ml_tri_np, BF16)
    ml_trit = jnp.asarray(ml_tri_np.T, BF16)
    s5_ec, s5_ftc, s5_c0, s5_atab = _s5_tables(s5_a_re, s5_a_im, s5_log_dt, s5_b_re, s5_b_im, s5_c_re, s5_c_im)
    p3 = p.reshape(DEPTH, t, PLE_DIM)

    w_in_b = w_in.astype(BF16)
    for i in range(DEPTH):
        w, b = w_in_b[i], b_in[i]
        sl = lambda k: (w[:, o[k]:o[k + 1]], b[o[k]:o[k + 1]])
        (wq, bq), (wk, bk), (wv, bv), (_, ba), (wg_, bg_) = sl(0), sl(1), sl(2), sl(3), sl(4)
        wz = jnp.dot(w_in[i, :, o[3]:o[4]], gla_w_a2[i], precision=hi).astype(BF16)
        bz = jnp.dot(ba, gla_w_a2[i], precision=hi) + gla_b_a2[i]
        (wu_, bu_) = sl(5)
        w_gla = jnp.concatenate([wq, wk, wz, wv, wg_, wu_], axis=1)
        b_gla = _row(jnp.concatenate([bq, bk, bz, bv, bg_, bu_]))
        y_gla, u = _gla_mixer(x, w_gla, b_gla, _row(gla_norm_g[i]), tri)

        x2 = x.reshape(t, dm)
        y5 = _s5_mixer(u, s5_ec[i], s5_ftc[i], s5_c0[i], _row(s5_d[i]), s5_atab[i])

        (wmq, bmq), (wmk, bmk), (wmv, bmv), (wmo, bmo) = sl(6), sl(7), sl(8), sl(11)
        w_ml = jnp.concatenate([wmq, wmk, wmv, wmo], axis=1)
        b_ml = _row(jnp.concatenate([bmq, bmk, bmv, bmo]))
        w_if = w_in[i, :, o[9]:o[11]]
        b_if = b[o[9]:o[11]]
        if_h = w_if.astype(BF16)
        if_l = (w_if - if_h.astype(F32)).astype(BF16)
        wgc = jnp.stack([_pad_lanes(if_h), _pad_lanes(if_l)])
        wgr = jnp.stack([if_h.T, if_l.T])
        y_ml = _ml_mixer(x, w_ml, b_ml, wgc, _pad_lanes(_row(b_if)), wgr, b_if.reshape(-1, 1).astype(F32),
                         ml_conv_w[i].astype(F32), _row(ml_conv_b[i]), _row(ml_norm_g[i]), ml_tri, ml_trit)

        (wgt, bgt) = sl(12)
        x1 = _merge(x2, y_gla.reshape(t, BRANCH_W), y5, y_ml.reshape(t, BRANCH_W),
                    wgt, _row(bgt), s5_w_glu[i].astype(BF16), _row(s5_b_glu[i]),
                    w_up[i].astype(BF16), w_o[i].astype(BF16), _row(ln1_g[i]), _row(ln1_b[i]))

        pwg = ple_w_gate[i].astype(BF16)
        pwp = ple_w_proj[i].astype(BF16)
        j = i // 2
        if i % 2 == 0:
            x2n = _ffn_layer(x1, p3, i, ffn_wg[j].astype(BF16), ffn_wu[j].astype(BF16), ffn_wd[j].astype(BF16),
                             pwg, pwp, _row(ln2_g[i]), _row(ln2_b[i]))
        else:
            wr = moe_router[j]
            wr_h = wr.astype(BF16)
            wr_l = (wr - wr_h.astype(F32)).astype(BF16)
            sel, pr, rk, cnt = _router(x1, jnp.stack([_pad_lanes(wr_h), _pad_lanes(wr_l)]),
                                       _pad_lanes(_row(moe_router_b[j])))
            counts = cnt[0, :N_EXPERTS].astype(jnp.int32)
            starts = jnp.cumsum(counts) - counts
            sel2 = sel[:, :TOP_K]
            pos = rk[:, :TOP_K].astype(jnp.int32) + jnp.sum(
                jnp.where(sel2[..., None] == jnp.arange(N_EXPERTS), starts, 0), axis=-1)
            pos_flat = pos.reshape(-1)
            xs = _dispatch(pos_flat, x1)
            ys = _moe_grouped(_moe_items(counts, t * TOP_K), xs, moe_wg[j], moe_wu[j], moe_wd[j])
            x2n = _combine_layer(pos_flat, x1, pr, p3, i, pwg, pwp, _row(ln2_g[i]), _row(ln2_b[i]), ys)
        x = x2n.reshape(bsz, s, dm)
    return x
```

```python
import functools
import math

import numpy as np
import jax
import jax.numpy as jnp
from jax import lax
from jax.experimental import pallas as pl
from jax.experimental.pallas import tpu as pltpu

F32 = jnp.float32
BF16 = jnp.bfloat16

D_MODEL = 1024
DEPTH = 2
N_BRANCH = 3
BRANCH_W = 512
HEADS = 4
DK = 64
DV = BRANCH_W // HEADS
GLA_RANK = 16
GLA_TAU = 16.0
CHUNK = 64
ML_CHUNK = 256
S5_GROUP = 16
S5_GROUPS = BRANCH_W // S5_GROUP
S5_STATE = 64
S5_CHUNK = 16
ML_CONV = 4
N_EXPERTS = 8
PLE_DIM = 256
DN_ALPHA = (2.0 * DEPTH) ** 0.25
LN_EPS = 1e-5

IN_WIDTHS = (
    HEADS * DK, HEADS * DK, BRANCH_W, GLA_RANK, BRANCH_W,
    BRANCH_W,
    HEADS * DK, HEADS * DK, BRANCH_W, HEADS, HEADS, BRANCH_W,
    N_BRANCH * D_MODEL,
)
IN_OFF = tuple(int(o) for o in np.concatenate([[0], np.cumsum(IN_WIDTHS)]))

LANES = 128
SUBLANES = 8
SEQ_BLOCK = 256
GLA_SEQ_PER_STEP = 4
ML_SEQ_PER_STEP = 1
PAIR_W = 2 * DK
VMEM_LIMIT = 56 * 1024 * 1024


def _cparams(sem):
    return pltpu.CompilerParams(dimension_semantics=sem, vmem_limit_bytes=VMEM_LIMIT)


def _dot(a, b):
    return jnp.dot(a, b, preferred_element_type=F32)


def _dot_nt(a, b):
    return lax.dot_general(a, b, (((1,), (1,)), ((), ())), preferred_element_type=F32)


def _dot_tn(a, b):
    return lax.dot_general(a, b, (((0,), (0,)), ((), ())), preferred_element_type=F32)


def _split3(a):
    hi = a.astype(BF16)
    r = a - hi.astype(F32)
    mid = r.astype(BF16)
    lo = (r - mid.astype(F32)).astype(BF16)
    return hi, mid, lo


def _split2(a):
    hi = a.astype(BF16)
    lo = (a - hi.astype(F32)).astype(BF16)
    return hi, lo


def _log_sigmoid(x):
    return jnp.minimum(x, 0.0) - jnp.log(1.0 + jnp.exp(-jnp.abs(x)))


def _sigmoid(x):
    return 0.5 * jnp.tanh(0.5 * x) + 0.5


def _silu(x):
    return x * _sigmoid(x)


def _gelu_tanh(x):
    return 0.5 * x * (1.0 + jnp.tanh(math.sqrt(2.0 / math.pi) * (x + 0.044715 * (x * x * x))))


def _layer_norm(v, g, b):
    mu = jnp.mean(v, axis=-1, keepdims=True)
    c = v - mu
    var = jnp.mean(c * c, axis=-1, keepdims=True)
    return c * lax.rsqrt(var + LN_EPS) * g + b


def _head_norm(o):
    mu = jnp.mean(o, axis=-1, keepdims=True)
    c = o - mu
    var = jnp.mean(c * c, axis=-1, keepdims=True)
    return c * lax.rsqrt(var + LN_EPS)


def _chunk_tri(n, chunk):
    i = np.arange(n)
    return ((i[:, None] >= i[None, :]) & (i[:, None] // chunk == i[None, :] // chunk)).astype(np.float32)


def _full(shape):
    nd = len(shape)
    return pl.BlockSpec(shape, lambda *_: (0,) * nd, pipeline_mode=pl.Buffered(1))


def _gla_kernel(x_ref, w_ref, b_ref, ng_ref, tri_ref, y_ref, u_ref, st_ref, o_ref):
    @pl.when(pl.program_id(1) == 0)
    def _():
        st_ref[...] = jnp.zeros_like(st_ref)

    hk = HEADS * DK
    nb, lb, d = x_ref.shape
    xb = x_ref[...].reshape(nb * lb, d).astype(BF16)
    h = _dot(xb, w_ref[...]) + b_ref[...]
    q = h[:, 0:hk]
    k = h[:, hk:2 * hk]
    z = h[:, 2 * hk:3 * hk]
    v = h[:, 3 * hk:3 * hk + BRANCH_W]
    g = h[:, 3 * hk + BRANCH_W:3 * hk + 2 * BRANCH_W]
    u_ref[...] = h[:, 3 * hk + 2 * BRANCH_W:3 * hk + 3 * BRANCH_W].reshape(nb, lb, BRANCH_W).astype(u_ref.dtype)

    la = _log_sigmoid(z) * (1.0 / GLA_TAU)
    tri = tri_ref[...]
    la_h, la_m, la_l = _split3(la)
    cum = jnp.concatenate(
        [_dot(tri, la_h[r:r + lb]) + _dot(tri, la_m[r:r + lb]) + _dot(tri, la_l[r:r + lb])
         for r in range(0, nb * lb, lb)], axis=0)
    qd = q * (DK ** -0.5) * jnp.exp(cum)
    ki = k * jnp.exp(-cum)

    lane = lax.broadcasted_iota(jnp.int32, (1, PAIR_W), 1)
    row_i = lax.broadcasted_iota(jnp.int32, (CHUNK, 2 * CHUNK), 0)
    col_i = lax.broadcasted_iota(jnp.int32, (CHUNK, 2 * CHUNK), 1)
    causal = row_i >= col_i % CHUNK
    bd_r = lax.broadcasted_iota(jnp.int32, (2 * DV, PAIR_W), 0)
    bd_c = lax.broadcasted_iota(jnp.int32, (2 * DV, PAIR_W), 1)
    blockdiag = (bd_r >= DV) == (bd_c >= DK)
    vb_r = lax.broadcasted_iota(jnp.int32, (2 * CHUNK, 2 * DV), 0)
    vb_c = lax.broadcasted_iota(jnp.int32, (2 * CHUNK, 2 * DV), 1)
    v_blocks = (vb_r >= CHUNK) == (vb_c >= DV)

    for c in range(lb // CHUNK):
        for bb, p in [(bb, p) for bb in range(nb) for p in range(HEADS // 2)]:
            st = st_ref[bb, p]
            lsl = slice(p * PAIR_W, (p + 1) * PAIR_W)
            rsl = slice(bb * lb + c * CHUNK, bb * lb + (c + 1) * CHUNK)
            qd_c = qd[rsl, lsl]
            ki_c = ki[rsl, lsl]
            cum_c = cum[rsl, lsl]
            last = cum_c[CHUNK - 1:CHUNK, :]
            kt = (k[rsl, lsl] * jnp.exp(last - cum_c)).astype(BF16)
            qd_b = qd_c.astype(BF16)
            inter = _dot_nt(qd_b, st.astype(BF16))
            k_bd = jnp.concatenate([jnp.where(lane < DK, ki_c, 0.0), jnp.where(lane >= DK, ki_c, 0.0)],
                                   axis=0).astype(BF16)
            att = jnp.where(causal, _dot_nt(qd_b, k_bd), 0.0)
            v_c = v[rsl, p * 2 * DV:(p + 1) * 2 * DV]
            v_bd = jnp.where(v_blocks, jnp.concatenate([v_c, v_c], axis=0), 0.0).astype(BF16)
            o_ref[rsl, p * 2 * DV:(p + 1) * 2 * DV] = _dot(att.astype(BF16), v_bd) + inter
            upd = _dot_tn(v_c.astype(BF16), kt)
            st_ref[bb, p] = st * jnp.exp(last) + jnp.where(blockdiag, upd, 0.0)

    ng = ng_ref[...]
    for head in range(HEADS):
        hsl = slice(head * DV, (head + 1) * DV)
        y = _head_norm(o_ref[:, hsl]) * ng[:, hsl] * _silu(g[:, hsl])
        y_ref[:, :, hsl] = y.reshape(nb, lb, DV).astype(y_ref.dtype)


def _gla_mixer(x, w, b, ng, tri):
    bsz, s, d = x.shape
    wcols = w.shape[1]
    nb = GLA_SEQ_PER_STEP
    return pl.pallas_call(
        _gla_kernel,
        grid=(bsz // nb, s // SEQ_BLOCK),
        in_specs=[
            pl.BlockSpec((nb, SEQ_BLOCK, d), lambda i, j: (i, j, 0)),
            _full((d, wcols)), _full((1, wcols)), _full((1, BRANCH_W)),
            _full((SEQ_BLOCK, SEQ_BLOCK)),
        ],
        out_specs=[pl.BlockSpec((nb, SEQ_BLOCK, BRANCH_W), lambda i, j: (i, j, 0))] * 2,
        out_shape=[jax.ShapeDtypeStruct((bsz, s, BRANCH_W), BF16)] * 2,
        scratch_shapes=[pltpu.VMEM((nb, HEADS // 2, 2 * DV, PAIR_W), F32),
                        pltpu.VMEM((nb * SEQ_BLOCK, BRANCH_W), F32)],
        compiler_params=_cparams(("arbitrary", "arbitrary")),
        name="gla_mixer",
    )(x, w, b, ng, tri)


ML_ST_ROWS = 2 * DV + LANES


def _ml_kernel(x_ref, w_ref, b_ref, wgc_ref, bgc_ref, wgr_ref, bgr_ref, cw_ref, cb_ref, ng_ref,
               tri_ref, trit_ref, y_ref, ct_ref, m_ref, carry_ref, o_ref):
    @pl.when(pl.program_id(1) == 0)
    def _():
        ct_ref[...] = jnp.zeros_like(ct_ref)
        m_ref[...] = jnp.zeros_like(m_ref)
        carry_ref[...] = jnp.zeros_like(carry_ref)

    hk = HEADS * DK
    nb, lb, d = x_ref.shape
    x32 = x_ref[...].reshape(nb * lb, d)
    xh, xl = _split2(x32)
    h = _dot(xh, w_ref[...]) + b_ref[...]
    qk = h[:, 0:2 * hk]
    v = h[:, 2 * hk:2 * hk + BRANCH_W]
    o_pre = h[:, 2 * hk + BRANCH_W:2 * hk + 2 * BRANCH_W]

    cw = cw_ref[...]
    conv = []
    for bb in range(nb):
        qk_b = qk[bb * lb:(bb + 1) * lb]
        ext = jnp.concatenate([carry_ref[bb], qk_b], axis=0)
        first = SUBLANES - (ML_CONV - 1)
        acc = cb_ref[...] + ext[first:first + lb] * cw[0:1]
        for j in range(1, ML_CONV):
            acc = acc + ext[first + j:first + j + lb] * cw[j:j + 1]
        carry_ref[bb] = qk_b[lb - SUBLANES:lb]
        conv.append(acc)
    qkc = _silu(jnp.concatenate(conv, axis=0))
    qf = qkc[:, 0:hk]
    kf = qkc[:, hk:2 * hk] * (DK ** -0.5)
    vb = v.astype(BF16)

    gc = (_dot(xh, wgc_ref[0]) + _dot(xl, wgc_ref[0]) + _dot(xh, wgc_ref[1])) + bgc_ref[...]
    gr = (_dot_nt(wgr_ref[0], xh) + _dot_nt(wgr_ref[0], xl) + _dot_nt(wgr_ref[1], xh)) + bgr_ref[...]
    lf_c = _log_sigmoid(gc)
    lf_r = _log_sigmoid(gr)
    tri = tri_ref[...]
    trit = trit_ref[...]
    c_h, c_m, c_l = _split3(lf_c)
    r_h, r_m, r_l = _split3(lf_r)
    blocks = [slice(r, r + lb) for r in range(0, nb * lb, lb)]
    bc = jnp.concatenate([_dot(tri, c_h[r]) + _dot(tri, c_m[r]) + _dot(tri, c_l[r]) for r in blocks],
                         axis=0)
    br = jnp.concatenate([_dot(r_h[:, r], trit) + _dot(r_m[:, r], trit) + _dot(r_l[:, r], trit)
                          for r in blocks], axis=1)

    lane = lax.broadcasted_iota(jnp.int32, (1, PAIR_W), 1)
    row_i = lax.broadcasted_iota(jnp.int32, (ML_CHUNK, ML_CHUNK), 0)
    col_i = lax.broadcasted_iota(jnp.int32, (ML_CHUNK, ML_CHUNK), 1)
    causal = row_i >= col_i
    sr = lax.broadcasted_iota(jnp.int32, (ML_ST_ROWS, PAIR_W), 0)
    sc_ = lax.broadcasted_iota(jnp.int32, (ML_ST_ROWS, PAIR_W), 1)
    first = sc_ < DK
    rows_h0 = (sr < DV) | (sr == 2 * DV)
    rows_h1 = ((sr >= DV) & (sr < 2 * DV)) | (sr == 2 * DV + 1)
    st_mask = (rows_h0 & first) | (rows_h1 & ~first)
    ones_blk = jnp.ones((ML_CHUNK, LANES), BF16)

    for bb, p in [(bb, p) for bb in range(nb) for p in range(HEADS // 2)]:
        ct = ct_ref[bb, p]
        lsl = slice(p * PAIR_W, (p + 1) * PAIR_W)
        m_pair = [m_ref[bb, 2 * p + hh][0:1, 0:1] for hh in range(2)]
        for c in range(lb // ML_CHUNK):
            rsl = slice(bb * lb + c * ML_CHUNK, bb * lb + (c + 1) * ML_CHUNK)
            q_c = qf[rsl, lsl]
            k_c = kf[rsl, lsl]
            k_cb = k_c.astype(BF16)
            inter_mm = _dot_nt(q_c.astype(BF16), ct.astype(BF16))
            wt_cols, decays = [], []
            for hh in range(2):
                head = 2 * p + hh
                m_st = m_pair[hh]
                b_col = bc[rsl, HEADS + head:HEADS + head + 1]
                i_col = gc[rsl, head:head + 1]
                b_row = br[HEADS + head:HEADS + head + 1, rsl]
                i_row = gr[head:head + 1, rsl]
                dmat = jnp.where(causal, b_col - b_row + i_row, -jnp.inf)
                inter = b_col + m_st
                m_row = jnp.maximum(inter, jnp.max(dmat, axis=-1, keepdims=True))
                wts = jnp.exp(dmat - m_row)
                in_head = (lane >= hh * DK) & (lane < (hh + 1) * DK)
                qm = jnp.where(in_head, q_c, 0.0).astype(BF16)
                sc = _dot_nt(qm, k_cb) * wts
                w_inter = jnp.exp(inter - m_row)
                num = _dot(sc.astype(BF16), vb[rsl, head * DV:(head + 1) * DV]) \
                    + w_inter * inter_mm[:, hh * DV:(hh + 1) * DV]
                den = jnp.sum(sc, axis=-1, keepdims=True) \
                    + w_inter * inter_mm[:, 2 * DV + hh:2 * DV + hh + 1]
                o_ref[rsl, head * DV:(head + 1) * DV] = num / jnp.maximum(jnp.abs(den), jnp.exp(-m_row))
                g_tot = b_col[ML_CHUNK - 1:ML_CHUNK, :]
                tail = g_tot - b_col + i_col
                m_new = jnp.maximum(g_tot + m_st, jnp.max(tail, axis=0, keepdims=True))
                wt_cols.append(jnp.exp(tail - m_new))
                decays.append(jnp.exp(g_tot + m_st - m_new))
                m_pair[hh] = m_new
            wk = (k_c * jnp.where(lane < DK, wt_cols[0], wt_cols[1])).astype(BF16)
            vp = jnp.concatenate([vb[rsl, p * 2 * DV:(p + 1) * 2 * DV], ones_blk], axis=1)
            upd = _dot_tn(vp, wk)
            ct = ct * jnp.where(lane < DK, decays[0], decays[1]) + jnp.where(st_mask, upd, 0.0)
        ct_ref[bb, p] = ct
        for hh in range(2):
            m_ref[bb, 2 * p + hh] = jnp.broadcast_to(m_pair[hh], m_ref.shape[2:])

    ng = ng_ref[...]
    for head in range(HEADS):
        hsl = slice(head * DV, (head + 1) * DV)
        y = _head_norm(o_ref[:, hsl]) * ng[:, hsl] * _sigmoid(o_pre[:, hsl])
        y_ref[:, :, hsl] = y.reshape(nb, lb, DV).astype(y_ref.dtype)


def _ml_mixer(x, w, b, wgc, bgc, wgr, bgr, cw, cb, ng, tri, trit):
    bsz, s, d = x.shape
    wcols = w.shape[1]
    nb = ML_SEQ_PER_STEP
    return pl.pallas_call(
        _ml_kernel,
        grid=(bsz // nb, s // SEQ_BLOCK),
        in_specs=[
            pl.BlockSpec((nb, SEQ_BLOCK, d), lambda i, j: (i, j, 0)),
            _full((d, wcols)), _full((1, wcols)),
            _full((2, d, LANES)), _full((1, LANES)),
            _full((2, 2 * HEADS, d)), _full((2 * HEADS, 1)),
            _full((ML_CONV, 2 * HEADS * DK)), _full((1, 2 * HEADS * DK)),
            _full((1, BRANCH_W)),
            _full((SEQ_BLOCK, SEQ_BLOCK)), _full((SEQ_BLOCK, SEQ_BLOCK)),
        ],
        out_specs=pl.BlockSpec((nb, SEQ_BLOCK, BRANCH_W), lambda i, j: (i, j, 0)),
        out_shape=jax.ShapeDtypeStruct((bsz, s, BRANCH_W), BF16),
        scratch_shapes=[pltpu.VMEM((nb, HEADS // 2, ML_ST_ROWS, PAIR_W), F32),
                        pltpu.VMEM((nb, HEADS, SUBLANES, LANES), F32),
                        pltpu.VMEM((nb, SUBLANES, 2 * HEADS * DK), F32),
                        pltpu.VMEM((nb * SEQ_BLOCK, BRANCH_W), F32)],
        compiler_params=_cparams(("arbitrary", "arbitrary")),
        name="mlstm_mixer",
    )(x, w, b, wgc, bgc, wgr, bgr, cw, cb, ng, tri, trit)


S5_IO_STEPS = 4
S5_PERM_BLOCK = 256
S5_HALF = 256
S5_HALF_STATE = (S5_HALF // S5_GROUP) * S5_STATE


def _s5_kernel(bsz, nc, u_ref, perm_ref, permt_ref, ec_ref, ftc_ref, c0_ref, d_ref, a_ref, y_ref,
               in_ref, yall_ref, xs_ref, xsb_ref, kexp_ref, c0x_ref):
    s = pl.program_id(1)
    lc = S5_CHUNK
    rows = bsz * nc
    sw = S5_HALF_STATE
    pb = perm_ref.shape[0]
    cb = pb // lc
    n_io = S5_IO_STEPS
    subs = rows * lc // (n_io * pb)

    def expand_state(blk):
        tile = jnp.concatenate([blk[:, 0:LANES]] * (sw // LANES) + [blk[:, LANES:2 * LANES]] * (sw // LANES),
                               axis=1)
        r = lax.broadcasted_iota(jnp.int32, tile.shape, 0) // S5_GROUP
        c = (lax.broadcasted_iota(jnp.int32, tile.shape, 1) % sw) // S5_STATE
        return jnp.where(r == c, tile, 0.0).astype(BF16)

    @pl.when(s == 0)
    def _():
        c0x_ref[...] = expand_state(c0_ref[...])
        xs_ref[...] = jnp.zeros_like(xs_ref)

    @pl.when(s < n_io)
    def _():
        for sb in range(subs):
            pu = _dot(perm_ref[...], u_ref[sb * pb:(sb + 1) * pb, :]).astype(BF16)
            r0 = pl.multiple_of((s * subs + sb) * cb, cb)
            for l in range(lc):
                in_ref[l, pl.ds(r0, cb), :] = pu[l * cb:(l + 1) * cb]

    @pl.when((s >= n_io) & (s < n_io + lc))
    def _():
        l = s - n_io
        u = in_ref[l]
        et = expand_state(ec_ref[0])
        kblk = _dot_nt(et, c0x_ref[...])
        r = lax.broadcasted_iota(jnp.int32, kblk.shape, 0)
        c = lax.broadcasted_iota(jnp.int32, kblk.shape, 1)
        kblk = kblk + jnp.where((r == c) & (l == lc - 1), d_ref[...], 0.0)
        kexp_ref[pl.ds(pl.multiple_of(l * S5_HALF, S5_HALF), S5_HALF), :] = kblk.astype(BF16)
        for jc in range(2 * sw // S5_HALF):
            part = _dot(u, et[:, jc * S5_HALF:(jc + 1) * S5_HALF])
            for jj in range(S5_HALF // LANES):
                xs_ref[jc * (S5_HALF // LANES) + jj] += part[:, jj * LANES:(jj + 1) * LANES]

    @pl.when(s == n_io + lc - 1)
    def _():
        ar = a_ref[0, 0:1, :]
        ai = a_ref[0, 1:2, :]
        nt = sw // LANES

        def body(c, carry):
            sr, si = carry
            idx = pl.ds(c, bsz, stride=nc)
            xr = jnp.concatenate([xs_ref[j, idx, :] for j in range(nt)], axis=1)
            xi = jnp.concatenate([xs_ref[nt + j, idx, :] for j in range(nt)], axis=1)
            for j in range(nt):
                xs_ref[j, idx, :] = sr[:, j * LANES:(j + 1) * LANES]
                xs_ref[nt + j, idx, :] = si[:, j * LANES:(j + 1) * LANES]
            return ar * sr - ai * si + xr, ar * si + ai * sr + xi

        zero = jnp.zeros((bsz, sw), F32)
        lax.fori_loop(0, nc, body, (zero, zero))
        for j in range(2 * nt):
            xsb_ref[:, j * LANES:(j + 1) * LANES] = xs_ref[j].astype(BF16)

    for lo in range(lc):
        @pl.when(s == n_io + lc + lo)
        def _(lo=lo):
            acc = _dot_nt(xsb_ref[...], expand_state(ftc_ref[0]))
            for l in range(lo + 1):
                j = lc - 1 - lo + l
                acc = acc + _dot(in_ref[l], kexp_ref[j * S5_HALF:(j + 1) * S5_HALF, :])
            yall_ref[lo] = acc.astype(BF16)

    @pl.when(s >= n_io + 2 * lc)
    def _():
        t = s - (n_io + 2 * lc)
        for sb in range(subs):
            r0 = pl.multiple_of((t * subs + sb) * cb, cb)
            ycat = jnp.concatenate([yall_ref[l, pl.ds(r0, cb), :] for l in range(lc)], axis=0)
            y_ref[sb * pb:(sb + 1) * pb, :] = _dot(permt_ref[...], ycat).astype(y_ref.dtype)


def _s5_mixer(u, ec, ftc, c0, d, a):
    bsz, s, _ = u.shape
    lc = S5_CHUNK
    nc = s // lc
    t = bsz * s
    nh = BRANCH_W // S5_HALF
    n_io = S5_IO_STEPS
    tb = t // n_io
    pb = S5_PERM_BLOCK
    src = (np.arange(pb) % (pb // lc)) * lc + np.arange(pb) // (pb // lc)
    perm_np = (src[:, None] == np.arange(pb)[None, :]).astype(np.float32)
    perm, permt = jnp.asarray(perm_np, BF16), jnp.asarray(perm_np.T, BF16)
    n_steps = 2 * n_io + 2 * lc
    y = pl.pallas_call(
        functools.partial(_s5_kernel, bsz, nc),
        grid=(nh, n_steps),
        in_specs=[
            pl.BlockSpec((tb, S5_HALF), lambda h, t: (jnp.minimum(t, n_io - 1), h)),
            _full((pb, pb)), _full((pb, pb)),
            pl.BlockSpec((1, S5_HALF, 2 * LANES), lambda h, t: (jnp.clip(t - n_io, 0, lc - 1), h, 0)),
            pl.BlockSpec((1, S5_HALF, 2 * LANES), lambda h, t: (jnp.clip(t - n_io - lc, 0, lc - 1), h, 0)),
            pl.BlockSpec((S5_HALF, 2 * LANES), lambda h, t: (h, 0)),
            pl.BlockSpec((1, S5_HALF), lambda h, t: (0, h)),
            pl.BlockSpec((1, 2, S5_HALF_STATE), lambda h, t: (h, 0, 0)),
        ],
        out_specs=pl.BlockSpec((tb, S5_HALF), lambda h, t: (jnp.maximum(t - n_io - 2 * lc, 0), h)),
        out_shape=jax.ShapeDtypeStruct((t, BRANCH_W), BF16),
        scratch_shapes=[pltpu.VMEM((lc, bsz * nc, S5_HALF), BF16),
                        pltpu.VMEM((lc, bsz * nc, S5_HALF), BF16),
                        pltpu.VMEM((2 * S5_HALF_STATE // LANES, bsz * nc, LANES), F32),
                        pltpu.VMEM((bsz * nc, 2 * S5_HALF_STATE), BF16),
                        pltpu.VMEM((lc * S5_HALF, S5_HALF), BF16),
                        pltpu.VMEM((S5_HALF, 2 * S5_HALF_STATE), BF16)],
        compiler_params=_cparams(("arbitrary", "arbitrary")),
        name="s5_mixer",
    )(u.reshape(t, BRANCH_W), perm, permt, ec, ftc, c0, d, a)
    return y


def _s5_tables(a_re, a_im, log_dt, b_re, b_im, c_re, c_im):
    nl = a_re.shape[0]
    g, p_, n, lc = S5_GROUPS, S5_STATE, S5_GROUP, S5_CHUNK
    a_re, a_im, b_re, b_im, c_re, c_im = (v.astype(F32) for v in (a_re, a_im, b_re, b_im, c_re, c_im))
    dt = jnp.exp(log_dt.astype(F32))[..., None]
    adt_r = (a_re * dt)[:, None, :, None, :]
    adt_i = (a_im * dt)[:, None, :, None, :]

    def powers(steps):
        st = jnp.asarray(np.asarray(steps, np.float32).reshape(1, -1, 1, 1, 1))
        mag = jnp.exp(adt_r * st)
        return mag * jnp.cos(adt_i * st), mag * jnp.sin(adt_i * st)

    lbr, lbi = powers([1])
    are, aim = a_re[:, None, :, None, :], a_im[:, None, :, None, :]
    den = are * are + aim * aim
    qr = ((lbr - 1.0) * are + lbi * aim) / den
    qi = (lbi * are - (lbr - 1.0) * aim) / den
    bt_re, bt_im = b_re.transpose(0, 1, 3, 2)[:, None], b_im.transpose(0, 1, 3, 2)[:, None]
    bbr = qr * bt_re - qi * bt_im
    bbi = qr * bt_im + qi * bt_re
    cr, ci = c_re[:, None], c_im[:, None]
    pr, pi = powers(np.arange(lc - 1, -1, -1))
    er, ei = pr * bbr - pi * bbi, pr * bbi + pi * bbr
    ec = jnp.concatenate([er, er, ei, ei], axis=-1).reshape(nl, lc, g * n, 4 * p_)
    pfr, pfi = powers(np.arange(1, lc + 1))
    fr, fi = cr * pfr - ci * pfi, -(cr * pfi + ci * pfr)
    ftc = jnp.concatenate([fr, fr, fi, fi], axis=-1).reshape(nl, lc, g * n, 4 * p_)
    c0 = jnp.concatenate([c_re, c_re, -c_im, -c_im], axis=-1).reshape(nl, g * n, 4 * p_)
    nh = BRANCH_W // S5_HALF
    ar, ai = powers([lc])
    a_tab = jnp.stack([ar.reshape(nl, nh, S5_HALF_STATE), ai.reshape(nl, nh, S5_HALF_STATE)], axis=2)
    return ec, ftc, c0, a_tab


def _merge_kernel(x_ref, yg_ref, y5_ref, ym_ref, wgate_ref, bgate_ref, wglu_ref, bglu_ref,
                  wup_ref, wo_ref, g_ref, b_ref, o_ref):
    x = x_ref[...]
    xb = x.astype(BF16)
    y5 = _gelu_tanh(y5_ref[...].astype(F32))
    y5 = y5 * _sigmoid(_dot(y5.astype(BF16), wglu_ref[...]) + bglu_ref[...])
    ys = (yg_ref[...], y5.astype(BF16), ym_ref[...])
    acc = None
    for r in range(N_BRANCH):
        gate = _sigmoid(_dot(xb, wgate_ref[:, r * D_MODEL:(r + 1) * D_MODEL])
                        + bgate_ref[:, r * D_MODEL:(r + 1) * D_MODEL])
        term = gate * _dot(ys[r], wup_ref[r])
        acc = term if acc is None else acc + term
    mix = _dot(acc.astype(BF16), wo_ref[...])
    o_ref[...] = _layer_norm(DN_ALPHA * x + mix, g_ref[...], b_ref[...])


def _merge(x2, yg, y5, ym, wgate, bgate, wglu, bglu, wup, wo, g, b, tm=512):
    t, dm = x2.shape
    row = lambda w: pl.BlockSpec((tm, w), lambda i: (i, 0))
    return pl.pallas_call(
        _merge_kernel,
        grid=(t // tm,),
        in_specs=[row(dm), row(BRANCH_W), row(BRANCH_W), row(BRANCH_W),
                  _full(wgate.shape), _full(bgate.shape), _full(wglu.shape), _full(bglu.shape),
                  _full(wup.shape), _full(wo.shape), _full(g.shape), _full(b.shape)],
        out_specs=row(dm),
        out_shape=jax.ShapeDtypeStruct((t, dm), F32),
        compiler_params=_cparams(("parallel",)),
        name="merge_ln1",
    )(x2, yg, y5, ym, wgate, bgate, wglu, bglu, wup, wo, g, b)


def _ple_ln2(x, xb, f, p_ref, pwg_ref, pwp_ref, g_ref, b_ref):
    e = _sigmoid(_dot(xb, pwg_ref[...])) * _dot(p_ref[...].astype(BF16), pwp_ref[...])
    return _layer_norm(DN_ALPHA * x + f + e, g_ref[...], b_ref[...])


def _ffn_kernel(x_ref, p_ref, wg_ref, wu_ref, wd_ref, pwg_ref, pwp_ref, g_ref, b_ref, o_ref):
    x = x_ref[...]
    xb = x.astype(BF16)
    hid = (_silu(_dot(xb, wg_ref[...])) * _dot(xb, wu_ref[...])).astype(BF16)
    f = _dot(hid, wd_ref[...])
    o_ref[...] = _ple_ln2(x, xb, f, p_ref, pwg_ref, pwp_ref, g_ref, b_ref)


def _ffn_layer(x2, p3, layer, wg, wu, wd, pwg, pwp, g, b, tm=256):
    t, dm = x2.shape
    row = lambda w: pl.BlockSpec((tm, w), lambda i: (i, 0))
    return pl.pallas_call(
        _ffn_kernel,
        grid=(t // tm,),
        in_specs=[row(dm), pl.BlockSpec((None, tm, PLE_DIM), lambda i: (layer, i, 0)),
                  _full(wg.shape), _full(wu.shape), _full(wd.shape),
                  _full(pwg.shape), _full(pwp.shape), _full(g.shape), _full(b.shape)],
        out_specs=row(dm),
        out_shape=jax.ShapeDtypeStruct((t, dm), F32),
        compiler_params=_cparams(("parallel",)),
        name="ffn_ple_ln2",
    )(x2, p3, wg, wu, wd, pwg, pwp, g, b)


def _router_kernel(x_ref, w_ref, b_ref, tril_ref, sel_ref, pr_ref, rk_ref, cnt_ref, base_ref):
    @pl.when(pl.program_id(0) == 0)
    def _():
        base_ref[...] = jnp.zeros_like(base_ref)

    xh, xl = _split2(x_ref[...])
    logits = _dot(xh, w_ref[0]) + _dot(xl, w_ref[0]) + _dot(xh, w_ref[1]) + b_ref[...]
    lane = lax.broadcasted_iota(jnp.int32, logits.shape, 1)
    neg = -jnp.inf
    logits = jnp.where(lane < N_EXPERTS, logits, neg)
    m1 = jnp.max(logits, axis=-1, keepdims=True)
    i1 = jnp.min(jnp.where(logits == m1, lane, LANES), axis=-1, keepdims=True)
    rest = jnp.where(lane == i1, neg, logits)
    m2 = jnp.max(rest, axis=-1, keepdims=True)
    i2 = jnp.min(jnp.where(rest == m2, lane, LANES), axis=-1, keepdims=True)
    e2 = jnp.exp(m2 - m1)
    p1 = 1.0 / (1.0 + e2)
    p2 = e2 / (1.0 + e2)
    hot = jnp.where((lane == i1) | (lane == i2), 1.0, 0.0)
    base = base_ref[0:1, :]
    before = _dot(tril_ref[...], hot.astype(BF16)) + base
    r1 = jnp.sum(jnp.where(lane == i1, before, 0.0), axis=-1, keepdims=True)
    r2 = jnp.sum(jnp.where(lane == i2, before, 0.0), axis=-1, keepdims=True)
    sel_ref[...] = jnp.where(lane == 0, i1, jnp.where(lane == 1, i2, 0))
    pr_ref[...] = jnp.where(lane == 0, p1, jnp.where(lane == 1, p2, 0.0))
    rk_ref[...] = jnp.where(lane == 0, r1, jnp.where(lane == 1, r2, 0.0))
    total = base + jnp.sum(hot, axis=0, keepdims=True)
    base_ref[...] = jnp.broadcast_to(total, base_ref.shape)
    cnt_ref[...] = jnp.broadcast_to(total, cnt_ref.shape)


def _router(x2, w, b, tm=512):
    t, dm = x2.shape
    i = np.arange(tm)
    tril = jnp.asarray((i[:, None] > i[None, :]).astype(np.float32), BF16)
    blk = pl.BlockSpec((tm, LANES), lambda i: (i, 0))
    return pl.pallas_call(
        _router_kernel,
        grid=(t // tm,),
        in_specs=[pl.BlockSpec((tm, dm), lambda i: (i, 0)), _full(w.shape), _full(b.shape), _full((tm, tm))],
        out_specs=[blk, blk, blk, pl.BlockSpec((SUBLANES, LANES), lambda i: (0, 0))],
        out_shape=[jax.ShapeDtypeStruct((t, LANES), jnp.int32), jax.ShapeDtypeStruct((t, LANES), F32),
                   jax.ShapeDtypeStruct((t, LANES), F32), jax.ShapeDtypeStruct((SUBLANES, LANES), F32)],
        scratch_shapes=[pltpu.VMEM((SUBLANES, LANES), F32)],
        compiler_params=_cparams(("arbitrary",)),
        name="moe_router",
    )(x2, w, b, tril)


TOP_K = 2
MOE_TILE = 1024
MOE_SUB = 256
MOE_FF_TILE = 512
DISPATCH_TILE = 512
DMA_ISSUE_UNROLL = 8


ROW_SUB = D_MODEL // LANES


def _row_copy_wait(src_rows, dst_rows, sem):
    pltpu.make_async_copy(src_rows, dst_rows, sem).wait()


def _rows_to_tiles(dst_ref, val):
    n = val.shape[0]
    for j in range(ROW_SUB):
        dst_ref[pl.ds(j, n, stride=ROW_SUB), :] = val[:, j * LANES:(j + 1) * LANES]


def _tiles_to_rows(src_ref, n):
    return jnp.concatenate([src_ref[pl.ds(j, n, stride=ROW_SUB), :] for j in range(ROW_SUB)], axis=1)


def _row_tile(ref, r):
    return ref.at[pl.ds(pl.multiple_of(r * ROW_SUB, ROW_SUB), ROW_SUB), :]


def _dispatch_kernel(pos_ref, x_ref, xs_ref, stage_ref, sem):
    tm = x_ref.shape[0]
    base = pl.program_id(0) * tm * TOP_K
    _rows_to_tiles(stage_ref, x_ref[...])

    def body(r, carry):
        for k in range(TOP_K):
            dst = pos_ref[base + r * TOP_K + k]
            pltpu.make_async_copy(_row_tile(stage_ref, r), _row_tile(xs_ref, dst), sem).start(priority=k)
        return carry

    lax.fori_loop(0, tm, body, 0, unroll=DMA_ISSUE_UNROLL)
    for _ in range(TOP_K):
        _row_copy_wait(stage_ref, xs_ref.at[pl.ds(0, tm * ROW_SUB), :], sem)


def _dispatch(pos_flat, x2, tm=DISPATCH_TILE):
    t, dm = x2.shape
    return pl.pallas_call(
        _dispatch_kernel,
        grid_spec=pltpu.PrefetchScalarGridSpec(
            num_scalar_prefetch=1,
            grid=(t // tm,),
            in_specs=[pl.BlockSpec((tm, dm), lambda i, pos: (i, 0))],
            out_specs=pl.BlockSpec(memory_space=pl.ANY),
            scratch_shapes=[pltpu.VMEM((tm * ROW_SUB, LANES), F32), pltpu.SemaphoreType.DMA],
        ),
        out_shape=jax.ShapeDtypeStruct((t * TOP_K * ROW_SUB, LANES), F32),
        compiler_params=_cparams(("arbitrary",)),
        name="moe_dispatch",
    )(pos_flat, x2)


def _moe_group_kernel(tile_ref, exp_ref, lo_ref, hi_ref, xs_ref, wg_ref, wu_ref, wd_ref, o_ref,
                      acc_ref, xb_ref, wgb_ref, wub_ref, wdb_ref):
    i = pl.program_id(0)
    f = pl.program_id(1)
    lo = lo_ref[i]
    hi = hi_ref[i]
    tm = acc_ref.shape[0]

    @pl.when(hi > lo)
    def _():
        @pl.when((lo == 0) & (f == 0))
        def _():
            acc_ref[...] = jnp.zeros_like(acc_ref)

        @pl.when(f == 0)
        def _():
            xb_ref[...] = _tiles_to_rows(xs_ref, tm).astype(BF16)

        whole = (lo == 0) & (hi == tm)

        @pl.when(whole)
        def _():
            xb = xb_ref[...]
            hid = (_silu(_dot(xb, wg_ref[0].astype(BF16))) * _dot(xb, wu_ref[0].astype(BF16))).astype(BF16)
            acc_ref[...] += _dot(hid, wd_ref[0].astype(BF16))

        @pl.when(jnp.logical_not(whole))
        def _():
            wgb_ref[...] = wg_ref[0].astype(BF16)
            wub_ref[...] = wu_ref[0].astype(BF16)
            wdb_ref[...] = wd_ref[0].astype(BF16)
            for sub in range(tm // MOE_SUB):
                @pl.when((hi > sub * MOE_SUB) & (lo < (sub + 1) * MOE_SUB))
                def _(sub=sub):
                    rsl = slice(sub * MOE_SUB, (sub + 1) * MOE_SUB)
                    xb = xb_ref[rsl, :]
                    hid = _silu(_dot(xb, wgb_ref[...])) * _dot(xb, wub_ref[...])
                    rid = lax.broadcasted_iota(jnp.int32, (MOE_SUB, 1), 0) + sub * MOE_SUB
                    hid = jnp.where((rid >= lo) & (rid < hi), hid, 0.0).astype(BF16)
                    acc_ref[rsl, :] += _dot(hid, wdb_ref[...])

        @pl.when((hi == tm) & (f == pl.num_programs(1) - 1))
        def _():
            _rows_to_tiles(o_ref, acc_ref[...])


def _moe_grouped(items, xs, wg, wu, wd, tm=MOE_TILE, tf=MOE_FF_TILE):
    tile, exp, lo, hi = items
    dm = wg.shape[1]
    dff = wg.shape[2]
    return pl.pallas_call(
        _moe_group_kernel,
        grid_spec=pltpu.PrefetchScalarGridSpec(
            num_scalar_prefetch=4,
            grid=(tile.shape[0], dff // tf),
            in_specs=[pl.BlockSpec((tm * ROW_SUB, LANES), lambda i, f, tl, ex, lo_, hi_: (tl[i], 0)),
                      pl.BlockSpec((1, dm, tf), lambda i, f, tl, ex, lo_, hi_: (ex[i], 0, f)),
                      pl.BlockSpec((1, dm, tf), lambda i, f, tl, ex, lo_, hi_: (ex[i], 0, f)),
                      pl.BlockSpec((1, tf, dm), lambda i, f, tl, ex, lo_, hi_: (ex[i], f, 0))],
            out_specs=pl.BlockSpec((tm * ROW_SUB, LANES), lambda i, f, tl, ex, lo_, hi_: (tl[i], 0)),
            scratch_shapes=[pltpu.VMEM((tm, dm), F32), pltpu.VMEM((tm, dm), BF16), pltpu.VMEM((dm, tf), BF16),
                            pltpu.VMEM((dm, tf), BF16), pltpu.VMEM((tf, dm), BF16)],
        ),
        out_shape=jax.ShapeDtypeStruct(xs.shape, F32),
        compiler_params=_cparams(("arbitrary", "arbitrary")),
        name="moe_grouped",
    )(tile, exp, lo, hi, xs, wg, wu, wd)


def _moe_items(counts, n_rows, tm=MOE_TILE):
    n_tiles = n_rows // tm
    ends = jnp.cumsum(counts)
    cuts = jnp.sort(jnp.concatenate([jnp.arange(n_tiles + 1, dtype=jnp.int32) * tm, ends[:-1]]))
    start, stop = cuts[:-1], cuts[1:]
    tile = jnp.minimum(start // tm, n_tiles - 1)
    exp = jnp.minimum(jnp.sum(ends[None, :] <= start[:, None], axis=1), N_EXPERTS - 1)
    lo = start - tile * tm
    hi = stop - tile * tm
    return tile.astype(jnp.int32), exp.astype(jnp.int32), lo.astype(jnp.int32), hi.astype(jnp.int32)


def _combine_kernel(pos_ref, x_ref, pr_ref, p_ref, pwg_ref, pwp_ref, g_ref, b_ref, ys_ref, o_ref, gat_ref, sem):
    tm = x_ref.shape[0]
    base = pl.program_id(0) * tm * TOP_K

    def body(r, carry):
        for k in range(TOP_K):
            src = pos_ref[base + r * TOP_K + k]
            pltpu.make_async_copy(_row_tile(ys_ref, src), _row_tile(gat_ref.at[k], r), sem).start(priority=k)
        return carry

    lax.fori_loop(0, tm, body, 0, unroll=DMA_ISSUE_UNROLL)
    x = x_ref[...]
    xb = x.astype(BF16)
    e = _sigmoid(_dot(xb, pwg_ref[...])) * _dot(p_ref[...].astype(BF16), pwp_ref[...])
    for k in range(TOP_K):
        _row_copy_wait(ys_ref.at[pl.ds(0, tm * ROW_SUB), :], gat_ref.at[k], sem)
    pr = pr_ref[...]
    f = pr[:, 0:1] * _tiles_to_rows(gat_ref.at[0], tm)
    for k in range(1, TOP_K):
        f = f + pr[:, k:k + 1] * _tiles_to_rows(gat_ref.at[k], tm)
    o_ref[...] = _layer_norm(DN_ALPHA * x + f + e, g_ref[...], b_ref[...])


def _combine_layer(pos_flat, x2, pr, p3, layer, pwg, pwp, g, b, ys, tm=DISPATCH_TILE):
    t, dm = x2.shape
    row = lambda w: pl.BlockSpec((tm, w), lambda i, pos: (i, 0))
    full = lambda a: pl.BlockSpec(a.shape, lambda i, pos: (0,) * a.ndim, pipeline_mode=pl.Buffered(1))
    return pl.pallas_call(
        _combine_kernel,
        grid_spec=pltpu.PrefetchScalarGridSpec(
            num_scalar_prefetch=1,
            grid=(t // tm,),
            in_specs=[row(dm), row(LANES), pl.BlockSpec((None, tm, PLE_DIM), lambda i, pos: (layer, i, 0)),
                      full(pwg), full(pwp), full(g), full(b),
                      pl.BlockSpec(memory_space=pl.ANY)],
            out_specs=row(dm),
            scratch_shapes=[pltpu.VMEM((TOP_K, tm * ROW_SUB, LANES), F32), pltpu.SemaphoreType.DMA],
        ),
        out_shape=jax.ShapeDtypeStruct((t, dm), F32),
        compiler_params=_cparams(("arbitrary",)),
        name="moe_combine_ple_ln2",
    )(pos_flat, x2, pr, p3, pwg, pwp, g, b, ys)


def _row(v):
    return v.reshape(1, -1).astype(F32)


def _pad_lanes(w):
    return jnp.pad(w, ((0, 0), (0, LANES - w.shape[1])))


def kernel(x, p, w_in, b_in, gla_w_a2, gla_b_a2, gla_norm_g, s5_a_re, s5_a_im, s5_log_dt, s5_b_re, s5_b_im,
           s5_c_re, s5_c_im, s5_d, s5_w_glu, s5_b_glu, ml_conv_w, ml_conv_b, ml_norm_g, w_up, w_o, ln1_g, ln1_b,
           ffn_wg, ffn_wu, ffn_wd, moe_router, moe_router_b, moe_wg, moe_wu, moe_wd, ple_w_gate, ple_w_proj,
           ln2_g, ln2_b):
    bsz, s, dm = x.shape
    t = bsz * s
    hi = lax.Precision.HIGHEST
    o = IN_OFF
    tri = jnp.asarray(_chunk_tri(SEQ_BLOCK, CHUNK), BF16)
    ml_tri_np = _chunk_tri(SEQ_BLOCK, ML_CHUNK)
    ml_tri = jnp.asarray(ml_tri_np, BF16)
    ml_trit = jnp.asarray(ml_tri_np.T, BF16)
    s5_ec, s5_ftc, s5_c0, s5_atab = _s5_tables(s5_a_re, s5_a_im, s5_log_dt, s5_b_re, s5_b_im, s5_c_re, s5_c_im)
    p3 = p.reshape(DEPTH, t, PLE_DIM)

    w_in_b = w_in.astype(BF16)
    for i in range(DEPTH):
        w, b = w_in_b[i], b_in[i]
        sl = lambda k: (w[:, o[k]:o[k + 1]], b[o[k]:o[k + 1]])
        (wq, bq), (wk, bk), (wv, bv), (_, ba), (wg_, bg_) = sl(0), sl(1), sl(2), sl(3), sl(4)
        wz = jnp.dot(w_in[i, :, o[3]:o[4]], gla_w_a2[i], precision=hi).astype(BF16)
        bz = jnp.dot(ba, gla_w_a2[i], precision=hi) + gla_b_a2[i]
        (wu_, bu_) = sl(5)
        w_gla = jnp.concatenate([wq, wk, wz, wv, wg_, wu_], axis=1)
        b_gla = _row(jnp.concatenate([bq, bk, bz, bv, bg_, bu_]))
        y_gla, u = _gla_mixer(x, w_gla, b_gla, _row(gla_norm_g[i]), tri)

        x2 = x.reshape(t, dm)
        y5 = _s5_mixer(u, s5_ec[i], s5_ftc[i], s5_c0[i], _row(s5_d[i]), s5_atab[i])

        (wmq, bmq), (wmk, bmk), (wmv, bmv), (wmo, bmo) = sl(6), sl(7), sl(8), sl(11)
        w_ml = jnp.concatenate([wmq, wmk, wmv, wmo], axis=1)
        b_ml = _row(jnp.concatenate([bmq, bmk, bmv, bmo]))
        w_if = w_in[i, :, o[9]:o[11]]
        b_if = b[o[9]:o[11]]
        if_h = w_if.astype(BF16)
        if_l = (w_if - if_h.astype(F32)).astype(BF16)
        wgc = jnp.stack([_pad_lanes(if_h), _pad_lanes(if_l)])
        wgr = jnp.stack([if_h.T, if_l.T])
        y_ml = _ml_mixer(x, w_ml, b_ml, wgc, _pad_lanes(_row(b_if)), wgr, b_if.reshape(-1, 1).astype(F32),
                         ml_conv_w[i].astype(F32), _row(ml_conv_b[i]), _row(ml_norm_g[i]), ml_tri, ml_trit)

        (wgt, bgt) = sl(12)
        x1 = _merge(x2, y_gla.reshape(t, BRANCH_W), y5, y_ml.reshape(t, BRANCH_W),
                    wgt, _row(bgt), s5_w_glu[i].astype(BF16), _row(s5_b_glu[i]),
                    w_up[i].astype(BF16), w_o[i].astype(BF16), _row(ln1_g[i]), _row(ln1_b[i]))

        pwg = ple_w_gate[i].astype(BF16)
        pwp = ple_w_proj[i].astype(BF16)
        j = i // 2
        if i % 2 == 0:
            x2n = _ffn_layer(x1, p3, i, ffn_wg[j].astype(BF16), ffn_wu[j].astype(BF16), ffn_wd[j].astype(BF16),
                             pwg, pwp, _row(ln2_g[i]), _row(ln2_b[i]))
        else:
            wr = moe_router[j]
            wr_h = wr.astype(BF16)
            wr_l = (wr - wr_h.astype(F32)).astype(BF16)
            sel, pr, rk, cnt = _router(x1, jnp.stack([_pad_lanes(wr_h), _pad_lanes(wr_l)]),
                                       _pad_lanes(_row(moe_router_b[j])))
            counts = cnt[0, :N_EXPERTS].astype(jnp.int32)
            starts = jnp.cumsum(counts) - counts
            sel2 = sel[:, :TOP_K]
            pos = rk[:, :TOP_K].astype(jnp.int32) + jnp.sum(
                jnp.where(sel2[..., None] == jnp.arange(N_EXPERTS), starts, 0), axis=-1)
            pos_flat = pos.reshape(-1)
            xs = _dispatch(pos_flat, x1)
            ys = _moe_grouped(_moe_items(counts, t * TOP_K), xs, moe_wg[j], moe_wu[j], moe_wd[j])
            x2n = _combine_layer(pos_flat, x1, pr, p3, i, pwg, pwp, _row(ln2_g[i]), _row(ln2_b[i]), ys)
        x = x2n.reshape(bsz, s, dm)
    return x
```

```python
import functools
import math

import numpy as np
import jax
import jax.numpy as jnp
from jax import lax
from jax.experimental import pallas as pl
from jax.experimental.pallas import tpu as pltpu

F32 = jnp.float32
BF16 = jnp.bfloat16

D_MODEL = 1024
DEPTH = 2
N_BRANCH = 3
BRANCH_W = 512
HEADS = 4
DK = 64
DV = BRANCH_W // HEADS
GLA_RANK = 16
GLA_TAU = 16.0
CHUNK = 64
ML_CHUNK = 256
S5_GROUP = 16
S5_GROUPS = BRANCH_W // S5_GROUP
S5_STATE = 64
S5_CHUNK = 16
ML_CONV = 4
N_EXPERTS = 8
PLE_DIM = 256
DN_ALPHA = (2.0 * DEPTH) ** 0.25
LN_EPS = 1e-5

IN_WIDTHS = (
    HEADS * DK, HEADS * DK, BRANCH_W, GLA_RANK, BRANCH_W,
    BRANCH_W,
    HEADS * DK, HEADS * DK, BRANCH_W, HEADS, HEADS, BRANCH_W,
    N_BRANCH * D_MODEL,
)
IN_OFF = tuple(int(o) for o in np.concatenate([[0], np.cumsum(IN_WIDTHS)]))

LANES = 128
SUBLANES = 8
SEQ_BLOCK = 256
GLA_SEQ_PER_STEP = 4
ML_SEQ_PER_STEP = 1
PAIR_W = 2 * DK
VMEM_LIMIT = 56 * 1024 * 1024


def _cparams(sem):
    return pltpu.CompilerParams(dimension_semantics=sem, vmem_limit_bytes=VMEM_LIMIT)


def _dot(a, b):
    return jnp.dot(a, b, preferred_element_type=F32)


def _dot_nt(a, b):
    return lax.dot_general(a, b, (((1,), (1,)), ((), ())), preferred_element_type=F32)


def _dot_tn(a, b):
    return lax.dot_general(a, b, (((0,), (0,)), ((), ())), preferred_element_type=F32)


def _split3(a):
    hi = a.astype(BF16)
    r = a - hi.astype(F32)
    mid = r.astype(BF16)
    lo = (r - mid.astype(F32)).astype(BF16)
    return hi, mid, lo


def _split2(a):
    hi = a.astype(BF16)
    lo = (a - hi.astype(F32)).astype(BF16)
    return hi, lo


def _log_sigmoid(x):
    return jnp.minimum(x, 0.0) - jnp.log(1.0 + jnp.exp(-jnp.abs(x)))


def _sigmoid(x):
    return 0.5 * jnp.tanh(0.5 * x) + 0.5


def _silu(x):
    return x * _sigmoid(x)


def _gelu_tanh(x):
    return 0.5 * x * (1.0 + jnp.tanh(math.sqrt(2.0 / math.pi) * (x + 0.044715 * (x * x * x))))


def _layer_norm(v, g, b):
    mu = jnp.mean(v, axis=-1, keepdims=True)
    c = v - mu
    var = jnp.mean(c * c, axis=-1, keepdims=True)
    return c * lax.rsqrt(var + LN_EPS) * g + b


def _head_norm(o):
    mu = jnp.mean(o, axis=-1, keepdims=True)
    c = o - mu
    var = jnp.mean(c * c, axis=-1, keepdims=True)
    return c * lax.rsqrt(var + LN_EPS)


def _chunk_tri(n, chunk):
    i = np.arange(n)
    return ((i[:, None] >= i[None, :]) & (i[:, None] // chunk == i[None, :] // chunk)).astype(np.float32)


def _full(shape):
    nd = len(shape)
    return pl.BlockSpec(shape, lambda *_: (0,) * nd, pipeline_mode=pl.Buffered(1))


def _gla_kernel(x_ref, w_ref, b_ref, ng_ref, tri_ref, y_ref, u_ref, st_ref, o_ref):
    @pl.when(pl.program_id(1) == 0)
    def _():
        st_ref[...] = jnp.zeros_like(st_ref)

    hk = HEADS * DK
    nb, lb, d = x_ref.shape
    xb = x_ref[...].reshape(nb * lb, d).astype(BF16)
    h = _dot(xb, w_ref[...]) + b_ref[...]
    q = h[:, 0:hk]
    k = h[:, hk:2 * hk]
    z = h[:, 2 * hk:3 * hk]
    v = h[:, 3 * hk:3 * hk + BRANCH_W]
    g = h[:, 3 * hk + BRANCH_W:3 * hk + 2 * BRANCH_W]
    u_ref[...] = h[:, 3 * hk + 2 * BRANCH_W:3 * hk + 3 * BRANCH_W].reshape(nb, lb, BRANCH_W).astype(u_ref.dtype)

    la = _log_sigmoid(z) * (1.0 / GLA_TAU)
    tri = tri_ref[...]
    la_h, la_m, la_l = _split3(la)
    cum = jnp.concatenate(
        [_dot(tri, la_h[r:r + lb]) + _dot(tri, la_m[r:r + lb]) + _dot(tri, la_l[r:r + lb])
         for r in range(0, nb * lb, lb)], axis=0)
    qd = q * (DK ** -0.5) * jnp.exp(cum)
    ki = k * jnp.exp(-cum)

    lane = lax.broadcasted_iota(jnp.int32, (1, PAIR_W), 1)
    row_i = lax.broadcasted_iota(jnp.int32, (CHUNK, 2 * CHUNK), 0)
    col_i = lax.broadcasted_iota(jnp.int32, (CHUNK, 2 * CHUNK), 1)
    causal = row_i >= col_i % CHUNK
    bd_r = lax.broadcasted_iota(jnp.int32, (2 * DV, PAIR_W), 0)
    bd_c = lax.broadcasted_iota(jnp.int32, (2 * DV, PAIR_W), 1)
    blockdiag = (bd_r >= DV) == (bd_c >= DK)
    vb_r = lax.broadcasted_iota(jnp.int32, (2 * CHUNK, 2 * DV), 0)
    vb_c = lax.broadcasted_iota(jnp.int32, (2 * CHUNK, 2 * DV), 1)
    v_blocks = (vb_r >= CHUNK) == (vb_c >= DV)

    for c in range(lb // CHUNK):
        for bb, p in [(bb, p) for bb in range(nb) for p in range(HEADS // 2)]:
            st = st_ref[bb, p]
            lsl = slice(p * PAIR_W, (p + 1) * PAIR_W)
            rsl = slice(bb * lb + c * CHUNK, bb * lb + (c + 1) * CHUNK)
            qd_c = qd[rsl, lsl]
            ki_c = ki[rsl, lsl]
            cum_c = cum[rsl, lsl]
            last = cum_c[CHUNK - 1:CHUNK, :]
            kt = (k[rsl, lsl] * jnp.exp(last - cum_c)).astype(BF16)
            qd_b = qd_c.astype(BF16)
            inter = _dot_nt(qd_b, st.astype(BF16))
            k_bd = jnp.concatenate([jnp.where(lane < DK, ki_c, 0.0), jnp.where(lane >= DK, ki_c, 0.0)],
                                   axis=0).astype(BF16)
            att = jnp.where(causal, _dot_nt(qd_b, k_bd), 0.0)
            v_c = v[rsl, p * 2 * DV:(p + 1) * 2 * DV]
            v_bd = jnp.where(v_blocks, jnp.concatenate([v_c, v_c], axis=0), 0.0).astype(BF16)
            o_ref[rsl, p * 2 * DV:(p + 1) * 2 * DV] = _dot(att.astype(BF16), v_bd) + inter
            upd = _dot_tn(v_c.astype(BF16), kt)
            st_ref[bb, p] = st * jnp.exp(last) + jnp.where(blockdiag, upd, 0.0)

    ng = ng_ref[...]
    for head in range(HEADS):
        hsl = slice(head * DV, (head + 1) * DV)
        y = _head_norm(o_ref[:, hsl]) * ng[:, hsl] * _silu(g[:, hsl])
        y_ref[:, :, hsl] = y.reshape(nb, lb, DV).astype(y_ref.dtype)


def _gla_mixer(x, w, b, ng, tri):
    bsz, s, d = x.shape
    wcols = w.shape[1]
    nb = GLA_SEQ_PER_STEP
    return pl.pallas_call(
        _gla_kernel,
        grid=(bsz // nb, s // SEQ_BLOCK),
        in_specs=[
            pl.BlockSpec((nb, SEQ_BLOCK, d), lambda i, j: (i, j, 0)),
            _full((d, wcols)), _full((1, wcols)), _full((1, BRANCH_W)),
            _full((SEQ_BLOCK, SEQ_BLOCK)),
        ],
        out_specs=[pl.BlockSpec((nb, SEQ_BLOCK, BRANCH_W), lambda i, j: (i, j, 0))] * 2,
        out_shape=[jax.ShapeDtypeStruct((bsz, s, BRANCH_W), BF16)] * 2,
        scratch_shapes=[pltpu.VMEM((nb, HEADS // 2, 2 * DV, PAIR_W), F32),
                        pltpu.VMEM((nb * SEQ_BLOCK, BRANCH_W), F32)],
        compiler_params=_cparams(("arbitrary", "arbitrary")),
        name="gla_mixer",
    )(x, w, b, ng, tri)


ML_ST_ROWS = 2 * DV + LANES


def _ml_kernel(x_ref, w_ref, b_ref, wgc_ref, bgc_ref, wgr_ref, bgr_ref, cw_ref, cb_ref, ng_ref,
               tri_ref, trit_ref, y_ref, ct_ref, m_ref, carry_ref, o_ref):
    @pl.when(pl.program_id(1) == 0)
    def _():
        ct_ref[...] = jnp.zeros_like(ct_ref)
        m_ref[...] = jnp.zeros_like(m_ref)
        carry_ref[...] = jnp.zeros_like(carry_ref)

    hk = HEADS * DK
    nb, lb, d = x_ref.shape
    x32 = x_ref[...].reshape(nb * lb, d)
    xh, xl = _split2(x32)
    h = _dot(xh, w_ref[...]) + b_ref[...]
    qk = h[:, 0:2 * hk]
    v = h[:, 2 * hk:2 * hk + BRANCH_W]
    o_pre = h[:, 2 * hk + BRANCH_W:2 * hk + 2 * BRANCH_W]

    cw = cw_ref[...]
    conv = []
    for bb in range(nb):
        qk_b = qk[bb * lb:(bb + 1) * lb]
        ext = jnp.concatenate([carry_ref[bb], qk_b], axis=0)
        first = SUBLANES - (ML_CONV - 1)
        acc = cb_ref[...] + ext[first:first + lb] * cw[0:1]
        for j in range(1, ML_CONV):
            acc = acc + ext[first + j:first + j + lb] * cw[j:j + 1]
        carry_ref[bb] = qk_b[lb - SUBLANES:lb]
        conv.append(acc)
    qkc = _silu(jnp.concatenate(conv, axis=0))
    qf = qkc[:, 0:hk]
    kf = qkc[:, hk:2 * hk] * (DK ** -0.5)
    vb = v.astype(BF16)

    gc = (_dot(xh, wgc_ref[0]) + _dot(xl, wgc_ref[0]) + _dot(xh, wgc_ref[1])) + bgc_ref[...]
    gr = (_dot_nt(wgr_ref[0], xh) + _dot_nt(wgr_ref[0], xl) + _dot_nt(wgr_ref[1], xh)) + bgr_ref[...]
    lf_c = _log_sigmoid(gc)
    lf_r = _log_sigmoid(gr)
    tri = tri_ref[...]
    trit = trit_ref[...]
    c_h, c_m, c_l = _split3(lf_c)
    r_h, r_m, r_l = _split3(lf_r)
    blocks = [slice(r, r + lb) for r in range(0, nb * lb, lb)]
    bc = jnp.concatenate([_dot(tri, c_h[r]) + _dot(tri, c_m[r]) + _dot(tri, c_l[r]) for r in blocks],
                         axis=0)
    br = jnp.concatenate([_dot(r_h[:, r], trit) + _dot(r_m[:, r], trit) + _dot(r_l[:, r], trit)
                          for r in blocks], axis=1)

    lane = lax.broadcasted_iota(jnp.int32, (1, PAIR_W), 1)
    row_i = lax.broadcasted_iota(jnp.int32, (ML_CHUNK, ML_CHUNK), 0)
    col_i = lax.broadcasted_iota(jnp.int32, (ML_CHUNK, ML_CHUNK), 1)
    causal = row_i >= col_i
    sr = lax.broadcasted_iota(jnp.int32, (ML_ST_ROWS, PAIR_W), 0)
    sc_ = lax.broadcasted_iota(jnp.int32, (ML_ST_ROWS, PAIR_W), 1)
    first = sc_ < DK
    rows_h0 = (sr < DV) | (sr == 2 * DV)
    rows_h1 = ((sr >= DV) & (sr < 2 * DV)) | (sr == 2 * DV + 1)
    st_mask = (rows_h0 & first) | (rows_h1 & ~first)
    ones_blk = jnp.ones((ML_CHUNK, LANES), BF16)

    for bb, p in [(bb, p) for bb in range(nb) for p in range(HEADS // 2)]:
        ct = ct_ref[bb, p]
        lsl = slice(p * PAIR_W, (p + 1) * PAIR_W)
        m_pair = [m_ref[bb, 2 * p + hh][0:1, 0:1] for hh in range(2)]
        for c in range(lb // ML_CHUNK):
            rsl = slice(bb * lb + c * ML_CHUNK, bb * lb + (c + 1) * ML_CHUNK)
            q_c = qf[rsl, lsl]
            k_c = kf[rsl, lsl]
            k_cb = k_c.astype(BF16)
            inter_mm = _dot_nt(q_c.astype(BF16), ct.astype(BF16))
            wt_cols, decays = [], []
            for hh in range(2):
                head = 2 * p + hh
                m_st = m_pair[hh]
                b_col = bc[rsl, HEADS + head:HEADS + head + 1]
                i_col = gc[rsl, head:head + 1]
                b_row = br[HEADS + head:HEADS + head + 1, rsl]
                i_row = gr[head:head + 1, rsl]
                dmat = jnp.where(causal, b_col - b_row + i_row, -jnp.inf)
                inter = b_col + m_st
                m_row = jnp.maximum(inter, jnp.max(dmat, axis=-1, keepdims=True))
                wts = jnp.exp(dmat - m_row)
                in_head = (lane >= hh * DK) & (lane < (hh + 1) * DK)
                qm = jnp.where(in_head, q_c, 0.0).astype(BF16)
                sc = _dot_nt(qm, k_cb) * wts
                w_inter = jnp.exp(inter - m_row)
                num = _dot(sc.astype(BF16), vb[rsl, head * DV:(head + 1) * DV]) \
                    + w_inter * inter_mm[:, hh * DV:(hh + 1) * DV]
                den = jnp.sum(sc, axis=-1, keepdims=True) \
                    + w_inter * inter_mm[:, 2 * DV + hh:2 * DV + hh + 1]
                o_ref[rsl, head * DV:(head + 1) * DV] = num / jnp.maximum(jnp.abs(den), jnp.exp(-m_row))
                g_tot = b_col[ML_CHUNK - 1:ML_CHUNK, :]
                tail = g_tot - b_col + i_col
                m_new = jnp.maximum(g_tot + m_st, jnp.max(tail, axis=0, keepdims=True))
                wt_cols.append(jnp.exp(tail - m_new))
                decays.append(jnp.exp(g_tot + m_st - m_new))
                m_pair[hh] = m_new
            wk = (k_c * jnp.where(lane < DK, wt_cols[0], wt_cols[1])).astype(BF16)
            vp = jnp.concatenate([vb[rsl, p * 2 * DV:(p + 1) * 2 * DV], ones_blk], axis=1)
            upd = _dot_tn(vp, wk)
            ct = ct * jnp.where(lane < DK, decays[0], decays[1]) + jnp.where(st_mask, upd, 0.0)
        ct_ref[bb, p] = ct
        for hh in range(2):
            m_ref[bb, 2 * p + hh] = jnp.broadcast_to(m_pair[hh], m_ref.shape[2:])

    ng = ng_ref[...]
    for head in range(HEADS):
        hsl = slice(head * DV, (head + 1) * DV)
        y = _head_norm(o_ref[:, hsl]) * ng[:, hsl] * _sigmoid(o_pre[:, hsl])
        y_ref[:, :, hsl] = y.reshape(nb, lb, DV).astype(y_ref.dtype)


def _ml_mixer(x, w, b, wgc, bgc, wgr, bgr, cw, cb, ng, tri, trit):
    bsz, s, d = x.shape
    wcols = w.shape[1]
    nb = ML_SEQ_PER_STEP
    return pl.pallas_call(
        _ml_kernel,
        grid=(bsz // nb, s // SEQ_BLOCK),
        in_specs=[
            pl.BlockSpec((nb, SEQ_BLOCK, d), lambda i, j: (i, j, 0)),
            _full((d, wcols)), _full((1, wcols)),
            _full((2, d, LANES)), _full((1, LANES)),
            _full((2, 2 * HEADS, d)), _full((2 * HEADS, 1)),
            _full((ML_CONV, 2 * HEADS * DK)), _full((1, 2 * HEADS * DK)),
            _full((1, BRANCH_W)),
            _full((SEQ_BLOCK, SEQ_BLOCK)), _full((SEQ_BLOCK, SEQ_BLOCK)),
        ],
        out_specs=pl.BlockSpec((nb, SEQ_BLOCK, BRANCH_W), lambda i, j: (i, j, 0)),
        out_shape=jax.ShapeDtypeStruct((bsz, s, BRANCH_W), BF16),
        scratch_shapes=[pltpu.VMEM((nb, HEADS // 2, ML_ST_ROWS, PAIR_W), F32),
                        pltpu.VMEM((nb, HEADS, SUBLANES, LANES), F32),
                        pltpu.VMEM((nb, SUBLANES, 2 * HEADS * DK), F32),
                        pltpu.VMEM((nb * SEQ_BLOCK, BRANCH_W), F32)],
        compiler_params=_cparams(("arbitrary", "arbitrary")),
        name="mlstm_mixer",
    )(x, w, b, wgc, bgc, wgr, bgr, cw, cb, ng, tri, trit)


S5_IO_STEPS = 4
S5_PERM_BLOCK = 256
S5_HALF = 256
S5_HALF_STATE = (S5_HALF // S5_GROUP) * S5_STATE


def _s5_kernel(bsz, nc, u_ref, perm_ref, permt_ref, ec_ref, ftc_ref, c0_ref, d_ref, a_ref, y_ref,
               in_ref, yall_ref, xs_ref, xsb_ref, kexp_ref, c0x_ref):
    s = pl.program_id(1)
    lc = S5_CHUNK
    rows = bsz * nc
    sw = S5_HALF_STATE
    pb = perm_ref.shape[0]
    cb = pb // lc
    n_io = S5_IO_STEPS
    subs = rows * lc // (n_io * pb)

    def expand_state(blk):
        tile = jnp.concatenate([blk[:, 0:LANES]] * (sw // LANES) + [blk[:, LANES:2 * LANES]] * (sw // LANES),
                               axis=1)
        r = lax.broadcasted_iota(jnp.int32, tile.shape, 0) // S5_GROUP
        c = (lax.broadcasted_iota(jnp.int32, tile.shape, 1) % sw) // S5_STATE
        return jnp.where(r == c, tile, 0.0).astype(BF16)

    @pl.when(s == 0)
    def _():
        c0x_ref[...] = expand_state(c0_ref[...])
        xs_ref[...] = jnp.zeros_like(xs_ref)

    @pl.when(s < n_io)
    def _():
        for sb in range(subs):
            pu = _dot(perm_ref[...], u_ref[sb * pb:(sb + 1) * pb, :]).astype(BF16)
            r0 = pl.multiple_of((s * subs + sb) * cb, cb)
            for l in range(lc):
                in_ref[l, pl.ds(r0, cb), :] = pu[l * cb:(l + 1) * cb]

    @pl.when((s >= n_io) & (s < n_io + lc))
    def _():
        l = s - n_io
        u = in_ref[l]
        et = expand_state(ec_ref[0])
        kblk = _dot_nt(et, c0x_ref[...])
        r = lax.broadcasted_iota(jnp.int32, kblk.shape, 0)
        c = lax.broadcasted_iota(jnp.int32, kblk.shape, 1)
        kblk = kblk + jnp.where((r == c) & (l == lc - 1), d_ref[...], 0.0)
        kexp_ref[pl.ds(pl.multiple_of(l * S5_HALF, S5_HALF), S5_HALF), :] = kblk.astype(BF16)
        for jc in range(2 * sw // S5_HALF):
            part = _dot(u, et[:, jc * S5_HALF:(jc + 1) * S5_HALF])
            for jj in range(S5_HALF // LANES):
                xs_ref[jc * (S5_HALF // LANES) + jj] += part[:, jj * LANES:(jj + 1) * LANES]

    @pl.when(s == n_io + lc - 1)
    def _():
        ar = a_ref[0, 0:1, :]
        ai = a_ref[0, 1:2, :]
        nt = sw // LANES

        def body(c, carry):
            sr, si = carry
            idx = pl.ds(c, bsz, stride=nc)
            xr = jnp.concatenate([xs_ref[j, idx, :] for j in range(nt)], axis=1)
            xi = jnp.concatenate([xs_ref[nt + j, idx, :] for j in range(nt)], axis=1)
            for j in range(nt):
                xs_ref[j, idx, :] = sr[:, j * LANES:(j + 1) * LANES]
                xs_ref[nt + j, idx, :] = si[:, j * LANES:(j + 1) * LANES]
            return ar * sr - ai * si + xr, ar * si + ai * sr + xi

        zero = jnp.zeros((bsz, sw), F32)
        lax.fori_loop(0, nc, body, (zero, zero))
        for j in range(2 * nt):
            xsb_ref[:, j * LANES:(j + 1) * LANES] = xs_ref[j].astype(BF16)

    for lo in range(lc):
        @pl.when(s == n_io + lc + lo)
        def _(lo=lo):
            acc = _dot_nt(xsb_ref[...], expand_state(ftc_ref[0]))
            for l in range(lo + 1):
                j = lc - 1 - lo + l
                acc = acc + _dot(in_ref[l], kexp_ref[j * S5_HALF:(j + 1) * S5_HALF, :])
            yall_ref[lo] = acc.astype(BF16)

    @pl.when(s >= n_io + 2 * lc)
    def _():
        t = s - (n_io + 2 * lc)
        for sb in range(subs):
            r0 = pl.multiple_of((t * subs + sb) * cb, cb)
            ycat = jnp.concatenate([yall_ref[l, pl.ds(r0, cb), :] for l in range(lc)], axis=0)
            y_ref[sb * pb:(sb + 1) * pb, :] = _dot(permt_ref[...], ycat).astype(y_ref.dtype)


def _s5_mixer(u, ec, ftc, c0, d, a):
    bsz, s, _ = u.shape
    lc = S5_CHUNK
    nc = s // lc
    t = bsz * s
    nh = BRANCH_W // S5_HALF
    n_io = S5_IO_STEPS
    tb = t // n_io
    pb = S5_PERM_BLOCK
    src = (np.arange(pb) % (pb // lc)) * lc + np.arange(pb) // (pb // lc)
    perm_np = (src[:, None] == np.arange(pb)[None, :]).astype(np.float32)
    perm, permt = jnp.asarray(perm_np, BF16), jnp.asarray(perm_np.T, BF16)
    n_steps = 2 * n_io + 2 * lc
    y = pl.pallas_call(
        functools.partial(_s5_kernel, bsz, nc),
        grid=(nh, n_steps),
        in_specs=[
            pl.BlockSpec((tb, S5_HALF), lambda h, t: (jnp.minimum(t, n_io - 1), h)),
            _full((pb, pb)), _full((pb, pb)),
            pl.BlockSpec((1, S5_HALF, 2 * LANES), lambda h, t: (jnp.clip(t - n_io, 0, lc - 1), h, 0)),
            pl.BlockSpec((1, S5_HALF, 2 * LANES), lambda h, t: (jnp.clip(t - n_io - lc, 0, lc - 1), h, 0)),
            pl.BlockSpec((S5_HALF, 2 * LANES), lambda h, t: (h, 0)),
            pl.BlockSpec((1, S5_HALF), lambda h, t: (0, h)),
            pl.BlockSpec((1, 2, S5_HALF_STATE), lambda h, t: (h, 0, 0)),
        ],
        out_specs=pl.BlockSpec((tb, S5_HALF), lambda h, t: (jnp.maximum(t - n_io - 2 * lc, 0), h)),
        out_shape=jax.ShapeDtypeStruct((t, BRANCH_W), BF16),
        scratch_shapes=[pltpu.VMEM((lc, bsz * nc, S5_HALF), BF16),
                        pltpu.VMEM((lc, bsz * nc, S5_HALF), BF16),
                        pltpu.VMEM((2 * S5_HALF_STATE // LANES, bsz * nc, LANES), F32),
                        pltpu.VMEM((bsz * nc, 2 * S5_HALF_STATE), BF16),
                        pltpu.VMEM((lc * S5_HALF, S5_HALF), BF16),
                        pltpu.VMEM((S5_HALF, 2 * S5_HALF_STATE), BF16)],
        compiler_params=_cparams(("arbitrary", "arbitrary")),
        name="s5_mixer",
    )(u.reshape(t, BRANCH_W), perm, permt, ec, ftc, c0, d, a)
    return y


def _s5_tables(a_re, a_im, log_dt, b_re, b_im, c_re, c_im):
    nl = a_re.shape[0]
    g, p_, n, lc = S5_GROUPS, S5_STATE, S5_GROUP, S5_CHUNK
    a_re, a_im, b_re, b_im, c_re, c_im = (v.astype(F32) for v in (a_re, a_im, b_re, b_im, c_re, c_im))
    dt = jnp.exp(log_dt.astype(F32))[..., None]
    adt_r = (a_re * dt)[:, None, :, None, :]
    adt_i = (a_im * dt)[:, None, :, None, :]

    def powers(steps):
        st = jnp.asarray(np.asarray(steps, np.float32).reshape(1, -1, 1, 1, 1))
        mag = jnp.exp(adt_r * st)
        return mag * jnp.cos(adt_i * st), mag * jnp.sin(adt_i * st)

    lbr, lbi = powers([1])
    are, aim = a_re[:, None, :, None, :], a_im[:, None, :, None, :]
    den = are * are + aim * aim
    qr = ((lbr - 1.0) * are + lbi * aim) / den
    qi = (lbi * are - (lbr - 1.0) * aim) / den
    bt_re, bt_im = b_re.transpose(0, 1, 3, 2)[:, None], b_im.transpose(0, 1, 3, 2)[:, None]
    bbr = qr * bt_re - qi * bt_im
    bbi = qr * bt_im + qi * bt_re
    cr, ci = c_re[:, None], c_im[:, None]
    pr, pi = powers(np.arange(lc - 1, -1, -1))
    er, ei = pr * bbr - pi * bbi, pr * bbi + pi * bbr
    ec = jnp.concatenate([er, er, ei, ei], axis=-1).reshape(nl, lc, g * n, 4 * p_)
    pfr, pfi = powers(np.arange(1, lc + 1))
    fr, fi = cr * pfr - ci * pfi, -(cr * pfi + ci * pfr)
    ftc = jnp.concatenate([fr, fr, fi, fi], axis=-1).reshape(nl, lc, g * n, 4 * p_)
    c0 = jnp.concatenate([c_re, c_re, -c_im, -c_im], axis=-1).reshape(nl, g * n, 4 * p_)
    nh = BRANCH_W // S5_HALF
    ar, ai = powers([lc])
    a_tab = jnp.stack([ar.reshape(nl, nh, S5_HALF_STATE), ai.reshape(nl, nh, S5_HALF_STATE)], axis=2)
    return ec, ftc, c0, a_tab


def _s5g_kernel(bsz, nc, u_ref, perm_ref, permt_ref, ec_ref, eg_ref, c0_ref, ftc_ref, d_ref, a_ref, y_ref,
                in_ref, ug_ref, xs_ref, xsb_ref):
    s = pl.program_id(1)
    lc = S5_CHUNK
    rows = bsz * nc
    sw = S5_HALF_STATE
    ng = S5_HALF // S5_GROUP
    gw = lc * S5_GROUP
    pb = perm_ref.shape[0]
    cb = pb // lc
    n_io = S5_IO_STEPS
    subs = rows * lc // (n_io * pb)
    nt = sw // LANES

    def pair_mask(shape, which):
        return (lax.broadcasted_iota(jnp.int32, shape, 1) // S5_STATE) % 2 == which

    @pl.when(s < n_io)
    def _():
        for sb in range(subs):
            pu = _dot(perm_ref[...], u_ref[sb * pb:(sb + 1) * pb, :]).astype(BF16)
            r0 = pl.multiple_of((s * subs + sb) * cb, cb)
            for l in range(lc):
                in_ref[l, pl.ds(r0, cb), :] = pu[l * cb:(l + 1) * cb]

    @pl.when(s == n_io)
    def _():
        for g in range(ng):
            ug_ref[g] = jnp.concatenate([in_ref[l][:, g * S5_GROUP:(g + 1) * S5_GROUP] for l in range(lc)],
                                        axis=1)
        for j in range(ng // 2):
            xin = None
            for gi in range(2):
                e_g = ec_ref[2 * j + gi].astype(F32)
                e_g = jnp.where(pair_mask(e_g.shape, gi), e_g, 0.0).astype(BF16)
                part = _dot(ug_ref[2 * j + gi], e_g)
                xin = part if xin is None else xin + part
            xs_ref[j] = xin[:, 0:LANES]
            xs_ref[nt + j] = xin[:, LANES:2 * LANES]
        ar = a_ref[0, 0:1, :]
        ai = a_ref[0, 1:2, :]

        def body(c, carry):
            sr, si = carry
            idx = pl.ds(c, bsz, stride=nc)
            xr = jnp.concatenate([xs_ref[j, idx, :] for j in range(nt)], axis=1)
            xi = jnp.concatenate([xs_ref[nt + j, idx, :] for j in range(nt)], axis=1)
            for j in range(nt):
                xs_ref[j, idx, :] = sr[:, j * LANES:(j + 1) * LANES]
                xs_ref[nt + j, idx, :] = si[:, j * LANES:(j + 1) * LANES]
            return ar * sr - ai * si + xr, ar * si + ai * sr + xi

        zero = jnp.zeros((bsz, sw), F32)
        lax.fori_loop(0, nc, body, (zero, zero))
        for j in range(2 * nt):
            xsb_ref[:, j * LANES:(j + 1) * LANES] = xs_ref[j].astype(BF16)

    @pl.when(s == n_io + 1)
    def _():
        row = lax.broadcasted_iota(jnp.int32, (gw, gw), 0)
        col = lax.broadcasted_iota(jnp.int32, (gw, gw), 1)
        for g in range(ng):
            j = g // 2
            c_rows = jnp.concatenate([c0_ref[g]] * lc, axis=0)
            kw = _dot_nt(eg_ref[g], c_rows)
            kpad = jnp.concatenate([kw, jnp.zeros_like(kw)], axis=0)
            tz = jnp.zeros((gw, gw), F32)
            for lo in range(lc):
                sh = (lc - 1 - lo) * S5_GROUP
                tz = jnp.where(col // S5_GROUP == lo, kpad[sh:sh + gw, :], tz)
            tz = tz + jnp.where(row == col, d_ref[g], 0.0)
            xp = jnp.concatenate([xsb_ref[:, j * LANES:(j + 1) * LANES],
                                  xsb_ref[:, sw + j * LANES:sw + (j + 1) * LANES]], axis=1)
            ft = ftc_ref[g].astype(F32)
            ft = jnp.where(pair_mask(ft.shape, g % 2), ft, 0.0).astype(BF16)
            yg = _dot(ug_ref[g], tz.astype(BF16)) + _dot_nt(xp, ft)
            ug_ref[g] = yg.astype(BF16)
        for l in range(lc):
            in_ref[l] = jnp.concatenate([ug_ref[g][:, l * S5_GROUP:(l + 1) * S5_GROUP] for g in range(ng)],
                                        axis=1)

    @pl.when(s >= n_io + 2)
    def _():
        t = s - (n_io + 2)
        for sb in range(subs):
            r0 = pl.multiple_of((t * subs + sb) * cb, cb)
            ycat = jnp.concatenate([in_ref[l, pl.ds(r0, cb), :] for l in range(lc)], axis=0)
            y_ref[sb * pb:(sb + 1) * pb, :] = _dot(permt_ref[...], ycat).astype(y_ref.dtype)


def _s5g_mixer(u, ec, eg, c0, ftc, d, a):
    bsz, s, _ = u.shape
    lc = S5_CHUNK
    nc = s // lc
    t = bsz * s
    nh = BRANCH_W // S5_HALF
    ng = S5_HALF // S5_GROUP
    n_io = S5_IO_STEPS
    tb = t // n_io
    pb = S5_PERM_BLOCK
    src = (np.arange(pb) % (pb // lc)) * lc + np.arange(pb) // (pb // lc)
    perm_np = (src[:, None] == np.arange(pb)[None, :]).astype(np.float32)
    perm, permt = jnp.asarray(perm_np, BF16), jnp.asarray(perm_np.T, BF16)
    once = lambda shape: pl.BlockSpec(shape, lambda h, t: (h,) + (0,) * (len(shape) - 1),
                                      pipeline_mode=pl.Buffered(1))
    return pl.pallas_call(
        functools.partial(_s5g_kernel, bsz, nc),
        grid=(nh, 2 * n_io + 2),
        in_specs=[
            pl.BlockSpec((tb, S5_HALF), lambda h, t: (jnp.minimum(t, n_io - 1), h)),
            _full((pb, pb)), _full((pb, pb)),
            once((ng,) + ec.shape[1:]), once((ng,) + eg.shape[1:]), once((ng,) + c0.shape[1:]),
            once((ng,) + ftc.shape[1:]), once((ng,) + d.shape[1:]),
            pl.BlockSpec((1, 2, S5_HALF_STATE), lambda h, t: (h, 0, 0)),
        ],
        out_specs=pl.BlockSpec((tb, S5_HALF), lambda h, t: (jnp.maximum(t - n_io - 2, 0), h)),
        out_shape=jax.ShapeDtypeStruct((t, BRANCH_W), BF16),
        scratch_shapes=[pltpu.VMEM((lc, bsz * nc, S5_HALF), BF16),
                        pltpu.VMEM((ng, bsz * nc, lc * S5_GROUP), BF16),
                        pltpu.VMEM((2 * S5_HALF_STATE // LANES, bsz * nc, LANES), F32),
                        pltpu.VMEM((bsz * nc, 2 * S5_HALF_STATE), BF16)],
        compiler_params=_cparams(("arbitrary", "arbitrary")),
        name="s5_mixer",
    )(u.reshape(t, BRANCH_W), perm, permt, ec, eg, c0, ftc, d, a)


def _s5g_tables(a_re, a_im, log_dt, b_re, b_im, c_re, c_im, d_skip):
    nl = a_re.shape[0]
    g, p_, n, lc = S5_GROUPS, S5_STATE, S5_GROUP, S5_CHUNK
    a_re, a_im, b_re, b_im, c_re, c_im = (v.astype(F32) for v in (a_re, a_im, b_re, b_im, c_re, c_im))
    dt = jnp.exp(log_dt.astype(F32))[..., None]
    adt_r = (a_re * dt)[:, :, None, None, :]
    adt_i = (a_im * dt)[:, :, None, None, :]

    def powers(steps):
        st = jnp.asarray(np.asarray(steps, np.float32).reshape(1, 1, -1, 1, 1))
        mag = jnp.exp(adt_r * st)
        return mag * jnp.cos(adt_i * st), mag * jnp.sin(adt_i * st)

    lbr, lbi = powers([1])
    are, aim = a_re[:, :, None, None, :], a_im[:, :, None, None, :]
    den = are * are + aim * aim
    qr = ((lbr - 1.0) * are + lbi * aim) / den
    qi = (lbi * are - (lbr - 1.0) * aim) / den
    bt_re, bt_im = b_re.transpose(0, 1, 3, 2)[:, :, None], b_im.transpose(0, 1, 3, 2)[:, :, None]
    bbr = qr * bt_re - qi * bt_im
    bbi = qr * bt_im + qi * bt_re
    cr, ci = c_re[:, :, None], c_im[:, :, None]
    pr, pi = powers(np.arange(lc - 1, -1, -1))
    er, ei = pr * bbr - pi * bbi, pr * bbi + pi * bbr
    ec = jnp.concatenate([er, er, ei, ei], axis=-1).reshape(nl, g, lc * n, 4 * p_)
    eg = jnp.concatenate([er, ei], axis=-1).reshape(nl, g, lc * n, 2 * p_)
    pfr, pfi = powers(np.arange(1, lc + 1))
    fr, fi = cr * pfr - ci * pfi, -(cr * pfi + ci * pfr)
    ftc = jnp.concatenate([fr, fr, fi, fi], axis=-1).reshape(nl, g, lc * n, 4 * p_)
    c0 = jnp.concatenate([c_re, -c_im], axis=-1)
    dd = jnp.tile(d_skip.astype(F32).reshape(nl, g, 1, n), (1, 1, 1, lc))
    ar, ai = powers([lc])
    nh = BRANCH_W // S5_HALF
    a_tab = jnp.stack([ar.reshape(nl, nh, S5_HALF_STATE), ai.reshape(nl, nh, S5_HALF_STATE)], axis=2)
    return ec.astype(BF16), eg.astype(BF16), c0.astype(BF16), ftc.astype(BF16), dd, a_tab


def _merge_kernel(x_ref, yg_ref, y5_ref, ym_ref, wgate_ref, bgate_ref, wglu_ref, bglu_ref,
                  wup_ref, wo_ref, g_ref, b_ref, o_ref):
    x = x_ref[...]
    xb = x.astype(BF16)
    y5 = _gelu_tanh(y5_ref[...].astype(F32))
    y5 = y5 * _sigmoid(_dot(y5.astype(BF16), wglu_ref[...]) + bglu_ref[...])
    ys = (yg_ref[...], y5.astype(BF16), ym_ref[...])
    acc = None
    for r in range(N_BRANCH):
        gate = _sigmoid(_dot(xb, wgate_ref[:, r * D_MODEL:(r + 1) * D_MODEL])
                        + bgate_ref[:, r * D_MODEL:(r + 1) * D_MODEL])
        term = gate * _dot(ys[r], wup_ref[r])
        acc = term if acc is None else acc + term
    mix = _dot(acc.astype(BF16), wo_ref[...])
    o_ref[...] = _layer_norm(DN_ALPHA * x + mix, g_ref[...], b_ref[...])


def _merge(x2, yg, y5, ym, wgate, bgate, wglu, bglu, wup, wo, g, b, tm=512):
    t, dm = x2.shape
    row = lambda w: pl.BlockSpec((tm, w), lambda i: (i, 0))
    return pl.pallas_call(
        _merge_kernel,
        grid=(t // tm,),
        in_specs=[row(dm), row(BRANCH_W), row(BRANCH_W), row(BRANCH_W),
                  _full(wgate.shape), _full(bgate.shape), _full(wglu.shape), _full(bglu.shape),
                  _full(wup.shape), _full(wo.shape), _full(g.shape), _full(b.shape)],
        out_specs=row(dm),
        out_shape=jax.ShapeDtypeStruct((t, dm), F32),
        compiler_params=_cparams(("parallel",)),
        name="merge_ln1",
    )(x2, yg, y5, ym, wgate, bgate, wglu, bglu, wup, wo, g, b)


def _ple_ln2(x, xb, f, p_ref, pwg_ref, pwp_ref, g_ref, b_ref):
    e = _sigmoid(_dot(xb, pwg_ref[...])) * _dot(p_ref[...].astype(BF16), pwp_ref[...])
    return _layer_norm(DN_ALPHA * x + f + e, g_ref[...], b_ref[...])


def _ffn_kernel(x_ref, p_ref, wg_ref, wu_ref, wd_ref, pwg_ref, pwp_ref, g_ref, b_ref, o_ref):
    x = x_ref[...]
    xb = x.astype(BF16)
    hid = (_silu(_dot(xb, wg_ref[...])) * _dot(xb, wu_ref[...])).astype(BF16)
    f = _dot(hid, wd_ref[...])
    o_ref[...] = _ple_ln2(x, xb, f, p_ref, pwg_ref, pwp_ref, g_ref, b_ref)


def _ffn_layer(x2, p3, layer, wg, wu, wd, pwg, pwp, g, b, tm=256):
    t, dm = x2.shape
    row = lambda w: pl.BlockSpec((tm, w), lambda i: (i, 0))
    return pl.pallas_call(
        _ffn_kernel,
        grid=(t // tm,),
        in_specs=[row(dm), pl.BlockSpec((None, tm, PLE_DIM), lambda i: (layer, i, 0)),
                  _full(wg.shape), _full(wu.shape), _full(wd.shape),
                  _full(pwg.shape), _full(pwp.shape), _full(g.shape), _full(b.shape)],
        out_specs=row(dm),
        out_shape=jax.ShapeDtypeStruct((t, dm), F32),
        compiler_params=_cparams(("parallel",)),
        name="ffn_ple_ln2",
    )(x2, p3, wg, wu, wd, pwg, pwp, g, b)


def _router_kernel(x_ref, w_ref, b_ref, tril_ref, sel_ref, pr_ref, rk_ref, cnt_ref, base_ref):
    @pl.when(pl.program_id(0) == 0)
    def _():
        base_ref[...] = jnp.zeros_like(base_ref)

    xh, xl = _split2(x_ref[...])
    logits = _dot(xh, w_ref[0]) + _dot(xl, w_ref[0]) + _dot(xh, w_ref[1]) + b_ref[...]
    lane = lax.broadcasted_iota(jnp.int32, logits.shape, 1)
    neg = -jnp.inf
    logits = jnp.where(lane < N_EXPERTS, logits, neg)
    m1 = jnp.max(logits, axis=-1, keepdims=True)
    i1 = jnp.min(jnp.where(logits == m1, lane, LANES), axis=-1, keepdims=True)
    rest = jnp.where(lane == i1, neg, logits)
    m2 = jnp.max(rest, axis=-1, keepdims=True)
    i2 = jnp.min(jnp.where(rest == m2, lane, LANES), axis=-1, keepdims=True)
    e2 = jnp.exp(m2 - m1)
    p1 = 1.0 / (1.0 + e2)
    p2 = e2 / (1.0 + e2)
    hot = jnp.where((lane == i1) | (lane == i2), 1.0, 0.0)
    base = base_ref[0:1, :]
    before = _dot(tril_ref[...], hot.astype(BF16)) + base
    r1 = jnp.sum(jnp.where(lane == i1, before, 0.0), axis=-1, keepdims=True)
    r2 = jnp.sum(jnp.where(lane == i2, before, 0.0), axis=-1, keepdims=True)
    sel_ref[...] = jnp.where(lane == 0, i1, jnp.where(lane == 1, i2, 0))
    pr_ref[...] = jnp.where(lane == 0, p1, jnp.where(lane == 1, p2, 0.0))
    rk_ref[...] = jnp.where(lane == 0, r1, jnp.where(lane == 1, r2, 0.0))
    total = base + jnp.sum(hot, axis=0, keepdims=True)
    base_ref[...] = jnp.broadcast_to(total, base_ref.shape)
    cnt_ref[...] = jnp.broadcast_to(total, cnt_ref.shape)


def _router(x2, w, b, tm=512):
    t, dm = x2.shape
    i = np.arange(tm)
    tril = jnp.asarray((i[:, None] > i[None, :]).astype(np.float32), BF16)
    blk = pl.BlockSpec((tm, LANES), lambda i: (i, 0))
    return pl.pallas_call(
        _router_kernel,
        grid=(t // tm,),
        in_specs=[pl.BlockSpec((tm, dm), lambda i: (i, 0)), _full(w.shape), _full(b.shape), _full((tm, tm))],
        out_specs=[blk, blk, blk, pl.BlockSpec((SUBLANES, LANES), lambda i: (0, 0))],
        out_shape=[jax.ShapeDtypeStruct((t, LANES), jnp.int32), jax.ShapeDtypeStruct((t, LANES), F32),
                   jax.ShapeDtypeStruct((t, LANES), F32), jax.ShapeDtypeStruct((SUBLANES, LANES), F32)],
        scratch_shapes=[pltpu.VMEM((SUBLANES, LANES), F32)],
        compiler_params=_cparams(("arbitrary",)),
        name="moe_router",
    )(x2, w, b, tril)


TOP_K = 2
MOE_TILE = 1024
MOE_SUB = 256
MOE_FF_TILE = 512
DISPATCH_TILE = 512
DMA_ISSUE_UNROLL = 8


ROW_SUB = D_MODEL // LANES


def _row_copy_wait(src_rows, dst_rows, sem):
    pltpu.make_async_copy(src_rows, dst_rows, sem).wait()


def _rows_to_tiles(dst_ref, val):
    n = val.shape[0]
    for j in range(ROW_SUB):
        dst_ref[pl.ds(j, n, stride=ROW_SUB), :] = val[:, j * LANES:(j + 1) * LANES]


def _tiles_to_rows(src_ref, n):
    return jnp.concatenate([src_ref[pl.ds(j, n, stride=ROW_SUB), :] for j in range(ROW_SUB)], axis=1)


def _row_tile(ref, r):
    return ref.at[pl.ds(pl.multiple_of(r * ROW_SUB, ROW_SUB), ROW_SUB), :]


def _dispatch_kernel(pos_ref, x_ref, xs_ref, stage_ref, sem):
    tm = x_ref.shape[0]
    base = pl.program_id(0) * tm * TOP_K
    _rows_to_tiles(stage_ref, x_ref[...])

    def body(r, carry):
        for k in range(TOP_K):
            dst = pos_ref[base + r * TOP_K + k]
            pltpu.make_async_copy(_row_tile(stage_ref, r), _row_tile(xs_ref, dst), sem).start(priority=k)
        return carry

    lax.fori_loop(0, tm, body, 0, unroll=DMA_ISSUE_UNROLL)
    for _ in range(TOP_K):
        _row_copy_wait(stage_ref, xs_ref.at[pl.ds(0, tm * ROW_SUB), :], sem)


def _dispatch(pos_flat, x2, tm=DISPATCH_TILE):
    t, dm = x2.shape
    return pl.pallas_call(
        _dispatch_kernel,
        grid_spec=pltpu.PrefetchScalarGridSpec(
            num_scalar_prefetch=1,
            grid=(t // tm,),
            in_specs=[pl.BlockSpec((tm, dm), lambda i, pos: (i, 0))],
            out_specs=pl.BlockSpec(memory_space=pl.ANY),
            scratch_shapes=[pltpu.VMEM((tm * ROW_SUB, LANES), F32), pltpu.SemaphoreType.DMA],
        ),
        out_shape=jax.ShapeDtypeStruct((t * TOP_K * ROW_SUB, LANES), F32),
        compiler_params=_cparams(("arbitrary",)),
        name="moe_dispatch",
    )(pos_flat, x2)


def _moe_group_kernel(tile_ref, exp_ref, lo_ref, hi_ref, xs_ref, wg_ref, wu_ref, wd_ref, o_ref,
                      acc_ref, xb_ref, wgb_ref, wub_ref, wdb_ref):
    i = pl.program_id(0)
    f = pl.program_id(1)
    lo = lo_ref[i]
    hi = hi_ref[i]
    tm = acc_ref.shape[0]

    @pl.when(hi > lo)
    def _():
        @pl.when((lo == 0) & (f == 0))
        def _():
            acc_ref[...] = jnp.zeros_like(acc_ref)

        @pl.when(f == 0)
        def _():
            xb_ref[...] = _tiles_to_rows(xs_ref, tm).astype(BF16)

        whole = (lo == 0) & (hi == tm)

        @pl.when(whole)
        def _():
            xb = xb_ref[...]
            hid = (_silu(_dot(xb, wg_ref[0].astype(BF16))) * _dot(xb, wu_ref[0].astype(BF16))).astype(BF16)
            acc_ref[...] += _dot(hid, wd_ref[0].astype(BF16))

        @pl.when(jnp.logical_not(whole))
        def _():
            wgb_ref[...] = wg_ref[0].astype(BF16)
            wub_ref[...] = wu_ref[0].astype(BF16)
            wdb_ref[...] = wd_ref[0].astype(BF16)
            for sub in range(tm // MOE_SUB):
                @pl.when((hi > sub * MOE_SUB) & (lo < (sub + 1) * MOE_SUB))
                def _(sub=sub):
                    rsl = slice(sub * MOE_SUB, (sub + 1) * MOE_SUB)
                    xb = xb_ref[rsl, :]
                    hid = _silu(_dot(xb, wgb_ref[...])) * _dot(xb, wub_ref[...])
                    rid = lax.broadcasted_iota(jnp.int32, (MOE_SUB, 1), 0) + sub * MOE_SUB
                    hid = jnp.where((rid >= lo) & (rid < hi), hid, 0.0).astype(BF16)
                    acc_ref[rsl, :] += _dot(hid, wdb_ref[...])

        @pl.when((hi == tm) & (f == pl.num_programs(1) - 1))
        def _():
            _rows_to_tiles(o_ref, acc_ref[...])


def _moe_grouped(items, xs, wg, wu, wd, tm=MOE_TILE, tf=MOE_FF_TILE):
    tile, exp, lo, hi = items
    dm = wg.shape[1]
    dff = wg.shape[2]
    return pl.pallas_call(
        _moe_group_kernel,
        grid_spec=pltpu.PrefetchScalarGridSpec(
            num_scalar_prefetch=4,
            grid=(tile.shape[0], dff // tf),
            in_specs=[pl.BlockSpec((tm * ROW_SUB, LANES), lambda i, f, tl, ex, lo_, hi_: (tl[i], 0)),
                      pl.BlockSpec((1, dm, tf), lambda i, f, tl, ex, lo_, hi_: (ex[i], 0, f)),
                      pl.BlockSpec((1, dm, tf), lambda i, f, tl, ex, lo_, hi_: (ex[i], 0, f)),
                      pl.BlockSpec((1, tf, dm), lambda i, f, tl, ex, lo_, hi_: (ex[i], f, 0))],
            out_specs=pl.BlockSpec((tm * ROW_SUB, LANES), lambda i, f, tl, ex, lo_, hi_: (tl[i], 0)),
            scratch_shapes=[pltpu.VMEM((tm, dm), F32), pltpu.VMEM((tm, dm), BF16), pltpu.VMEM((dm, tf), BF16),
                            pltpu.VMEM((dm, tf), BF16), pltpu.VMEM((tf, dm), BF16)],
        ),
        out_shape=jax.ShapeDtypeStruct(xs.shape, F32),
        compiler_params=_cparams(("arbitrary", "arbitrary")),
        name="moe_grouped",
    )(tile, exp, lo, hi, xs, wg, wu, wd)


def _moe_items(counts, n_rows, tm=MOE_TILE):
    n_tiles = n_rows // tm
    ends = jnp.cumsum(counts)
    cuts = jnp.sort(jnp.concatenate([jnp.arange(n_tiles + 1, dtype=jnp.int32) * tm, ends[:-1]]))
    start, stop = cuts[:-1], cuts[1:]
    tile = jnp.minimum(start // tm, n_tiles - 1)
    exp = jnp.minimum(jnp.sum(ends[None, :] <= start[:, None], axis=1), N_EXPERTS - 1)
    lo = start - tile * tm
    hi = stop - tile * tm
    return tile.astype(jnp.int32), exp.astype(jnp.int32), lo.astype(jnp.int32), hi.astype(jnp.int32)


def _combine_kernel(pos_ref, x_ref, pr_ref, p_ref, pwg_ref, pwp_ref, g_ref, b_ref, ys_ref, o_ref, gat_ref, sem):
    tm = x_ref.shape[0]
    base = pl.program_id(0) * tm * TOP_K

    def body(r, carry):
        for k in range(TOP_K):
            src = pos_ref[base + r * TOP_K + k]
            pltpu.make_async_copy(_row_tile(ys_ref, src), _row_tile(gat_ref.at[k], r), sem).start(priority=k)
        return carry

    lax.fori_loop(0, tm, body, 0, unroll=DMA_ISSUE_UNROLL)
    x = x_ref[...]
    xb = x.astype(BF16)
    e = _sigmoid(_dot(xb, pwg_ref[...])) * _dot(p_ref[...].astype(BF16), pwp_ref[...])
    for k in range(TOP_K):
        _row_copy_wait(ys_ref.at[pl.ds(0, tm * ROW_SUB), :], gat_ref.at[k], sem)
    pr = pr_ref[...]
    f = pr[:, 0:1] * _tiles_to_rows(gat_ref.at[0], tm)
    for k in range(1, TOP_K):
        f = f + pr[:, k:k + 1] * _tiles_to_rows(gat_ref.at[k], tm)
    o_ref[...] = _layer_norm(DN_ALPHA * x + f + e, g_ref[...], b_ref[...])


def _combine_layer(pos_flat, x2, pr, p3, layer, pwg, pwp, g, b, ys, tm=DISPATCH_TILE):
    t, dm = x2.shape
    row = lambda w: pl.BlockSpec((tm, w), lambda i, pos: (i, 0))
    full = lambda a: pl.BlockSpec(a.shape, lambda i, pos: (0,) * a.ndim, pipeline_mode=pl.Buffered(1))
    return pl.pallas_call(
        _combine_kernel,
        grid_spec=pltpu.PrefetchScalarGridSpec(
            num_scalar_prefetch=1,
            grid=(t // tm,),
            in_specs=[row(dm), row(LANES), pl.BlockSpec((None, tm, PLE_DIM), lambda i, pos: (layer, i, 0)),
                      full(pwg), full(pwp), full(g), full(b),
                      pl.BlockSpec(memory_space=pl.ANY)],
            out_specs=row(dm),
            scratch_shapes=[pltpu.VMEM((TOP_K, tm * ROW_SUB, LANES), F32), pltpu.SemaphoreType.DMA],
        ),
        out_shape=jax.ShapeDtypeStruct((t, dm), F32),
        compiler_params=_cparams(("arbitrary",)),
        name="moe_combine_ple_ln2",
    )(pos_flat, x2, pr, p3, pwg, pwp, g, b, ys)


def _row(v):
    return v.reshape(1, -1).astype(F32)


def _pad_lanes(w):
    return jnp.pad(w, ((0, 0), (0, LANES - w.shape[1])))


def kernel(x, p, w_in, b_in, gla_w_a2, gla_b_a2, gla_norm_g, s5_a_re, s5_a_im, s5_log_dt, s5_b_re, s5_b_im,
           s5_c_re, s5_c_im, s5_d, s5_w_glu, s5_b_glu, ml_conv_w, ml_conv_b, ml_norm_g, w_up, w_o, ln1_g, ln1_b,
           ffn_wg, ffn_wu, ffn_wd, moe_router, moe_router_b, moe_wg, moe_wu, moe_wd, ple_w_gate, ple_w_proj,
           ln2_g, ln2_b):
    bsz, s, dm = x.shape
    t = bsz * s
    hi = lax.Precision.HIGHEST
    o = IN_OFF
    tri = jnp.asarray(_chunk_tri(SEQ_BLOCK, CHUNK), BF16)
    ml_tri_np = _chunk_tri(SEQ_BLOCK, ML_CHUNK)
    ml_tri = jnp.asarray(ml_tri_np, BF16)
    ml_trit = jnp.asarray(ml_tri_np.T, BF16)
    s5_ec, s5_eg, s5_c0, s5_ftc, s5_dd, s5_atab = _s5g_tables(s5_a_re, s5_a_im, s5_log_dt, s5_b_re, s5_b_im,
                                                              s5_c_re, s5_c_im, s5_d)
    p3 = p.reshape(DEPTH, t, PLE_DIM)

    w_in_b = w_in.astype(BF16)
    for i in range(DEPTH):
        w, b = w_in_b[i], b_in[i]
        sl = lambda k: (w[:, o[k]:o[k + 1]], b[o[k]:o[k + 1]])
        (wq, bq), (wk, bk), (wv, bv), (_, ba), (wg_, bg_) = sl(0), sl(1), sl(2), sl(3), sl(4)
        wz = jnp.dot(w_in[i, :, o[3]:o[4]], gla_w_a2[i], precision=hi).astype(BF16)
        bz = jnp.dot(ba, gla_w_a2[i], precision=hi) + gla_b_a2[i]
        (wu_, bu_) = sl(5)
        w_gla = jnp.concatenate([wq, wk, wz, wv, wg_, wu_], axis=1)
        b_gla = _row(jnp.concatenate([bq, bk, bz, bv, bg_, bu_]))
        y_gla, u = _gla_mixer(x, w_gla, b_gla, _row(gla_norm_g[i]), tri)

        x2 = x.reshape(t, dm)
        y5 = _s5g_mixer(u, s5_ec[i], s5_eg[i], s5_c0[i], s5_ftc[i], s5_dd[i], s5_atab[i])

        (wmq, bmq), (wmk, bmk), (wmv, bmv), (wmo, bmo) = sl(6), sl(7), sl(8), sl(11)
        w_ml = jnp.concatenate([wmq, wmk, wmv, wmo], axis=1)
        b_ml = _row(jnp.concatenate([bmq, bmk, bmv, bmo]))
        w_if = w_in[i, :, o[9]:o[11]]
        b_if = b[o[9]:o[11]]
        if_h = w_if.astype(BF16)
        if_l = (w_if - if_h.astype(F32)).astype(BF16)
        wgc = jnp.stack([_pad_lanes(if_h), _pad_lanes(if_l)])
        wgr = jnp.stack([if_h.T, if_l.T])
        y_ml = _ml_mixer(x, w_ml, b_ml, wgc, _pad_lanes(_row(b_if)), wgr, b_if.reshape(-1, 1).astype(F32),
                         ml_conv_w[i].astype(F32), _row(ml_conv_b[i]), _row(ml_norm_g[i]), ml_tri, ml_trit)

        (wgt, bgt) = sl(12)
        x1 = _merge(x2, y_gla.reshape(t, BRANCH_W), y5, y_ml.reshape(t, BRANCH_W),
                    wgt, _row(bgt), s5_w_glu[i].astype(BF16), _row(s5_b_glu[i]),
                    w_up[i].astype(BF16), w_o[i].astype(BF16), _row(ln1_g[i]), _row(ln1_b[i]))

        pwg = ple_w_gate[i].astype(BF16)
        pwp = ple_w_proj[i].astype(BF16)
        j = i // 2
        if i % 2 == 0:
            x2n = _ffn_layer(x1, p3, i, ffn_wg[j].astype(BF16), ffn_wu[j].astype(BF16), ffn_wd[j].astype(BF16),
                             pwg, pwp, _row(ln2_g[i]), _row(ln2_b[i]))
        else:
            wr = moe_router[j]
            wr_h = wr.astype(BF16)
            wr_l = (wr - wr_h.astype(F32)).astype(BF16)
            sel, pr, rk, cnt = _router(x1, jnp.stack([_pad_lanes(wr_h), _pad_lanes(wr_l)]),
                                       _pad_lanes(_row(moe_router_b[j])))
            counts = cnt[0, :N_EXPERTS].astype(jnp.int32)
            starts = jnp.cumsum(counts) - counts
            sel2 = sel[:, :TOP_K]
            pos = rk[:, :TOP_K].astype(jnp.int32) + jnp.sum(
                jnp.where(sel2[..., None] == jnp.arange(N_EXPERTS), starts, 0), axis=-1)
            pos_flat = pos.reshape(-1)
            xs = _dispatch(pos_flat, x1)
            ys = _moe_grouped(_moe_items(counts, t * TOP_K), xs, moe_wg[j], moe_wu[j], moe_wd[j])
            x2n = _combine_layer(pos_flat, x1, pr, p3, i, pwg, pwp, _row(ln2_g[i]), _row(ln2_b[i]), ys)
        x = x2n.reshape(bsz, s, dm)
    return x
```

```python
import functools
import math

import numpy as np
import jax
import jax.numpy as jnp
from jax import lax
from jax.experimental import pallas as pl
from jax.experimental.pallas import tpu as pltpu

F32 = jnp.float32
BF16 = jnp.bfloat16

D_MODEL = 1024
DEPTH = 2
N_BRANCH = 3
BRANCH_W = 512
HEADS = 4
DK = 64
DV = BRANCH_W // HEADS
GLA_RANK = 16
GLA_TAU = 16.0
CHUNK = 64
ML_CHUNK = 256
S5_GROUP = 16
S5_GROUPS = BRANCH_W // S5_GROUP
S5_STATE = 64
S5_CHUNK = 16
ML_CONV = 4
N_EXPERTS = 8
PLE_DIM = 256
DN_ALPHA = (2.0 * DEPTH) ** 0.25
LN_EPS = 1e-5

IN_WIDTHS = (
    HEADS * DK, HEADS * DK, BRANCH_W, GLA_RANK, BRANCH_W,
    BRANCH_W,
    HEADS * DK, HEADS * DK, BRANCH_W, HEADS, HEADS, BRANCH_W,
    N_BRANCH * D_MODEL,
)
IN_OFF = tuple(int(o) for o in np.concatenate([[0], np.cumsum(IN_WIDTHS)]))

LANES = 128
SUBLANES = 8
SEQ_BLOCK = 256
GLA_SEQ_PER_STEP = 4
ML_SEQ_PER_STEP = 1
PAIR_W = 2 * DK
VMEM_LIMIT = 56 * 1024 * 1024


def _cparams(sem):
    return pltpu.CompilerParams(dimension_semantics=sem, vmem_limit_bytes=VMEM_LIMIT)


def _dot(a, b):
    return jnp.dot(a, b, preferred_element_type=F32)


def _dot_nt(a, b):
    return lax.dot_general(a, b, (((1,), (1,)), ((), ())), preferred_element_type=F32)


def _dot_tn(a, b):
    return lax.dot_general(a, b, (((0,), (0,)), ((), ())), preferred_element_type=F32)


def _split3(a):
    hi = a.astype(BF16)
    r = a - hi.astype(F32)
    mid = r.astype(BF16)
    lo = (r - mid.astype(F32)).astype(BF16)
    return hi, mid, lo


def _split2(a):
    hi = a.astype(BF16)
    lo = (a - hi.astype(F32)).astype(BF16)
    return hi, lo


def _log_sigmoid(x):
    return jnp.minimum(x, 0.0) - jnp.log(1.0 + jnp.exp(-jnp.abs(x)))


def _sigmoid(x):
    return 0.5 * jnp.tanh(0.5 * x) + 0.5


def _silu(x):
    return x * _sigmoid(x)


def _gelu_tanh(x):
    return 0.5 * x * (1.0 + jnp.tanh(math.sqrt(2.0 / math.pi) * (x + 0.044715 * (x * x * x))))


def _layer_norm(v, g, b):
    mu = jnp.mean(v, axis=-1, keepdims=True)
    c = v - mu
    var = jnp.mean(c * c, axis=-1, keepdims=True)
    return c * lax.rsqrt(var + LN_EPS) * g + b


def _head_norm(o):
    mu = jnp.mean(o, axis=-1, keepdims=True)
    c = o - mu
    var = jnp.mean(c * c, axis=-1, keepdims=True)
    return c * lax.rsqrt(var + LN_EPS)


def _chunk_tri(n, chunk):
    i = np.arange(n)
    return ((i[:, None] >= i[None, :]) & (i[:, None] // chunk == i[None, :] // chunk)).astype(np.float32)


def _full(shape):
    nd = len(shape)
    return pl.BlockSpec(shape, lambda *_: (0,) * nd, pipeline_mode=pl.Buffered(1))


def _gla_kernel(x_ref, w_ref, b_ref, ng_ref, tri_ref, y_ref, u_ref, st_ref, o_ref):
    @pl.when(pl.program_id(1) == 0)
    def _():
        st_ref[...] = jnp.zeros_like(st_ref)

    hk = HEADS * DK
    nb, lb, d = x_ref.shape
    xb = x_ref[...].reshape(nb * lb, d).astype(BF16)
    h = _dot(xb, w_ref[...]) + b_ref[...]
    q = h[:, 0:hk]
    k = h[:, hk:2 * hk]
    z = h[:, 2 * hk:3 * hk]
    v = h[:, 3 * hk:3 * hk + BRANCH_W]
    g = h[:, 3 * hk + BRANCH_W:3 * hk + 2 * BRANCH_W]
    u_ref[...] = h[:, 3 * hk + 2 * BRANCH_W:3 * hk + 3 * BRANCH_W].reshape(nb, lb, BRANCH_W).astype(u_ref.dtype)

    la = _log_sigmoid(z) * (1.0 / GLA_TAU)
    tri = tri_ref[...]
    la_h, la_m, la_l = _split3(la)
    cum = jnp.concatenate(
        [_dot(tri, la_h[r:r + lb]) + _dot(tri, la_m[r:r + lb]) + _dot(tri, la_l[r:r + lb])
         for r in range(0, nb * lb, lb)], axis=0)
    qd = q * (DK ** -0.5) * jnp.exp(cum)
    ki = k * jnp.exp(-cum)

    lane = lax.broadcasted_iota(jnp.int32, (1, PAIR_W), 1)
    row_i = lax.broadcasted_iota(jnp.int32, (CHUNK, 2 * CHUNK), 0)
    col_i = lax.broadcasted_iota(jnp.int32, (CHUNK, 2 * CHUNK), 1)
    causal = row_i >= col_i % CHUNK
    bd_r = lax.broadcasted_iota(jnp.int32, (2 * DV, PAIR_W), 0)
    bd_c = lax.broadcasted_iota(jnp.int32, (2 * DV, PAIR_W), 1)
    blockdiag = (bd_r >= DV) == (bd_c >= DK)
    vb_r = lax.broadcasted_iota(jnp.int32, (2 * CHUNK, 2 * DV), 0)
    vb_c = lax.broadcasted_iota(jnp.int32, (2 * CHUNK, 2 * DV), 1)
    v_blocks = (vb_r >= CHUNK) == (vb_c >= DV)

    for c in range(lb // CHUNK):
        for bb, p in [(bb, p) for bb in range(nb) for p in range(HEADS // 2)]:
            st = st_ref[bb, p]
            lsl = slice(p * PAIR_W, (p + 1) * PAIR_W)
            rsl = slice(bb * lb + c * CHUNK, bb * lb + (c + 1) * CHUNK)
            qd_c = qd[rsl, lsl]
            ki_c = ki[rsl, lsl]
            cum_c = cum[rsl, lsl]
            last = cum_c[CHUNK - 1:CHUNK, :]
            kt = (k[rsl, lsl] * jnp.exp(last - cum_c)).astype(BF16)
            qd_b = qd_c.astype(BF16)
            inter = _dot_nt(qd_b, st.astype(BF16))
            k_bd = jnp.concatenate([jnp.where(lane < DK, ki_c, 0.0), jnp.where(lane >= DK, ki_c, 0.0)],
                                   axis=0).astype(BF16)
            att = jnp.where(causal, _dot_nt(qd_b, k_bd), 0.0)
            v_c = v[rsl, p * 2 * DV:(p + 1) * 2 * DV]
            v_bd = jnp.where(v_blocks, jnp.concatenate([v_c, v_c], axis=0), 0.0).astype(BF16)
            o_ref[rsl, p * 2 * DV:(p + 1) * 2 * DV] = _dot(att.astype(BF16), v_bd) + inter
            upd = _dot_tn(v_c.astype(BF16), kt)
            st_ref[bb, p] = st * jnp.exp(last) + jnp.where(blockdiag, upd, 0.0)

    ng = ng_ref[...]
    for head in range(HEADS):
        hsl = slice(head * DV, (head + 1) * DV)
        y = _head_norm(o_ref[:, hsl]) * ng[:, hsl] * _silu(g[:, hsl])
        y_ref[:, :, hsl] = y.reshape(nb, lb, DV).astype(y_ref.dtype)


def _gla_mixer(x, w, b, ng, tri):
    bsz, s, d = x.shape
    wcols = w.shape[1]
    nb = GLA_SEQ_PER_STEP
    return pl.pallas_call(
        _gla_kernel,
        grid=(bsz // nb, s // SEQ_BLOCK),
        in_specs=[
            pl.BlockSpec((nb, SEQ_BLOCK, d), lambda i, j: (i, j, 0)),
            _full((d, wcols)), _full((1, wcols)), _full((1, BRANCH_W)),
            _full((SEQ_BLOCK, SEQ_BLOCK)),
        ],
        out_specs=[pl.BlockSpec((nb, SEQ_BLOCK, BRANCH_W), lambda i, j: (i, j, 0))] * 2,
        out_shape=[jax.ShapeDtypeStruct((bsz, s, BRANCH_W), BF16)] * 2,
        scratch_shapes=[pltpu.VMEM((nb, HEADS // 2, 2 * DV, PAIR_W), F32),
                        pltpu.VMEM((nb * SEQ_BLOCK, BRANCH_W), F32)],
        compiler_params=_cparams(("arbitrary", "arbitrary")),
        name="gla_mixer",
    )(x, w, b, ng, tri)


ML_ST_ROWS = 2 * DV + LANES


def _ml_kernel(x_ref, w_ref, b_ref, wgc_ref, bgc_ref, wgr_ref, bgr_ref, cw_ref, cb_ref, ng_ref,
               tri_ref, trit_ref, y_ref, ct_ref, m_ref, carry_ref, o_ref):
    @pl.when(pl.program_id(1) == 0)
    def _():
        ct_ref[...] = jnp.zeros_like(ct_ref)
        m_ref[...] = jnp.zeros_like(m_ref)
        carry_ref[...] = jnp.zeros_like(carry_ref)

    hk = HEADS * DK
    nb, lb, d = x_ref.shape
    x32 = x_ref[...].reshape(nb * lb, d)
    xh, xl = _split2(x32)
    h = _dot(xh, w_ref[...]) + b_ref[...]
    qk = h[:, 0:2 * hk]
    v = h[:, 2 * hk:2 * hk + BRANCH_W]
    o_pre = h[:, 2 * hk + BRANCH_W:2 * hk + 2 * BRANCH_W]

    cw = cw_ref[...]
    conv = []
    for bb in range(nb):
        qk_b = qk[bb * lb:(bb + 1) * lb]
        ext = jnp.concatenate([carry_ref[bb], qk_b], axis=0)
        first = SUBLANES - (ML_CONV - 1)
        acc = cb_ref[...] + ext[first:first + lb] * cw[0:1]
        for j in range(1, ML_CONV):
            acc = acc + ext[first + j:first + j + lb] * cw[j:j + 1]
        carry_ref[bb] = qk_b[lb - SUBLANES:lb]
        conv.append(acc)
    qkc = _silu(jnp.concatenate(conv, axis=0))
    qf = qkc[:, 0:hk]
    kf = qkc[:, hk:2 * hk] * (DK ** -0.5)
    vb = v.astype(BF16)

    gc = (_dot(xh, wgc_ref[0]) + _dot(xl, wgc_ref[0]) + _dot(xh, wgc_ref[1])) + bgc_ref[...]
    gr = (_dot_nt(wgr_ref[0], xh) + _dot_nt(wgr_ref[0], xl) + _dot_nt(wgr_ref[1], xh)) + bgr_ref[...]
    lf_c = _log_sigmoid(gc)
    lf_r = _log_sigmoid(gr)
    tri = tri_ref[...]
    trit = trit_ref[...]
    c_h, c_m, c_l = _split3(lf_c)
    r_h, r_m, r_l = _split3(lf_r)
    blocks = [slice(r, r + lb) for r in range(0, nb * lb, lb)]
    bc = jnp.concatenate([_dot(tri, c_h[r]) + _dot(tri, c_m[r]) + _dot(tri, c_l[r]) for r in blocks],
                         axis=0)
    br = jnp.concatenate([_dot(r_h[:, r], trit) + _dot(r_m[:, r], trit) + _dot(r_l[:, r], trit)
                          for r in blocks], axis=1)

    lane = lax.broadcasted_iota(jnp.int32, (1, PAIR_W), 1)
    row_i = lax.broadcasted_iota(jnp.int32, (ML_CHUNK, ML_CHUNK), 0)
    col_i = lax.broadcasted_iota(jnp.int32, (ML_CHUNK, ML_CHUNK), 1)
    causal = row_i >= col_i
    sr = lax.broadcasted_iota(jnp.int32, (ML_ST_ROWS, PAIR_W), 0)
    sc_ = lax.broadcasted_iota(jnp.int32, (ML_ST_ROWS, PAIR_W), 1)
    first = sc_ < DK
    rows_h0 = (sr < DV) | (sr == 2 * DV)
    rows_h1 = ((sr >= DV) & (sr < 2 * DV)) | (sr == 2 * DV + 1)
    st_mask = (rows_h0 & first) | (rows_h1 & ~first)
    ones_blk = jnp.ones((ML_CHUNK, LANES), BF16)

    for bb, p in [(bb, p) for bb in range(nb) for p in range(HEADS // 2)]:
        ct = ct_ref[bb, p]
        lsl = slice(p * PAIR_W, (p + 1) * PAIR_W)
        m_pair = [m_ref[bb, 2 * p + hh][0:1, 0:1] for hh in range(2)]
        for c in range(lb // ML_CHUNK):
            rsl = slice(bb * lb + c * ML_CHUNK, bb * lb + (c + 1) * ML_CHUNK)
            q_c = qf[rsl, lsl]
            k_c = kf[rsl, lsl]
            k_cb = k_c.astype(BF16)
            inter_mm = _dot_nt(q_c.astype(BF16), ct.astype(BF16))
            wt_cols, decays = [], []
            for hh in range(2):
                head = 2 * p + hh
                m_st = m_pair[hh]
                b_col = bc[rsl, HEADS + head:HEADS + head + 1]
                i_col = gc[rsl, head:head + 1]
                b_row = br[HEADS + head:HEADS + head + 1, rsl]
                i_row = gr[head:head + 1, rsl]
                dmat = jnp.where(causal, b_col - b_row + i_row, -jnp.inf)
                inter = b_col + m_st
                m_row = jnp.maximum(inter, jnp.max(dmat, axis=-1, keepdims=True))
                wts = jnp.exp(dmat - m_row)
                in_head = (lane >= hh * DK) & (lane < (hh + 1) * DK)
                qm = jnp.where(in_head, q_c, 0.0).astype(BF16)
                sc = _dot_nt(qm, k_cb) * wts
                w_inter = jnp.exp(inter - m_row)
                num = _dot(sc.astype(BF16), vb[rsl, head * DV:(head + 1) * DV]) \
                    + w_inter * inter_mm[:, hh * DV:(hh + 1) * DV]
                den = jnp.sum(sc, axis=-1, keepdims=True) \
                    + w_inter * inter_mm[:, 2 * DV + hh:2 * DV + hh + 1]
                o_ref[rsl, head * DV:(head + 1) * DV] = num / jnp.maximum(jnp.abs(den), jnp.exp(-m_row))
                g_tot = b_col[ML_CHUNK - 1:ML_CHUNK, :]
                tail = g_tot - b_col + i_col
                m_new = jnp.maximum(g_tot + m_st, jnp.max(tail, axis=0, keepdims=True))
                wt_cols.append(jnp.exp(tail - m_new))
                decays.append(jnp.exp(g_tot + m_st - m_new))
                m_pair[hh] = m_new
            wk = (k_c * jnp.where(lane < DK, wt_cols[0], wt_cols[1])).astype(BF16)
            vp = jnp.concatenate([vb[rsl, p * 2 * DV:(p + 1) * 2 * DV], ones_blk], axis=1)
            upd = _dot_tn(vp, wk)
            ct = ct * jnp.where(lane < DK, decays[0], decays[1]) + jnp.where(st_mask, upd, 0.0)
        ct_ref[bb, p] = ct
        for hh in range(2):
            m_ref[bb, 2 * p + hh] = jnp.broadcast_to(m_pair[hh], m_ref.shape[2:])

    ng = ng_ref[...]
    for head in range(HEADS):
        hsl = slice(head * DV, (head + 1) * DV)
        y = _head_norm(o_ref[:, hsl]) * ng[:, hsl] * _sigmoid(o_pre[:, hsl])
        y_ref[:, :, hsl] = y.reshape(nb, lb, DV).astype(y_ref.dtype)


def _ml_mixer(x, w, b, wgc, bgc, wgr, bgr, cw, cb, ng, tri, trit):
    bsz, s, d = x.shape
    wcols = w.shape[1]
    nb = ML_SEQ_PER_STEP
    return pl.pallas_call(
        _ml_kernel,
        grid=(bsz // nb, s // SEQ_BLOCK),
        in_specs=[
            pl.BlockSpec((nb, SEQ_BLOCK, d), lambda i, j: (i, j, 0)),
            _full((d, wcols)), _full((1, wcols)),
            _full((2, d, LANES)), _full((1, LANES)),
            _full((2, 2 * HEADS, d)), _full((2 * HEADS, 1)),
            _full((ML_CONV, 2 * HEADS * DK)), _full((1, 2 * HEADS * DK)),
            _full((1, BRANCH_W)),
            _full((SEQ_BLOCK, SEQ_BLOCK)), _full((SEQ_BLOCK, SEQ_BLOCK)),
        ],
        out_specs=pl.BlockSpec((nb, SEQ_BLOCK, BRANCH_W), lambda i, j: (i, j, 0)),
        out_shape=jax.ShapeDtypeStruct((bsz, s, BRANCH_W), BF16),
        scratch_shapes=[pltpu.VMEM((nb, HEADS // 2, ML_ST_ROWS, PAIR_W), F32),
                        pltpu.VMEM((nb, HEADS, SUBLANES, LANES), F32),
                        pltpu.VMEM((nb, SUBLANES, 2 * HEADS * DK), F32),
                        pltpu.VMEM((nb * SEQ_BLOCK, BRANCH_W), F32)],
        compiler_params=_cparams(("arbitrary", "arbitrary")),
        name="mlstm_mixer",
    )(x, w, b, wgc, bgc, wgr, bgr, cw, cb, ng, tri, trit)


S5_IO_STEPS = 4
S5_PERM_BLOCK = 256
S5_HALF = 256
S5_HALF_STATE = (S5_HALF // S5_GROUP) * S5_STATE


def _s5_kernel(bsz, nc, u_ref, perm_ref, permt_ref, ec_ref, eg_ref, c0_ref, ftc_ref, d_ref, a_ref, y_ref,
               in_ref, ug_ref, xs_ref, xsb_ref):
    s = pl.program_id(1)
    lc = S5_CHUNK
    rows = bsz * nc
    sw = S5_HALF_STATE
    ng = S5_HALF // S5_GROUP
    gw = lc * S5_GROUP
    pb = perm_ref.shape[0]
    cb = pb // lc
    n_io = S5_IO_STEPS
    subs = rows * lc // (n_io * pb)
    nt = sw // LANES

    def pair_mask(shape, which):
        return (lax.broadcasted_iota(jnp.int32, shape, 1) // S5_STATE) % 2 == which

    @pl.when(s < n_io)
    def _():
        for sb in range(subs):
            pu = _dot(perm_ref[...], u_ref[sb * pb:(sb + 1) * pb, :]).astype(BF16)
            r0 = pl.multiple_of((s * subs + sb) * cb, cb)
            for l in range(lc):
                in_ref[l, pl.ds(r0, cb), :] = pu[l * cb:(l + 1) * cb]

    @pl.when(s == n_io)
    def _():
        for g in range(ng):
            ug_ref[g] = jnp.concatenate([in_ref[l][:, g * S5_GROUP:(g + 1) * S5_GROUP] for l in range(lc)],
                                        axis=1)
        for j in range(ng // 2):
            xin = None
            for gi in range(2):
                e_g = ec_ref[2 * j + gi].astype(F32)
                e_g = jnp.where(pair_mask(e_g.shape, gi), e_g, 0.0).astype(BF16)
                part = _dot(ug_ref[2 * j + gi], e_g)
                xin = part if xin is None else xin + part
            xs_ref[j] = xin[:, 0:LANES]
            xs_ref[nt + j] = xin[:, LANES:2 * LANES]
        ar = a_ref[0, 0:1, :]
        ai = a_ref[0, 1:2, :]

        def body(c, carry):
            sr, si = carry
            idx = pl.ds(c, bsz, stride=nc)
            xr = jnp.concatenate([xs_ref[j, idx, :] for j in range(nt)], axis=1)
            xi = jnp.concatenate([xs_ref[nt + j, idx, :] for j in range(nt)], axis=1)
            for j in range(nt):
                xs_ref[j, idx, :] = sr[:, j * LANES:(j + 1) * LANES]
                xs_ref[nt + j, idx, :] = si[:, j * LANES:(j + 1) * LANES]
            return ar * sr - ai * si + xr, ar * si + ai * sr + xi

        zero = jnp.zeros((bsz, sw), F32)
        lax.fori_loop(0, nc, body, (zero, zero), unroll=2)
        for j in range(2 * nt):
            xsb_ref[:, j * LANES:(j + 1) * LANES] = xs_ref[j].astype(BF16)

    @pl.when(s == n_io + 1)
    def _():
        row = lax.broadcasted_iota(jnp.int32, (gw, gw), 0)
        col = lax.broadcasted_iota(jnp.int32, (gw, gw), 1)
        for g in range(ng):
            j = g // 2
            c_rows = jnp.concatenate([c0_ref[g]] * lc, axis=0)
            kw = _dot_nt(eg_ref[g], c_rows)
            kpad = jnp.concatenate([kw, jnp.zeros_like(kw)], axis=0)
            tz = jnp.zeros((gw, gw), F32)
            for lo in range(lc):
                sh = (lc - 1 - lo) * S5_GROUP
                tz = jnp.where(col // S5_GROUP == lo, kpad[sh:sh + gw, :], tz)
            tz = tz + jnp.where(row == col, d_ref[g], 0.0)
            xp = jnp.concatenate([xsb_ref[:, j * LANES:(j + 1) * LANES],
                                  xsb_ref[:, sw + j * LANES:sw + (j + 1) * LANES]], axis=1)
            ft = ftc_ref[g].astype(F32)
            ft = jnp.where(pair_mask(ft.shape, g % 2), ft, 0.0).astype(BF16)
            yg = _dot(ug_ref[g], tz.astype(BF16)) + _dot_nt(xp, ft)
            ug_ref[g] = yg.astype(BF16)
        for l in range(lc):
            in_ref[l] = jnp.concatenate([ug_ref[g][:, l * S5_GROUP:(l + 1) * S5_GROUP] for g in range(ng)],
                                        axis=1)

    @pl.when(s >= n_io + 2)
    def _():
        t = s - (n_io + 2)
        for sb in range(subs):
            r0 = pl.multiple_of((t * subs + sb) * cb, cb)
            ycat = jnp.concatenate([in_ref[l, pl.ds(r0, cb), :] for l in range(lc)], axis=0)
            y_ref[sb * pb:(sb + 1) * pb, :] = _dot(permt_ref[...], ycat).astype(y_ref.dtype)


def _s5_mixer(u, ec, eg, c0, ftc, d, a):
    bsz, s, _ = u.shape
    lc = S5_CHUNK
    nc = s // lc
    t = bsz * s
    nh = BRANCH_W // S5_HALF
    ng = S5_HALF // S5_GROUP
    n_io = S5_IO_STEPS
    tb = t // n_io
    pb = S5_PERM_BLOCK
    src = (np.arange(pb) % (pb // lc)) * lc + np.arange(pb) // (pb // lc)
    perm_np = (src[:, None] == np.arange(pb)[None, :]).astype(np.float32)
    perm, permt = jnp.asarray(perm_np, BF16), jnp.asarray(perm_np.T, BF16)
    once = lambda shape: pl.BlockSpec(shape, lambda h, t: (h,) + (0,) * (len(shape) - 1),
                                      pipeline_mode=pl.Buffered(1))
    return pl.pallas_call(
        functools.partial(_s5_kernel, bsz, nc),
        grid=(nh, 2 * n_io + 2),
        in_specs=[
            pl.BlockSpec((tb, S5_HALF), lambda h, t: (jnp.minimum(t, n_io - 1), h)),
            _full((pb, pb)), _full((pb, pb)),
            once((ng,) + ec.shape[1:]), once((ng,) + eg.shape[1:]), once((ng,) + c0.shape[1:]),
            once((ng,) + ftc.shape[1:]), once((ng,) + d.shape[1:]),
            pl.BlockSpec((1, 2, S5_HALF_STATE), lambda h, t: (h, 0, 0)),
        ],
        out_specs=pl.BlockSpec((tb, S5_HALF), lambda h, t: (jnp.maximum(t - n_io - 2, 0), h)),
        out_shape=jax.ShapeDtypeStruct((t, BRANCH_W), BF16),
        scratch_shapes=[pltpu.VMEM((lc, bsz * nc, S5_HALF), BF16),
                        pltpu.VMEM((ng, bsz * nc, lc * S5_GROUP), BF16),
                        pltpu.VMEM((2 * S5_HALF_STATE // LANES, bsz * nc, LANES), F32),
                        pltpu.VMEM((bsz * nc, 2 * S5_HALF_STATE), BF16)],
        compiler_params=_cparams(("arbitrary", "arbitrary")),
        name="s5_mixer",
    )(u.reshape(t, BRANCH_W), perm, permt, ec, eg, c0, ftc, d, a)


def _s5_tables(a_re, a_im, log_dt, b_re, b_im, c_re, c_im, d_skip):
    nl = a_re.shape[0]
    g, p_, n, lc = S5_GROUPS, S5_STATE, S5_GROUP, S5_CHUNK
    a_re, a_im, b_re, b_im, c_re, c_im = (v.astype(F32) for v in (a_re, a_im, b_re, b_im, c_re, c_im))
    dt = jnp.exp(log_dt.astype(F32))[..., None]
    adt_r = (a_re * dt)[:, :, None, None, :]
    adt_i = (a_im * dt)[:, :, None, None, :]

    def powers(steps):
        st = jnp.asarray(np.asarray(steps, np.float32).reshape(1, 1, -1, 1, 1))
        mag = jnp.exp(adt_r * st)
        return mag * jnp.cos(adt_i * st), mag * jnp.sin(adt_i * st)

    lbr, lbi = powers([1])
    are, aim = a_re[:, :, None, None, :], a_im[:, :, None, None, :]
    den = are * are + aim * aim
    qr = ((lbr - 1.0) * are + lbi * aim) / den
    qi = (lbi * are - (lbr - 1.0) * aim) / den
    bt_re, bt_im = b_re.transpose(0, 1, 3, 2)[:, :, None], b_im.transpose(0, 1, 3, 2)[:, :, None]
    bbr = qr * bt_re - qi * bt_im
    bbi = qr * bt_im + qi * bt_re
    cr, ci = c_re[:, :, None], c_im[:, :, None]
    pr, pi = powers(np.arange(lc - 1, -1, -1))
    er, ei = pr * bbr - pi * bbi, pr * bbi + pi * bbr
    ec = jnp.concatenate([er, er, ei, ei], axis=-1).reshape(nl, g, lc * n, 4 * p_)
    eg = jnp.concatenate([er, ei], axis=-1).reshape(nl, g, lc * n, 2 * p_)
    pfr, pfi = powers(np.arange(1, lc + 1))
    fr, fi = cr * pfr - ci * pfi, -(cr * pfi + ci * pfr)
    ftc = jnp.concatenate([fr, fr, fi, fi], axis=-1).reshape(nl, g, lc * n, 4 * p_)
    c0 = jnp.concatenate([c_re, -c_im], axis=-1)
    dd = jnp.tile(d_skip.astype(F32).reshape(nl, g, 1, n), (1, 1, 1, lc))
    ar, ai = powers([lc])
    nh = BRANCH_W // S5_HALF
    a_tab = jnp.stack([ar.reshape(nl, nh, S5_HALF_STATE), ai.reshape(nl, nh, S5_HALF_STATE)], axis=2)
    return ec.astype(BF16), eg.astype(BF16), c0.astype(BF16), ftc.astype(BF16), dd, a_tab


def _merge_kernel(x_ref, yg_ref, y5_ref, ym_ref, wgate_ref, bgate_ref, wglu_ref, bglu_ref,
                  wup_ref, wo_ref, g_ref, b_ref, o_ref):
    x = x_ref[...]
    xb = x.astype(BF16)
    y5 = _gelu_tanh(y5_ref[...].astype(F32))
    y5 = y5 * _sigmoid(_dot(y5.astype(BF16), wglu_ref[...]) + bglu_ref[...])
    ys = (yg_ref[...], y5.astype(BF16), ym_ref[...])
    acc = None
    for r in range(N_BRANCH):
        gate = _sigmoid(_dot(xb, wgate_ref[:, r * D_MODEL:(r + 1) * D_MODEL])
                        + bgate_ref[:, r * D_MODEL:(r + 1) * D_MODEL])
        term = gate * _dot(ys[r], wup_ref[r])
        acc = term if acc is None else acc + term
    mix = _dot(acc.astype(BF16), wo_ref[...])
    o_ref[...] = _layer_norm(DN_ALPHA * x + mix, g_ref[...], b_ref[...])


def _merge(x2, yg, y5, ym, wgate, bgate, wglu, bglu, wup, wo, g, b, tm=512):
    t, dm = x2.shape
    row = lambda w: pl.BlockSpec((tm, w), lambda i: (i, 0))
    return pl.pallas_call(
        _merge_kernel,
        grid=(t // tm,),
        in_specs=[row(dm), row(BRANCH_W), row(BRANCH_W), row(BRANCH_W),
                  _full(wgate.shape), _full(bgate.shape), _full(wglu.shape), _full(bglu.shape),
                  _full(wup.shape), _full(wo.shape), _full(g.shape), _full(b.shape)],
        out_specs=row(dm),
        out_shape=jax.ShapeDtypeStruct((t, dm), F32),
        compiler_params=_cparams(("parallel",)),
        name="merge_ln1",
    )(x2, yg, y5, ym, wgate, bgate, wglu, bglu, wup, wo, g, b)


def _ple_ln2(x, xb, f, p_ref, pwg_ref, pwp_ref, g_ref, b_ref):
    e = _sigmoid(_dot(xb, pwg_ref[...])) * _dot(p_ref[...].astype(BF16), pwp_ref[...])
    return _layer_norm(DN_ALPHA * x + f + e, g_ref[...], b_ref[...])


def _ffn_kernel(x_ref, p_ref, wg_ref, wu_ref, wd_ref, pwg_ref, pwp_ref, g_ref, b_ref, o_ref):
    x = x_ref[...]
    xb = x.astype(BF16)
    hid = (_silu(_dot(xb, wg_ref[...])) * _dot(xb, wu_ref[...])).astype(BF16)
    f = _dot(hid, wd_ref[...])
    o_ref[...] = _ple_ln2(x, xb, f, p_ref, pwg_ref, pwp_ref, g_ref, b_ref)


def _ffn_layer(x2, p3, layer, wg, wu, wd, pwg, pwp, g, b, tm=256):
    t, dm = x2.shape
    row = lambda w: pl.BlockSpec((tm, w), lambda i: (i, 0))
    return pl.pallas_call(
        _ffn_kernel,
        grid=(t // tm,),
        in_specs=[row(dm), pl.BlockSpec((None, tm, PLE_DIM), lambda i: (layer, i, 0)),
                  _full(wg.shape), _full(wu.shape), _full(wd.shape),
                  _full(pwg.shape), _full(pwp.shape), _full(g.shape), _full(b.shape)],
        out_specs=row(dm),
        out_shape=jax.ShapeDtypeStruct((t, dm), F32),
        compiler_params=_cparams(("parallel",)),
        name="ffn_ple_ln2",
    )(x2, p3, wg, wu, wd, pwg, pwp, g, b)


def _router_kernel(x_ref, w_ref, b_ref, tril_ref, sel_ref, pr_ref, rk_ref, cnt_ref, base_ref):
    @pl.when(pl.program_id(0) == 0)
    def _():
        base_ref[...] = jnp.zeros_like(base_ref)

    xh, xl = _split2(x_ref[...])
    logits = _dot(xh, w_ref[0]) + _dot(xl, w_ref[0]) + _dot(xh, w_ref[1]) + b_ref[...]
    lane = lax.broadcasted_iota(jnp.int32, logits.shape, 1)
    neg = -jnp.inf
    logits = jnp.where(lane < N_EXPERTS, logits, neg)
    m1 = jnp.max(logits, axis=-1, keepdims=True)
    i1 = jnp.min(jnp.where(logits == m1, lane, LANES), axis=-1, keepdims=True)
    rest = jnp.where(lane == i1, neg, logits)
    m2 = jnp.max(rest, axis=-1, keepdims=True)
    i2 = jnp.min(jnp.where(rest == m2, lane, LANES), axis=-1, keepdims=True)
    e2 = jnp.exp(m2 - m1)
    p1 = 1.0 / (1.0 + e2)
    p2 = e2 / (1.0 + e2)
    hot = jnp.where((lane == i1) | (lane == i2), 1.0, 0.0)
    base = base_ref[0:1, :]
    before = _dot(tril_ref[...], hot.astype(BF16)) + base
    r1 = jnp.sum(jnp.where(lane == i1, before, 0.0), axis=-1, keepdims=True)
    r2 = jnp.sum(jnp.where(lane == i2, before, 0.0), axis=-1, keepdims=True)
    sel_ref[...] = jnp.where(lane == 0, i1, jnp.where(lane == 1, i2, 0))
    pr_ref[...] = jnp.where(lane == 0, p1, jnp.where(lane == 1, p2, 0.0))
    rk_ref[...] = jnp.where(lane == 0, r1, jnp.where(lane == 1, r2, 0.0))
    total = base + jnp.sum(hot, axis=0, keepdims=True)
    base_ref[...] = jnp.broadcast_to(total, base_ref.shape)
    cnt_ref[...] = jnp.broadcast_to(total, cnt_ref.shape)


def _router(x2, w, b, tm=1024):
    t, dm = x2.shape
    i = np.arange(tm)
    tril = jnp.asarray((i[:, None] > i[None, :]).astype(np.float32), BF16)
    blk = pl.BlockSpec((tm, LANES), lambda i: (i, 0))
    return pl.pallas_call(
        _router_kernel,
        grid=(t // tm,),
        in_specs=[pl.BlockSpec((tm, dm), lambda i: (i, 0)), _full(w.shape), _full(b.shape), _full((tm, tm))],
        out_specs=[blk, blk, blk, pl.BlockSpec((SUBLANES, LANES), lambda i: (0, 0))],
        out_shape=[jax.ShapeDtypeStruct((t, LANES), jnp.int32), jax.ShapeDtypeStruct((t, LANES), F32),
                   jax.ShapeDtypeStruct((t, LANES), F32), jax.ShapeDtypeStruct((SUBLANES, LANES), F32)],
        scratch_shapes=[pltpu.VMEM((SUBLANES, LANES), F32)],
        compiler_params=_cparams(("arbitrary",)),
        name="moe_router",
    )(x2, w, b, tril)


TOP_K = 2
MOE_TILE = 1024
MOE_SUB = 256
MOE_FF_TILE = 512
DISPATCH_TILE = 512
DMA_ISSUE_UNROLL = 8


ROW_SUB = D_MODEL // LANES


def _row_copy_wait(src_rows, dst_rows, sem):
    pltpu.make_async_copy(src_rows, dst_rows, sem).wait()


def _rows_to_tiles(dst_ref, val):
    n = val.shape[0]
    for j in range(ROW_SUB):
        dst_ref[pl.ds(j, n, stride=ROW_SUB), :] = val[:, j * LANES:(j + 1) * LANES]


def _tiles_to_rows(src_ref, n):
    return jnp.concatenate([src_ref[pl.ds(j, n, stride=ROW_SUB), :] for j in range(ROW_SUB)], axis=1)


def _row_tile(ref, r):
    return ref.at[pl.ds(pl.multiple_of(r * ROW_SUB, ROW_SUB), ROW_SUB), :]


def _dispatch_kernel(pos_ref, x_ref, xs_ref, stage_ref, sem):
    tm = x_ref.shape[0]
    base = pl.program_id(0) * tm * TOP_K
    _rows_to_tiles(stage_ref, x_ref[...])

    def body(r, carry):
        for k in range(TOP_K):
            dst = pos_ref[base + r * TOP_K + k]
            pltpu.make_async_copy(_row_tile(stage_ref, r), _row_tile(xs_ref, dst), sem).start(priority=k)
        return carry

    lax.fori_loop(0, tm, body, 0, unroll=DMA_ISSUE_UNROLL)
    for _ in range(TOP_K):
        _row_copy_wait(stage_ref, xs_ref.at[pl.ds(0, tm * ROW_SUB), :], sem)


def _dispatch(pos_flat, x2, tm=DISPATCH_TILE):
    t, dm = x2.shape
    return pl.pallas_call(
        _dispatch_kernel,
        grid_spec=pltpu.PrefetchScalarGridSpec(
            num_scalar_prefetch=1,
            grid=(t // tm,),
            in_specs=[pl.BlockSpec((tm, dm), lambda i, pos: (i, 0))],
            out_specs=pl.BlockSpec(memory_space=pl.ANY),
            scratch_shapes=[pltpu.VMEM((tm * ROW_SUB, LANES), F32), pltpu.SemaphoreType.DMA],
        ),
        out_shape=jax.ShapeDtypeStruct((t * TOP_K * ROW_SUB, LANES), F32),
        compiler_params=_cparams(("arbitrary",)),
        name="moe_dispatch",
    )(pos_flat, x2)


def _moe_group_kernel(tile_ref, exp_ref, lo_ref, hi_ref, xs_ref, wg_ref, wu_ref, wd_ref, o_ref,
                      acc_ref, xb_ref, wgb_ref, wub_ref, wdb_ref):
    i = pl.program_id(0)
    f = pl.program_id(1)
    lo = lo_ref[i]
    hi = hi_ref[i]
    tm = acc_ref.shape[0]

    @pl.when(hi > lo)
    def _():
        @pl.when((lo == 0) & (f == 0))
        def _():
            acc_ref[...] = jnp.zeros_like(acc_ref)

        @pl.when(f == 0)
        def _():
            xb_ref[...] = _tiles_to_rows(xs_ref, tm).astype(BF16)

        whole = (lo == 0) & (hi == tm)

        @pl.when(whole)
        def _():
            xb = xb_ref[...]
            hid = (_silu(_dot(xb, wg_ref[0].astype(BF16))) * _dot(xb, wu_ref[0].astype(BF16))).astype(BF16)
            acc_ref[...] += _dot(hid, wd_ref[0].astype(BF16))

        @pl.when(jnp.logical_not(whole))
        def _():
            wgb_ref[...] = wg_ref[0].astype(BF16)
            wub_ref[...] = wu_ref[0].astype(BF16)
            wdb_ref[...] = wd_ref[0].astype(BF16)
            for sub in range(tm // MOE_SUB):
                @pl.when((hi > sub * MOE_SUB) & (lo < (sub + 1) * MOE_SUB))
                def _(sub=sub):
                    rsl = slice(sub * MOE_SUB, (sub + 1) * MOE_SUB)
                    xb = xb_ref[rsl, :]
                    hid = _silu(_dot(xb, wgb_ref[...])) * _dot(xb, wub_ref[...])
                    rid = lax.broadcasted_iota(jnp.int32, (MOE_SUB, 1), 0) + sub * MOE_SUB
                    hid = jnp.where((rid >= lo) & (rid < hi), hid, 0.0).astype(BF16)
                    acc_ref[rsl, :] += _dot(hid, wdb_ref[...])

        @pl.when((hi == tm) & (f == pl.num_programs(1) - 1))
        def _():
            _rows_to_tiles(o_ref, acc_ref[...])


def _moe_grouped(items, xs, wg, wu, wd, tm=MOE_TILE, tf=MOE_FF_TILE):
    tile, exp, lo, hi = items
    dm = wg.shape[1]
    dff = wg.shape[2]
    return pl.pallas_call(
        _moe_group_kernel,
        grid_spec=pltpu.PrefetchScalarGridSpec(
            num_scalar_prefetch=4,
            grid=(tile.shape[0], dff // tf),
            in_specs=[pl.BlockSpec((tm * ROW_SUB, LANES), lambda i, f, tl, ex, lo_, hi_: (tl[i], 0)),
                      pl.BlockSpec((1, dm, tf), lambda i, f, tl, ex, lo_, hi_: (ex[i], 0, f)),
                      pl.BlockSpec((1, dm, tf), lambda i, f, tl, ex, lo_, hi_: (ex[i], 0, f)),
                      pl.BlockSpec((1, tf, dm), lambda i, f, tl, ex, lo_, hi_: (ex[i], f, 0))],
            out_specs=pl.BlockSpec((tm * ROW_SUB, LANES), lambda i, f, tl, ex, lo_, hi_: (tl[i], 0)),
            scratch_shapes=[pltpu.VMEM((tm, dm), F32), pltpu.VMEM((tm, dm), BF16), pltpu.VMEM((dm, tf), BF16),
                            pltpu.VMEM((dm, tf), BF16), pltpu.VMEM((tf, dm), BF16)],
        ),
        out_shape=jax.ShapeDtypeStruct(xs.shape, F32),
        compiler_params=_cparams(("arbitrary", "arbitrary")),
        name="moe_grouped",
    )(tile, exp, lo, hi, xs, wg, wu, wd)


def _moe_items(counts, n_rows, tm=MOE_TILE):
    n_tiles = n_rows // tm
    ends = jnp.cumsum(counts)
    cuts = jnp.sort(jnp.concatenate([jnp.arange(n_tiles + 1, dtype=jnp.int32) * tm, ends[:-1]]))
    start, stop = cuts[:-1], cuts[1:]
    tile = jnp.minimum(start // tm, n_tiles - 1)
    exp = jnp.minimum(jnp.sum(ends[None, :] <= start[:, None], axis=1), N_EXPERTS - 1)
    lo = start - tile * tm
    hi = stop - tile * tm
    return tile.astype(jnp.int32), exp.astype(jnp.int32), lo.astype(jnp.int32), hi.astype(jnp.int32)


def _combine_kernel(pos_ref, x_ref, pr_ref, p_ref, pwg_ref, pwp_ref, g_ref, b_ref, ys_ref, o_ref, gat_ref, sem):
    tm = x_ref.shape[0]
    base = pl.program_id(0) * tm * TOP_K

    def body(r, carry):
        for k in range(TOP_K):
            src = pos_ref[base + r * TOP_K + k]
            pltpu.make_async_copy(_row_tile(ys_ref, src), _row_tile(gat_ref.at[k], r), sem).start(priority=k)
        return carry

    lax.fori_loop(0, tm, body, 0, unroll=DMA_ISSUE_UNROLL)
    x = x_ref[...]
    xb = x.astype(BF16)
    e = _sigmoid(_dot(xb, pwg_ref[...])) * _dot(p_ref[...].astype(BF16), pwp_ref[...])
    for k in range(TOP_K):
        _row_copy_wait(ys_ref.at[pl.ds(0, tm * ROW_SUB), :], gat_ref.at[k], sem)
    pr = pr_ref[...]
    f = pr[:, 0:1] * _tiles_to_rows(gat_ref.at[0], tm)
    for k in range(1, TOP_K):
        f = f + pr[:, k:k + 1] * _tiles_to_rows(gat_ref.at[k], tm)
    o_ref[...] = _layer_norm(DN_ALPHA * x + f + e, g_ref[...], b_ref[...])


def _combine_layer(pos_flat, x2, pr, p3, layer, pwg, pwp, g, b, ys, tm=DISPATCH_TILE):
    t, dm = x2.shape
    row = lambda w: pl.BlockSpec((tm, w), lambda i, pos: (i, 0))
    full = lambda a: pl.BlockSpec(a.shape, lambda i, pos: (0,) * a.ndim, pipeline_mode=pl.Buffered(1))
    return pl.pallas_call(
        _combine_kernel,
        grid_spec=pltpu.PrefetchScalarGridSpec(
            num_scalar_prefetch=1,
            grid=(t // tm,),
            in_specs=[row(dm), row(LANES), pl.BlockSpec((None, tm, PLE_DIM), lambda i, pos: (layer, i, 0)),
                      full(pwg), full(pwp), full(g), full(b),
                      pl.BlockSpec(memory_space=pl.ANY)],
            out_specs=row(dm),
            scratch_shapes=[pltpu.VMEM((TOP_K, tm * ROW_SUB, LANES), F32), pltpu.SemaphoreType.DMA],
        ),
        out_shape=jax.ShapeDtypeStruct((t, dm), F32),
        compiler_params=_cparams(("arbitrary",)),
        name="moe_combine_ple_ln2",
    )(pos_flat, x2, pr, p3, pwg, pwp, g, b, ys)


def _row(v):
    return v.reshape(1, -1).astype(F32)


def _pad_lanes(w):
    return jnp.pad(w, ((0, 0), (0, LANES - w.shape[1])))


def kernel(x, p, w_in, b_in, gla_w_a2, gla_b_a2, gla_norm_g, s5_a_re, s5_a_im, s5_log_dt, s5_b_re, s5_b_im,
           s5_c_re, s5_c_im, s5_d, s5_w_glu, s5_b_glu, ml_conv_w, ml_conv_b, ml_norm_g, w_up, w_o, ln1_g, ln1_b,
           ffn_wg, ffn_wu, ffn_wd, moe_router, moe_router_b, moe_wg, moe_wu, moe_wd, ple_w_gate, ple_w_proj,
           ln2_g, ln2_b):
    bsz, s, dm = x.shape
    t = bsz * s
    hi = lax.Precision.HIGHEST
    o = IN_OFF
    tri = jnp.asarray(_chunk_tri(SEQ_BLOCK, CHUNK), BF16)
    ml_tri_np = _chunk_tri(SEQ_BLOCK, ML_CHUNK)
    ml_tri = jnp.asarray(ml_tri_np, BF16)
    ml_trit = jnp.asarray(ml_tri_np.T, BF16)
    s5_ec, s5_eg, s5_c0, s5_ftc, s5_dd, s5_atab = _s5_tables(s5_a_re, s5_a_im, s5_log_dt, s5_b_re, s5_b_im,
                                                             s5_c_re, s5_c_im, s5_d)
    p3 = p.reshape(DEPTH, t, PLE_DIM)

    w_in_b = w_in.astype(BF16)
    for i in range(DEPTH):
        w, b = w_in_b[i], b_in[i]
        sl = lambda k: (w[:, o[k]:o[k + 1]], b[o[k]:o[k + 1]])
        (wq, bq), (wk, bk), (wv, bv), (_, ba), (wg_, bg_) = sl(0), sl(1), sl(2), sl(3), sl(4)
        wz = jnp.dot(w_in[i, :, o[3]:o[4]], gla_w_a2[i], precision=hi).astype(BF16)
        bz = jnp.dot(ba, gla_w_a2[i], precision=hi) + gla_b_a2[i]
        (wu_, bu_) = sl(5)
        w_gla = jnp.concatenate([wq, wk, wz, wv, wg_, wu_], axis=1)
        b_gla = _row(jnp.concatenate([bq, bk, bz, bv, bg_, bu_]))
        y_gla, u = _gla_mixer(x, w_gla, b_gla, _row(gla_norm_g[i]), tri)

        x2 = x.reshape(t, dm)
        y5 = _s5_mixer(u, s5_ec[i], s5_eg[i], s5_c0[i], s5_ftc[i], s5_dd[i], s5_atab[i])

        (wmq, bmq), (wmk, bmk), (wmv, bmv), (wmo, bmo) = sl(6), sl(7), sl(8), sl(11)
        w_ml = jnp.concatenate([wmq, wmk, wmv, wmo], axis=1)
        b_ml = _row(jnp.concatenate([bmq, bmk, bmv, bmo]))
        w_if = w_in[i, :, o[9]:o[11]]
        b_if = b[o[9]:o[11]]
        if_h = w_if.astype(BF16)
        if_l = (w_if - if_h.astype(F32)).astype(BF16)
        wgc = jnp.stack([_pad_lanes(if_h), _pad_lanes(if_l)])
        wgr = jnp.stack([if_h.T, if_l.T])
        y_ml = _ml_mixer(x, w_ml, b_ml, wgc, _pad_lanes(_row(b_if)), wgr, b_if.reshape(-1, 1).astype(F32),
                         ml_conv_w[i].astype(F32), _row(ml_conv_b[i]), _row(ml_norm_g[i]), ml_tri, ml_trit)

        (wgt, bgt) = sl(12)
        x1 = _merge(x2, y_gla.reshape(t, BRANCH_W), y5, y_ml.reshape(t, BRANCH_W),
                    wgt, _row(bgt), s5_w_glu[i].astype(BF16), _row(s5_b_glu[i]),
                    w_up[i].astype(BF16), w_o[i].astype(BF16), _row(ln1_g[i]), _row(ln1_b[i]))

        pwg = ple_w_gate[i].astype(BF16)
        pwp = ple_w_proj[i].astype(BF16)
        j = i // 2
        if i % 2 == 0:
            x2n = _ffn_layer(x1, p3, i, ffn_wg[j].astype(BF16), ffn_wu[j].astype(BF16), ffn_wd[j].astype(BF16),
                             pwg, pwp, _row(ln2_g[i]), _row(ln2_b[i]))
        else:
            wr = moe_router[j]
            wr_h = wr.astype(BF16)
            wr_l = (wr - wr_h.astype(F32)).astype(BF16)
            sel, pr, rk, cnt = _router(x1, jnp.stack([_pad_lanes(wr_h), _pad_lanes(wr_l)]),
                                       _pad_lanes(_row(moe_router_b[j])))
            counts = cnt[0, :N_EXPERTS].astype(jnp.int32)
            starts = jnp.cumsum(counts) - counts
            sel2 = sel[:, :TOP_K]
            pos = rk[:, :TOP_K].astype(jnp.int32) + jnp.sum(
                jnp.where(sel2[..., None] == jnp.arange(N_EXPERTS), starts, 0), axis=-1)
            pos_flat = pos.reshape(-1)
            xs = _dispatch(pos_flat, x1)
            ys = _moe_grouped(_moe_items(counts, t * TOP_K), xs, moe_wg[j], moe_wu[j], moe_wd[j])
            x2n = _combine_layer(pos_flat, x1, pr, p3, i, pwg, pwp, _row(ln2_g[i]), _row(ln2_b[i]), ys)
        x = x2n.reshape(bsz, s, dm)
    return x
```

```python
import functools
import math

import numpy as np
import jax
import jax.numpy as jnp
from jax import lax
from jax.experimental import pallas as pl
from jax.experimental.pallas import tpu as pltpu

F32 = jnp.float32
BF16 = jnp.bfloat16

D_MODEL = 1024
DEPTH = 2
N_BRANCH = 3
BRANCH_W = 512
HEADS = 4
DK = 64
DV = BRANCH_W // HEADS
GLA_RANK = 16
GLA_TAU = 16.0
CHUNK = 64
ML_CHUNK = 256
S5_GROUP = 16
S5_GROUPS = BRANCH_W // S5_GROUP
S5_STATE = 64
S5_CHUNK = 16
ML_CONV = 4
N_EXPERTS = 8
PLE_DIM = 256
DN_ALPHA = (2.0 * DEPTH) ** 0.25
LN_EPS = 1e-5

IN_WIDTHS = (
    HEADS * DK, HEADS * DK, BRANCH_W, GLA_RANK, BRANCH_W,
    BRANCH_W,
    HEADS * DK, HEADS * DK, BRANCH_W, HEADS, HEADS, BRANCH_W,
    N_BRANCH * D_MODEL,
)
IN_OFF = tuple(int(o) for o in np.concatenate([[0], np.cumsum(IN_WIDTHS)]))

LANES = 128
SUBLANES = 8
SEQ_BLOCK = 256
GLA_SEQ_PER_STEP = 4
ML_SEQ_PER_STEP = 1
PAIR_W = 2 * DK
VMEM_LIMIT = 56 * 1024 * 1024


def _cparams(sem):
    return pltpu.CompilerParams(dimension_semantics=sem, vmem_limit_bytes=VMEM_LIMIT)


def _dot(a, b):
    return jnp.dot(a, b, preferred_element_type=F32)


def _dot_nt(a, b):
    return lax.dot_general(a, b, (((1,), (1,)), ((), ())), preferred_element_type=F32)


def _dot_tn(a, b):
    return lax.dot_general(a, b, (((0,), (0,)), ((), ())), preferred_element_type=F32)


def _split3(a):
    hi = a.astype(BF16)
    r = a - hi.astype(F32)
    mid = r.astype(BF16)
    lo = (r - mid.astype(F32)).astype(BF16)
    return hi, mid, lo


def _split2(a):
    hi = a.astype(BF16)
    lo = (a - hi.astype(F32)).astype(BF16)
    return hi, lo


def _log_sigmoid(x):
    return jnp.minimum(x, 0.0) - jnp.log(1.0 + jnp.exp(-jnp.abs(x)))


def _sigmoid(x):
    return 0.5 * jnp.tanh(0.5 * x) + 0.5


def _silu(x):
    return x * _sigmoid(x)


def _gelu_tanh(x):
    return 0.5 * x * (1.0 + jnp.tanh(math.sqrt(2.0 / math.pi) * (x + 0.044715 * (x * x * x))))


def _layer_norm(v, g, b):
    mu = jnp.mean(v, axis=-1, keepdims=True)
    c = v - mu
    var = jnp.mean(c * c, axis=-1, keepdims=True)
    return c * lax.rsqrt(var + LN_EPS) * g + b


def _head_norm(o):
    mu = jnp.mean(o, axis=-1, keepdims=True)
    c = o - mu
    var = jnp.mean(c * c, axis=-1, keepdims=True)
    return c * lax.rsqrt(var + LN_EPS)


def _chunk_tri(n, chunk):
    i = np.arange(n)
    return ((i[:, None] >= i[None, :]) & (i[:, None] // chunk == i[None, :] // chunk)).astype(np.float32)


def _full(shape):
    nd = len(shape)
    return pl.BlockSpec(shape, lambda *_: (0,) * nd, pipeline_mode=pl.Buffered(1))


def _gla_kernel(x_ref, w_ref, b_ref, ng_ref, tri_ref, y_ref, u_ref, st_ref, o_ref):
    @pl.when(pl.program_id(1) == 0)
    def _():
        st_ref[...] = jnp.zeros_like(st_ref)

    hk = HEADS * DK
    nb, lb, d = x_ref.shape
    xb = x_ref[...].reshape(nb * lb, d).astype(BF16)
    h = _dot(xb, w_ref[...]) + b_ref[...]
    q = h[:, 0:hk]
    k = h[:, hk:2 * hk]
    z = h[:, 2 * hk:3 * hk]
    v = h[:, 3 * hk:3 * hk + BRANCH_W]
    g = h[:, 3 * hk + BRANCH_W:3 * hk + 2 * BRANCH_W]
    u_ref[...] = h[:, 3 * hk + 2 * BRANCH_W:3 * hk + 3 * BRANCH_W].reshape(nb, lb, BRANCH_W).astype(u_ref.dtype)

    la = _log_sigmoid(z) * (1.0 / GLA_TAU)
    tri = tri_ref[...]
    la_h, la_m, la_l = _split3(la)
    cum = jnp.concatenate(
        [_dot(tri, la_h[r:r + lb]) + _dot(tri, la_m[r:r + lb]) + _dot(tri, la_l[r:r + lb])
         for r in range(0, nb * lb, lb)], axis=0)
    qd = q * (DK ** -0.5) * jnp.exp(cum)
    ki = k * jnp.exp(-cum)

    lane = lax.broadcasted_iota(jnp.int32, (1, PAIR_W), 1)
    row_i = lax.broadcasted_iota(jnp.int32, (CHUNK, 2 * CHUNK), 0)
    col_i = lax.broadcasted_iota(jnp.int32, (CHUNK, 2 * CHUNK), 1)
    causal = row_i >= col_i % CHUNK
    bd_r = lax.broadcasted_iota(jnp.int32, (2 * DV, PAIR_W), 0)
    bd_c = lax.broadcasted_iota(jnp.int32, (2 * DV, PAIR_W), 1)
    blockdiag = (bd_r >= DV) == (bd_c >= DK)
    vb_r = lax.broadcasted_iota(jnp.int32, (2 * CHUNK, 2 * DV), 0)
    vb_c = lax.broadcasted_iota(jnp.int32, (2 * CHUNK, 2 * DV), 1)
    v_blocks = (vb_r >= CHUNK) == (vb_c >= DV)

    for c in range(lb // CHUNK):
        for bb, p in [(bb, p) for bb in range(nb) for p in range(HEADS // 2)]:
            st = st_ref[bb, p]
            lsl = slice(p * PAIR_W, (p + 1) * PAIR_W)
            rsl = slice(bb * lb + c * CHUNK, bb * lb + (c + 1) * CHUNK)
            qd_c = qd[rsl, lsl]
            ki_c = ki[rsl, lsl]
            cum_c = cum[rsl, lsl]
            last = cum_c[CHUNK - 1:CHUNK, :]
            kt = (k[rsl, lsl] * jnp.exp(last - cum_c)).astype(BF16)
            qd_b = qd_c.astype(BF16)
            inter = _dot_nt(qd_b, st.astype(BF16))
            k_bd = jnp.concatenate([jnp.where(lane < DK, ki_c, 0.0), jnp.where(lane >= DK, ki_c, 0.0)],
                                   axis=0).astype(BF16)
            att = jnp.where(causal, _dot_nt(qd_b, k_bd), 0.0)
            v_c = v[rsl, p * 2 * DV:(p + 1) * 2 * DV]
            v_bd = jnp.where(v_blocks, jnp.concatenate([v_c, v_c], axis=0), 0.0).astype(BF16)
            o_ref[rsl, p * 2 * DV:(p + 1) * 2 * DV] = _dot(att.astype(BF16), v_bd) + inter
            upd = _dot_tn(v_c.astype(BF16), kt)
            st_ref[bb, p] = st * jnp.exp(last) + jnp.where(blockdiag, upd, 0.0)

    ng = ng_ref[...]
    for head in range(HEADS):
        hsl = slice(head * DV, (head + 1) * DV)
        y = _head_norm(o_ref[:, hsl]) * ng[:, hsl] * _silu(g[:, hsl])
        y_ref[:, :, hsl] = y.reshape(nb, lb, DV).astype(y_ref.dtype)


def _gla_mixer(x, w, b, ng, tri):
    bsz, s, d = x.shape
    wcols = w.shape[1]
    nb = GLA_SEQ_PER_STEP
    return pl.pallas_call(
        _gla_kernel,
        grid=(bsz // nb, s // SEQ_BLOCK),
        in_specs=[
            pl.BlockSpec((nb, SEQ_BLOCK, d), lambda i, j: (i, j, 0)),
            _full((d, wcols)), _full((1, wcols)), _full((1, BRANCH_W)),
            _full((SEQ_BLOCK, SEQ_BLOCK)),
        ],
        out_specs=[pl.BlockSpec((nb, SEQ_BLOCK, BRANCH_W), lambda i, j: (i, j, 0))] * 2,
        out_shape=[jax.ShapeDtypeStruct((bsz, s, BRANCH_W), BF16)] * 2,
        scratch_shapes=[pltpu.VMEM((nb, HEADS // 2, 2 * DV, PAIR_W), F32),
                        pltpu.VMEM((nb * SEQ_BLOCK, BRANCH_W), F32)],
        compiler_params=_cparams(("arbitrary", "arbitrary")),
        name="gla_mixer",
    )(x, w, b, ng, tri)


ML_ST_ROWS = 2 * DV + LANES


def _ml_kernel(x_ref, w_ref, b_ref, wgc_ref, bgc_ref, wgr_ref, bgr_ref, cw_ref, cb_ref, ng_ref,
               tri_ref, trit_ref, y_ref, ct_ref, m_ref, carry_ref, o_ref):
    @pl.when(pl.program_id(1) == 0)
    def _():
        ct_ref[...] = jnp.zeros_like(ct_ref)
        m_ref[...] = jnp.zeros_like(m_ref)
        carry_ref[...] = jnp.zeros_like(carry_ref)

    hk = HEADS * DK
    nb, lb, d = x_ref.shape
    x32 = x_ref[...].reshape(nb * lb, d)
    xh, xl = _split2(x32)
    h = _dot(xh, w_ref[...]) + b_ref[...]
    qk = h[:, 0:2 * hk]
    v = h[:, 2 * hk:2 * hk + BRANCH_W]
    o_pre = h[:, 2 * hk + BRANCH_W:2 * hk + 2 * BRANCH_W]

    cw = cw_ref[...]
    conv = []
    for bb in range(nb):
        qk_b = qk[bb * lb:(bb + 1) * lb]
        ext = jnp.concatenate([carry_ref[bb], qk_b], axis=0)
        first = SUBLANES - (ML_CONV - 1)
        acc = cb_ref[...] + ext[first:first + lb] * cw[0:1]
        for j in range(1, ML_CONV):
            acc = acc + ext[first + j:first + j + lb] * cw[j:j + 1]
        carry_ref[bb] = qk_b[lb - SUBLANES:lb]
        conv.append(acc)
    qkc = _silu(jnp.concatenate(conv, axis=0))
    qf = qkc[:, 0:hk]
    kf = qkc[:, hk:2 * hk] * (DK ** -0.5)
    vb = v.astype(BF16)

    gc = (_dot(xh, wgc_ref[0]) + _dot(xl, wgc_ref[0]) + _dot(xh, wgc_ref[1])) + bgc_ref[...]
    gr = (_dot_nt(wgr_ref[0], xh) + _dot_nt(wgr_ref[0], xl) + _dot_nt(wgr_ref[1], xh)) + bgr_ref[...]
    lf_c = _log_sigmoid(gc)
    lf_r = _log_sigmoid(gr)
    tri = tri_ref[...]
    trit = trit_ref[...]
    c_h, c_m, c_l = _split3(lf_c)
    r_h, r_m, r_l = _split3(lf_r)
    blocks = [slice(r, r + lb) for r in range(0, nb * lb, lb)]
    bc = jnp.concatenate([_dot(tri, c_h[r]) + _dot(tri, c_m[r]) + _dot(tri, c_l[r]) for r in blocks],
                         axis=0)
    br = jnp.concatenate([_dot(r_h[:, r], trit) + _dot(r_m[:, r], trit) + _dot(r_l[:, r], trit)
                          for r in blocks], axis=1)

    lane = lax.broadcasted_iota(jnp.int32, (1, PAIR_W), 1)
    row_i = lax.broadcasted_iota(jnp.int32, (ML_CHUNK, ML_CHUNK), 0)
    col_i = lax.broadcasted_iota(jnp.int32, (ML_CHUNK, ML_CHUNK), 1)
    causal = row_i >= col_i
    sr = lax.broadcasted_iota(jnp.int32, (ML_ST_ROWS, PAIR_W), 0)
    sc_ = lax.broadcasted_iota(jnp.int32, (ML_ST_ROWS, PAIR_W), 1)
    first = sc_ < DK
    rows_h0 = (sr < DV) | (sr == 2 * DV)
    rows_h1 = ((sr >= DV) & (sr < 2 * DV)) | (sr == 2 * DV + 1)
    st_mask = (rows_h0 & first) | (rows_h1 & ~first)
    ones_blk = jnp.ones((ML_CHUNK, LANES), BF16)

    for bb, p in [(bb, p) for bb in range(nb) for p in range(HEADS // 2)]:
        ct = ct_ref[bb, p]
        lsl = slice(p * PAIR_W, (p + 1) * PAIR_W)
        m_pair = [m_ref[bb, 2 * p + hh][0:1, 0:1] for hh in range(2)]
        for c in range(lb // ML_CHUNK):
            rsl = slice(bb * lb + c * ML_CHUNK, bb * lb + (c + 1) * ML_CHUNK)
            q_c = qf[rsl, lsl]
            k_c = kf[rsl, lsl]
            k_cb = k_c.astype(BF16)
            inter_mm = _dot_nt(q_c.astype(BF16), ct.astype(BF16))
            wt_cols, decays = [], []
            for hh in range(2):
                head = 2 * p + hh
                m_st = m_pair[hh]
                b_col = bc[rsl, HEADS + head:HEADS + head + 1]
                i_col = gc[rsl, head:head + 1]
                b_row = br[HEADS + head:HEADS + head + 1, rsl]
                i_row = gr[head:head + 1, rsl]
                dmat = jnp.where(causal, b_col - b_row + i_row, -jnp.inf)
                inter = b_col + m_st
                m_row = jnp.maximum(inter, jnp.max(dmat, axis=-1, keepdims=True))
                wts = jnp.exp(dmat - m_row)
                in_head = (lane >= hh * DK) & (lane < (hh + 1) * DK)
                qm = jnp.where(in_head, q_c, 0.0).astype(BF16)
                sc = _dot_nt(qm, k_cb) * wts
                w_inter = jnp.exp(inter - m_row)
                num = _dot(sc.astype(BF16), vb[rsl, head * DV:(head + 1) * DV]) \
                    + w_inter * inter_mm[:, hh * DV:(hh + 1) * DV]
                den = jnp.sum(sc, axis=-1, keepdims=True) \
                    + w_inter * inter_mm[:, 2 * DV + hh:2 * DV + hh + 1]
                o_ref[rsl, head * DV:(head + 1) * DV] = num / jnp.maximum(jnp.abs(den), jnp.exp(-m_row))
                g_tot = b_col[ML_CHUNK - 1:ML_CHUNK, :]
                tail = g_tot - b_col + i_col
                m_new = jnp.maximum(g_tot + m_st, jnp.max(tail, axis=0, keepdims=True))
                wt_cols.append(jnp.exp(tail - m_new))
                decays.append(jnp.exp(g_tot + m_st - m_new))
                m_pair[hh] = m_new
            wk = (k_c * jnp.where(lane < DK, wt_cols[0], wt_cols[1])).astype(BF16)
            vp = jnp.concatenate([vb[rsl, p * 2 * DV:(p + 1) * 2 * DV], ones_blk], axis=1)
            upd = _dot_tn(vp, wk)
            ct = ct * jnp.where(lane < DK, decays[0], decays[1]) + jnp.where(st_mask, upd, 0.0)
        ct_ref[bb, p] = ct
        for hh in range(2):
            m_ref[bb, 2 * p + hh] = jnp.broadcast_to(m_pair[hh], m_ref.shape[2:])

    ng = ng_ref[...]
    for head in range(HEADS):
        hsl = slice(head * DV, (head + 1) * DV)
        y = _head_norm(o_ref[:, hsl]) * ng[:, hsl] * _sigmoid(o_pre[:, hsl])
        y_ref[:, :, hsl] = y.reshape(nb, lb, DV).astype(y_ref.dtype)


def _ml_mixer(x, w, b, wgc, bgc, wgr, bgr, cw, cb, ng, tri, trit):
    bsz, s, d = x.shape
    wcols = w.shape[1]
    nb = ML_SEQ_PER_STEP
    return pl.pallas_call(
        _ml_kernel,
        grid=(bsz // nb, s // SEQ_BLOCK),
        in_specs=[
            pl.BlockSpec((nb, SEQ_BLOCK, d), lambda i, j: (i, j, 0)),
            _full((d, wcols)), _full((1, wcols)),
            _full((2, d, LANES)), _full((1, LANES)),
            _full((2, 2 * HEADS, d)), _full((2 * HEADS, 1)),
            _full((ML_CONV, 2 * HEADS * DK)), _full((1, 2 * HEADS * DK)),
            _full((1, BRANCH_W)),
            _full((SEQ_BLOCK, SEQ_BLOCK)), _full((SEQ_BLOCK, SEQ_BLOCK)),
        ],
        out_specs=pl.BlockSpec((nb, SEQ_BLOCK, BRANCH_W), lambda i, j: (i, j, 0)),
        out_shape=jax.ShapeDtypeStruct((bsz, s, BRANCH_W), BF16),
        scratch_shapes=[pltpu.VMEM((nb, HEADS // 2, ML_ST_ROWS, PAIR_W), F32),
                        pltpu.VMEM((nb, HEADS, SUBLANES, LANES), F32),
                        pltpu.VMEM((nb, SUBLANES, 2 * HEADS * DK), F32),
                        pltpu.VMEM((nb * SEQ_BLOCK, BRANCH_W), F32)],
        compiler_params=_cparams(("arbitrary", "arbitrary")),
        name="mlstm_mixer",
    )(x, w, b, wgc, bgc, wgr, bgr, cw, cb, ng, tri, trit)


S5_IO_STEPS = 4
S5_PERM_BLOCK = 256
S5_HALF = 256
S5_HALF_STATE = (S5_HALF // S5_GROUP) * S5_STATE


def _s5_kernel(bsz, nc, u_ref, perm_ref, permt_ref, ec_ref, eg_ref, c0_ref, ftc_ref, d_ref, a_ref, y_ref,
               in_ref, ug_ref, xs_ref, xsb_ref):
    s = pl.program_id(1)
    lc = S5_CHUNK
    rows = bsz * nc
    sw = S5_HALF_STATE
    ng = S5_HALF // S5_GROUP
    gw = lc * S5_GROUP
    pb = perm_ref.shape[0]
    cb = pb // lc
    n_io = S5_IO_STEPS
    subs = rows * lc // (n_io * pb)
    nt = sw // LANES

    def pair_mask(shape, which):
        return (lax.broadcasted_iota(jnp.int32, shape, 1) // S5_STATE) % 2 == which

    @pl.when(s < n_io)
    def _():
        for sb in range(subs):
            pu = _dot(perm_ref[...], u_ref[sb * pb:(sb + 1) * pb, :]).astype(BF16)
            r0 = pl.multiple_of((s * subs + sb) * cb, cb)
            for l in range(lc):
                in_ref[l, pl.ds(r0, cb), :] = pu[l * cb:(l + 1) * cb]

    @pl.when(s == n_io)
    def _():
        for g in range(ng):
            ug_ref[g] = jnp.concatenate([in_ref[l][:, g * S5_GROUP:(g + 1) * S5_GROUP] for l in range(lc)],
                                        axis=1)
        for j in range(ng // 2):
            xin = None
            for gi in range(2):
                e_g = ec_ref[2 * j + gi].astype(F32)
                e_g = jnp.where(pair_mask(e_g.shape, gi), e_g, 0.0).astype(BF16)
                part = _dot(ug_ref[2 * j + gi], e_g)
                xin = part if xin is None else xin + part
            xs_ref[j] = xin[:, 0:LANES]
            xs_ref[nt + j] = xin[:, LANES:2 * LANES]
        ar = a_ref[0, 0:1, :]
        ai = a_ref[0, 1:2, :]

        def body(c, carry):
            sr, si = carry
            idx = pl.ds(c, bsz, stride=nc)
            xr = jnp.concatenate([xs_ref[j, idx, :] for j in range(nt)], axis=1)
            xi = jnp.concatenate([xs_ref[nt + j, idx, :] for j in range(nt)], axis=1)
            for j in range(nt):
                xs_ref[j, idx, :] = sr[:, j * LANES:(j + 1) * LANES]
                xs_ref[nt + j, idx, :] = si[:, j * LANES:(j + 1) * LANES]
            return ar * sr - ai * si + xr, ar * si + ai * sr + xi

        zero = jnp.zeros((bsz, sw), F32)
        lax.fori_loop(0, nc, body, (zero, zero), unroll=2)
        for j in range(2 * nt):
            xsb_ref[:, j * LANES:(j + 1) * LANES] = xs_ref[j].astype(BF16)

    @pl.when(s == n_io + 1)
    def _():
        row = lax.broadcasted_iota(jnp.int32, (gw, gw), 0)
        col = lax.broadcasted_iota(jnp.int32, (gw, gw), 1)
        for g in range(ng):
            j = g // 2
            c_rows = jnp.concatenate([c0_ref[g]] * lc, axis=0)
            kw = _dot_nt(eg_ref[g], c_rows)
            kpad = jnp.concatenate([kw, jnp.zeros_like(kw)], axis=0)
            tz = jnp.zeros((gw, gw), F32)
            for lo in range(lc):
                sh = (lc - 1 - lo) * S5_GROUP
                tz = jnp.where(col // S5_GROUP == lo, kpad[sh:sh + gw, :], tz)
            tz = tz + jnp.where(row == col, d_ref[g], 0.0)
            xp = jnp.concatenate([xsb_ref[:, j * LANES:(j + 1) * LANES],
                                  xsb_ref[:, sw + j * LANES:sw + (j + 1) * LANES]], axis=1)
            ft = ftc_ref[g].astype(F32)
            ft = jnp.where(pair_mask(ft.shape, g % 2), ft, 0.0).astype(BF16)
            yg = _dot(ug_ref[g], tz.astype(BF16)) + _dot_nt(xp, ft)
            ug_ref[g] = yg.astype(BF16)
        for l in range(lc):
            in_ref[l] = jnp.concatenate([ug_ref[g][:, l * S5_GROUP:(l + 1) * S5_GROUP] for g in range(ng)],
                                        axis=1)

    @pl.when(s >= n_io + 2)
    def _():
        t = s - (n_io + 2)
        for sb in range(subs):
            r0 = pl.multiple_of((t * subs + sb) * cb, cb)
            ycat = jnp.concatenate([in_ref[l, pl.ds(r0, cb), :] for l in range(lc)], axis=0)
            y_ref[sb * pb:(sb + 1) * pb, :] = _dot(permt_ref[...], ycat).astype(y_ref.dtype)


def _s5_mixer(u, ec, eg, c0, ftc, d, a):
    bsz, s, _ = u.shape
    lc = S5_CHUNK
    nc = s // lc
    t = bsz * s
    nh = BRANCH_W // S5_HALF
    ng = S5_HALF // S5_GROUP
    n_io = S5_IO_STEPS
    tb = t // n_io
    pb = S5_PERM_BLOCK
    src = (np.arange(pb) % (pb // lc)) * lc + np.arange(pb) // (pb // lc)
    perm_np = (src[:, None] == np.arange(pb)[None, :]).astype(np.float32)
    perm, permt = jnp.asarray(perm_np, BF16), jnp.asarray(perm_np.T, BF16)
    once = lambda shape: pl.BlockSpec(shape, lambda h, t: (h,) + (0,) * (len(shape) - 1),
                                      pipeline_mode=pl.Buffered(1))
    return pl.pallas_call(
        functools.partial(_s5_kernel, bsz, nc),
        grid=(nh, 2 * n_io + 2),
        in_specs=[
            pl.BlockSpec((tb, S5_HALF), lambda h, t: (jnp.minimum(t, n_io - 1), h)),
            _full((pb, pb)), _full((pb, pb)),
            once((ng,) + ec.shape[1:]), once((ng,) + eg.shape[1:]), once((ng,) + c0.shape[1:]),
            once((ng,) + ftc.shape[1:]), once((ng,) + d.shape[1:]),
            pl.BlockSpec((1, 2, S5_HALF_STATE), lambda h, t: (h, 0, 0)),
        ],
        out_specs=pl.BlockSpec((tb, S5_HALF), lambda h, t: (jnp.maximum(t - n_io - 2, 0), h)),
        out_shape=jax.ShapeDtypeStruct((t, BRANCH_W), BF16),
        scratch_shapes=[pltpu.VMEM((lc, bsz * nc, S5_HALF), BF16),
                        pltpu.VMEM((ng, bsz * nc, lc * S5_GROUP), BF16),
                        pltpu.VMEM((2 * S5_HALF_STATE // LANES, bsz * nc, LANES), F32),
                        pltpu.VMEM((bsz * nc, 2 * S5_HALF_STATE), BF16)],
        compiler_params=_cparams(("arbitrary", "arbitrary")),
        name="s5_mixer",
    )(u.reshape(t, BRANCH_W), perm, permt, ec, eg, c0, ftc, d, a)


def _s5_tables(a_re, a_im, log_dt, b_re, b_im, c_re, c_im, d_skip):
    nl = a_re.shape[0]
    g, p_, n, lc = S5_GROUPS, S5_STATE, S5_GROUP, S5_CHUNK
    a_re, a_im, b_re, b_im, c_re, c_im = (v.astype(F32) for v in (a_re, a_im, b_re, b_im, c_re, c_im))
    dt = jnp.exp(log_dt.astype(F32))[..., None]
    adt_r = (a_re * dt)[:, :, None, None, :]
    adt_i = (a_im * dt)[:, :, None, None, :]

    def powers(steps):
        st = jnp.asarray(np.asarray(steps, np.float32).reshape(1, 1, -1, 1, 1))
        mag = jnp.exp(adt_r * st)
        return mag * jnp.cos(adt_i * st), mag * jnp.sin(adt_i * st)

    lbr, lbi = powers([1])
    are, aim = a_re[:, :, None, None, :], a_im[:, :, None, None, :]
    den = are * are + aim * aim
    qr = ((lbr - 1.0) * are + lbi * aim) / den
    qi = (lbi * are - (lbr - 1.0) * aim) / den
    bt_re, bt_im = b_re.transpose(0, 1, 3, 2)[:, :, None], b_im.transpose(0, 1, 3, 2)[:, :, None]
    bbr = qr * bt_re - qi * bt_im
    bbi = qr * bt_im + qi * bt_re
    cr, ci = c_re[:, :, None], c_im[:, :, None]
    pr, pi = powers(np.arange(lc - 1, -1, -1))
    er, ei = pr * bbr - pi * bbi, pr * bbi + pi * bbr
    ec = jnp.concatenate([er, er, ei, ei], axis=-1).reshape(nl, g, lc * n, 4 * p_)
    eg = jnp.concatenate([er, ei], axis=-1).reshape(nl, g, lc * n, 2 * p_)
    pfr, pfi = powers(np.arange(1, lc + 1))
    fr, fi = cr * pfr - ci * pfi, -(cr * pfi + ci * pfr)
    ftc = jnp.concatenate([fr, fr, fi, fi], axis=-1).reshape(nl, g, lc * n, 4 * p_)
    c0 = jnp.concatenate([c_re, -c_im], axis=-1)
    dd = jnp.tile(d_skip.astype(F32).reshape(nl, g, 1, n), (1, 1, 1, lc))
    ar, ai = powers([lc])
    nh = BRANCH_W // S5_HALF
    a_tab = jnp.stack([ar.reshape(nl, nh, S5_HALF_STATE), ai.reshape(nl, nh, S5_HALF_STATE)], axis=2)
    return ec.astype(BF16), eg.astype(BF16), c0.astype(BF16), ftc.astype(BF16), dd, a_tab


def _merge_kernel(x_ref, yg_ref, y5_ref, ym_ref, wgate_ref, bgate_ref, wglu_ref, bglu_ref,
                  wup_ref, wo_ref, g_ref, b_ref, o_ref):
    x = x_ref[...]
    xb = x.astype(BF16)
    y5 = _gelu_tanh(y5_ref[...].astype(F32))
    y5 = y5 * _sigmoid(_dot(y5.astype(BF16), wglu_ref[...]) + bglu_ref[...])
    ys = (yg_ref[...], y5.astype(BF16), ym_ref[...])
    acc = None
    for r in range(N_BRANCH):
        gate = _sigmoid(_dot(xb, wgate_ref[:, r * D_MODEL:(r + 1) * D_MODEL])
                        + bgate_ref[:, r * D_MODEL:(r + 1) * D_MODEL])
        term = gate * _dot(ys[r], wup_ref[r])
        acc = term if acc is None else acc + term
    mix = _dot(acc.astype(BF16), wo_ref[...])
    o_ref[...] = _layer_norm(DN_ALPHA * x + mix, g_ref[...], b_ref[...])


def _merge(x2, yg, y5, ym, wgate, bgate, wglu, bglu, wup, wo, g, b, tm=512):
    t, dm = x2.shape
    row = lambda w: pl.BlockSpec((tm, w), lambda i: (i, 0))
    return pl.pallas_call(
        _merge_kernel,
        grid=(t // tm,),
        in_specs=[row(dm), row(BRANCH_W), row(BRANCH_W), row(BRANCH_W),
                  _full(wgate.shape), _full(bgate.shape), _full(wglu.shape), _full(bglu.shape),
                  _full(wup.shape), _full(wo.shape), _full(g.shape), _full(b.shape)],
        out_specs=row(dm),
        out_shape=jax.ShapeDtypeStruct((t, dm), F32),
        compiler_params=_cparams(("parallel",)),
        name="merge_ln1",
    )(x2, yg, y5, ym, wgate, bgate, wglu, bglu, wup, wo, g, b)


def _ple_ln2(x, xb, f, p_ref, pwg_ref, pwp_ref, g_ref, b_ref):
    e = _sigmoid(_dot(xb, pwg_ref[...])) * _dot(p_ref[...].astype(BF16), pwp_ref[...])
    return _layer_norm(DN_ALPHA * x + f + e, g_ref[...], b_ref[...])


def _ffn_kernel(x_ref, p_ref, wg_ref, wu_ref, wd_ref, pwg_ref, pwp_ref, g_ref, b_ref, o_ref):
    x = x_ref[...]
    xb = x.astype(BF16)
    hid = (_silu(_dot(xb, wg_ref[...])) * _dot(xb, wu_ref[...])).astype(BF16)
    f = _dot(hid, wd_ref[...])
    o_ref[...] = _ple_ln2(x, xb, f, p_ref, pwg_ref, pwp_ref, g_ref, b_ref)


def _ffn_layer(x2, p3, layer, wg, wu, wd, pwg, pwp, g, b, tm=256):
    t, dm = x2.shape
    row = lambda w: pl.BlockSpec((tm, w), lambda i: (i, 0))
    return pl.pallas_call(
        _ffn_kernel,
        grid=(t // tm,),
        in_specs=[row(dm), pl.BlockSpec((None, tm, PLE_DIM), lambda i: (layer, i, 0)),
                  _full(wg.shape), _full(wu.shape), _full(wd.shape),
                  _full(pwg.shape), _full(pwp.shape), _full(g.shape), _full(b.shape)],
        out_specs=row(dm),
        out_shape=jax.ShapeDtypeStruct((t, dm), F32),
        compiler_params=_cparams(("parallel",)),
        name="ffn_ple_ln2",
    )(x2, p3, wg, wu, wd, pwg, pwp, g, b)


def _router_kernel(x_ref, w_ref, b_ref, tril_ref, sel_ref, pr_ref, rk_ref, cnt_ref, base_ref):
    @pl.when(pl.program_id(0) == 0)
    def _():
        base_ref[...] = jnp.zeros_like(base_ref)

    xh, xl = _split2(x_ref[...])
    logits = _dot(xh, w_ref[0]) + _dot(xl, w_ref[0]) + _dot(xh, w_ref[1]) + b_ref[...]
    lane = lax.broadcasted_iota(jnp.int32, logits.shape, 1)
    neg = -jnp.inf
    logits = jnp.where(lane < N_EXPERTS, logits, neg)
    m1 = jnp.max(logits, axis=-1, keepdims=True)
    i1 = jnp.min(jnp.where(logits == m1, lane, LANES), axis=-1, keepdims=True)
    rest = jnp.where(lane == i1, neg, logits)
    m2 = jnp.max(rest, axis=-1, keepdims=True)
    i2 = jnp.min(jnp.where(rest == m2, lane, LANES), axis=-1, keepdims=True)
    e2 = jnp.exp(m2 - m1)
    p1 = 1.0 / (1.0 + e2)
    p2 = e2 / (1.0 + e2)
    hot = jnp.where((lane == i1) | (lane == i2), 1.0, 0.0)
    base = base_ref[0:1, :]
    before = _dot(tril_ref[...], hot.astype(BF16)) + base
    r1 = jnp.sum(jnp.where(lane == i1, before, 0.0), axis=-1, keepdims=True)
    r2 = jnp.sum(jnp.where(lane == i2, before, 0.0), axis=-1, keepdims=True)
    sel_ref[...] = jnp.where(lane == 0, i1, jnp.where(lane == 1, i2, 0))
    pr_ref[...] = jnp.where(lane == 0, p1, jnp.where(lane == 1, p2, 0.0))
    rk_ref[...] = jnp.where(lane == 0, r1, jnp.where(lane == 1, r2, 0.0))
    total = base + jnp.sum(hot, axis=0, keepdims=True)
    base_ref[...] = jnp.broadcast_to(total, base_ref.shape)
    cnt_ref[...] = jnp.broadcast_to(total, cnt_ref.shape)


def _router(x2, w, b, tm=512):
    t, dm = x2.shape
    i = np.arange(tm)
    tril = jnp.asarray((i[:, None] > i[None, :]).astype(np.float32), BF16)
    blk = pl.BlockSpec((tm, LANES), lambda i: (i, 0))
    return pl.pallas_call(
        _router_kernel,
        grid=(t // tm,),
        in_specs=[pl.BlockSpec((tm, dm), lambda i: (i, 0)), _full(w.shape), _full(b.shape), _full((tm, tm))],
        out_specs=[blk, blk, blk, pl.BlockSpec((SUBLANES, LANES), lambda i: (0, 0))],
        out_shape=[jax.ShapeDtypeStruct((t, LANES), jnp.int32), jax.ShapeDtypeStruct((t, LANES), F32),
                   jax.ShapeDtypeStruct((t, LANES), F32), jax.ShapeDtypeStruct((SUBLANES, LANES), F32)],
        scratch_shapes=[pltpu.VMEM((SUBLANES, LANES), F32)],
        compiler_params=_cparams(("arbitrary",)),
        name="moe_router",
    )(x2, w, b, tril)


TOP_K = 2
MOE_TILE = 1024
MOE_SUB = 256
MOE_FF_TILE = 512
DISPATCH_TILE = 512
DMA_ISSUE_UNROLL = 16


ROW_SUB = D_MODEL // LANES


def _row_copy_wait(src_rows, dst_rows, sem):
    pltpu.make_async_copy(src_rows, dst_rows, sem).wait()


def _rows_to_tiles(dst_ref, val):
    n = val.shape[0]
    for j in range(ROW_SUB):
        dst_ref[pl.ds(j, n, stride=ROW_SUB), :] = val[:, j * LANES:(j + 1) * LANES]


def _tiles_to_rows(src_ref, n):
    return jnp.concatenate([src_ref[pl.ds(j, n, stride=ROW_SUB), :] for j in range(ROW_SUB)], axis=1)


def _row_tile(ref, r):
    return ref.at[pl.ds(pl.multiple_of(r * ROW_SUB, ROW_SUB), ROW_SUB), :]


def _dispatch_kernel(pos_ref, x_ref, xs_ref, stage_ref, sem):
    tm = x_ref.shape[0]
    base = pl.program_id(0) * tm * TOP_K
    _rows_to_tiles(stage_ref, x_ref[...])

    def body(r, carry):
        for k in range(TOP_K):
            dst = pos_ref[base + r * TOP_K + k]
            pltpu.make_async_copy(_row_tile(stage_ref, r), _row_tile(xs_ref, dst), sem).start(priority=k)
        return carry

    lax.fori_loop(0, tm, body, 0, unroll=DMA_ISSUE_UNROLL)
    for _ in range(TOP_K):
        _row_copy_wait(stage_ref, xs_ref.at[pl.ds(0, tm * ROW_SUB), :], sem)


def _dispatch(pos_flat, x2, tm=DISPATCH_TILE):
    t, dm = x2.shape
    return pl.pallas_call(
        _dispatch_kernel,
        grid_spec=pltpu.PrefetchScalarGridSpec(
            num_scalar_prefetch=1,
            grid=(t // tm,),
            in_specs=[pl.BlockSpec((tm, dm), lambda i, pos: (i, 0))],
            out_specs=pl.BlockSpec(memory_space=pl.ANY),
            scratch_shapes=[pltpu.VMEM((tm * ROW_SUB, LANES), F32), pltpu.SemaphoreType.DMA],
        ),
        out_shape=jax.ShapeDtypeStruct((t * TOP_K * ROW_SUB, LANES), F32),
        compiler_params=_cparams(("arbitrary",)),
        name="moe_dispatch",
    )(pos_flat, x2)


def _moe_group_kernel(tile_ref, exp_ref, lo_ref, hi_ref, xs_ref, wg_ref, wu_ref, wd_ref, o_ref,
                      acc_ref, xb_ref, wgb_ref, wub_ref, wdb_ref):
    i = pl.program_id(0)
    f = pl.program_id(1)
    lo = lo_ref[i]
    hi = hi_ref[i]
    tm = acc_ref.shape[0]

    @pl.when(hi > lo)
    def _():
        @pl.when((lo == 0) & (f == 0))
        def _():
            acc_ref[...] = jnp.zeros_like(acc_ref)

        @pl.when(f == 0)
        def _():
            xb_ref[...] = _tiles_to_rows(xs_ref, tm).astype(BF16)

        whole = (lo == 0) & (hi == tm)

        @pl.when(whole)
        def _():
            xb = xb_ref[...]
            hid = (_silu(_dot(xb, wg_ref[0].astype(BF16))) * _dot(xb, wu_ref[0].astype(BF16))).astype(BF16)
            acc_ref[...] += _dot(hid, wd_ref[0].astype(BF16))

        @pl.when(jnp.logical_not(whole))
        def _():
            wgb_ref[...] = wg_ref[0].astype(BF16)
            wub_ref[...] = wu_ref[0].astype(BF16)
            wdb_ref[...] = wd_ref[0].astype(BF16)
            for sub in range(tm // MOE_SUB):
                @pl.when((hi > sub * MOE_SUB) & (lo < (sub + 1) * MOE_SUB))
                def _(sub=sub):
                    rsl = slice(sub * MOE_SUB, (sub + 1) * MOE_SUB)
                    xb = xb_ref[rsl, :]
                    hid = _silu(_dot(xb, wgb_ref[...])) * _dot(xb, wub_ref[...])
                    rid = lax.broadcasted_iota(jnp.int32, (MOE_SUB, 1), 0) + sub * MOE_SUB
                    hid = jnp.where((rid >= lo) & (rid < hi), hid, 0.0).astype(BF16)
                    acc_ref[rsl, :] += _dot(hid, wdb_ref[...])

        @pl.when((hi == tm) & (f == pl.num_programs(1) - 1))
        def _():
            _rows_to_tiles(o_ref, acc_ref[...])


def _moe_grouped(items, xs, wg, wu, wd, tm=MOE_TILE, tf=MOE_FF_TILE):
    tile, exp, lo, hi = items
    dm = wg.shape[1]
    dff = wg.shape[2]
    return pl.pallas_call(
        _moe_group_kernel,
        grid_spec=pltpu.PrefetchScalarGridSpec(
            num_scalar_prefetch=4,
            grid=(tile.shape[0], dff // tf),
            in_specs=[pl.BlockSpec((tm * ROW_SUB, LANES), lambda i, f, tl, ex, lo_, hi_: (tl[i], 0)),
                      pl.BlockSpec((1, dm, tf), lambda i, f, tl, ex, lo_, hi_: (ex[i], 0, f)),
                      pl.BlockSpec((1, dm, tf), lambda i, f, tl, ex, lo_, hi_: (ex[i], 0, f)),
                      pl.BlockSpec((1, tf, dm), lambda i, f, tl, ex, lo_, hi_: (ex[i], f, 0))],
            out_specs=pl.BlockSpec((tm * ROW_SUB, LANES), lambda i, f, tl, ex, lo_, hi_: (tl[i], 0)),
            scratch_shapes=[pltpu.VMEM((tm, dm), F32), pltpu.VMEM((tm, dm), BF16), pltpu.VMEM((dm, tf), BF16),
                            pltpu.VMEM((dm, tf), BF16), pltpu.VMEM((tf, dm), BF16)],
        ),
        out_shape=jax.ShapeDtypeStruct(xs.shape, F32),
        compiler_params=_cparams(("arbitrary", "arbitrary")),
        name="moe_grouped",
    )(tile, exp, lo, hi, xs, wg, wu, wd)


def _moe_items(counts, n_rows, tm=MOE_TILE):
    n_tiles = n_rows // tm
    ends = jnp.cumsum(counts)
    cuts = jnp.sort(jnp.concatenate([jnp.arange(n_tiles + 1, dtype=jnp.int32) * tm, ends[:-1]]))
    start, stop = cuts[:-1], cuts[1:]
    tile = jnp.minimum(start // tm, n_tiles - 1)
    exp = jnp.minimum(jnp.sum(ends[None, :] <= start[:, None], axis=1), N_EXPERTS - 1)
    lo = start - tile * tm
    hi = stop - tile * tm
    return tile.astype(jnp.int32), exp.astype(jnp.int32), lo.astype(jnp.int32), hi.astype(jnp.int32)


def _combine_kernel(pos_ref, x_ref, pr_ref, p_ref, pwg_ref, pwp_ref, g_ref, b_ref, ys_ref, o_ref, gat_ref, sem):
    tm = x_ref.shape[0]
    base = pl.program_id(0) * tm * TOP_K

    def body(r, carry):
        for k in range(TOP_K):
            src = pos_ref[base + r * TOP_K + k]
            pltpu.make_async_copy(_row_tile(ys_ref, src), _row_tile(gat_ref.at[k], r), sem).start(priority=k)
        return carry

    lax.fori_loop(0, tm, body, 0, unroll=DMA_ISSUE_UNROLL)
    x = x_ref[...]
    xb = x.astype(BF16)
    e = _sigmoid(_dot(xb, pwg_ref[...])) * _dot(p_ref[...].astype(BF16), pwp_ref[...])
    for k in range(TOP_K):
        _row_copy_wait(ys_ref.at[pl.ds(0, tm * ROW_SUB), :], gat_ref.at[k], sem)
    pr = pr_ref[...]
    f = pr[:, 0:1] * _tiles_to_rows(gat_ref.at[0], tm)
    for k in range(1, TOP_K):
        f = f + pr[:, k:k + 1] * _tiles_to_rows(gat_ref.at[k], tm)
    o_ref[...] = _layer_norm(DN_ALPHA * x + f + e, g_ref[...], b_ref[...])


def _combine_layer(pos_flat, x2, pr, p3, layer, pwg, pwp, g, b, ys, tm=DISPATCH_TILE):
    t, dm = x2.shape
    row = lambda w: pl.BlockSpec((tm, w), lambda i, pos: (i, 0))
    full = lambda a: pl.BlockSpec(a.shape, lambda i, pos: (0,) * a.ndim, pipeline_mode=pl.Buffered(1))
    return pl.pallas_call(
        _combine_kernel,
        grid_spec=pltpu.PrefetchScalarGridSpec(
            num_scalar_prefetch=1,
            grid=(t // tm,),
            in_specs=[row(dm), row(LANES), pl.BlockSpec((None, tm, PLE_DIM), lambda i, pos: (layer, i, 0)),
                      full(pwg), full(pwp), full(g), full(b),
                      pl.BlockSpec(memory_space=pl.ANY)],
            out_specs=row(dm),
            scratch_shapes=[pltpu.VMEM((TOP_K, tm * ROW_SUB, LANES), F32), pltpu.SemaphoreType.DMA],
        ),
        out_shape=jax.ShapeDtypeStruct((t, dm), F32),
        compiler_params=_cparams(("arbitrary",)),
        name="moe_combine_ple_ln2",
    )(pos_flat, x2, pr, p3, pwg, pwp, g, b, ys)


def _row(v):
    return v.reshape(1, -1).astype(F32)


def _pad_lanes(w):
    return jnp.pad(w, ((0, 0), (0, LANES - w.shape[1])))


def kernel(x, p, w_in, b_in, gla_w_a2, gla_b_a2, gla_norm_g, s5_a_re, s5_a_im, s5_log_dt, s5_b_re, s5_b_im,
           s5_c_re, s5_c_im, s5_d, s5_w_glu, s5_b_glu, ml_conv_w, ml_conv_b, ml_norm_g, w_up, w_o, ln1_g, ln1_b,
           ffn_wg, ffn_wu, ffn_wd, moe_router, moe_router_b, moe_wg, moe_wu, moe_wd, ple_w_gate, ple_w_proj,
           ln2_g, ln2_b):
    bsz, s, dm = x.shape
    t = bsz * s
    hi = lax.Precision.HIGHEST
    o = IN_OFF
    tri = jnp.asarray(_chunk_tri(SEQ_BLOCK, CHUNK), BF16)
    ml_tri_np = _chunk_tri(SEQ_BLOCK, ML_CHUNK)
    ml_tri = jnp.asarray(ml_tri_np, BF16)
    ml_trit = jnp.asarray(ml_tri_np.T, BF16)
    s5_ec, s5_eg, s5_c0, s5_ftc, s5_dd, s5_atab = _s5_tables(s5_a_re, s5_a_im, s5_log_dt, s5_b_re, s5_b_im,
                                                             s5_c_re, s5_c_im, s5_d)
    p3 = p.reshape(DEPTH, t, PLE_DIM)

    w_in_b = w_in.astype(BF16)
    for i in range(DEPTH):
        w, b = w_in_b[i], b_in[i]
        sl = lambda k: (w[:, o[k]:o[k + 1]], b[o[k]:o[k + 1]])
        (wq, bq), (wk, bk), (wv, bv), (_, ba), (wg_, bg_) = sl(0), sl(1), sl(2), sl(3), sl(4)
        wz = jnp.dot(w_in[i, :, o[3]:o[4]], gla_w_a2[i], precision=hi).astype(BF16)
        bz = jnp.dot(ba, gla_w_a2[i], precision=hi) + gla_b_a2[i]
        (wu_, bu_) = sl(5)
        w_gla = jnp.concatenate([wq, wk, wz, wv, wg_, wu_], axis=1)
        b_gla = _row(jnp.concatenate([bq, bk, bz, bv, bg_, bu_]))
        y_gla, u = _gla_mixer(x, w_gla, b_gla, _row(gla_norm_g[i]), tri)

        x2 = x.reshape(t, dm)
        y5 = _s5_mixer(u, s5_ec[i], s5_eg[i], s5_c0[i], s5_ftc[i], s5_dd[i], s5_atab[i])

        (wmq, bmq), (wmk, bmk), (wmv, bmv), (wmo, bmo) = sl(6), sl(7), sl(8), sl(11)
        w_ml = jnp.concatenate([wmq, wmk, wmv, wmo], axis=1)
        b_ml = _row(jnp.concatenate([bmq, bmk, bmv, bmo]))
        w_if = w_in[i, :, o[9]:o[11]]
        b_if = b[o[9]:o[11]]
        if_h = w_if.astype(BF16)
        if_l = (w_if - if_h.astype(F32)).astype(BF16)
        wgc = jnp.stack([_pad_lanes(if_h), _pad_lanes(if_l)])
        wgr = jnp.stack([if_h.T, if_l.T])
        y_ml = _ml_mixer(x, w_ml, b_ml, wgc, _pad_lanes(_row(b_if)), wgr, b_if.reshape(-1, 1).astype(F32),
                         ml_conv_w[i].astype(F32), _row(ml_conv_b[i]), _row(ml_norm_g[i]), ml_tri, ml_trit)

        (wgt, bgt) = sl(12)
        x1 = _merge(x2, y_gla.reshape(t, BRANCH_W), y5, y_ml.reshape(t, BRANCH_W),
                    wgt, _row(bgt), s5_w_glu[i].astype(BF16), _row(s5_b_glu[i]),
                    w_up[i].astype(BF16), w_o[i].astype(BF16), _row(ln1_g[i]), _row(ln1_b[i]))

        pwg = ple_w_gate[i].astype(BF16)
        pwp = ple_w_proj[i].astype(BF16)
        j = i // 2
        if i % 2 == 0:
            x2n = _ffn_layer(x1, p3, i, ffn_wg[j].astype(BF16), ffn_wu[j].astype(BF16), ffn_wd[j].astype(BF16),
                             pwg, pwp, _row(ln2_g[i]), _row(ln2_b[i]))
        else:
            wr = moe_router[j]
            wr_h = wr.astype(BF16)
            wr_l = (wr - wr_h.astype(F32)).astype(BF16)
            sel, pr, rk, cnt = _router(x1, jnp.stack([_pad_lanes(wr_h), _pad_lanes(wr_l)]),
                                       _pad_lanes(_row(moe_router_b[j])))
            counts = cnt[0, :N_EXPERTS].astype(jnp.int32)
            starts = jnp.cumsum(counts) - counts
            sel2 = sel[:, :TOP_K]
            pos = rk[:, :TOP_K].astype(jnp.int32) + jnp.sum(
                jnp.where(sel2[..., None] == jnp.arange(N_EXPERTS), starts, 0), axis=-1)
            pos_flat = pos.reshape(-1)
            xs = _dispatch(pos_flat, x1)
            ys = _moe_grouped(_moe_items(counts, t * TOP_K), xs, moe_wg[j], moe_wu[j], moe_wd[j])
            x2n = _combine_layer(pos_flat, x1, pr, p3, i, pwg, pwp, _row(ln2_g[i]), _row(ln2_b[i]), ys)
        x = x2n.reshape(bsz, s, dm)
    return x
```

```python
import functools
import math

import numpy as np
import jax
import jax.numpy as jnp
from jax import lax
from jax.experimental import pallas as pl
from jax.experimental.pallas import tpu as pltpu

F32 = jnp.float32
BF16 = jnp.bfloat16

D_MODEL = 1024
DEPTH = 2
N_BRANCH = 3
BRANCH_W = 512
HEADS = 4
DK = 64
DV = BRANCH_W // HEADS
GLA_RANK = 16
GLA_TAU = 16.0
CHUNK = 64
ML_CHUNK = 256
S5_GROUP = 16
S5_GROUPS = BRANCH_W // S5_GROUP
S5_STATE = 64
S5_CHUNK = 16
ML_CONV = 4
N_EXPERTS = 8
PLE_DIM = 256
DN_ALPHA = (2.0 * DEPTH) ** 0.25
LN_EPS = 1e-5

IN_WIDTHS = (
    HEADS * DK, HEADS * DK, BRANCH_W, GLA_RANK, BRANCH_W,
    BRANCH_W,
    HEADS * DK, HEADS * DK, BRANCH_W, HEADS, HEADS, BRANCH_W,
    N_BRANCH * D_MODEL,
)
IN_OFF = tuple(int(o) for o in np.concatenate([[0], np.cumsum(IN_WIDTHS)]))

LANES = 128
SUBLANES = 8
SEQ_BLOCK = 256
GLA_SEQ_PER_STEP = 4
ML_SEQ_PER_STEP = 1
PAIR_W = 2 * DK
VMEM_LIMIT = 56 * 1024 * 1024


def _cparams(sem):
    return pltpu.CompilerParams(dimension_semantics=sem, vmem_limit_bytes=VMEM_LIMIT)


def _dot(a, b):
    return jnp.dot(a, b, preferred_element_type=F32)


def _dot_nt(a, b):
    return lax.dot_general(a, b, (((1,), (1,)), ((), ())), preferred_element_type=F32)


def _dot_tn(a, b):
    return lax.dot_general(a, b, (((0,), (0,)), ((), ())), preferred_element_type=F32)


def _split3(a):
    hi = a.astype(BF16)
    r = a - hi.astype(F32)
    mid = r.astype(BF16)
    lo = (r - mid.astype(F32)).astype(BF16)
    return hi, mid, lo


def _split2(a):
    hi = a.astype(BF16)
    lo = (a - hi.astype(F32)).astype(BF16)
    return hi, lo


def _log_sigmoid(x):
    return jnp.minimum(x, 0.0) - jnp.log(1.0 + jnp.exp(-jnp.abs(x)))


def _sigmoid(x):
    return 0.5 * jnp.tanh(0.5 * x) + 0.5


def _silu(x):
    return x * _sigmoid(x)


def _gelu_tanh(x):
    return 0.5 * x * (1.0 + jnp.tanh(math.sqrt(2.0 / math.pi) * (x + 0.044715 * (x * x * x))))


def _layer_norm(v, g, b):
    mu = jnp.mean(v, axis=-1, keepdims=True)
    c = v - mu
    var = jnp.mean(c * c, axis=-1, keepdims=True)
    return c * lax.rsqrt(var + LN_EPS) * g + b


def _head_norm(o):
    mu = jnp.mean(o, axis=-1, keepdims=True)
    c = o - mu
    var = jnp.mean(c * c, axis=-1, keepdims=True)
    return c * lax.rsqrt(var + LN_EPS)


def _chunk_tri(n, chunk):
    i = np.arange(n)
    return ((i[:, None] >= i[None, :]) & (i[:, None] // chunk == i[None, :] // chunk)).astype(np.float32)


def _full(shape):
    nd = len(shape)
    return pl.BlockSpec(shape, lambda *_: (0,) * nd, pipeline_mode=pl.Buffered(1))


def _gla_kernel(x_ref, w_ref, b_ref, ng_ref, tri_ref, y_ref, u_ref, st_ref, o_ref):
    @pl.when(pl.program_id(1) == 0)
    def _():
        st_ref[...] = jnp.zeros_like(st_ref)

    hk = HEADS * DK
    nb, lb, d = x_ref.shape
    xb = x_ref[...].reshape(nb * lb, d).astype(BF16)
    h = _dot(xb, w_ref[...]) + b_ref[...]
    q = h[:, 0:hk]
    k = h[:, hk:2 * hk]
    z = h[:, 2 * hk:3 * hk]
    v = h[:, 3 * hk:3 * hk + BRANCH_W]
    g = h[:, 3 * hk + BRANCH_W:3 * hk + 2 * BRANCH_W]
    u_ref[...] = h[:, 3 * hk + 2 * BRANCH_W:3 * hk + 3 * BRANCH_W].reshape(nb, lb, BRANCH_W).astype(u_ref.dtype)

    la = _log_sigmoid(z) * (1.0 / GLA_TAU)
    tri = tri_ref[...]
    la_h, la_m, la_l = _split3(la)
    cum = jnp.concatenate(
        [_dot(tri, la_h[r:r + lb]) + _dot(tri, la_m[r:r + lb]) + _dot(tri, la_l[r:r + lb])
         for r in range(0, nb * lb, lb)], axis=0)
    qd = q * (DK ** -0.5) * jnp.exp(cum)
    ki = k * jnp.exp(-cum)

    lane = lax.broadcasted_iota(jnp.int32, (1, PAIR_W), 1)
    row_i = lax.broadcasted_iota(jnp.int32, (CHUNK, 2 * CHUNK), 0)
    col_i = lax.broadcasted_iota(jnp.int32, (CHUNK, 2 * CHUNK), 1)
    causal = row_i >= col_i % CHUNK
    bd_r = lax.broadcasted_iota(jnp.int32, (2 * DV, PAIR_W), 0)
    bd_c = lax.broadcasted_iota(jnp.int32, (2 * DV, PAIR_W), 1)
    blockdiag = (bd_r >= DV) == (bd_c >= DK)
    vb_r = lax.broadcasted_iota(jnp.int32, (2 * CHUNK, 2 * DV), 0)
    vb_c = lax.broadcasted_iota(jnp.int32, (2 * CHUNK, 2 * DV), 1)
    v_blocks = (vb_r >= CHUNK) == (vb_c >= DV)

    for c in range(lb // CHUNK):
        for bb, p in [(bb, p) for bb in range(nb) for p in range(HEADS // 2)]:
            st = st_ref[bb, p]
            lsl = slice(p * PAIR_W, (p + 1) * PAIR_W)
            rsl = slice(bb * lb + c * CHUNK, bb * lb + (c + 1) * CHUNK)
            qd_c = qd[rsl, lsl]
            ki_c = ki[rsl, lsl]
            cum_c = cum[rsl, lsl]
            last = cum_c[CHUNK - 1:CHUNK, :]
            kt = (k[rsl, lsl] * jnp.exp(last - cum_c)).astype(BF16)
            qd_b = qd_c.astype(BF16)
            inter = _dot_nt(qd_b, st.astype(BF16))
            k_bd = jnp.concatenate([jnp.where(lane < DK, ki_c, 0.0), jnp.where(lane >= DK, ki_c, 0.0)],
                                   axis=0).astype(BF16)
            att = jnp.where(causal, _dot_nt(qd_b, k_bd), 0.0)
            v_c = v[rsl, p * 2 * DV:(p + 1) * 2 * DV]
            v_bd = jnp.where(v_blocks, jnp.concatenate([v_c, v_c], axis=0), 0.0).astype(BF16)
            o_ref[rsl, p * 2 * DV:(p + 1) * 2 * DV] = _dot(att.astype(BF16), v_bd) + inter
            upd = _dot_tn(v_c.astype(BF16), kt)
            st_ref[bb, p] = st * jnp.exp(last) + jnp.where(blockdiag, upd, 0.0)

    ng = ng_ref[...]
    for head in range(HEADS):
        hsl = slice(head * DV, (head + 1) * DV)
        y = _head_norm(o_ref[:, hsl]) * ng[:, hsl] * _silu(g[:, hsl])
        y_ref[:, :, hsl] = y.reshape(nb, lb, DV).astype(y_ref.dtype)


def _gla_mixer(x, w, b, ng, tri):
    bsz, s, d = x.shape
    wcols = w.shape[1]
    nb = GLA_SEQ_PER_STEP
    return pl.pallas_call(
        _gla_kernel,
        grid=(bsz // nb, s // SEQ_BLOCK),
        in_specs=[
            pl.BlockSpec((nb, SEQ_BLOCK, d), lambda i, j: (i, j, 0)),
            _full((d, wcols)), _full((1, wcols)), _full((1, BRANCH_W)),
            _full((SEQ_BLOCK, SEQ_BLOCK)),
        ],
        out_specs=[pl.BlockSpec((nb, SEQ_BLOCK, BRANCH_W), lambda i, j: (i, j, 0))] * 2,
        out_shape=[jax.ShapeDtypeStruct((bsz, s, BRANCH_W), BF16)] * 2,
        scratch_shapes=[pltpu.VMEM((nb, HEADS // 2, 2 * DV, PAIR_W), F32),
                        pltpu.VMEM((nb * SEQ_BLOCK, BRANCH_W), F32)],
        compiler_params=_cparams(("arbitrary", "arbitrary")),
        name="gla_mixer",
    )(x, w, b, ng, tri)


ML_ST_ROWS = 2 * DV + LANES


def _ml_kernel(x_ref, w_ref, b_ref, wgc_ref, bgc_ref, wgr_ref, bgr_ref, cw_ref, cb_ref, ng_ref,
               tri_ref, trit_ref, y_ref, ct_ref, m_ref, carry_ref, o_ref):
    @pl.when(pl.program_id(1) == 0)
    def _():
        ct_ref[...] = jnp.zeros_like(ct_ref)
        m_ref[...] = jnp.zeros_like(m_ref)
        carry_ref[...] = jnp.zeros_like(carry_ref)

    hk = HEADS * DK
    nb, lb, d = x_ref.shape
    x32 = x_ref[...].reshape(nb * lb, d)
    xh, xl = _split2(x32)
    h = _dot(xh, w_ref[...]) + b_ref[...]
    qk = h[:, 0:2 * hk]
    v = h[:, 2 * hk:2 * hk + BRANCH_W]
    o_pre = h[:, 2 * hk + BRANCH_W:2 * hk + 2 * BRANCH_W]

    cw = cw_ref[...]
    conv = []
    for bb in range(nb):
        qk_b = qk[bb * lb:(bb + 1) * lb]
        ext = jnp.concatenate([carry_ref[bb], qk_b], axis=0)
        first = SUBLANES - (ML_CONV - 1)
        acc = cb_ref[...] + ext[first:first + lb] * cw[0:1]
        for j in range(1, ML_CONV):
            acc = acc + ext[first + j:first + j + lb] * cw[j:j + 1]
        carry_ref[bb] = qk_b[lb - SUBLANES:lb]
        conv.append(acc)
    qkc = _silu(jnp.concatenate(conv, axis=0))
    qf = qkc[:, 0:hk]
    kf = qkc[:, hk:2 * hk] * (DK ** -0.5)
    vb = v.astype(BF16)

    gc = (_dot(xh, wgc_ref[0]) + _dot(xl, wgc_ref[0]) + _dot(xh, wgc_ref[1])) + bgc_ref[...]
    gr = (_dot_nt(wgr_ref[0], xh) + _dot_nt(wgr_ref[0], xl) + _dot_nt(wgr_ref[1], xh)) + bgr_ref[...]
    lf_c = _log_sigmoid(gc)
    lf_r = _log_sigmoid(gr)
    tri = tri_ref[...]
    trit = trit_ref[...]
    c_h, c_m, c_l = _split3(lf_c)
    r_h, r_m, r_l = _split3(lf_r)
    blocks = [slice(r, r + lb) for r in range(0, nb * lb, lb)]
    bc = jnp.concatenate([_dot(tri, c_h[r]) + _dot(tri, c_m[r]) + _dot(tri, c_l[r]) for r in blocks],
                         axis=0)
    br = jnp.concatenate([_dot(r_h[:, r], trit) + _dot(r_m[:, r], trit) + _dot(r_l[:, r], trit)
                          for r in blocks], axis=1)

    lane = lax.broadcasted_iota(jnp.int32, (1, PAIR_W), 1)
    row_i = lax.broadcasted_iota(jnp.int32, (ML_CHUNK, ML_CHUNK), 0)
    col_i = lax.broadcasted_iota(jnp.int32, (ML_CHUNK, ML_CHUNK), 1)
    causal = row_i >= col_i
    sr = lax.broadcasted_iota(jnp.int32, (ML_ST_ROWS, PAIR_W), 0)
    sc_ = lax.broadcasted_iota(jnp.int32, (ML_ST_ROWS, PAIR_W), 1)
    first = sc_ < DK
    rows_h0 = (sr < DV) | (sr == 2 * DV)
    rows_h1 = ((sr >= DV) & (sr < 2 * DV)) | (sr == 2 * DV + 1)
    st_mask = (rows_h0 & first) | (rows_h1 & ~first)
    ones_blk = jnp.ones((ML_CHUNK, LANES), BF16)

    for bb, p in [(bb, p) for bb in range(nb) for p in range(HEADS // 2)]:
        ct = ct_ref[bb, p]
        lsl = slice(p * PAIR_W, (p + 1) * PAIR_W)
        m_pair = [m_ref[bb, 2 * p + hh][0:1, 0:1] for hh in range(2)]
        for c in range(lb // ML_CHUNK):
            rsl = slice(bb * lb + c * ML_CHUNK, bb * lb + (c + 1) * ML_CHUNK)
            q_c = qf[rsl, lsl]
            k_c = kf[rsl, lsl]
            k_cb = k_c.astype(BF16)
            inter_mm = _dot_nt(q_c.astype(BF16), ct.astype(BF16))
            wt_cols, decays = [], []
            for hh in range(2):
                head = 2 * p + hh
                m_st = m_pair[hh]
                b_col = bc[rsl, HEADS + head:HEADS + head + 1]
                i_col = gc[rsl, head:head + 1]
                b_row = br[HEADS + head:HEADS + head + 1, rsl]
                i_row = gr[head:head + 1, rsl]
                dmat = jnp.where(causal, b_col - b_row + i_row, -jnp.inf)
                inter = b_col + m_st
                m_row = jnp.maximum(inter, jnp.max(dmat, axis=-1, keepdims=True))
                wts = jnp.exp(dmat - m_row)
                in_head = (lane >= hh * DK) & (lane < (hh + 1) * DK)
                qm = jnp.where(in_head, q_c, 0.0).astype(BF16)
                sc = _dot_nt(qm, k_cb) * wts
                w_inter = jnp.exp(inter - m_row)
                num = _dot(sc.astype(BF16), vb[rsl, head * DV:(head + 1) * DV]) \
                    + w_inter * inter_mm[:, hh * DV:(hh + 1) * DV]
                den = jnp.sum(sc, axis=-1, keepdims=True) \
                    + w_inter * inter_mm[:, 2 * DV + hh:2 * DV + hh + 1]
                o_ref[rsl, head * DV:(head + 1) * DV] = num / jnp.maximum(jnp.abs(den), jnp.exp(-m_row))
                g_tot = b_col[ML_CHUNK - 1:ML_CHUNK, :]
                tail = g_tot - b_col + i_col
                m_new = jnp.maximum(g_tot + m_st, jnp.max(tail, axis=0, keepdims=True))
                wt_cols.append(jnp.exp(tail - m_new))
                decays.append(jnp.exp(g_tot + m_st - m_new))
                m_pair[hh] = m_new
            wk = (k_c * jnp.where(lane < DK, wt_cols[0], wt_cols[1])).astype(BF16)
            vp = jnp.concatenate([vb[rsl, p * 2 * DV:(p + 1) * 2 * DV], ones_blk], axis=1)
            upd = _dot_tn(vp, wk)
            ct = ct * jnp.where(lane < DK, decays[0], decays[1]) + jnp.where(st_mask, upd, 0.0)
        ct_ref[bb, p] = ct
        for hh in range(2):
            m_ref[bb, 2 * p + hh] = jnp.broadcast_to(m_pair[hh], m_ref.shape[2:])

    ng = ng_ref[...]
    for head in range(HEADS):
        hsl = slice(head * DV, (head + 1) * DV)
        y = _head_norm(o_ref[:, hsl]) * ng[:, hsl] * _sigmoid(o_pre[:, hsl])
        y_ref[:, :, hsl] = y.reshape(nb, lb, DV).astype(y_ref.dtype)


def _ml_mixer(x, w, b, wgc, bgc, wgr, bgr, cw, cb, ng, tri, trit):
    bsz, s, d = x.shape
    wcols = w.shape[1]
    nb = ML_SEQ_PER_STEP
    return pl.pallas_call(
        _ml_kernel,
        grid=(bsz // nb, s // SEQ_BLOCK),
        in_specs=[
            pl.BlockSpec((nb, SEQ_BLOCK, d), lambda i, j: (i, j, 0)),
            _full((d, wcols)), _full((1, wcols)),
            _full((2, d, LANES)), _full((1, LANES)),
            _full((2, 2 * HEADS, d)), _full((2 * HEADS, 1)),
            _full((ML_CONV, 2 * HEADS * DK)), _full((1, 2 * HEADS * DK)),
            _full((1, BRANCH_W)),
            _full((SEQ_BLOCK, SEQ_BLOCK)), _full((SEQ_BLOCK, SEQ_BLOCK)),
        ],
        out_specs=pl.BlockSpec((nb, SEQ_BLOCK, BRANCH_W), lambda i, j: (i, j, 0)),
        out_shape=jax.ShapeDtypeStruct((bsz, s, BRANCH_W), BF16),
        scratch_shapes=[pltpu.VMEM((nb, HEADS // 2, ML_ST_ROWS, PAIR_W), F32),
                        pltpu.VMEM((nb, HEADS, SUBLANES, LANES), F32),
                        pltpu.VMEM((nb, SUBLANES, 2 * HEADS * DK), F32),
                        pltpu.VMEM((nb * SEQ_BLOCK, BRANCH_W), F32)],
        compiler_params=_cparams(("arbitrary", "arbitrary")),
        name="mlstm_mixer",
    )(x, w, b, wgc, bgc, wgr, bgr, cw, cb, ng, tri, trit)


S5_IO_STEPS = 4
S5_PERM_BLOCK = 256
S5_HALF = 256
S5_HALF_STATE = (S5_HALF // S5_GROUP) * S5_STATE


def _s5_kernel(bsz, nc, u_ref, perm_ref, permt_ref, ec_ref, eg_ref, c0_ref, ftc_ref, d_ref, a_ref, y_ref,
               in_ref, ug_ref, xs_ref, xsb_ref):
    s = pl.program_id(1)
    lc = S5_CHUNK
    rows = bsz * nc
    sw = S5_HALF_STATE
    ng = S5_HALF // S5_GROUP
    gw = lc * S5_GROUP
    pb = perm_ref.shape[0]
    cb = pb // lc
    n_io = S5_IO_STEPS
    subs = rows * lc // (n_io * pb)
    nt = sw // LANES

    def pair_mask(shape, which):
        return (lax.broadcasted_iota(jnp.int32, shape, 1) // S5_STATE) % 2 == which

    @pl.when(s < n_io)
    def _():
        for sb in range(subs):
            pu = _dot(perm_ref[...], u_ref[sb * pb:(sb + 1) * pb, :]).astype(BF16)
            r0 = pl.multiple_of((s * subs + sb) * cb, cb)
            for l in range(lc):
                in_ref[l, pl.ds(r0, cb), :] = pu[l * cb:(l + 1) * cb]

    @pl.when(s == n_io)
    def _():
        for g in range(ng):
            ug_ref[g] = jnp.concatenate([in_ref[l][:, g * S5_GROUP:(g + 1) * S5_GROUP] for l in range(lc)],
                                        axis=1)
        for j in range(ng // 2):
            xin = None
            for gi in range(2):
                e_g = ec_ref[2 * j + gi].astype(F32)
                e_g = jnp.where(pair_mask(e_g.shape, gi), e_g, 0.0).astype(BF16)
                part = _dot(ug_ref[2 * j + gi], e_g)
                xin = part if xin is None else xin + part
            for b in range(bsz):
                blk = slice(b * nc, (b + 1) * nc)
                xs_ref[j, pl.ds(b, nc, stride=bsz), :] = xin[blk, 0:LANES]
                xs_ref[nt + j, pl.ds(b, nc, stride=bsz), :] = xin[blk, LANES:2 * LANES]
        ar = a_ref[0, 0:1, :]
        ai = a_ref[0, 1:2, :]

        def body(c, carry):
            sr, si = carry
            idx = pl.ds(pl.multiple_of(c * bsz, bsz), bsz)
            xr = jnp.concatenate([xs_ref[j, idx, :] for j in range(nt)], axis=1)
            xi = jnp.concatenate([xs_ref[nt + j, idx, :] for j in range(nt)], axis=1)
            for j in range(nt):
                xs_ref[j, idx, :] = sr[:, j * LANES:(j + 1) * LANES]
                xs_ref[nt + j, idx, :] = si[:, j * LANES:(j + 1) * LANES]
            return ar * sr - ai * si + xr, ar * si + ai * sr + xi

        zero = jnp.zeros((bsz, sw), F32)
        lax.fori_loop(0, nc, body, (zero, zero), unroll=2)
        for j in range(2 * nt):
            for b in range(bsz):
                xsb_ref[b * nc:(b + 1) * nc, j * LANES:(j + 1) * LANES] = \
                    xs_ref[j, pl.ds(b, nc, stride=bsz), :].astype(BF16)

    @pl.when(s == n_io + 1)
    def _():
        row = lax.broadcasted_iota(jnp.int32, (gw, gw), 0)
        col = lax.broadcasted_iota(jnp.int32, (gw, gw), 1)
        for g in range(ng):
            j = g // 2
            c_rows = jnp.concatenate([c0_ref[g]] * lc, axis=0)
            kw = _dot_nt(eg_ref[g], c_rows)
            kpad = jnp.concatenate([kw, jnp.zeros_like(kw)], axis=0)
            tz = jnp.zeros((gw, gw), F32)
            for lo in range(lc):
                sh = (lc - 1 - lo) * S5_GROUP
                tz = jnp.where(col // S5_GROUP == lo, kpad[sh:sh + gw, :], tz)
            tz = tz + jnp.where(row == col, d_ref[g], 0.0)
            xp = jnp.concatenate([xsb_ref[:, j * LANES:(j + 1) * LANES],
                                  xsb_ref[:, sw + j * LANES:sw + (j + 1) * LANES]], axis=1)
            ft = ftc_ref[g].astype(F32)
            ft = jnp.where(pair_mask(ft.shape, g % 2), ft, 0.0).astype(BF16)
            yg = _dot(ug_ref[g], tz.astype(BF16)) + _dot_nt(xp, ft)
            ug_ref[g] = yg.astype(BF16)
        for l in range(lc):
            in_ref[l] = jnp.concatenate([ug_ref[g][:, l * S5_GROUP:(l + 1) * S5_GROUP] for g in range(ng)],
                                        axis=1)

    @pl.when(s >= n_io + 2)
    def _():
        t = s - (n_io + 2)
        for sb in range(subs):
            r0 = pl.multiple_of((t * subs + sb) * cb, cb)
            ycat = jnp.concatenate([in_ref[l, pl.ds(r0, cb), :] for l in range(lc)], axis=0)
            y_ref[sb * pb:(sb + 1) * pb, :] = _dot(permt_ref[...], ycat).astype(y_ref.dtype)


def _s5_mixer(u, ec, eg, c0, ftc, d, a):
    bsz, s, _ = u.shape
    lc = S5_CHUNK
    nc = s // lc
    t = bsz * s
    nh = BRANCH_W // S5_HALF
    ng = S5_HALF // S5_GROUP
    n_io = S5_IO_STEPS
    tb = t // n_io
    pb = S5_PERM_BLOCK
    src = (np.arange(pb) % (pb // lc)) * lc + np.arange(pb) // (pb // lc)
    perm_np = (src[:, None] == np.arange(pb)[None, :]).astype(np.float32)
    perm, permt = jnp.asarray(perm_np, BF16), jnp.asarray(perm_np.T, BF16)
    once = lambda shape: pl.BlockSpec(shape, lambda h, t: (h,) + (0,) * (len(shape) - 1),
                                      pipeline_mode=pl.Buffered(1))
    return pl.pallas_call(
        functools.partial(_s5_kernel, bsz, nc),
        grid=(nh, 2 * n_io + 2),
        in_specs=[
            pl.BlockSpec((tb, S5_HALF), lambda h, t: (jnp.minimum(t, n_io - 1), h)),
            _full((pb, pb)), _full((pb, pb)),
            once((ng,) + ec.shape[1:]), once((ng,) + eg.shape[1:]), once((ng,) + c0.shape[1:]),
            once((ng,) + ftc.shape[1:]), once((ng,) + d.shape[1:]),
            pl.BlockSpec((1, 2, S5_HALF_STATE), lambda h, t: (h, 0, 0)),
        ],
        out_specs=pl.BlockSpec((tb, S5_HALF), lambda h, t: (jnp.maximum(t - n_io - 2, 0), h)),
        out_shape=jax.ShapeDtypeStruct((t, BRANCH_W), BF16),
        scratch_shapes=[pltpu.VMEM((lc, bsz * nc, S5_HALF), BF16),
                        pltpu.VMEM((ng, bsz * nc, lc * S5_GROUP), BF16),
                        pltpu.VMEM((2 * S5_HALF_STATE // LANES, bsz * nc, LANES), F32),
                        pltpu.VMEM((bsz * nc, 2 * S5_HALF_STATE), BF16)],
        compiler_params=_cparams(("arbitrary", "arbitrary")),
        name="s5_mixer",
    )(u.reshape(t, BRANCH_W), perm, permt, ec, eg, c0, ftc, d, a)


def _s5_tables(a_re, a_im, log_dt, b_re, b_im, c_re, c_im, d_skip):
    nl = a_re.shape[0]
    g, p_, n, lc = S5_GROUPS, S5_STATE, S5_GROUP, S5_CHUNK
    a_re, a_im, b_re, b_im, c_re, c_im = (v.astype(F32) for v in (a_re, a_im, b_re, b_im, c_re, c_im))
    dt = jnp.exp(log_dt.astype(F32))[..., None]
    adt_r = (a_re * dt)[:, :, None, None, :]
    adt_i = (a_im * dt)[:, :, None, None, :]

    def powers(steps):
        st = jnp.asarray(np.asarray(steps, np.float32).reshape(1, 1, -1, 1, 1))
        mag = jnp.exp(adt_r * st)
        return mag * jnp.cos(adt_i * st), mag * jnp.sin(adt_i * st)

    lbr, lbi = powers([1])
    are, aim = a_re[:, :, None, None, :], a_im[:, :, None, None, :]
    den = are * are + aim * aim
    qr = ((lbr - 1.0) * are + lbi * aim) / den
    qi = (lbi * are - (lbr - 1.0) * aim) / den
    bt_re, bt_im = b_re.transpose(0, 1, 3, 2)[:, :, None], b_im.transpose(0, 1, 3, 2)[:, :, None]
    bbr = qr * bt_re - qi * bt_im
    bbi = qr * bt_im + qi * bt_re
    cr, ci = c_re[:, :, None], c_im[:, :, None]
    pr, pi = powers(np.arange(lc - 1, -1, -1))
    er, ei = pr * bbr - pi * bbi, pr * bbi + pi * bbr
    ec = jnp.concatenate([er, er, ei, ei], axis=-1).reshape(nl, g, lc * n, 4 * p_)
    eg = jnp.concatenate([er, ei], axis=-1).reshape(nl, g, lc * n, 2 * p_)
    pfr, pfi = powers(np.arange(1, lc + 1))
    fr, fi = cr * pfr - ci * pfi, -(cr * pfi + ci * pfr)
    ftc = jnp.concatenate([fr, fr, fi, fi], axis=-1).reshape(nl, g, lc * n, 4 * p_)
    c0 = jnp.concatenate([c_re, -c_im], axis=-1)
    dd = jnp.tile(d_skip.astype(F32).reshape(nl, g, 1, n), (1, 1, 1, lc))
    ar, ai = powers([lc])
    nh = BRANCH_W // S5_HALF
    a_tab = jnp.stack([ar.reshape(nl, nh, S5_HALF_STATE), ai.reshape(nl, nh, S5_HALF_STATE)], axis=2)
    return ec.astype(BF16), eg.astype(BF16), c0.astype(BF16), ftc.astype(BF16), dd, a_tab


def _merge_kernel(x_ref, yg_ref, y5_ref, ym_ref, wgate_ref, bgate_ref, wglu_ref, bglu_ref,
                  wup_ref, wo_ref, g_ref, b_ref, o_ref):
    x = x_ref[...]
    xb = x.astype(BF16)
    y5 = _gelu_tanh(y5_ref[...].astype(F32))
    y5 = y5 * _sigmoid(_dot(y5.astype(BF16), wglu_ref[...]) + bglu_ref[...])
    ys = (yg_ref[...], y5.astype(BF16), ym_ref[...])
    acc = None
    for r in range(N_BRANCH):
        gate = _sigmoid(_dot(xb, wgate_ref[:, r * D_MODEL:(r + 1) * D_MODEL])
                        + bgate_ref[:, r * D_MODEL:(r + 1) * D_MODEL])
        term = gate * _dot(ys[r], wup_ref[r])
        acc = term if acc is None else acc + term
    mix = _dot(acc.astype(BF16), wo_ref[...])
    o_ref[...] = _layer_norm(DN_ALPHA * x + mix, g_ref[...], b_ref[...])


def _merge(x2, yg, y5, ym, wgate, bgate, wglu, bglu, wup, wo, g, b, tm=512):
    t, dm = x2.shape
    row = lambda w: pl.BlockSpec((tm, w), lambda i: (i, 0))
    return pl.pallas_call(
        _merge_kernel,
        grid=(t // tm,),
        in_specs=[row(dm), row(BRANCH_W), row(BRANCH_W), row(BRANCH_W),
                  _full(wgate.shape), _full(bgate.shape), _full(wglu.shape), _full(bglu.shape),
                  _full(wup.shape), _full(wo.shape), _full(g.shape), _full(b.shape)],
        out_specs=row(dm),
        out_shape=jax.ShapeDtypeStruct((t, dm), F32),
        compiler_params=_cparams(("parallel",)),
        name="merge_ln1",
    )(x2, yg, y5, ym, wgate, bgate, wglu, bglu, wup, wo, g, b)


def _ple_ln2(x, xb, f, p_ref, pwg_ref, pwp_ref, g_ref, b_ref):
    e = _sigmoid(_dot(xb, pwg_ref[...])) * _dot(p_ref[...].astype(BF16), pwp_ref[...])
    return _layer_norm(DN_ALPHA * x + f + e, g_ref[...], b_ref[...])


def _ffn_kernel(x_ref, p_ref, wg_ref, wu_ref, wd_ref, pwg_ref, pwp_ref, g_ref, b_ref, o_ref):
    x = x_ref[...]
    xb = x.astype(BF16)
    hid = (_silu(_dot(xb, wg_ref[...])) * _dot(xb, wu_ref[...])).astype(BF16)
    f = _dot(hid, wd_ref[...])
    o_ref[...] = _ple_ln2(x, xb, f, p_ref, pwg_ref, pwp_ref, g_ref, b_ref)


def _ffn_layer(x2, p3, layer, wg, wu, wd, pwg, pwp, g, b, tm=256):
    t, dm = x2.shape
    row = lambda w: pl.BlockSpec((tm, w), lambda i: (i, 0))
    return pl.pallas_call(
        _ffn_kernel,
        grid=(t // tm,),
        in_specs=[row(dm), pl.BlockSpec((None, tm, PLE_DIM), lambda i: (layer, i, 0)),
                  _full(wg.shape), _full(wu.shape), _full(wd.shape),
                  _full(pwg.shape), _full(pwp.shape), _full(g.shape), _full(b.shape)],
        out_specs=row(dm),
        out_shape=jax.ShapeDtypeStruct((t, dm), F32),
        compiler_params=_cparams(("parallel",)),
        name="ffn_ple_ln2",
    )(x2, p3, wg, wu, wd, pwg, pwp, g, b)


def _router_kernel(x_ref, w_ref, b_ref, tril_ref, sel_ref, pr_ref, rk_ref, cnt_ref, base_ref):
    @pl.when(pl.program_id(0) == 0)
    def _():
        base_ref[...] = jnp.zeros_like(base_ref)

    xh, xl = _split2(x_ref[...])
    logits = _dot(xh, w_ref[0]) + _dot(xl, w_ref[0]) + _dot(xh, w_ref[1]) + b_ref[...]
    lane = lax.broadcasted_iota(jnp.int32, logits.shape, 1)
    neg = -jnp.inf
    logits = jnp.where(lane < N_EXPERTS, logits, neg)
    m1 = jnp.max(logits, axis=-1, keepdims=True)
    i1 = jnp.min(jnp.where(logits == m1, lane, LANES), axis=-1, keepdims=True)
    rest = jnp.where(lane == i1, neg, logits)
    m2 = jnp.max(rest, axis=-1, keepdims=True)
    i2 = jnp.min(jnp.where(rest == m2, lane, LANES), axis=-1, keepdims=True)
    e2 = jnp.exp(m2 - m1)
    p1 = 1.0 / (1.0 + e2)
    p2 = e2 / (1.0 + e2)
    hot = jnp.where((lane == i1) | (lane == i2), 1.0, 0.0)
    base = base_ref[0:1, :]
    before = _dot(tril_ref[...], hot.astype(BF16)) + base
    r1 = jnp.sum(jnp.where(lane == i1, before, 0.0), axis=-1, keepdims=True)
    r2 = jnp.sum(jnp.where(lane == i2, before, 0.0), axis=-1, keepdims=True)
    sel_ref[...] = jnp.where(lane == 0, i1, jnp.where(lane == 1, i2, 0))
    pr_ref[...] = jnp.where(lane == 0, p1, jnp.where(lane == 1, p2, 0.0))
    rk_ref[...] = jnp.where(lane == 0, r1, jnp.where(lane == 1, r2, 0.0))
    total = base + jnp.sum(hot, axis=0, keepdims=True)
    base_ref[...] = jnp.broadcast_to(total, base_ref.shape)
    cnt_ref[...] = jnp.broadcast_to(total, cnt_ref.shape)


def _router(x2, w, b, tm=512):
    t, dm = x2.shape
    i = np.arange(tm)
    tril = jnp.asarray((i[:, None] > i[None, :]).astype(np.float32), BF16)
    blk = pl.BlockSpec((tm, LANES), lambda i: (i, 0))
    return pl.pallas_call(
        _router_kernel,
        grid=(t // tm,),
        in_specs=[pl.BlockSpec((tm, dm), lambda i: (i, 0)), _full(w.shape), _full(b.shape), _full((tm, tm))],
        out_specs=[blk, blk, blk, pl.BlockSpec((SUBLANES, LANES), lambda i: (0, 0))],
        out_shape=[jax.ShapeDtypeStruct((t, LANES), jnp.int32), jax.ShapeDtypeStruct((t, LANES), F32),
                   jax.ShapeDtypeStruct((t, LANES), F32), jax.ShapeDtypeStruct((SUBLANES, LANES), F32)],
        scratch_shapes=[pltpu.VMEM((SUBLANES, LANES), F32)],
        compiler_params=_cparams(("arbitrary",)),
        name="moe_router",
    )(x2, w, b, tril)


TOP_K = 2
MOE_TILE = 1024
MOE_SUB = 256
MOE_FF_TILE = 512
DISPATCH_TILE = 512
DMA_ISSUE_UNROLL = 16


ROW_SUB = D_MODEL // LANES


def _row_copy_wait(src_rows, dst_rows, sem):
    pltpu.make_async_copy(src_rows, dst_rows, sem).wait()


def _rows_to_tiles(dst_ref, val):
    n = val.shape[0]
    for j in range(ROW_SUB):
        dst_ref[pl.ds(j, n, stride=ROW_SUB), :] = val[:, j * LANES:(j + 1) * LANES]


def _tiles_to_rows(src_ref, n):
    return jnp.concatenate([src_ref[pl.ds(j, n, stride=ROW_SUB), :] for j in range(ROW_SUB)], axis=1)


def _row_tile(ref, r):
    return ref.at[pl.ds(pl.multiple_of(r * ROW_SUB, ROW_SUB), ROW_SUB), :]


def _dispatch_kernel(pos_ref, x_ref, xs_ref, stage_ref, sem):
    tm = x_ref.shape[0]
    base = pl.program_id(0) * tm * TOP_K
    _rows_to_tiles(stage_ref, x_ref[...])

    def body(r, carry):
        for k in range(TOP_K):
            dst = pos_ref[base + r * TOP_K + k]
            pltpu.make_async_copy(_row_tile(stage_ref, r), _row_tile(xs_ref, dst), sem).start(priority=k)
        return carry

    lax.fori_loop(0, tm, body, 0, unroll=DMA_ISSUE_UNROLL)
    for _ in range(TOP_K):
        _row_copy_wait(stage_ref, xs_ref.at[pl.ds(0, tm * ROW_SUB), :], sem)


def _dispatch(pos_flat, x2, tm=DISPATCH_TILE):
    t, dm = x2.shape
    return pl.pallas_call(
        _dispatch_kernel,
        grid_spec=pltpu.PrefetchScalarGridSpec(
            num_scalar_prefetch=1,
            grid=(t // tm,),
            in_specs=[pl.BlockSpec((tm, dm), lambda i, pos: (i, 0))],
            out_specs=pl.BlockSpec(memory_space=pl.ANY),
            scratch_shapes=[pltpu.VMEM((tm * ROW_SUB, LANES), F32), pltpu.SemaphoreType.DMA],
        ),
        out_shape=jax.ShapeDtypeStruct((t * TOP_K * ROW_SUB, LANES), F32),
        compiler_params=_cparams(("arbitrary",)),
        name="moe_dispatch",
    )(pos_flat, x2)


def _moe_group_kernel(tile_ref, exp_ref, lo_ref, hi_ref, xs_ref, wg_ref, wu_ref, wd_ref, o_ref,
                      acc_ref, xb_ref, wgb_ref, wub_ref, wdb_ref):
    i = pl.program_id(0)
    f = pl.program_id(1)
    lo = lo_ref[i]
    hi = hi_ref[i]
    tm = acc_ref.shape[0]

    @pl.when(hi > lo)
    def _():
        @pl.when((lo == 0) & (f == 0))
        def _():
            acc_ref[...] = jnp.zeros_like(acc_ref)

        @pl.when(f == 0)
        def _():
            xb_ref[...] = _tiles_to_rows(xs_ref, tm).astype(BF16)

        whole = (lo == 0) & (hi == tm)

        @pl.when(whole)
        def _():
            xb = xb_ref[...]
            hid = (_silu(_dot(xb, wg_ref[0].astype(BF16))) * _dot(xb, wu_ref[0].astype(BF16))).astype(BF16)
            acc_ref[...] += _dot(hid, wd_ref[0].astype(BF16))

        @pl.when(jnp.logical_not(whole))
        def _():
            wgb_ref[...] = wg_ref[0].astype(BF16)
            wub_ref[...] = wu_ref[0].astype(BF16)
            wdb_ref[...] = wd_ref[0].astype(BF16)
            for sub in range(tm // MOE_SUB):
                @pl.when((hi > sub * MOE_SUB) & (lo < (sub + 1) * MOE_SUB))
                def _(sub=sub):
                    rsl = slice(sub * MOE_SUB, (sub + 1) * MOE_SUB)
                    xb = xb_ref[rsl, :]
                    hid = _silu(_dot(xb, wgb_ref[...])) * _dot(xb, wub_ref[...])
                    rid = lax.broadcasted_iota(jnp.int32, (MOE_SUB, 1), 0) + sub * MOE_SUB
                    hid = jnp.where((rid >= lo) & (rid < hi), hid, 0.0).astype(BF16)
                    acc_ref[rsl, :] += _dot(hid, wdb_ref[...])

        @pl.when((hi == tm) & (f == pl.num_programs(1) - 1))
        def _():
            _rows_to_tiles(o_ref, acc_ref[...])


def _moe_grouped(items, xs, wg, wu, wd, tm=MOE_TILE, tf=MOE_FF_TILE):
    tile, exp, lo, hi = items
    dm = wg.shape[1]
    dff = wg.shape[2]
    return pl.pallas_call(
        _moe_group_kernel,
        grid_spec=pltpu.PrefetchScalarGridSpec(
            num_scalar_prefetch=4,
            grid=(tile.shape[0], dff // tf),
            in_specs=[pl.BlockSpec((tm * ROW_SUB, LANES), lambda i, f, tl, ex, lo_, hi_: (tl[i], 0)),
                      pl.BlockSpec((1, dm, tf), lambda i, f, tl, ex, lo_, hi_: (ex[i], 0, f)),
                      pl.BlockSpec((1, dm, tf), lambda i, f, tl, ex, lo_, hi_: (ex[i], 0, f)),
                      pl.BlockSpec((1, tf, dm), lambda i, f, tl, ex, lo_, hi_: (ex[i], f, 0))],
            out_specs=pl.BlockSpec((tm * ROW_SUB, LANES), lambda i, f, tl, ex, lo_, hi_: (tl[i], 0)),
            scratch_shapes=[pltpu.VMEM((tm, dm), F32), pltpu.VMEM((tm, dm), BF16), pltpu.VMEM((dm, tf), BF16),
                            pltpu.VMEM((dm, tf), BF16), pltpu.VMEM((tf, dm), BF16)],
        ),
        out_shape=jax.ShapeDtypeStruct(xs.shape, F32),
        compiler_params=_cparams(("arbitrary", "arbitrary")),
        name="moe_grouped",
    )(tile, exp, lo, hi, xs, wg, wu, wd)


def _moe_items(counts, n_rows, tm=MOE_TILE):
    n_tiles = n_rows // tm
    ends = jnp.cumsum(counts)
    cuts = jnp.sort(jnp.concatenate([jnp.arange(n_tiles + 1, dtype=jnp.int32) * tm, ends[:-1]]))
    start, stop = cuts[:-1], cuts[1:]
    tile = jnp.minimum(start // tm, n_tiles - 1)
    exp = jnp.minimum(jnp.sum(ends[None, :] <= start[:, None], axis=1), N_EXPERTS - 1)
    lo = start - tile * tm
    hi = stop - tile * tm
    return tile.astype(jnp.int32), exp.astype(jnp.int32), lo.astype(jnp.int32), hi.astype(jnp.int32)


def _combine_kernel(pos_ref, x_ref, pr_ref, p_ref, pwg_ref, pwp_ref, g_ref, b_ref, ys_ref, o_ref, gat_ref, sem):
    tm = x_ref.shape[0]
    base = pl.program_id(0) * tm * TOP_K

    def body(r, carry):
        for k in range(TOP_K):
            src = pos_ref[base + r * TOP_K + k]
            pltpu.make_async_copy(_row_tile(ys_ref, src), _row_tile(gat_ref.at[k], r), sem).start(priority=k)
        return carry

    lax.fori_loop(0, tm, body, 0, unroll=DMA_ISSUE_UNROLL)
    x = x_ref[...]
    xb = x.astype(BF16)
    e = _sigmoid(_dot(xb, pwg_ref[...])) * _dot(p_ref[...].astype(BF16), pwp_ref[...])
    for k in range(TOP_K):
        _row_copy_wait(ys_ref.at[pl.ds(0, tm * ROW_SUB), :], gat_ref.at[k], sem)
    pr = pr_ref[...]
    f = pr[:, 0:1] * _tiles_to_rows(gat_ref.at[0], tm)
    for k in range(1, TOP_K):
        f = f + pr[:, k:k + 1] * _tiles_to_rows(gat_ref.at[k], tm)
    o_ref[...] = _layer_norm(DN_ALPHA * x + f + e, g_ref[...], b_ref[...])


def _combine_layer(pos_flat, x2, pr, p3, layer, pwg, pwp, g, b, ys, tm=DISPATCH_TILE):
    t, dm = x2.shape
    row = lambda w: pl.BlockSpec((tm, w), lambda i, pos: (i, 0))
    full = lambda a: pl.BlockSpec(a.shape, lambda i, pos: (0,) * a.ndim, pipeline_mode=pl.Buffered(1))
    return pl.pallas_call(
        _combine_kernel,
        grid_spec=pltpu.PrefetchScalarGridSpec(
            num_scalar_prefetch=1,
            grid=(t // tm,),
            in_specs=[row(dm), row(LANES), pl.BlockSpec((None, tm, PLE_DIM), lambda i, pos: (layer, i, 0)),
                      full(pwg), full(pwp), full(g), full(b),
                      pl.BlockSpec(memory_space=pl.ANY)],
            out_specs=row(dm),
            scratch_shapes=[pltpu.VMEM((TOP_K, tm * ROW_SUB, LANES), F32), pltpu.SemaphoreType.DMA],
        ),
        out_shape=jax.ShapeDtypeStruct((t, dm), F32),
        compiler_params=_cparams(("arbitrary",)),
        name="moe_combine_ple_ln2",
    )(pos_flat, x2, pr, p3, pwg, pwp, g, b, ys)


def _row(v):
    return v.reshape(1, -1).astype(F32)


def _pad_lanes(w):
    return jnp.pad(w, ((0, 0), (0, LANES - w.shape[1])))


def kernel(x, p, w_in, b_in, gla_w_a2, gla_b_a2, gla_norm_g, s5_a_re, s5_a_im, s5_log_dt, s5_b_re, s5_b_im,
           s5_c_re, s5_c_im, s5_d, s5_w_glu, s5_b_glu, ml_conv_w, ml_conv_b, ml_norm_g, w_up, w_o, ln1_g, ln1_b,
           ffn_wg, ffn_wu, ffn_wd, moe_router, moe_router_b, moe_wg, moe_wu, moe_wd, ple_w_gate, ple_w_proj,
           ln2_g, ln2_b):
    bsz, s, dm = x.shape
    t = bsz * s
    hi = lax.Precision.HIGHEST
    o = IN_OFF
    tri = jnp.asarray(_chunk_tri(SEQ_BLOCK, CHUNK), BF16)
    ml_tri_np = _chunk_tri(SEQ_BLOCK, ML_CHUNK)
    ml_tri = jnp.asarray(ml_tri_np, BF16)
    ml_trit = jnp.asarray(ml_tri_np.T, BF16)
    s5_ec, s5_eg, s5_c0, s5_ftc, s5_dd, s5_atab = _s5_tables(s5_a_re, s5_a_im, s5_log_dt, s5_b_re, s5_b_im,
                                                             s5_c_re, s5_c_im, s5_d)
    p3 = p.reshape(DEPTH, t, PLE_DIM)

    w_in_b = w_in.astype(BF16)
    for i in range(DEPTH):
        w, b = w_in_b[i], b_in[i]
        sl = lambda k: (w[:, o[k]:o[k + 1]], b[o[k]:o[k + 1]])
        (wq, bq), (wk, bk), (wv, bv), (_, ba), (wg_, bg_) = sl(0), sl(1), sl(2), sl(3), sl(4)
        wz = jnp.dot(w_in[i, :, o[3]:o[4]], gla_w_a2[i], precision=hi).astype(BF16)
        bz = jnp.dot(ba, gla_w_a2[i], precision=hi) + gla_b_a2[i]
        (wu_, bu_) = sl(5)
        w_gla = jnp.concatenate([wq, wk, wz, wv, wg_, wu_], axis=1)
        b_gla = _row(jnp.concatenate([bq, bk, bz, bv, bg_, bu_]))
        y_gla, u = _gla_mixer(x, w_gla, b_gla, _row(gla_norm_g[i]), tri)

        x2 = x.reshape(t, dm)
        y5 = _s5_mixer(u, s5_ec[i], s5_eg[i], s5_c0[i], s5_ftc[i], s5_dd[i], s5_atab[i])

        (wmq, bmq), (wmk, bmk), (wmv, bmv), (wmo, bmo) = sl(6), sl(7), sl(8), sl(11)
        w_ml = jnp.concatenate([wmq, wmk, wmv, wmo], axis=1)
        b_ml = _row(jnp.concatenate([bmq, bmk, bmv, bmo]))
        w_if = w_in[i, :, o[9]:o[11]]
        b_if = b[o[9]:o[11]]
        if_h = w_if.astype(BF16)
        if_l = (w_if - if_h.astype(F32)).astype(BF16)
        wgc = jnp.stack([_pad_lanes(if_h), _pad_lanes(if_l)])
        wgr = jnp.stack([if_h.T, if_l.T])
        y_ml = _ml_mixer(x, w_ml, b_ml, wgc, _pad_lanes(_row(b_if)), wgr, b_if.reshape(-1, 1).astype(F32),
                         ml_conv_w[i].astype(F32), _row(ml_conv_b[i]), _row(ml_norm_g[i]), ml_tri, ml_trit)

        (wgt, bgt) = sl(12)
        x1 = _merge(x2, y_gla.reshape(t, BRANCH_W), y5, y_ml.reshape(t, BRANCH_W),
                    wgt, _row(bgt), s5_w_glu[i].astype(BF16), _row(s5_b_glu[i]),
                    w_up[i].astype(BF16), w_o[i].astype(BF16), _row(ln1_g[i]), _row(ln1_b[i]))

        pwg = ple_w_gate[i].astype(BF16)
        pwp = ple_w_proj[i].astype(BF16)
        j = i // 2
        if i % 2 == 0:
            x2n = _ffn_layer(x1, p3, i, ffn_wg[j].astype(BF16), ffn_wu[j].astype(BF16), ffn_wd[j].astype(BF16),
                             pwg, pwp, _row(ln2_g[i]), _row(ln2_b[i]))
        else:
            wr = moe_router[j]
            wr_h = wr.astype(BF16)
            wr_l = (wr - wr_h.astype(F32)).astype(BF16)
            sel, pr, rk, cnt = _router(x1, jnp.stack([_pad_lanes(wr_h), _pad_lanes(wr_l)]),
                                       _pad_lanes(_row(moe_router_b[j])))
            counts = cnt[0, :N_EXPERTS].astype(jnp.int32)
            starts = jnp.cumsum(counts) - counts
            sel2 = sel[:, :TOP_K]
            pos = rk[:, :TOP_K].astype(jnp.int32) + jnp.sum(
                jnp.where(sel2[..., None] == jnp.arange(N_EXPERTS), starts, 0), axis=-1)
            pos_flat = pos.reshape(-1)
            xs = _dispatch(pos_flat, x1)
            ys = _moe_grouped(_moe_items(counts, t * TOP_K), xs, moe_wg[j], moe_wu[j], moe_wd[j])
            x2n = _combine_layer(pos_flat, x1, pr, p3, i, pwg, pwp, _row(ln2_g[i]), _row(ln2_b[i]), ys)
        x = x2n.reshape(bsz, s, dm)
    return x
```

```python
import functools
import math

import numpy as np
import jax
import jax.numpy as jnp
from jax import lax
from jax.experimental import pallas as pl
from jax.experimental.pallas import tpu as pltpu

F32 = jnp.float32
BF16 = jnp.bfloat16

D_MODEL = 1024
DEPTH = 2
N_BRANCH = 3
BRANCH_W = 512
HEADS = 4
DK = 64
DV = BRANCH_W // HEADS
GLA_RANK = 16
GLA_TAU = 16.0
CHUNK = 64
ML_CHUNK = 256
S5_GROUP = 16
S5_GROUPS = BRANCH_W // S5_GROUP
S5_STATE = 64
S5_CHUNK = 16
ML_CONV = 4
N_EXPERTS = 8
PLE_DIM = 256
DN_ALPHA = (2.0 * DEPTH) ** 0.25
LN_EPS = 1e-5

IN_WIDTHS = (
    HEADS * DK, HEADS * DK, BRANCH_W, GLA_RANK, BRANCH_W,
    BRANCH_W,
    HEADS * DK, HEADS * DK, BRANCH_W, HEADS, HEADS, BRANCH_W,
    N_BRANCH * D_MODEL,
)
IN_OFF = tuple(int(o) for o in np.concatenate([[0], np.cumsum(IN_WIDTHS)]))

LANES = 128
SUBLANES = 8
SEQ_BLOCK = 256
GLA_SEQ_PER_STEP = 4
ML_SEQ_PER_STEP = 1
PAIR_W = 2 * DK
VMEM_LIMIT = 56 * 1024 * 1024


def _cparams(sem):
    return pltpu.CompilerParams(dimension_semantics=sem, vmem_limit_bytes=VMEM_LIMIT)


def _dot(a, b):
    return jnp.dot(a, b, preferred_element_type=F32)


def _dot_nt(a, b):
    return lax.dot_general(a, b, (((1,), (1,)), ((), ())), preferred_element_type=F32)


def _dot_tn(a, b):
    return lax.dot_general(a, b, (((0,), (0,)), ((), ())), preferred_element_type=F32)


def _split3(a):
    hi = a.astype(BF16)
    r = a - hi.astype(F32)
    mid = r.astype(BF16)
    lo = (r - mid.astype(F32)).astype(BF16)
    return hi, mid, lo


def _split2(a):
    hi = a.astype(BF16)
    lo = (a - hi.astype(F32)).astype(BF16)
    return hi, lo


def _log_sigmoid(x):
    return jnp.minimum(x, 0.0) - jnp.log(1.0 + jnp.exp(-jnp.abs(x)))


def _sigmoid(x):
    return 0.5 * jnp.tanh(0.5 * x) + 0.5


def _silu(x):
    return x * _sigmoid(x)


def _gelu_tanh(x):
    return 0.5 * x * (1.0 + jnp.tanh(math.sqrt(2.0 / math.pi) * (x + 0.044715 * (x * x * x))))


def _layer_norm(v, g, b):
    mu = jnp.mean(v, axis=-1, keepdims=True)
    c = v - mu
    var = jnp.mean(c * c, axis=-1, keepdims=True)
    return c * lax.rsqrt(var + LN_EPS) * g + b


def _head_norm(o):
    mu = jnp.mean(o, axis=-1, keepdims=True)
    c = o - mu
    var = jnp.mean(c * c, axis=-1, keepdims=True)
    return c * lax.rsqrt(var + LN_EPS)


def _chunk_tri(n, chunk):
    i = np.arange(n)
    return ((i[:, None] >= i[None, :]) & (i[:, None] // chunk == i[None, :] // chunk)).astype(np.float32)


def _full(shape):
    nd = len(shape)
    return pl.BlockSpec(shape, lambda *_: (0,) * nd, pipeline_mode=pl.Buffered(1))


def _gla_kernel(x_ref, w_ref, b_ref, ng_ref, tri_ref, y_ref, u_ref, st_ref, o_ref):
    @pl.when(pl.program_id(1) == 0)
    def _():
        st_ref[...] = jnp.zeros_like(st_ref)

    hk = HEADS * DK
    nb, lb, d = x_ref.shape
    xb = x_ref[...].reshape(nb * lb, d).astype(BF16)
    h = _dot(xb, w_ref[...]) + b_ref[...]
    q = h[:, 0:hk]
    k = h[:, hk:2 * hk]
    z = h[:, 2 * hk:3 * hk]
    v = h[:, 3 * hk:3 * hk + BRANCH_W]
    g = h[:, 3 * hk + BRANCH_W:3 * hk + 2 * BRANCH_W]
    u_ref[...] = h[:, 3 * hk + 2 * BRANCH_W:3 * hk + 3 * BRANCH_W].reshape(nb, lb, BRANCH_W).astype(u_ref.dtype)

    la = _log_sigmoid(z) * (1.0 / GLA_TAU)
    tri = tri_ref[...]
    la_h, la_m, la_l = _split3(la)
    cum = jnp.concatenate(
        [_dot(tri, la_h[r:r + lb]) + _dot(tri, la_m[r:r + lb]) + _dot(tri, la_l[r:r + lb])
         for r in range(0, nb * lb, lb)], axis=0)
    qd = q * (DK ** -0.5) * jnp.exp(cum)
    ki = k * jnp.exp(-cum)

    lane = lax.broadcasted_iota(jnp.int32, (1, PAIR_W), 1)
    row_i = lax.broadcasted_iota(jnp.int32, (CHUNK, 2 * CHUNK), 0)
    col_i = lax.broadcasted_iota(jnp.int32, (CHUNK, 2 * CHUNK), 1)
    causal = row_i >= col_i % CHUNK
    bd_r = lax.broadcasted_iota(jnp.int32, (2 * DV, PAIR_W), 0)
    bd_c = lax.broadcasted_iota(jnp.int32, (2 * DV, PAIR_W), 1)
    blockdiag = (bd_r >= DV) == (bd_c >= DK)
    vb_r = lax.broadcasted_iota(jnp.int32, (2 * CHUNK, 2 * DV), 0)
    vb_c = lax.broadcasted_iota(jnp.int32, (2 * CHUNK, 2 * DV), 1)
    v_blocks = (vb_r >= CHUNK) == (vb_c >= DV)

    for c in range(lb // CHUNK):
        for bb, p in [(bb, p) for bb in range(nb) for p in range(HEADS // 2)]:
            st = st_ref[bb, p]
            lsl = slice(p * PAIR_W, (p + 1) * PAIR_W)
            rsl = slice(bb * lb + c * CHUNK, bb * lb + (c + 1) * CHUNK)
            qd_c = qd[rsl, lsl]
            ki_c = ki[rsl, lsl]
            cum_c = cum[rsl, lsl]
            last = cum_c[CHUNK - 1:CHUNK, :]
            kt = (k[rsl, lsl] * jnp.exp(last - cum_c)).astype(BF16)
            qd_b = qd_c.astype(BF16)
            inter = _dot_nt(qd_b, st.astype(BF16))
            k_bd = jnp.concatenate([jnp.where(lane < DK, ki_c, 0.0), jnp.where(lane >= DK, ki_c, 0.0)],
                                   axis=0).astype(BF16)
            att = jnp.where(causal, _dot_nt(qd_b, k_bd), 0.0)
            v_c = v[rsl, p * 2 * DV:(p + 1) * 2 * DV]
            v_bd = jnp.where(v_blocks, jnp.concatenate([v_c, v_c], axis=0), 0.0).astype(BF16)
            o_ref[rsl, p * 2 * DV:(p + 1) * 2 * DV] = _dot(att.astype(BF16), v_bd) + inter
            upd = _dot_tn(v_c.astype(BF16), kt)
            st_ref[bb, p] = st * jnp.exp(last) + jnp.where(blockdiag, upd, 0.0)

    ng = ng_ref[...]
    for head in range(HEADS):
        hsl = slice(head * DV, (head + 1) * DV)
        y = _head_norm(o_ref[:, hsl]) * ng[:, hsl] * _silu(g[:, hsl])
        y_ref[:, :, hsl] = y.reshape(nb, lb, DV).astype(y_ref.dtype)


def _gla_mixer(x, w, b, ng, tri):
    bsz, s, d = x.shape
    wcols = w.shape[1]
    nb = GLA_SEQ_PER_STEP
    return pl.pallas_call(
        _gla_kernel,
        grid=(bsz // nb, s // SEQ_BLOCK),
        in_specs=[
            pl.BlockSpec((nb, SEQ_BLOCK, d), lambda i, j: (i, j, 0)),
            _full((d, wcols)), _full((1, wcols)), _full((1, BRANCH_W)),
            _full((SEQ_BLOCK, SEQ_BLOCK)),
        ],
        out_specs=[pl.BlockSpec((nb, SEQ_BLOCK, BRANCH_W), lambda i, j: (i, j, 0))] * 2,
        out_shape=[jax.ShapeDtypeStruct((bsz, s, BRANCH_W), BF16)] * 2,
        scratch_shapes=[pltpu.VMEM((nb, HEADS // 2, 2 * DV, PAIR_W), F32),
                        pltpu.VMEM((nb * SEQ_BLOCK, BRANCH_W), F32)],
        compiler_params=_cparams(("arbitrary", "arbitrary")),
        name="gla_mixer",
    )(x, w, b, ng, tri)


ML_ST_ROWS = 2 * DV + LANES


def _ml_kernel(x_ref, w_ref, b_ref, wgc_ref, bgc_ref, wgr_ref, bgr_ref, cw_ref, cb_ref, ng_ref,
               tri_ref, trit_ref, y_ref, ct_ref, m_ref, carry_ref, o_ref):
    @pl.when(pl.program_id(1) == 0)
    def _():
        ct_ref[...] = jnp.zeros_like(ct_ref)
        m_ref[...] = jnp.zeros_like(m_ref)
        carry_ref[...] = jnp.zeros_like(carry_ref)

    hk = HEADS * DK
    nb, lb, d = x_ref.shape
    x32 = x_ref[...].reshape(nb * lb, d)
    xh, xl = _split2(x32)
    h = _dot(xh, w_ref[...]) + b_ref[...]
    qk = h[:, 0:2 * hk]
    v = h[:, 2 * hk:2 * hk + BRANCH_W]
    o_pre = h[:, 2 * hk + BRANCH_W:2 * hk + 2 * BRANCH_W]

    cw = cw_ref[...]
    conv = []
    for bb in range(nb):
        qk_b = qk[bb * lb:(bb + 1) * lb]
        ext = jnp.concatenate([carry_ref[bb], qk_b], axis=0)
        first = SUBLANES - (ML_CONV - 1)
        acc = cb_ref[...] + ext[first:first + lb] * cw[0:1]
        for j in range(1, ML_CONV):
            acc = acc + ext[first + j:first + j + lb] * cw[j:j + 1]
        carry_ref[bb] = qk_b[lb - SUBLANES:lb]
        conv.append(acc)
    qkc = _silu(jnp.concatenate(conv, axis=0))
    qf = qkc[:, 0:hk]
    kf = qkc[:, hk:2 * hk] * (DK ** -0.5)
    vb = v.astype(BF16)

    gc = (_dot(xh, wgc_ref[0]) + _dot(xl, wgc_ref[0]) + _dot(xh, wgc_ref[1])) + bgc_ref[...]
    gr = (_dot_nt(wgr_ref[0], xh) + _dot_nt(wgr_ref[0], xl) + _dot_nt(wgr_ref[1], xh)) + bgr_ref[...]
    lf_c = _log_sigmoid(gc)
    lf_r = _log_sigmoid(gr)
    tri = tri_ref[...]
    trit = trit_ref[...]
    c_h, c_m, c_l = _split3(lf_c)
    r_h, r_m, r_l = _split3(lf_r)
    blocks = [slice(r, r + lb) for r in range(0, nb * lb, lb)]
    bc = jnp.concatenate([_dot(tri, c_h[r]) + _dot(tri, c_m[r]) + _dot(tri, c_l[r]) for r in blocks],
                         axis=0)
    br = jnp.concatenate([_dot(r_h[:, r], trit) + _dot(r_m[:, r], trit) + _dot(r_l[:, r], trit)
                          for r in blocks], axis=1)

    lane = lax.broadcasted_iota(jnp.int32, (1, PAIR_W), 1)
    row_i = lax.broadcasted_iota(jnp.int32, (ML_CHUNK, ML_CHUNK), 0)
    col_i = lax.broadcasted_iota(jnp.int32, (ML_CHUNK, ML_CHUNK), 1)
    causal = row_i >= col_i
    sr = lax.broadcasted_iota(jnp.int32, (ML_ST_ROWS, PAIR_W), 0)
    sc_ = lax.broadcasted_iota(jnp.int32, (ML_ST_ROWS, PAIR_W), 1)
    first = sc_ < DK
    rows_h0 = (sr < DV) | (sr == 2 * DV)
    rows_h1 = ((sr >= DV) & (sr < 2 * DV)) | (sr == 2 * DV + 1)
    st_mask = (rows_h0 & first) | (rows_h1 & ~first)
    ones_blk = jnp.ones((ML_CHUNK, LANES), BF16)

    for bb, p in [(bb, p) for bb in range(nb) for p in range(HEADS // 2)]:
        ct = ct_ref[bb, p]
        lsl = slice(p * PAIR_W, (p + 1) * PAIR_W)
        m_pair = [m_ref[bb, 2 * p + hh][0:1, 0:1] for hh in range(2)]
        for c in range(lb // ML_CHUNK):
            rsl = slice(bb * lb + c * ML_CHUNK, bb * lb + (c + 1) * ML_CHUNK)
            q_c = qf[rsl, lsl]
            k_c = kf[rsl, lsl]
            k_cb = k_c.astype(BF16)
            inter_mm = _dot_nt(q_c.astype(BF16), ct.astype(BF16))
            wt_cols, decays = [], []
            for hh in range(2):
                head = 2 * p + hh
                m_st = m_pair[hh]
                b_col = bc[rsl, HEADS + head:HEADS + head + 1]
                i_col = gc[rsl, head:head + 1]
                b_row = br[HEADS + head:HEADS + head + 1, rsl]
                i_row = gr[head:head + 1, rsl]
                dmat = jnp.where(causal, b_col - b_row + i_row, -jnp.inf)
                inter = b_col + m_st
                m_row = jnp.maximum(inter, jnp.max(dmat, axis=-1, keepdims=True))
                wts = jnp.exp(dmat - m_row)
                in_head = (lane >= hh * DK) & (lane < (hh + 1) * DK)
                qm = jnp.where(in_head, q_c, 0.0).astype(BF16)
                sc = _dot_nt(qm, k_cb) * wts
                w_inter = jnp.exp(inter - m_row)
                num = _dot(sc.astype(BF16), vb[rsl, head * DV:(head + 1) * DV]) \
                    + w_inter * inter_mm[:, hh * DV:(hh + 1) * DV]
                den = jnp.sum(sc, axis=-1, keepdims=True) \
                    + w_inter * inter_mm[:, 2 * DV + hh:2 * DV + hh + 1]
                o_ref[rsl, head * DV:(head + 1) * DV] = num / jnp.maximum(jnp.abs(den), jnp.exp(-m_row))
                g_tot = b_col[ML_CHUNK - 1:ML_CHUNK, :]
                tail = g_tot - b_col + i_col
                m_new = jnp.maximum(g_tot + m_st, jnp.max(tail, axis=0, keepdims=True))
                wt_cols.append(jnp.exp(tail - m_new))
                decays.append(jnp.exp(g_tot + m_st - m_new))
                m_pair[hh] = m_new
            wk = (k_c * jnp.where(lane < DK, wt_cols[0], wt_cols[1])).astype(BF16)
            vp = jnp.concatenate([vb[rsl, p * 2 * DV:(p + 1) * 2 * DV], ones_blk], axis=1)
            upd = _dot_tn(vp, wk)
            ct = ct * jnp.where(lane < DK, decays[0], decays[1]) + jnp.where(st_mask, upd, 0.0)
        ct_ref[bb, p] = ct
        for hh in range(2):
            m_ref[bb, 2 * p + hh] = jnp.broadcast_to(m_pair[hh], m_ref.shape[2:])

    ng = ng_ref[...]
    for head in range(HEADS):
        hsl = slice(head * DV, (head + 1) * DV)
        y = _head_norm(o_ref[:, hsl]) * ng[:, hsl] * _sigmoid(o_pre[:, hsl])
        y_ref[:, :, hsl] = y.reshape(nb, lb, DV).astype(y_ref.dtype)


def _ml_mixer(x, w, b, wgc, bgc, wgr, bgr, cw, cb, ng, tri, trit):
    bsz, s, d = x.shape
    wcols = w.shape[1]
    nb = ML_SEQ_PER_STEP
    return pl.pallas_call(
        _ml_kernel,
        grid=(bsz // nb, s // SEQ_BLOCK),
        in_specs=[
            pl.BlockSpec((nb, SEQ_BLOCK, d), lambda i, j: (i, j, 0)),
            _full((d, wcols)), _full((1, wcols)),
            _full((2, d, LANES)), _full((1, LANES)),
            _full((2, 2 * HEADS, d)), _full((2 * HEADS, 1)),
            _full((ML_CONV, 2 * HEADS * DK)), _full((1, 2 * HEADS * DK)),
            _full((1, BRANCH_W)),
            _full((SEQ_BLOCK, SEQ_BLOCK)), _full((SEQ_BLOCK, SEQ_BLOCK)),
        ],
        out_specs=pl.BlockSpec((nb, SEQ_BLOCK, BRANCH_W), lambda i, j: (i, j, 0)),
        out_shape=jax.ShapeDtypeStruct((bsz, s, BRANCH_W), BF16),
        scratch_shapes=[pltpu.VMEM((nb, HEADS // 2, ML_ST_ROWS, PAIR_W), F32),
                        pltpu.VMEM((nb, HEADS, SUBLANES, LANES), F32),
                        pltpu.VMEM((nb, SUBLANES, 2 * HEADS * DK), F32),
                        pltpu.VMEM((nb * SEQ_BLOCK, BRANCH_W), F32)],
        compiler_params=_cparams(("arbitrary", "arbitrary")),
        name="mlstm_mixer",
    )(x, w, b, wgc, bgc, wgr, bgr, cw, cb, ng, tri, trit)


S5_IO_STEPS = 4
S5_PERM_BLOCK = 256
S5_HALF = 256
S5_HALF_STATE = (S5_HALF // S5_GROUP) * S5_STATE


def _s5_kernel(bsz, nc, u_ref, perm_ref, permt_ref, ec_ref, eg_ref, c0_ref, ftc_ref, d_ref, a_ref, y_ref,
               in_ref, ug_ref, xs_ref, xsb_ref):
    s = pl.program_id(1)
    lc = S5_CHUNK
    rows = bsz * nc
    sw = S5_HALF_STATE
    ng = S5_HALF // S5_GROUP
    gw = lc * S5_GROUP
    pb = perm_ref.shape[0]
    cb = pb // lc
    n_io = S5_IO_STEPS
    subs = rows * lc // (n_io * pb)
    nt = sw // LANES

    def pair_mask(shape, which):
        return (lax.broadcasted_iota(jnp.int32, shape, 1) // S5_STATE) % 2 == which

    @pl.when(s < n_io)
    def _():
        for sb in range(subs):
            pu = _dot(perm_ref[...], u_ref[sb * pb:(sb + 1) * pb, :]).astype(BF16)
            r0 = pl.multiple_of((s * subs + sb) * cb, cb)
            for l in range(lc):
                in_ref[l, pl.ds(r0, cb), :] = pu[l * cb:(l + 1) * cb]

    @pl.when(s == n_io)
    def _():
        for g in range(ng):
            ug_ref[g] = jnp.concatenate([in_ref[l][:, g * S5_GROUP:(g + 1) * S5_GROUP] for l in range(lc)],
                                        axis=1)
        for j in range(ng // 2):
            xin = None
            for gi in range(2):
                e_g = ec_ref[2 * j + gi].astype(F32)
                e_g = jnp.where(pair_mask(e_g.shape, gi), e_g, 0.0).astype(BF16)
                part = _dot(ug_ref[2 * j + gi], e_g)
                xin = part if xin is None else xin + part
            for b in range(bsz):
                blk = slice(b * nc, (b + 1) * nc)
                xs_ref[j, pl.ds(b, nc, stride=bsz), :] = xin[blk, 0:LANES]
                xs_ref[nt + j, pl.ds(b, nc, stride=bsz), :] = xin[blk, LANES:2 * LANES]
        ar = a_ref[0, 0:1, :]
        ai = a_ref[0, 1:2, :]

        def body(c, carry):
            sr, si = carry
            idx = pl.ds(pl.multiple_of(c * bsz, bsz), bsz)
            xr = jnp.concatenate([xs_ref[j, idx, :] for j in range(nt)], axis=1)
            xi = jnp.concatenate([xs_ref[nt + j, idx, :] for j in range(nt)], axis=1)
            for j in range(nt):
                xs_ref[j, idx, :] = sr[:, j * LANES:(j + 1) * LANES]
                xs_ref[nt + j, idx, :] = si[:, j * LANES:(j + 1) * LANES]
            return ar * sr - ai * si + xr, ar * si + ai * sr + xi

        zero = jnp.zeros((bsz, sw), F32)
        lax.fori_loop(0, nc, body, (zero, zero), unroll=2)
        for j in range(2 * nt):
            for b in range(bsz):
                xsb_ref[b * nc:(b + 1) * nc, j * LANES:(j + 1) * LANES] = \
                    xs_ref[j, pl.ds(b, nc, stride=bsz), :].astype(BF16)

    @pl.when(s == n_io + 1)
    def _():
        row = lax.broadcasted_iota(jnp.int32, (gw, gw), 0)
        col = lax.broadcasted_iota(jnp.int32, (gw, gw), 1)
        for g in range(ng):
            j = g // 2
            c_rows = jnp.concatenate([c0_ref[g]] * lc, axis=0)
            kw = _dot_nt(eg_ref[g], c_rows)
            kpad = jnp.concatenate([kw, jnp.zeros_like(kw)], axis=0)
            tz = jnp.zeros((gw, gw), F32)
            for lo in range(lc):
                sh = (lc - 1 - lo) * S5_GROUP
                tz = jnp.where(col // S5_GROUP == lo, kpad[sh:sh + gw, :], tz)
            tz = tz + jnp.where(row == col, d_ref[g], 0.0)
            xp = jnp.concatenate([xsb_ref[:, j * LANES:(j + 1) * LANES],
                                  xsb_ref[:, sw + j * LANES:sw + (j + 1) * LANES]], axis=1)
            ft = ftc_ref[g].astype(F32)
            ft = jnp.where(pair_mask(ft.shape, g % 2), ft, 0.0).astype(BF16)
            yg = _dot(ug_ref[g], tz.astype(BF16)) + _dot_nt(xp, ft)
            ug_ref[g] = yg.astype(BF16)
        for l in range(lc):
            in_ref[l] = jnp.concatenate([ug_ref[g][:, l * S5_GROUP:(l + 1) * S5_GROUP] for g in range(ng)],
                                        axis=1)

    @pl.when(s >= n_io + 2)
    def _():
        t = s - (n_io + 2)
        for sb in range(subs):
            r0 = pl.multiple_of((t * subs + sb) * cb, cb)
            ycat = jnp.concatenate([in_ref[l, pl.ds(r0, cb), :] for l in range(lc)], axis=0)
            y_ref[sb * pb:(sb + 1) * pb, :] = _dot(permt_ref[...], ycat).astype(y_ref.dtype)


def _s5_mixer(u, ec, eg, c0, ftc, d, a):
    bsz, s, _ = u.shape
    lc = S5_CHUNK
    nc = s // lc
    t = bsz * s
    nh = BRANCH_W // S5_HALF
    ng = S5_HALF // S5_GROUP
    n_io = S5_IO_STEPS
    tb = t // n_io
    pb = S5_PERM_BLOCK
    src = (np.arange(pb) % (pb // lc)) * lc + np.arange(pb) // (pb // lc)
    perm_np = (src[:, None] == np.arange(pb)[None, :]).astype(np.float32)
    perm, permt = jnp.asarray(perm_np, BF16), jnp.asarray(perm_np.T, BF16)
    once = lambda shape: pl.BlockSpec(shape, lambda h, t: (h,) + (0,) * (len(shape) - 1),
                                      pipeline_mode=pl.Buffered(1))
    return pl.pallas_call(
        functools.partial(_s5_kernel, bsz, nc),
        grid=(nh, 2 * n_io + 2),
        in_specs=[
            pl.BlockSpec((tb, S5_HALF), lambda h, t: (jnp.minimum(t, n_io - 1), h)),
            _full((pb, pb)), _full((pb, pb)),
            once((ng,) + ec.shape[1:]), once((ng,) + eg.shape[1:]), once((ng,) + c0.shape[1:]),
            once((ng,) + ftc.shape[1:]), once((ng,) + d.shape[1:]),
            pl.BlockSpec((1, 2, S5_HALF_STATE), lambda h, t: (h, 0, 0)),
        ],
        out_specs=pl.BlockSpec((tb, S5_HALF), lambda h, t: (jnp.maximum(t - n_io - 2, 0), h)),
        out_shape=jax.ShapeDtypeStruct((t, BRANCH_W), BF16),
        scratch_shapes=[pltpu.VMEM((lc, bsz * nc, S5_HALF), BF16),
                        pltpu.VMEM((ng, bsz * nc, lc * S5_GROUP), BF16),
                        pltpu.VMEM((2 * S5_HALF_STATE // LANES, bsz * nc, LANES), F32),
                        pltpu.VMEM((bsz * nc, 2 * S5_HALF_STATE), BF16)],
        compiler_params=_cparams(("arbitrary", "arbitrary")),
        name="s5_mixer",
    )(u.reshape(t, BRANCH_W), perm, permt, ec, eg, c0, ftc, d, a)


def _s5_tables(a_re, a_im, log_dt, b_re, b_im, c_re, c_im, d_skip):
    nl = a_re.shape[0]
    g, p_, n, lc = S5_GROUPS, S5_STATE, S5_GROUP, S5_CHUNK
    a_re, a_im, b_re, b_im, c_re, c_im = (v.astype(F32) for v in (a_re, a_im, b_re, b_im, c_re, c_im))
    dt = jnp.exp(log_dt.astype(F32))[..., None]
    adt_r = (a_re * dt)[:, :, None, None, :]
    adt_i = (a_im * dt)[:, :, None, None, :]

    def powers(steps):
        st = jnp.asarray(np.asarray(steps, np.float32).reshape(1, 1, -1, 1, 1))
        mag = jnp.exp(adt_r * st)
        return mag * jnp.cos(adt_i * st), mag * jnp.sin(adt_i * st)

    lbr, lbi = powers([1])
    are, aim = a_re[:, :, None, None, :], a_im[:, :, None, None, :]
    den = are * are + aim * aim
    qr = ((lbr - 1.0) * are + lbi * aim) / den
    qi = (lbi * are - (lbr - 1.0) * aim) / den
    bt_re, bt_im = b_re.transpose(0, 1, 3, 2)[:, :, None], b_im.transpose(0, 1, 3, 2)[:, :, None]
    bbr = qr * bt_re - qi * bt_im
    bbi = qr * bt_im + qi * bt_re
    cr, ci = c_re[:, :, None], c_im[:, :, None]
    pr, pi = powers(np.arange(lc - 1, -1, -1))
    er, ei = pr * bbr - pi * bbi, pr * bbi + pi * bbr
    ec = jnp.concatenate([er, er, ei, ei], axis=-1).reshape(nl, g, lc * n, 4 * p_)
    eg = jnp.concatenate([er, ei], axis=-1).reshape(nl, g, lc * n, 2 * p_)
    pfr, pfi = powers(np.arange(1, lc + 1))
    fr, fi = cr * pfr - ci * pfi, -(cr * pfi + ci * pfr)
    ftc = jnp.concatenate([fr, fr, fi, fi], axis=-1).reshape(nl, g, lc * n, 4 * p_)
    c0 = jnp.concatenate([c_re, -c_im], axis=-1)
    dd = jnp.tile(d_skip.astype(F32).reshape(nl, g, 1, n), (1, 1, 1, lc))
    ar, ai = powers([lc])
    nh = BRANCH_W // S5_HALF
    a_tab = jnp.stack([ar.reshape(nl, nh, S5_HALF_STATE), ai.reshape(nl, nh, S5_HALF_STATE)], axis=2)
    return ec.astype(BF16), eg.astype(BF16), c0.astype(BF16), ftc.astype(BF16), dd, a_tab


def _merge_kernel(x_ref, yg_ref, y5_ref, ym_ref, wgate_ref, bgate_ref, wglu_ref, bglu_ref,
                  wup_ref, wo_ref, g_ref, b_ref, o_ref):
    x = x_ref[...]
    xb = x.astype(BF16)
    y5 = _gelu_tanh(y5_ref[...].astype(F32))
    y5 = y5 * _sigmoid(_dot(y5.astype(BF16), wglu_ref[...]) + bglu_ref[...])
    ys = (yg_ref[...], y5.astype(BF16), ym_ref[...])
    acc = None
    for r in range(N_BRANCH):
        gate = _sigmoid(_dot(xb, wgate_ref[:, r * D_MODEL:(r + 1) * D_MODEL])
                        + bgate_ref[:, r * D_MODEL:(r + 1) * D_MODEL])
        term = gate * _dot(ys[r], wup_ref[r])
        acc = term if acc is None else acc + term
    mix = _dot(acc.astype(BF16), wo_ref[...])
    o_ref[...] = _layer_norm(DN_ALPHA * x + mix, g_ref[...], b_ref[...])


def _merge(x2, yg, y5, ym, wgate, bgate, wglu, bglu, wup, wo, g, b, tm=512):
    t, dm = x2.shape
    row = lambda w: pl.BlockSpec((tm, w), lambda i: (i, 0))
    return pl.pallas_call(
        _merge_kernel,
        grid=(t // tm,),
        in_specs=[row(dm), row(BRANCH_W), row(BRANCH_W), row(BRANCH_W),
                  _full(wgate.shape), _full(bgate.shape), _full(wglu.shape), _full(bglu.shape),
                  _full(wup.shape), _full(wo.shape), _full(g.shape), _full(b.shape)],
        out_specs=row(dm),
        out_shape=jax.ShapeDtypeStruct((t, dm), F32),
        compiler_params=_cparams(("parallel",)),
        name="merge_ln1",
    )(x2, yg, y5, ym, wgate, bgate, wglu, bglu, wup, wo, g, b)


def _ple_ln2(x, xb, f, p_ref, pwg_ref, pwp_ref, g_ref, b_ref):
    e = _sigmoid(_dot(xb, pwg_ref[...])) * _dot(p_ref[...].astype(BF16), pwp_ref[...])
    return _layer_norm(DN_ALPHA * x + f + e, g_ref[...], b_ref[...])


def _ffn_kernel(x_ref, p_ref, wg_ref, wu_ref, wd_ref, pwg_ref, pwp_ref, g_ref, b_ref, o_ref):
    x = x_ref[...]
    xb = x.astype(BF16)
    hid = (_silu(_dot(xb, wg_ref[...])) * _dot(xb, wu_ref[...])).astype(BF16)
    f = _dot(hid, wd_ref[...])
    o_ref[...] = _ple_ln2(x, xb, f, p_ref, pwg_ref, pwp_ref, g_ref, b_ref)


def _ffn_layer(x2, p3, layer, wg, wu, wd, pwg, pwp, g, b, tm=256):
    t, dm = x2.shape
    row = lambda w: pl.BlockSpec((tm, w), lambda i: (i, 0))
    return pl.pallas_call(
        _ffn_kernel,
        grid=(t // tm,),
        in_specs=[row(dm), pl.BlockSpec((None, tm, PLE_DIM), lambda i: (layer, i, 0)),
                  _full(wg.shape), _full(wu.shape), _full(wd.shape),
                  _full(pwg.shape), _full(pwp.shape), _full(g.shape), _full(b.shape)],
        out_specs=row(dm),
        out_shape=jax.ShapeDtypeStruct((t, dm), F32),
        compiler_params=_cparams(("parallel",)),
        name="ffn_ple_ln2",
    )(x2, p3, wg, wu, wd, pwg, pwp, g, b)


def _router_kernel(x_ref, w_ref, b_ref, tril_ref, sel_ref, pr_ref, rk_ref, cnt_ref, base_ref):
    @pl.when(pl.program_id(0) == 0)
    def _():
        base_ref[...] = jnp.zeros_like(base_ref)

    xh, xl = _split2(x_ref[...])
    logits = _dot(xh, w_ref[0]) + _dot(xl, w_ref[0]) + _dot(xh, w_ref[1]) + b_ref[...]
    lane = lax.broadcasted_iota(jnp.int32, logits.shape, 1)
    neg = -jnp.inf
    logits = jnp.where(lane < N_EXPERTS, logits, neg)
    m1 = jnp.max(logits, axis=-1, keepdims=True)
    i1 = jnp.min(jnp.where(logits == m1, lane, LANES), axis=-1, keepdims=True)
    rest = jnp.where(lane == i1, neg, logits)
    m2 = jnp.max(rest, axis=-1, keepdims=True)
    i2 = jnp.min(jnp.where(rest == m2, lane, LANES), axis=-1, keepdims=True)
    e2 = jnp.exp(m2 - m1)
    p1 = 1.0 / (1.0 + e2)
    p2 = e2 / (1.0 + e2)
    hot = jnp.where((lane == i1) | (lane == i2), 1.0, 0.0)
    base = base_ref[0:1, :]
    before = _dot(tril_ref[...], hot.astype(BF16)) + base
    r1 = jnp.sum(jnp.where(lane == i1, before, 0.0), axis=-1, keepdims=True)
    r2 = jnp.sum(jnp.where(lane == i2, before, 0.0), axis=-1, keepdims=True)
    sel_ref[...] = jnp.where(lane == 0, i1, jnp.where(lane == 1, i2, 0))
    pr_ref[...] = jnp.where(lane == 0, p1, jnp.where(lane == 1, p2, 0.0))
    rk_ref[...] = jnp.where(lane == 0, r1, jnp.where(lane == 1, r2, 0.0))
    total = base + jnp.sum(hot, axis=0, keepdims=True)
    base_ref[...] = jnp.broadcast_to(total, base_ref.shape)
    cnt_ref[...] = jnp.broadcast_to(total, cnt_ref.shape)


def _router(x2, w, b, tm=512):
    t, dm = x2.shape
    i = np.arange(tm)
    tril = jnp.asarray((i[:, None] > i[None, :]).astype(np.float32), BF16)
    blk = pl.BlockSpec((tm, LANES), lambda i: (i, 0))
    return pl.pallas_call(
        _router_kernel,
        grid=(t // tm,),
        in_specs=[pl.BlockSpec((tm, dm), lambda i: (i, 0)), _full(w.shape), _full(b.shape), _full((tm, tm))],
        out_specs=[blk, blk, blk, pl.BlockSpec((SUBLANES, LANES), lambda i: (0, 0))],
        out_shape=[jax.ShapeDtypeStruct((t, LANES), jnp.int32), jax.ShapeDtypeStruct((t, LANES), F32),
                   jax.ShapeDtypeStruct((t, LANES), F32), jax.ShapeDtypeStruct((SUBLANES, LANES), F32)],
        scratch_shapes=[pltpu.VMEM((SUBLANES, LANES), F32)],
        compiler_params=_cparams(("arbitrary",)),
        name="moe_router",
    )(x2, w, b, tril)


TOP_K = 2
MOE_TILE = 1024
MOE_SUB = 256
MOE_FF_TILE = 512
DISPATCH_TILE = 512
DMA_ISSUE_UNROLL = 16


ROW_SUB = D_MODEL // LANES


def _row_copy_wait(src_rows, dst_rows, sem):
    pltpu.make_async_copy(src_rows, dst_rows, sem).wait()


def _rows_to_tiles(dst_ref, val):
    n = val.shape[0]
    for j in range(ROW_SUB):
        dst_ref[pl.ds(j, n, stride=ROW_SUB), :] = val[:, j * LANES:(j + 1) * LANES]


def _tiles_to_rows(src_ref, n):
    return jnp.concatenate([src_ref[pl.ds(j, n, stride=ROW_SUB), :] for j in range(ROW_SUB)], axis=1)


def _row_tile(ref, r):
    return ref.at[pl.ds(pl.multiple_of(r * ROW_SUB, ROW_SUB), ROW_SUB), :]


def _dispatch_kernel(pos_ref, x_ref, xs_ref, stage_ref, sem):
    tm = x_ref.shape[0]
    base = pl.program_id(0) * tm * TOP_K
    _rows_to_tiles(stage_ref, x_ref[...])

    def body(r, carry):
        for k in range(TOP_K):
            dst = pos_ref[base + r * TOP_K + k]
            pltpu.make_async_copy(_row_tile(stage_ref, r), _row_tile(xs_ref, dst), sem).start(priority=k)
        return carry

    lax.fori_loop(0, tm, body, 0, unroll=DMA_ISSUE_UNROLL)
    for _ in range(TOP_K):
        _row_copy_wait(stage_ref, xs_ref.at[pl.ds(0, tm * ROW_SUB), :], sem)


def _dispatch(pos_flat, x2, tm=DISPATCH_TILE):
    t, dm = x2.shape
    return pl.pallas_call(
        _dispatch_kernel,
        grid_spec=pltpu.PrefetchScalarGridSpec(
            num_scalar_prefetch=1,
            grid=(t // tm,),
            in_specs=[pl.BlockSpec((tm, dm), lambda i, pos: (i, 0))],
            out_specs=pl.BlockSpec(memory_space=pl.ANY),
            scratch_shapes=[pltpu.VMEM((tm * ROW_SUB, LANES), F32), pltpu.SemaphoreType.DMA],
        ),
        out_shape=jax.ShapeDtypeStruct((t * TOP_K * ROW_SUB, LANES), F32),
        compiler_params=_cparams(("arbitrary",)),
        name="moe_dispatch",
    )(pos_flat, x2)


def _moe_group_kernel(tile_ref, exp_ref, lo_ref, hi_ref, xs_ref, wg_ref, wu_ref, wd_ref, o_ref,
                      acc_ref, xb_ref, wgb_ref, wub_ref, wdb_ref):
    i = pl.program_id(0)
    f = pl.program_id(1)
    lo = lo_ref[i]
    hi = hi_ref[i]
    tm = acc_ref.shape[0]

    @pl.when(hi > lo)
    def _():
        @pl.when((lo == 0) & (f == 0))
        def _():
            acc_ref[...] = jnp.zeros_like(acc_ref)

        @pl.when(f == 0)
        def _():
            xb_ref[...] = _tiles_to_rows(xs_ref, tm).astype(BF16)

        whole = (lo == 0) & (hi == tm)

        @pl.when(whole)
        def _():
            xb = xb_ref[...]
            hid = (_silu(_dot(xb, wg_ref[0].astype(BF16))) * _dot(xb, wu_ref[0].astype(BF16))).astype(BF16)
            acc_ref[...] += _dot(hid, wd_ref[0].astype(BF16))

        @pl.when(jnp.logical_not(whole))
        def _():
            wgb_ref[...] = wg_ref[0].astype(BF16)
            wub_ref[...] = wu_ref[0].astype(BF16)
            wdb_ref[...] = wd_ref[0].astype(BF16)
            for sub in range(tm // MOE_SUB):
                @pl.when((hi > sub * MOE_SUB) & (lo < (sub + 1) * MOE_SUB))
                def _(sub=sub):
                    rsl = slice(sub * MOE_SUB, (sub + 1) * MOE_SUB)
                    xb = xb_ref[rsl, :]
                    hid = _silu(_dot(xb, wgb_ref[...])) * _dot(xb, wub_ref[...])
                    rid = lax.broadcasted_iota(jnp.int32, (MOE_SUB, 1), 0) + sub * MOE_SUB
                    hid = jnp.where((rid >= lo) & (rid < hi), hid, 0.0).astype(BF16)
                    acc_ref[rsl, :] += _dot(hid, wdb_ref[...])

        @pl.when((hi == tm) & (f == pl.num_programs(1) - 1))
        def _():
            _rows_to_tiles(o_ref, acc_ref[...])


def _moe_grouped(items, xs, wg, wu, wd, tm=MOE_TILE, tf=MOE_FF_TILE):
    tile, exp, lo, hi = items
    dm = wg.shape[1]
    dff = wg.shape[2]
    return pl.pallas_call(
        _moe_group_kernel,
        grid_spec=pltpu.PrefetchScalarGridSpec(
            num_scalar_prefetch=4,
            grid=(tile.shape[0], dff // tf),
            in_specs=[pl.BlockSpec((tm * ROW_SUB, LANES), lambda i, f, tl, ex, lo_, hi_: (tl[i], 0)),
                      pl.BlockSpec((1, dm, tf), lambda i, f, tl, ex, lo_, hi_: (ex[i], 0, f)),
                      pl.BlockSpec((1, dm, tf), lambda i, f, tl, ex, lo_, hi_: (ex[i], 0, f)),
                      pl.BlockSpec((1, tf, dm), lambda i, f, tl, ex, lo_, hi_: (ex[i], f, 0))],
            out_specs=pl.BlockSpec((tm * ROW_SUB, LANES), lambda i, f, tl, ex, lo_, hi_: (tl[i], 0)),
            scratch_shapes=[pltpu.VMEM((tm, dm), F32), pltpu.VMEM((tm, dm), BF16), pltpu.VMEM((dm, tf), BF16),
                            pltpu.VMEM((dm, tf), BF16), pltpu.VMEM((tf, dm), BF16)],
        ),
        out_shape=jax.ShapeDtypeStruct(xs.shape, F32),
        compiler_params=_cparams(("arbitrary", "arbitrary")),
        name="moe_grouped",
    )(tile, exp, lo, hi, xs, wg, wu, wd)


def _moe_items(counts, n_rows, tm=MOE_TILE):
    n_tiles = n_rows // tm
    ends = jnp.cumsum(counts)
    cuts = jnp.sort(jnp.concatenate([jnp.arange(n_tiles + 1, dtype=jnp.int32) * tm, ends[:-1]]))
    start, stop = cuts[:-1], cuts[1:]
    tile = jnp.minimum(start // tm, n_tiles - 1)
    exp = jnp.minimum(jnp.sum(ends[None, :] <= start[:, None], axis=1), N_EXPERTS - 1)
    lo = start - tile * tm
    hi = stop - tile * tm
    return tile.astype(jnp.int32), exp.astype(jnp.int32), lo.astype(jnp.int32), hi.astype(jnp.int32)


def _combine_kernel(pos_ref, x_ref, pr_ref, p_ref, pwg_ref, pwp_ref, g_ref, b_ref, ys_ref, o_ref, gat_ref, sem):
    tm = x_ref.shape[0]
    step = pl.program_id(0)
    n_steps = pl.num_programs(0)

    def start_gather(which, slot):
        base = which * tm * TOP_K

        def body(r, carry):
            for k in range(TOP_K):
                src = pos_ref[base + r * TOP_K + k]
                pltpu.make_async_copy(_row_tile(ys_ref, src), _row_tile(gat_ref.at[slot, k], r),
                                      sem.at[slot]).start(priority=k)
            return carry

        lax.fori_loop(0, tm, body, 0, unroll=DMA_ISSUE_UNROLL)

    @pl.when(step == 0)
    def _():
        start_gather(0, 0)

    x = x_ref[...]
    xb = x.astype(BF16)
    pr = pr_ref[...]
    for slot in range(2):
        @pl.when(step % 2 == slot)
        def _(slot=slot):
            @pl.when(step + 1 < n_steps)
            def _():
                start_gather(step + 1, 1 - slot)

            e = _sigmoid(_dot(xb, pwg_ref[...])) * _dot(p_ref[...].astype(BF16), pwp_ref[...])
            for k in range(TOP_K):
                _row_copy_wait(ys_ref.at[pl.ds(0, tm * ROW_SUB), :], gat_ref.at[slot, k], sem.at[slot])
            f = pr[:, 0:1] * _tiles_to_rows(gat_ref.at[slot, 0], tm)
            for k in range(1, TOP_K):
                f = f + pr[:, k:k + 1] * _tiles_to_rows(gat_ref.at[slot, k], tm)
            o_ref[...] = _layer_norm(DN_ALPHA * x + f + e, g_ref[...], b_ref[...])


def _combine_layer(pos_flat, x2, pr, p3, layer, pwg, pwp, g, b, ys, tm=DISPATCH_TILE):
    t, dm = x2.shape
    row = lambda w: pl.BlockSpec((tm, w), lambda i, pos: (i, 0))
    full = lambda a: pl.BlockSpec(a.shape, lambda i, pos: (0,) * a.ndim, pipeline_mode=pl.Buffered(1))
    return pl.pallas_call(
        _combine_kernel,
        grid_spec=pltpu.PrefetchScalarGridSpec(
            num_scalar_prefetch=1,
            grid=(t // tm,),
            in_specs=[row(dm), row(LANES), pl.BlockSpec((None, tm, PLE_DIM), lambda i, pos: (layer, i, 0)),
                      full(pwg), full(pwp), full(g), full(b),
                      pl.BlockSpec(memory_space=pl.ANY)],
            out_specs=row(dm),
            scratch_shapes=[pltpu.VMEM((2, TOP_K, tm * ROW_SUB, LANES), F32), pltpu.SemaphoreType.DMA((2,))],
        ),
        out_shape=jax.ShapeDtypeStruct((t, dm), F32),
        compiler_params=_cparams(("arbitrary",)),
        name="moe_combine_ple_ln2",
    )(pos_flat, x2, pr, p3, pwg, pwp, g, b, ys)


def _row(v):
    return v.reshape(1, -1).astype(F32)


def _pad_lanes(w):
    return jnp.pad(w, ((0, 0), (0, LANES - w.shape[1])))


def kernel(x, p, w_in, b_in, gla_w_a2, gla_b_a2, gla_norm_g, s5_a_re, s5_a_im, s5_log_dt, s5_b_re, s5_b_im,
           s5_c_re, s5_c_im, s5_d, s5_w_glu, s5_b_glu, ml_conv_w, ml_conv_b, ml_norm_g, w_up, w_o, ln1_g, ln1_b,
           ffn_wg, ffn_wu, ffn_wd, moe_router, moe_router_b, moe_wg, moe_wu, moe_wd, ple_w_gate, ple_w_proj,
           ln2_g, ln2_b):
    bsz, s, dm = x.shape
    t = bsz * s
    hi = lax.Precision.HIGHEST
    o = IN_OFF
    tri = jnp.asarray(_chunk_tri(SEQ_BLOCK, CHUNK), BF16)
    ml_tri_np = _chunk_tri(SEQ_BLOCK, ML_CHUNK)
    ml_tri = jnp.asarray(ml_tri_np, BF16)
    ml_trit = jnp.asarray(ml_tri_np.T, BF16)
    s5_ec, s5_eg, s5_c0, s5_ftc, s5_dd, s5_atab = _s5_tables(s5_a_re, s5_a_im, s5_log_dt, s5_b_re, s5_b_im,
                                                             s5_c_re, s5_c_im, s5_d)
    p3 = p.reshape(DEPTH, t, PLE_DIM)

    w_in_b = w_in.astype(BF16)
    for i in range(DEPTH):
        w, b = w_in_b[i], b_in[i]
        sl = lambda k: (w[:, o[k]:o[k + 1]], b[o[k]:o[k + 1]])
        (wq, bq), (wk, bk), (wv, bv), (_, ba), (wg_, bg_) = sl(0), sl(1), sl(2), sl(3), sl(4)
        wz = jnp.dot(w_in[i, :, o[3]:o[4]], gla_w_a2[i], precision=hi).astype(BF16)
        bz = jnp.dot(ba, gla_w_a2[i], precision=hi) + gla_b_a2[i]
        (wu_, bu_) = sl(5)
        w_gla = jnp.concatenate([wq, wk, wz, wv, wg_, wu_], axis=1)
        b_gla = _row(jnp.concatenate([bq, bk, bz, bv, bg_, bu_]))
        y_gla, u = _gla_mixer(x, w_gla, b_gla, _row(gla_norm_g[i]), tri)

        x2 = x.reshape(t, dm)
        y5 = _s5_mixer(u, s5_ec[i], s5_eg[i], s5_c0[i], s5_ftc[i], s5_dd[i], s5_atab[i])

        (wmq, bmq), (wmk, bmk), (wmv, bmv), (wmo, bmo) = sl(6), sl(7), sl(8), sl(11)
        w_ml = jnp.concatenate([wmq, wmk, wmv, wmo], axis=1)
        b_ml = _row(jnp.concatenate([bmq, bmk, bmv, bmo]))
        w_if = w_in[i, :, o[9]:o[11]]
        b_if = b[o[9]:o[11]]
        if_h = w_if.astype(BF16)
        if_l = (w_if - if_h.astype(F32)).astype(BF16)
        wgc = jnp.stack([_pad_lanes(if_h), _pad_lanes(if_l)])
        wgr = jnp.stack([if_h.T, if_l.T])
        y_ml = _ml_mixer(x, w_ml, b_ml, wgc, _pad_lanes(_row(b_if)), wgr, b_if.reshape(-1, 1).astype(F32),
                         ml_conv_w[i].astype(F32), _row(ml_conv_b[i]), _row(ml_norm_g[i]), ml_tri, ml_trit)

        (wgt, bgt) = sl(12)
        x1 = _merge(x2, y_gla.reshape(t, BRANCH_W), y5, y_ml.reshape(t, BRANCH_W),
                    wgt, _row(bgt), s5_w_glu[i].astype(BF16), _row(s5_b_glu[i]),
                    w_up[i].astype(BF16), w_o[i].astype(BF16), _row(ln1_g[i]), _row(ln1_b[i]))

        pwg = ple_w_gate[i].astype(BF16)
        pwp = ple_w_proj[i].astype(BF16)
        j = i // 2
        if i % 2 == 0:
            x2n = _ffn_layer(x1, p3, i, ffn_wg[j].astype(BF16), ffn_wu[j].astype(BF16), ffn_wd[j].astype(BF16),
                             pwg, pwp, _row(ln2_g[i]), _row(ln2_b[i]))
        else:
            wr = moe_router[j]
            wr_h = wr.astype(BF16)
            wr_l = (wr - wr_h.astype(F32)).astype(BF16)
            sel, pr, rk, cnt = _router(x1, jnp.stack([_pad_lanes(wr_h), _pad_lanes(wr_l)]),
                                       _pad_lanes(_row(moe_router_b[j])))
            counts = cnt[0, :N_EXPERTS].astype(jnp.int32)
            starts = jnp.cumsum(counts) - counts
            sel2 = sel[:, :TOP_K]
            pos = rk[:, :TOP_K].astype(jnp.int32) + jnp.sum(
                jnp.where(sel2[..., None] == jnp.arange(N_EXPERTS), starts, 0), axis=-1)
            pos_flat = pos.reshape(-1)
            xs = _dispatch(pos_flat, x1)
            ys = _moe_grouped(_moe_items(counts, t * TOP_K), xs, moe_wg[j], moe_wu[j], moe_wd[j])
            x2n = _combine_layer(pos_flat, x1, pr, p3, i, pwg, pwp, _row(ln2_g[i]), _row(ln2_b[i]), ys)
        x = x2n.reshape(bsz, s, dm)
    return x
```
